```python
import math
import jax, jax.numpy as jnp
from jax import lax
import numpy as np

D_MODEL = 1024
BATCH = 16
SEQ = 256
DEPTH = 2
DEC_BATCH = 2
DEC_SEQ = 4096
PAST_LEN = 512

GRID_W = 64
BRANCH_WIDTH = D_MODEL
N_BRANCH = 3
POOL_WIDTH = BRANCH_WIDTH
POOL_WINDOWS = (2, 4, 8, 16)
POOL_GROUP = POOL_WIDTH // len(POOL_WINDOWS)
DIFF_HD = 64
DIFF_VD = 2 * DIFF_HD
DIFF_HEADS = BRANCH_WIDTH // DIFF_VD
MLA_NOPE = 128
MLA_ROPE = 64
MLA_QK = MLA_NOPE + MLA_ROPE
MLA_VD = 128
MLA_HEADS = BRANCH_WIDTH // MLA_VD
MLA_Q_RANK = 768
MLA_KV_RANK = 512
N_EXPERTS = 32
TOP_K = 4
D_FF = D_MODEL
SWIGLU_ALPHA = 1.702
SWIGLU_LIMIT = 7.0
ROPE_THETA = 10000.0
NORM_EPS = 1e-6
Q_BLOCK = 128
IN_SPLITS = (POOL_WIDTH, DIFF_HEADS * 2 * DIFF_HD, DIFF_HEADS * 2 * DIFF_HD, DIFF_HEADS * DIFF_VD,
             MLA_Q_RANK, MLA_KV_RANK, MLA_ROPE, N_BRANCH * BRANCH_WIDTH)
IN_COLS = sum(IN_SPLITS)

kernel_name = 'hybrid_diffusion_prefix_trunk_step'


def rms_norm(x, gain=None):
    xf = x.astype(jnp.float32)
    y = xf * lax.rsqrt(jnp.mean(xf * xf, axis=-1, keepdims=True) + NORM_EPS)
    if gain is not None:
        y = y * gain.astype(jnp.float32)
    return y.astype(x.dtype)


def axial_rope_angles(n_tokens, rot_dim):
    n_rows = n_tokens // GRID_W
    row = jnp.repeat(jnp.arange(n_rows, dtype=jnp.float32), GRID_W)
    col = jnp.tile(jnp.arange(GRID_W, dtype=jnp.float32), n_rows)
    n_freq = rot_dim // 4
    inv_freq = ROPE_THETA ** (-jnp.arange(n_freq, dtype=jnp.float32) / n_freq)
    ang = jnp.concatenate([row[:, None] * inv_freq, col[:, None] * inv_freq], axis=-1)
    return jnp.cos(ang), jnp.sin(ang)


def apply_rope(x, cos, sin):
    half = x.shape[-1] // 2
    shape = (1, cos.shape[0]) + (1,) * (x.ndim - 3) + (half,)
    cs, sn = cos.reshape(shape), sin.reshape(shape)
    x1 = x[..., :half].astype(jnp.float32)
    x2 = x[..., half:].astype(jnp.float32)
    return jnp.concatenate([x1 * cs - x2 * sn, x2 * cs + x1 * sn], axis=-1).astype(x.dtype)


def map_query_blocks(fn, q):
    b, n = q.shape[:2]
    nb = n // Q_BLOCK
    qb = jnp.moveaxis(q.reshape((b, nb, Q_BLOCK) + q.shape[2:]), 1, 0)
    out = jnp.moveaxis(lax.map(fn, qb), 0, 1)
    return out.reshape((b, n) + out.shape[3:])


def diff_attention(q, k, v, lam):
    kf, vf = k.astype(jnp.float32), v.astype(jnp.float32)
    scale = DIFF_HD ** -0.5

    def block(qb):
        s = jnp.einsum('bqhcd,bkhcd->bchqk', qb.astype(jnp.float32), kf) * scale
        p = jax.nn.softmax(s, axis=-1)
        a = p[:, 0] - lam * p[:, 1]
        return jnp.einsum('bhqk,bkhe->bqhe', a, vf).astype(v.dtype)

    return map_query_blocks(block, q)


def softmax_attention(q, k, v):
    kf, vf = k.astype(jnp.float32), v.astype(jnp.float32)
    scale = q.shape[-1] ** -0.5

    def block(qb):
        s = jnp.einsum('bqhd,bkhd->bhqk', qb.astype(jnp.float32), kf) * scale
        p = jax.nn.softmax(s, axis=-1)
        return jnp.einsum('bhqk,bkhe->bqhe', p, vf).astype(v.dtype)

    return map_query_blocks(block, q)


def multiscale_pool(u, pool_w, pool_scale):
    b, n, _ = u.shape
    uf = u.astype(jnp.float32)
    cs = jnp.concatenate([jnp.zeros((b, 1, POOL_WIDTH), jnp.float32), jnp.cumsum(uf, axis=1)], axis=1)
    t = np.arange(n, dtype=np.int32)
    outs = []
    for g, w in enumerate(POOL_WINDOWS):
        lo = np.clip(t - w // 2, 0, n).astype(np.int32)
        hi = np.clip(t - w // 2 + w, 0, n).astype(np.int32)
        csg = cs[..., g * POOL_GROUP:(g + 1) * POOL_GROUP]
        cnt = jnp.asarray(hi - lo, jnp.float32)[None, :, None]
        mean = (jnp.take(csg, hi, axis=1) - jnp.take(csg, lo, axis=1)) / cnt
        outs.append(mean - uf[..., g * POOL_GROUP:(g + 1) * POOL_GROUP])
    pooled = jnp.stack(outs, axis=2)
    mixed = jnp.einsum('bngc,gcd->bngd', pooled, pool_w.astype(jnp.float32)).reshape(b, n, POOL_WIDTH)
    return (mixed * pool_scale.astype(jnp.float32)).astype(u.dtype)


def mla_keys(ckv, krope, w_ukv, k_norm):
    b, n = ckv.shape[:2]
    kv = (ckv @ w_ukv).reshape(b, n, MLA_HEADS, MLA_NOPE + MLA_VD)
    k_nope, v = kv[..., :MLA_NOPE], kv[..., MLA_NOPE:]
    k = jnp.concatenate([k_nope, jnp.broadcast_to(krope[:, :, None, :], (b, n, MLA_HEADS, MLA_ROPE))], axis=-1)
    return rms_norm(k, k_norm), v


def rope_tail(x, cos, sin):
    return jnp.concatenate([x[..., :MLA_NOPE], apply_rope(x[..., MLA_NOPE:], cos, sin)], axis=-1)


def token_mixing(h, p, lam_init, ctx, rope):
    b, n, _ = h.shape
    u = h @ p['w_in']
    offs = [int(i) for i in np.cumsum(IN_SPLITS)[:-1]]
    u_pool, u_dq, u_dk, u_dv, u_cq, u_ckv, u_kr, u_gate = jnp.split(u, offs, axis=-1)

    y_pool = multiscale_pool(u_pool, p['pool_w'], p['pool_scale'])

    dq = rms_norm(u_dq.reshape(b, n, DIFF_HEADS, 2, DIFF_HD), p['diff_q_norm'])
    dk = rms_norm(u_dk.reshape(b, n, DIFF_HEADS, 2, DIFF_HD), p['diff_k_norm'])
    dv = u_dv.reshape(b, n, DIFF_HEADS, DIFF_VD)
    lp = p['diff_lambda'].astype(jnp.float32)
    lam = jnp.exp(jnp.sum(lp[0] * lp[1])) - jnp.exp(jnp.sum(lp[2] * lp[3])) + lam_init

    cq = rms_norm(u_cq, p['mla_q_a_norm'])
    mq = rms_norm((cq @ p['w_uq']).reshape(b, n, MLA_HEADS, MLA_QK), p['mla_q_norm'])
    ckv = rms_norm(u_ckv, p['mla_kv_a_norm'])
    mk, mv = mla_keys(ckv, u_kr, p['w_ukv'], p['mla_k_norm'])

    if ctx is None:
        keys_d, vals_d, keys_m, vals_m = dk, dv, mk, mv
        ctx_out = (dk.reshape(b, n, DIFF_HEADS, 2 * DIFF_HD), dv, ckv, u_kr)
    else:
        c_dk, c_dv, c_ckv, c_kr = ctx
        cos_d, sin_d, cos_m, sin_m = rope
        dq, dk = apply_rope(dq, cos_d, sin_d), apply_rope(dk, cos_d, sin_d)
        mq, mk = rope_tail(mq, cos_m, sin_m), rope_tail(mk, cos_m, sin_m)
        ctx_len = c_dk.shape[1]
        keys_d = jnp.concatenate([c_dk.reshape(b, ctx_len, DIFF_HEADS, 2, DIFF_HD).astype(dk.dtype), dk], axis=1)
        vals_d = jnp.concatenate([c_dv.astype(dv.dtype), dv], axis=1)
        c_mk, c_mv = mla_keys(c_ckv, c_kr, p['w_ukv'], p['mla_k_norm'])
        keys_m = jnp.concatenate([c_mk.astype(mk.dtype), mk], axis=1)
        vals_m = jnp.concatenate([c_mv.astype(mv.dtype), mv], axis=1)
        ctx_out = None

    od = diff_attention(dq, keys_d, vals_d, lam)
    y_diff = (rms_norm(od, p['diff_out_norm']) * (1.0 - lam_init)).reshape(b, n, BRANCH_WIDTH)
    y_mla = softmax_attention(mq, keys_m, vals_m).reshape(b, n, BRANCH_WIDTH)

    branches = jnp.stack([y_pool, y_diff.astype(h.dtype), y_mla], axis=2)
    z = jnp.einsum('bnrw,rwd->bnrd', branches, p['w_branch'])
    gates = jax.nn.sigmoid(u_gate.astype(jnp.float32).reshape(b, n, N_BRANCH, D_MODEL))
    merged = jnp.sum(gates * z.astype(jnp.float32), axis=2).astype(h.dtype)
    return merged @ p['w_out'], ctx_out


def moe_ffn(h, p):
    b, n, d = h.shape
    t = h.reshape(b * n, d)
    logits = (t @ p['router_w'] + p['router_b']).astype(jnp.float32)
    top_val, top_idx = lax.top_k(logits, TOP_K)
    top_w = jax.nn.softmax(top_val, axis=-1)
    combine = jnp.einsum('tk,tke->te', top_w, jax.nn.one_hot(top_idx, N_EXPERTS, dtype=jnp.float32))
    out = jnp.zeros((b * n, d), jnp.float32)
    for e in range(N_EXPERTS):
        gu = t @ p['moe_w_gu'][e] + p['moe_b_gu'][e]
        gate = jnp.minimum(gu[:, :D_FF], SWIGLU_LIMIT)
        up = jnp.clip(gu[:, D_FF:], -SWIGLU_LIMIT, SWIGLU_LIMIT)
        act = (up + 1.0) * gate * jax.nn.sigmoid(SWIGLU_ALPHA * gate)
        out = out + combine[:, e:e + 1] * (act @ p['moe_w_down'][e] + p['moe_b_down'][e])
    return out.astype(h.dtype).reshape(b, n, d)


def trunk_layer(x, cond, p, lam_init, ctx, rope):
    mod = jax.nn.silu(cond) @ p['w_ada'] + p['b_ada']
    sh1, sc1, g1, sh2, sc2, g2 = jnp.split(mod, 6, axis=-1)
    h = rms_norm(x) * (1.0 + sc1) + sh1
    y, ctx_out = token_mixing(h, p, lam_init, ctx, rope)
    x = x + g1 * y
    h = rms_norm(x) * (1.0 + sc2) + sh2
    x = x + g2 * moe_ffn(h, p)
    return x, ctx_out


def setup_inputs(seed: int = 0) -> dict:
    key = jax.random.key(seed)
    ks = jax.random.split(key, 32)
    f32 = jnp.float32

    def nrm(k, shape, scale):
        return scale * jax.random.normal(k, shape, f32)

    def gain(k, shape):
        return 1.0 + 0.05 * jax.random.normal(k, shape, f32)

    return {
        'x_prompt': nrm(ks[0], (BATCH, SEQ, D_MODEL), 1.0),
        'x_sample': nrm(ks[1], (DEC_BATCH, DEC_SEQ, D_MODEL), 1.0),
        'c': nrm(ks[2], (DEC_BATCH, D_MODEL), 1.0),
        'cache_diff_k': nrm(ks[3], (DEC_BATCH, DEPTH, PAST_LEN, DIFF_HEADS, 2 * DIFF_HD), 1.0),
        'cache_diff_v': nrm(ks[4], (DEC_BATCH, DEPTH, PAST_LEN, DIFF_HEADS, DIFF_VD), 1.0),
        'cache_mla_ckv': nrm(ks[5], (DEC_BATCH, DEPTH, PAST_LEN, MLA_KV_RANK), 1.0),
        'cache_mla_krope': nrm(ks[6], (DEC_BATCH, DEPTH, PAST_LEN, MLA_ROPE), 1.0),
        'c_ctx': nrm(ks[7], (D_MODEL,), 1.0),
        'w_ada': nrm(ks[8], (DEPTH, D_MODEL, 6 * D_MODEL), 0.5 * D_MODEL ** -0.5),
        'b_ada': nrm(ks[9], (DEPTH, 6 * D_MODEL), 0.02),
        'w_in': nrm(ks[10], (DEPTH, D_MODEL, IN_COLS), D_MODEL ** -0.5),
        'pool_w': nrm(ks[11], (DEPTH, len(POOL_WINDOWS), POOL_GROUP, POOL_GROUP), POOL_GROUP ** -0.5),
        'pool_scale': gain(ks[12], (DEPTH, POOL_WIDTH)),
        'diff_q_norm': gain(ks[13], (DEPTH, DIFF_HD)),
        'diff_k_norm': gain(ks[14], (DEPTH, DIFF_HD)),
        'diff_lambda': nrm(ks[15], (DEPTH, 4, DIFF_HD), 0.1),
        'diff_out_norm': gain(ks[16], (DEPTH, DIFF_VD)),
        'mla_q_a_norm': gain(ks[17], (DEPTH, MLA_Q_RANK)),
        'w_uq': nrm(ks[18], (DEPTH, MLA_Q_RANK, MLA_HEADS * MLA_QK), MLA_Q_RANK ** -0.5),
        'mla_kv_a_norm': gain(ks[19], (DEPTH, MLA_KV_RANK)),
        'w_ukv': nrm(ks[20], (DEPTH, MLA_KV_RANK, MLA_HEADS * (MLA_NOPE + MLA_VD)), MLA_KV_RANK ** -0.5),
        'mla_q_norm': gain(ks[21], (DEPTH, MLA_QK)),
        'mla_k_norm': gain(ks[22], (DEPTH, MLA_QK)),
        'w_branch': nrm(ks[23], (DEPTH, N_BRANCH, BRANCH_WIDTH, D_MODEL), BRANCH_WIDTH ** -0.5),
        'w_out': nrm(ks[24], (DEPTH, D_MODEL, D_MODEL), D_MODEL ** -0.5),
        'router_w': nrm(ks[25], (DEPTH, D_MODEL, N_EXPERTS), D_MODEL ** -0.5),
        'router_b': nrm(ks[26], (DEPTH, N_EXPERTS), 0.01),
        'moe_w_gu': nrm(ks[27], (DEPTH, N_EXPERTS, D_MODEL, 2 * D_FF), D_MODEL ** -0.5),
        'moe_b_gu': nrm(ks[28], (DEPTH, N_EXPERTS, 2 * D_FF), 0.02),
        'moe_w_down': nrm(ks[29], (DEPTH, N_EXPERTS, D_FF, D_MODEL), D_FF ** -0.5),
        'moe_b_down': nrm(ks[30], (DEPTH, N_EXPERTS, D_MODEL), 0.02),
    }


def reference(x_prompt, x_sample, c, cache_diff_k, cache_diff_v, cache_mla_ckv, cache_mla_krope, c_ctx,
              w_ada, b_ada, w_in, pool_w, pool_scale, diff_q_norm, diff_k_norm, diff_lambda, diff_out_norm,
              mla_q_a_norm, w_uq, mla_kv_a_norm, w_ukv, mla_q_norm, mla_k_norm, w_branch, w_out,
              router_w, router_b, moe_w_gu, moe_b_gu, moe_w_down, moe_b_down):
    n_lat = x_sample.shape[1]
    cos_d, sin_d = axial_rope_angles(n_lat, DIFF_HD)
    cos_m, sin_m = axial_rope_angles(n_lat, MLA_ROPE)
    rope = (cos_d, sin_d, cos_m, sin_m)
    cond_ctx = c_ctx[None, None, :]
    cond_lat = c[:, None, :]
    xp, xs = x_prompt, x_sample
    new_dk, new_dv, new_ckv, new_kr = [], [], [], []
    for l in range(DEPTH):
        p = {
            'w_ada': w_ada[l], 'b_ada': b_ada[l], 'w_in': w_in[l],
            'pool_w': pool_w[l], 'pool_scale': pool_scale[l],
            'diff_q_norm': diff_q_norm[l], 'diff_k_norm': diff_k_norm[l],
            'diff_lambda': diff_lambda[l], 'diff_out_norm': diff_out_norm[l],
            'mla_q_a_norm': mla_q_a_norm[l], 'w_uq': w_uq[l], 'mla_kv_a_norm': mla_kv_a_norm[l],
            'w_ukv': w_ukv[l], 'mla_q_norm': mla_q_norm[l], 'mla_k_norm': mla_k_norm[l],
            'w_branch': w_branch[l], 'w_out': w_out[l],
            'router_w': router_w[l], 'router_b': router_b[l],
            'moe_w_gu': moe_w_gu[l], 'moe_b_gu': moe_b_gu[l],
            'moe_w_down': moe_w_down[l], 'moe_b_down': moe_b_down[l],
        }
        lam_init = 0.8 - 0.6 * math.exp(-0.3 * l)
        xp, (dk, dv, ckv, kr) = trunk_layer(xp, cond_ctx, p, lam_init, None, None)
        new_dk.append(dk)
        new_dv.append(dv)
        new_ckv.append(ckv)
        new_kr.append(kr)
        ctx = (cache_diff_k[:, l], cache_diff_v[:, l], cache_mla_ckv[:, l], cache_mla_krope[:, l])
        xs, _ = trunk_layer(xs, cond_lat, p, lam_init, ctx, rope)
    new_diff_k = jnp.stack(new_dk, axis=1)
    new_diff_v = jnp.stack(new_dv, axis=1)
    new_mla_ckv = jnp.stack(new_ckv, axis=1)
    new_mla_krope = jnp.stack(new_kr, axis=1)
    return (xp, xs, new_diff_k, new_diff_v, new_mla_ckv, new_mla_krope)
```

```python
import functools
import math

import jax
import jax.numpy as jnp
import numpy as np
from jax import lax
from jax.experimental import pallas as pl
from jax.experimental.pallas import tpu as pltpu

F32 = jnp.float32
BF16 = jnp.bfloat16

D_MODEL = 1024
N_POOL_GROUPS = 4
POOL_GROUP = D_MODEL // N_POOL_GROUPS
DIFF_HD = 64
DIFF_HEADS = 8
HEAD_W = 128
MLA_HEADS = 8
MLA_NOPE = 128
MLA_ROPE = 64
MLA_QK = MLA_NOPE + MLA_ROPE
MLA_W = 256
MLA_Q_RANK = 768
MLA_KV_RANK = 512
N_EXPERTS = 32
TOP_K = 4
D_FF = 1024
EXPERT_LANES = 128
SWIGLU_ALPHA = 1.702
SWIGLU_LIMIT = 7.0
ROPE_THETA = 10000.0
NORM_EPS = 1e-6
GRID_W = 64
LOG2E = math.log2(math.e)
NEG_BIG = -1e30
VMEM_LIMIT = 56 * 1024 * 1024

ROW_BLOCK = 256
MOE_ROW_BLOCK = 1024
ATTN_Q_BLOCK = 512
ATTN_K_CHUNK = 512

_NT = (((1,), (1,)), ((), ()))


def _cparams(*sem):
    return pltpu.CompilerParams(dimension_semantics=sem, vmem_limit_bytes=VMEM_LIMIT)


def _resident(shape):
    nd = len(shape)
    return pl.BlockSpec(shape, lambda *_: (0,) * nd, pipeline_mode=pl.Buffered(1))


def _sigmoid(x):
    return 1.0 / (1.0 + jnp.exp(-x))


def _lane_iota(shape):
    return lax.broadcasted_iota(jnp.int32, shape, len(shape) - 1)


def _rms(x, width):
    ss = jnp.sum(x * x, axis=-1, keepdims=True)
    return x * lax.rsqrt(ss * (1.0 / width) + NORM_EPS)


def _rope(x, cos, sin):
    n = x.shape[-1]
    lane = _lane_iota(x.shape)
    partner = jnp.where((lane & 32) != 0, pltpu.roll(x, 32, 1), pltpu.roll(x, n - 32, 1))
    return x * cos + partner * sin


def _modulated_norm(x, shift, scale):
    return _rms(x, D_MODEL) * (1.0 + scale) + shift


def _ada_kernel(cond_ref, w_ref, b_ref, o_ref):
    cnd = cond_ref[...]
    act = cnd * _sigmoid(cnd)
    o_ref[0] = jnp.dot(act, w_ref[0], preferred_element_type=F32) + b_ref[0]


def _ada(cond8, w_ada, b_ada):
    depth = w_ada.shape[0]
    n_chunk = w_ada.shape[2] // D_MODEL
    return pl.pallas_call(
        _ada_kernel,
        grid=(depth, n_chunk),
        in_specs=[
            pl.BlockSpec((8, D_MODEL), lambda l, j: (0, 0)),
            pl.BlockSpec((1, D_MODEL, D_MODEL), lambda l, j: (l, 0, j)),
            pl.BlockSpec((1, 1, D_MODEL), lambda l, j: (l, 0, j)),
        ],
        out_specs=pl.BlockSpec((1, 8, D_MODEL), lambda l, j: (l, 0, j)),
        out_shape=jax.ShapeDtypeStruct((depth, 8, n_chunk * D_MODEL), F32),
        compiler_params=_cparams("arbitrary", "arbitrary"),
        name="ada",
    )(cond8, w_ada, b_ada.reshape(depth, 1, -1))


def _diff_qk_norm(u, gain, cos, sin):
    outs = []
    for h in range(DIFF_HEADS):
        c = u[:, h * HEAD_W:(h + 1) * HEAD_W]
        lane = _lane_iota(c.shape)
        sq = c * c
        lo = jnp.sum(jnp.where(lane < DIFF_HD, sq, 0.0), axis=-1, keepdims=True)
        hi = jnp.sum(jnp.where(lane >= DIFF_HD, sq, 0.0), axis=-1, keepdims=True)
        ms = jnp.where(lane < DIFF_HD, lo, hi) * (1.0 / DIFF_HD)
        y = c * lax.rsqrt(ms + NORM_EPS) * gain[:, h * HEAD_W:(h + 1) * HEAD_W]
        outs.append((y, _rope(y, cos, sin)))
    return outs


def _proj_a_kernel(x_ref, mod_ref, w_ref, gq_ref, gk_ref, cos_ref, sin_ref,
                   pool_ref, dq_ref, dkf_ref, dkb_ref, dvf_ref, dvb_ref, gate_ref):
    mod = mod_ref[0]
    h = _modulated_norm(x_ref[...], mod[0:1], mod[1:2]).astype(BF16)
    cos, sin = cos_ref[...], sin_ref[...]

    def seg(k):
        return jnp.dot(h, w_ref[:, k * D_MODEL:(k + 1) * D_MODEL], preferred_element_type=F32)

    pool_ref[...] = seg(0).astype(BF16)
    q_scale = DIFF_HD ** -0.5 * LOG2E
    for hd, (_, roped) in enumerate(_diff_qk_norm(seg(1), gq_ref[...], cos, sin)):
        dq_ref[:, hd * HEAD_W:(hd + 1) * HEAD_W] = (roped * q_scale).astype(BF16)
    for hd, (plain, roped) in enumerate(_diff_qk_norm(seg(2), gk_ref[...], cos, sin)):
        dkf_ref[:, hd * HEAD_W:(hd + 1) * HEAD_W] = plain
        dkb_ref[:, hd * HEAD_W:(hd + 1) * HEAD_W] = roped.astype(BF16)
    dv = seg(3)
    dvf_ref[...] = dv
    dvb_ref[...] = dv.astype(BF16)
    for r in range(3):
        gate_ref[:, r * D_MODEL:(r + 1) * D_MODEL] = _sigmoid(seg(4 + r)).astype(BF16)


def _mla_heads(u, gain, cos, sin, rope):
    outs = []
    for h in range(MLA_HEADS):
        c = u[:, h * MLA_W:(h + 1) * MLA_W]
        y = _rms(c, MLA_QK) * gain[:, h * MLA_W:(h + 1) * MLA_W]
        outs.append(_rope(y, cos, sin) if rope else y)
    return outs


def _mla_kv(ckv_n, krw, wkn_ref, wv_ref, gk, cos, sin, rope, k_ref, v_ref):
    kn = jnp.dot(ckv_n, wkn_ref[...], preferred_element_type=F32)
    kfull = jnp.concatenate([kn[:, h * MLA_W:(h + 1) * MLA_W] + krw for h in range(MLA_HEADS)], axis=-1)
    for hd, y in enumerate(_mla_heads(kfull, gk, cos, sin, rope)):
        k_ref[:, hd * MLA_W:(hd + 1) * MLA_W] = y.astype(BF16)
    v_ref[...] = jnp.dot(ckv_n, wv_ref[...], preferred_element_type=F32).astype(BF16)


def _proj_b_kernel(x_ref, mod_ref, w_ref, gqa_ref, gkva_ref, wuq_ref, wkn_ref, wv_ref, gq_ref, gk_ref,
                   cos_ref, sin_ref, mq_ref, ckv_ref, kr_ref, mk_ref, mv_ref):
    mod = mod_ref[0]
    h = _modulated_norm(x_ref[...], mod[0:1], mod[1:2]).astype(BF16)
    cos, sin = cos_ref[...], sin_ref[...]
    cq = jnp.dot(h, w_ref[:, :MLA_Q_RANK], preferred_element_type=F32)
    cq_n = (_rms(cq, MLA_Q_RANK) * gqa_ref[...]).astype(BF16)
    mq = jnp.dot(cq_n, wuq_ref[...], preferred_element_type=F32)
    q_scale = MLA_QK ** -0.5 * LOG2E
    for hd, y in enumerate(_mla_heads(mq, gq_ref[...], cos, sin, True)):
        mq_ref[:, hd * MLA_W:(hd + 1) * MLA_W] = (y * q_scale).astype(BF16)
    ckv = jnp.dot(h, w_ref[:, MLA_Q_RANK:MLA_Q_RANK + MLA_KV_RANK], preferred_element_type=F32)
    ckv_n = _rms(ckv, MLA_KV_RANK) * gkva_ref[...]
    ckv_ref[...] = ckv_n
    krw = jnp.dot(h, w_ref[:, MLA_Q_RANK + MLA_KV_RANK:], preferred_element_type=F32)
    kr_ref[...] = krw[:, MLA_NOPE:MLA_NOPE + MLA_ROPE]
    _mla_kv(ckv_n.astype(BF16), krw, wkn_ref, wv_ref, gk_ref[...], cos, sin, True, mk_ref, mv_ref)


def _cache_kv_kernel(ckv_ref, krw_ref, wkn_ref, wv_ref, gk_ref, mk_ref, mv_ref):
    _mla_kv(ckv_ref[...].astype(BF16), krw_ref[...], wkn_ref, wv_ref, gk_ref[...], None, None, False,
            mk_ref, mv_ref)


def _mod_spec(n_ctx_blocks, blocks_per_lat):
    def row(i):
        return jnp.where(i < n_ctx_blocks, 0, 1 + (i - n_ctx_blocks) // blocks_per_lat)
    return pl.BlockSpec((1, 8, D_MODEL), lambda i: (row(i), 0, 0))


def _rows(width):
    return pl.BlockSpec((ROW_BLOCK, width), lambda i: (i, 0))


def _proj_a(x, mod, w_a, gq, gk, cos, sin, mod_spec):
    t = x.shape[0]
    widths = (D_MODEL,) * 6 + (3 * D_MODEL,)
    dtypes = (BF16, BF16, F32, BF16, F32, BF16, BF16)
    return pl.pallas_call(
        _proj_a_kernel,
        grid=(t // ROW_BLOCK,),
        in_specs=[_rows(D_MODEL), mod_spec, _resident(w_a.shape), _resident(gq.shape), _resident(gk.shape),
                  _rows(HEAD_W), _rows(HEAD_W)],
        out_specs=[_rows(w) for w in widths],
        out_shape=[jax.ShapeDtypeStruct((t, w), dt) for w, dt in zip(widths, dtypes)],
        compiler_params=_cparams("arbitrary"),
        name="proj_a",
    )(x, mod, w_a, gq, gk, cos, sin)


def _proj_b(x, mod, w_b, gqa, gkva, wuq, wkn, wv, gq, gk, cos, sin, mod_spec):
    t = x.shape[0]
    widths = (MLA_HEADS * MLA_W, MLA_KV_RANK, MLA_ROPE, MLA_HEADS * MLA_W, D_MODEL)
    dtypes = (BF16, F32, F32, BF16, BF16)
    consts = (w_b, gqa, gkva, wuq, wkn, wv, gq, gk)
    return pl.pallas_call(
        _proj_b_kernel,
        grid=(t // ROW_BLOCK,),
        in_specs=[_rows(D_MODEL), mod_spec] + [_resident(a.shape) for a in consts] + [_rows(MLA_W), _rows(MLA_W)],
        out_specs=[_rows(w) for w in widths],
        out_shape=[jax.ShapeDtypeStruct((t, w), dt) for w, dt in zip(widths, dtypes)],
        compiler_params=_cparams("arbitrary"),
        name="proj_b",
    )(x, mod, *consts, cos, sin)


def _cache_kv(ckv, krw, wkn, wv, gk):
    n = ckv.shape[0]
    widths = (MLA_HEADS * MLA_W, D_MODEL)
    consts = (wkn, wv, gk)
    return pl.pallas_call(
        _cache_kv_kernel,
        grid=(n // ROW_BLOCK,),
        in_specs=[_rows(MLA_KV_RANK), _rows(MLA_W)] + [_resident(a.shape) for a in consts],
        out_specs=[_rows(w) for w in widths],
        out_shape=[jax.ShapeDtypeStruct((n, w), BF16) for w in widths],
        compiler_params=_cparams("arbitrary"),
        name="cache_kv",
    )(ckv, krw, *consts)


def _pool_kernel(prev_ref, cur_ref, next_ref, w_ref, scale_ref, o_ref, *, n_ctx_blocks, ctx_len, lat_len):
    i = pl.program_id(0)
    g = pl.program_id(1)
    rb = ROW_BLOCK
    is_ctx = i < n_ctx_blocks
    n_ctx_rows = n_ctx_blocks * rb
    seq_len = jnp.where(is_ctx, ctx_len, lat_len)
    seq_start = jnp.where(is_ctx, (i * rb // ctx_len) * ctx_len,
                          n_ctx_rows + ((i * rb - n_ctx_rows) // lat_len) * lat_len)
    window = 2 << g
    row_pos = i * rb - seq_start + lax.broadcasted_iota(jnp.int32, (rb, 3 * rb), 0)
    col_pos = (i - 1) * rb - seq_start + lax.broadcasted_iota(jnp.int32, (rb, 3 * rb), 1)
    lo = jnp.clip(row_pos - window // 2, 0, seq_len)
    hi = jnp.clip(row_pos - window // 2 + window, 0, seq_len)
    band = jnp.where((col_pos >= lo) & (col_pos < hi), 1.0, 0.0).astype(BF16)
    u = jnp.concatenate([prev_ref[...], cur_ref[...], next_ref[...]], axis=0)
    total = jnp.dot(band, u, preferred_element_type=F32)
    cnt = (hi[:, :1] - lo[:, :1]).astype(F32)
    pooled = total / cnt - cur_ref[...].astype(F32)
    mixed = jnp.dot(pooled.astype(BF16), w_ref[0], preferred_element_type=F32)
    o_ref[...] = (mixed * scale_ref[...]).astype(BF16)


def _pool(u_pool, pool_w, pool_scale, n_ctx_blocks, ctx_len, lat_len):
    t = u_pool.shape[0]
    nb = t // ROW_BLOCK
    blk = (ROW_BLOCK, POOL_GROUP)
    kern = functools.partial(_pool_kernel, n_ctx_blocks=n_ctx_blocks, ctx_len=ctx_len, lat_len=lat_len)
    return pl.pallas_call(
        kern,
        grid=(nb, N_POOL_GROUPS),
        in_specs=[
            pl.BlockSpec(blk, lambda i, g: (jnp.maximum(i - 1, 0), g)),
            pl.BlockSpec(blk, lambda i, g: (i, g)),
            pl.BlockSpec(blk, lambda i, g: (jnp.minimum(i + 1, nb - 1), g)),
            pl.BlockSpec((1, POOL_GROUP, POOL_GROUP), lambda i, g: (g, 0, 0)),
            pl.BlockSpec((1, POOL_GROUP), lambda i, g: (0, g)),
        ],
        out_specs=pl.BlockSpec(blk, lambda i, g: (i, g)),
        out_shape=jax.ShapeDtypeStruct((t, D_MODEL), BF16),
        compiler_params=_cparams("arbitrary", "arbitrary"),
        name="pool",
    )(u_pool, u_pool, u_pool, pool_w, pool_scale)


def _online_softmax_step(s, v, m, l, acc):
    m_new = jnp.maximum(m, jnp.max(s, axis=-1, keepdims=True))
    p = jnp.exp2(s - m_new)
    alpha = jnp.exp2(m - m_new)
    l_new = alpha * l + jnp.sum(p, axis=-1, keepdims=True)
    acc_new = alpha * acc + jnp.dot(p.astype(BF16), v, preferred_element_type=F32)
    return m_new, l_new, acc_new


def _softmax_init(tq, width):
    return (jnp.full((tq, 1), -jnp.inf, F32), jnp.zeros((tq, 1), F32), jnp.zeros((tq, width), F32))


def _key_chunks(tk, body, init):
    n_chunks = tk // ATTN_K_CHUNK if tk > ATTN_K_CHUNK else 1
    size = tk // n_chunks
    if n_chunks == 1:
        return body(0, size, init)
    return lax.fori_loop(0, n_chunks, lambda c, carry: body(pl.multiple_of(c * size, size), size, carry), init)


def _diff_attn_kernel(lam_ref, q_ref, k_ref, v_ref, g_ref, o_ref):
    q = q_ref[...]
    tq = q.shape[0]
    lane = _lane_iota(q.shape)
    q1 = jnp.where(lane < DIFF_HD, q, jnp.zeros_like(q))
    q2 = jnp.where(lane >= DIFF_HD, q, jnp.zeros_like(q))

    def body(start, size, carry):
        c1, c2 = carry
        k = k_ref[pl.ds(start, size), :]
        v = v_ref[pl.ds(start, size), :]
        s1 = lax.dot_general(q1, k, _NT, preferred_element_type=F32)
        s2 = lax.dot_general(q2, k, _NT, preferred_element_type=F32)
        return _online_softmax_step(s1, v, *c1), _online_softmax_step(s2, v, *c2)

    (_, l1, a1), (_, l2, a2) = _key_chunks(k_ref.shape[0], body, (_softmax_init(tq, HEAD_W),) * 2)
    o = a1 / l1 - lam_ref[0] * (a2 / l2)
    o_ref[...] = (_rms(o, HEAD_W) * g_ref[...]).astype(o_ref.dtype)


def _mla_attn_kernel(q_ref, k_ref, v_ref, o_ref):
    q = q_ref[...]

    def body(start, size, carry):
        k = k_ref[pl.ds(start, size), :]
        v = v_ref[pl.ds(start, size), :]
        s = lax.dot_general(q, k, _NT, preferred_element_type=F32)
        return _online_softmax_step(s, v, *carry)

    _, l, acc = _key_chunks(k_ref.shape[0], body, _softmax_init(q.shape[0], HEAD_W))
    o_ref[...] = (acc / l).astype(o_ref.dtype)


def _attention(kernel, q, k, v, extra, extra_specs, qk_width, name):
    b, tq_total, _ = q.shape
    tk = k.shape[1]
    n_heads = v.shape[2] // HEAD_W
    tq = min(ATTN_Q_BLOCK, tq_total)
    return pl.pallas_call(
        kernel,
        grid=(b, n_heads, tq_total // tq),
        in_specs=extra_specs[:1] + [
            pl.BlockSpec((None, tq, qk_width), lambda bi, h, qi: (bi, qi, h)),
            pl.BlockSpec((None, tk, qk_width), lambda bi, h, qi: (bi, 0, h)),
            pl.BlockSpec((None, tk, HEAD_W), lambda bi, h, qi: (bi, 0, h)),
        ] + extra_specs[1:],
        out_specs=pl.BlockSpec((None, tq, HEAD_W), lambda bi, h, qi: (bi, qi, h)),
        out_shape=jax.ShapeDtypeStruct((b, tq_total, n_heads * HEAD_W), BF16),
        compiler_params=_cparams("arbitrary", "arbitrary", "arbitrary"),
        name=name,
    )(*extra[:1], q, k, v, *extra[1:])


def _diff_attention(q, k, v, lam, out_gain, name):
    specs = [pl.BlockSpec(memory_space=pltpu.SMEM), pl.BlockSpec((1, HEAD_W), lambda bi, h, qi: (0, 0))]
    return _attention(_diff_attn_kernel, q, k, v, (lam, out_gain), specs, HEAD_W, name)


def _mla_attention(q, k, v, name):
    b, tq_total, _ = q.shape
    tk = k.shape[1]
    tq = min(ATTN_Q_BLOCK, tq_total)
    return pl.pallas_call(
        _mla_attn_kernel,
        grid=(b, MLA_HEADS, tq_total // tq),
        in_specs=[
            pl.BlockSpec((None, tq, MLA_W), lambda bi, h, qi: (bi, qi, h)),
            pl.BlockSpec((None, tk, MLA_W), lambda bi, h, qi: (bi, 0, h)),
            pl.BlockSpec((None, tk, HEAD_W), lambda bi, h, qi: (bi, 0, h)),
        ],
        out_specs=pl.BlockSpec((None, tq, HEAD_W), lambda bi, h, qi: (bi, qi, h)),
        out_shape=jax.ShapeDtypeStruct((b, tq_total, MLA_HEADS * HEAD_W), BF16),
        compiler_params=_cparams("arbitrary", "arbitrary", "arbitrary"),
        name=name,
    )(q, k, v)


def _merge_kernel(x_ref, mod_ref, yp_ref, yd_ref, ym_ref, gate_ref, wb_ref, wo_ref, rw_ref, rb_ref,
                  x1_ref, h2_ref, comb_ref):
    mod = mod_ref[0]
    merged = jnp.zeros(x_ref.shape, F32)
    for r, y_ref in enumerate((yp_ref, yd_ref, ym_ref)):
        z = jnp.dot(y_ref[...], wb_ref[r], preferred_element_type=F32)
        merged = merged + gate_ref[:, r * D_MODEL:(r + 1) * D_MODEL].astype(F32) * z
    y = jnp.dot(merged.astype(BF16), wo_ref[...], preferred_element_type=F32)
    x1 = x_ref[...] + mod[2:3] * y
    x1_ref[...] = x1
    h2 = _modulated_norm(x1, mod[3:4], mod[4:5])
    h2_ref[...] = h2.astype(BF16)
    logits = jnp.dot(h2, rw_ref[...], preferred_element_type=F32) + rb_ref[...]
    lane = _lane_iota(logits.shape)
    picks = []
    for _ in range(TOP_K):
        mx = jnp.max(logits, axis=-1, keepdims=True)
        first = jnp.min(jnp.where(logits == mx, lane, EXPERT_LANES), axis=-1, keepdims=True)
        hit = lane == first
        picks.append((mx, hit))
        logits = jnp.where(hit, -jnp.inf, logits)
    exps = [jnp.exp(v - picks[0][0]) for v, _ in picks]
    denom = exps[0] + exps[1] + exps[2] + exps[3]
    comb = jnp.zeros(logits.shape, F32)
    for e, (_, hit) in zip(exps, picks):
        comb = comb + jnp.where(hit, e / denom, 0.0)
    comb_ref[...] = comb


def _merge(x, mod, y_pool, y_diff, y_mla, gates, wb, wo, rw, rb, mod_spec):
    t = x.shape[0]
    consts = (wb, wo, rw, rb)
    return pl.pallas_call(
        _merge_kernel,
        grid=(t // ROW_BLOCK,),
        in_specs=[_rows(D_MODEL), mod_spec, _rows(D_MODEL), _rows(D_MODEL), _rows(D_MODEL), _rows(3 * D_MODEL)]
        + [_resident(a.shape) for a in consts],
        out_specs=[_rows(D_MODEL), _rows(D_MODEL), _rows(EXPERT_LANES)],
        out_shape=[jax.ShapeDtypeStruct((t, D_MODEL), F32), jax.ShapeDtypeStruct((t, D_MODEL), BF16),
                   jax.ShapeDtypeStruct((t, EXPERT_LANES), F32)],
        compiler_params=_cparams("arbitrary"),
        name="merge",
    )(x, mod, y_pool, y_diff, y_mla, gates, *consts)


def _moe_kernel(h_ref, comb_ref, x1_ref, mod_ref, wgu_ref, bgu_ref, wd_ref, bd_ref, o_ref, acc_ref):
    e = pl.program_id(1)

    @pl.when(e == 0)
    def _():
        acc_ref[...] = jnp.zeros_like(acc_ref)

    gu = jnp.dot(h_ref[...], wgu_ref[0], preferred_element_type=F32) + bgu_ref[0]
    gate = jnp.minimum(gu[:, :D_FF], SWIGLU_LIMIT)
    up = jnp.clip(gu[:, D_FF:], -SWIGLU_LIMIT, SWIGLU_LIMIT)
    act = (up + 1.0) * gate * _sigmoid(SWIGLU_ALPHA * gate)
    y = jnp.dot(act.astype(BF16), wd_ref[0], preferred_element_type=F32) + bd_ref[0]
    comb = comb_ref[...]
    weight = jnp.sum(jnp.where(_lane_iota(comb.shape) == e, comb, 0.0), axis=-1, keepdims=True)
    acc_ref[...] += weight * y

    @pl.when(e == N_EXPERTS - 1)
    def _():
        o_ref[...] = x1_ref[...] + mod_ref[0][5:6] * acc_ref[...]


def _moe(h2, comb, x1, mod, wgu, bgu, wd, bd, n_ctx_rows, lat_len):
    t = h2.shape[0]
    rb = MOE_ROW_BLOCK
    n_ctx_blocks = n_ctx_rows // rb

    def mod_row(i):
        return jnp.where(i < n_ctx_blocks, 0, 1 + (i - n_ctx_blocks) // (lat_len // rb))

    return pl.pallas_call(
        _moe_kernel,
        grid=(t // rb, N_EXPERTS),
        in_specs=[
            pl.BlockSpec((rb, D_MODEL), lambda i, e: (i, 0)),
            pl.BlockSpec((rb, EXPERT_LANES), lambda i, e: (i, 0)),
            pl.BlockSpec((rb, D_MODEL), lambda i, e: (i, 0)),
            pl.BlockSpec((1, 8, D_MODEL), lambda i, e: (mod_row(i), 0, 0)),
            pl.BlockSpec((1, D_MODEL, 2 * D_FF), lambda i, e: (e, 0, 0)),
            pl.BlockSpec((1, 1, 2 * D_FF), lambda i, e: (e, 0, 0)),
            pl.BlockSpec((1, D_FF, D_MODEL), lambda i, e: (e, 0, 0)),
            pl.BlockSpec((1, 1, D_MODEL), lambda i, e: (e, 0, 0)),
        ],
        out_specs=pl.BlockSpec((rb, D_MODEL), lambda i, e: (i, 0)),
        out_shape=jax.ShapeDtypeStruct((t, D_MODEL), F32),
        scratch_shapes=[pltpu.VMEM((rb, D_MODEL), F32)],
        compiler_params=_cparams("arbitrary", "arbitrary"),
        name="moe",
    )(h2, comb, x1, mod, wgu, bgu, wd, bd)


def _rope_tables(n_lat_tokens, rot_dim):
    n_rows = n_lat_tokens // GRID_W
    row = jnp.repeat(jnp.arange(n_rows, dtype=F32), GRID_W)
    col = jnp.tile(jnp.arange(GRID_W, dtype=F32), n_rows)
    n_freq = rot_dim // 4
    inv_freq = ROPE_THETA ** (-jnp.arange(n_freq, dtype=F32) / n_freq)
    ang = jnp.concatenate([row[:, None] * inv_freq, col[:, None] * inv_freq], axis=-1)
    return jnp.cos(ang), jnp.sin(ang)


def _token_tables(cos_lane, sin_lane, n_ctx_rows, n_lat_seqs):
    width = cos_lane.shape[1]
    cos = jnp.concatenate([jnp.ones((n_ctx_rows, width), F32)] + [cos_lane] * n_lat_seqs, axis=0)
    sin = jnp.concatenate([jnp.zeros((n_ctx_rows, width), F32)] + [sin_lane] * n_lat_seqs, axis=0)
    return cos, sin


def _pack_heads(w, n_heads, lo, hi, width):
    k = w.shape[0]
    per_head = w.shape[1] // n_heads
    part = w.reshape(k, n_heads, per_head)[:, :, lo:hi]
    part = jnp.pad(part, ((0, 0), (0, 0), (0, width - (hi - lo))))
    return part.reshape(k, n_heads * width)


def kernel(x_prompt, x_sample, c, cache_diff_k, cache_diff_v, cache_mla_ckv, cache_mla_krope, c_ctx, w_ada, b_ada,
           w_in, pool_w, pool_scale, diff_q_norm, diff_k_norm, diff_lambda, diff_out_norm, mla_q_a_norm, w_uq,
           mla_kv_a_norm, w_ukv, mla_q_norm, mla_k_norm, w_branch, w_out, router_w, router_b, moe_w_gu, moe_b_gu,
           moe_w_down, moe_b_down):
    batch, seq, d = x_prompt.shape
    dec_batch, dec_seq, _ = x_sample.shape
    depth = w_ada.shape[0]
    past = cache_diff_k.shape[2]
    n_ctx = batch * seq
    n_lat = dec_batch * dec_seq
    n_ctx_blocks = n_ctx // ROW_BLOCK
    mod_spec = _mod_spec(n_ctx_blocks, dec_seq // ROW_BLOCK)

    x = jnp.concatenate([x_prompt.reshape(n_ctx, d), x_sample.reshape(n_lat, d)], axis=0)
    cond8 = jnp.concatenate([c_ctx[None], c, jnp.zeros((8 - 1 - dec_batch, d), F32)], axis=0)
    mod_all = _ada(cond8, w_ada, b_ada).reshape(depth, 8, 6, d)[:, :1 + dec_batch]
    mod_all = jnp.pad(mod_all, ((0, 0), (0, 0), (0, 2), (0, 0)))

    cos_d, sin_d = _rope_tables(dec_seq, DIFF_HD)
    cos_m, sin_m = _rope_tables(dec_seq, MLA_ROPE)
    cos_a, sin_a = _token_tables(jnp.concatenate([cos_d] * 4, axis=1),
                                 jnp.concatenate([-sin_d, sin_d, -sin_d, sin_d], axis=1), n_ctx, dec_batch)
    one, zero = jnp.ones((dec_seq, MLA_NOPE), F32), jnp.zeros((dec_seq, MLA_NOPE), F32)
    cos_b, sin_b = _token_tables(jnp.concatenate([one, cos_m, cos_m, one[:, :64]], axis=1),
                                 jnp.concatenate([zero, -sin_m, sin_m, zero[:, :64]], axis=1), n_ctx, dec_batch)

    offs = np.cumsum((0, 1024, 1024, 1024, 1024, MLA_Q_RANK, MLA_KV_RANK, MLA_ROPE, 3 * 1024))
    new_dk, new_dv, new_ckv, new_kr = [], [], [], []
    for l in range(depth):
        lam_init = 0.8 - 0.6 * math.exp(-0.3 * l)
        wl = w_in[l]
        seg = [wl[:, offs[k]:offs[k + 1]] for k in range(8)]
        w_a = jnp.concatenate(seg[0:4] + [seg[7]], axis=1).astype(BF16)
        kr_wide = jnp.pad(seg[6], ((0, 0), (MLA_NOPE, MLA_W - MLA_NOPE - MLA_ROPE)))
        w_b = jnp.concatenate([seg[4], seg[5], kr_wide], axis=1).astype(BF16)
        wuq = _pack_heads(w_uq[l], MLA_HEADS, 0, MLA_QK, MLA_W).astype(BF16)
        wkn = _pack_heads(w_ukv[l], MLA_HEADS, 0, MLA_NOPE, MLA_W).astype(BF16)
        wv = _pack_heads(w_ukv[l], MLA_HEADS, MLA_NOPE, MLA_NOPE + HEAD_W, HEAD_W).astype(BF16)
        gq_d = jnp.tile(diff_q_norm[l], 2 * DIFF_HEADS)[None]
        gk_d = jnp.tile(diff_k_norm[l], 2 * DIFF_HEADS)[None]
        gq_m = jnp.tile(jnp.pad(mla_q_norm[l], (0, MLA_W - MLA_QK)), MLA_HEADS)[None]
        gk_m = jnp.tile(jnp.pad(mla_k_norm[l], (0, MLA_W - MLA_QK)), MLA_HEADS)[None]
        lp = diff_lambda[l]
        lam = (jnp.exp(jnp.sum(lp[0] * lp[1])) - jnp.exp(jnp.sum(lp[2] * lp[3])) + lam_init).reshape(1)
        out_gain = (diff_out_norm[l] * (1.0 - lam_init))[None]
        mod = mod_all[l]

        u_pool, dq, dk_f, dk_b, dv_f, dv_b, gates = _proj_a(x, mod, w_a, gq_d, gk_d, cos_a, sin_a, mod_spec)
        mq, ckv_n, kr, mk, mv = _proj_b(x, mod, w_b, mla_q_a_norm[l][None], mla_kv_a_norm[l][None], wuq, wkn, wv,
                                        gq_m, gk_m, cos_b, sin_b, mod_spec)
        new_dk.append(dk_f[:n_ctx].reshape(batch, seq, DIFF_HEADS, HEAD_W))
        new_dv.append(dv_f[:n_ctx].reshape(batch, seq, DIFF_HEADS, HEAD_W))
        new_ckv.append(ckv_n[:n_ctx].reshape(batch, seq, MLA_KV_RANK))
        new_kr.append(kr[:n_ctx].reshape(batch, seq, MLA_ROPE))

        y_pool = _pool(u_pool, pool_w[l].astype(BF16), pool_scale[l][None], n_ctx_blocks, seq, dec_seq)

        c_krw = jnp.pad(cache_mla_krope[:, l].reshape(dec_batch * past, MLA_ROPE),
                        ((0, 0), (MLA_NOPE, MLA_W - MLA_NOPE - MLA_ROPE)))
        c_mk, c_mv = _cache_kv(cache_mla_ckv[:, l].reshape(dec_batch * past, MLA_KV_RANK), c_krw, wkn, wv, gk_m)

        def ctx3(a):
            return a[:n_ctx].reshape(batch, seq, a.shape[1])

        def lat3(a, cached):
            new = a[n_ctx:].reshape(dec_batch, dec_seq, a.shape[1])
            return jnp.concatenate([cached.reshape(dec_batch, past, a.shape[1]).astype(a.dtype), new], axis=1)

        lat_q = lambda a: a[n_ctx:].reshape(dec_batch, dec_seq, a.shape[1])
        yd_ctx = _diff_attention(ctx3(dq), ctx3(dk_b), ctx3(dv_b), lam, out_gain, "diff_ctx")
        yd_lat = _diff_attention(lat_q(dq), lat3(dk_b, cache_diff_k[:, l]), lat3(dv_b, cache_diff_v[:, l]),
                                 lam, out_gain, "diff_lat")
        ym_ctx = _mla_attention(ctx3(mq), ctx3(mk), ctx3(mv), "mla_ctx")
        ym_lat = _mla_attention(lat_q(mq), lat3(mk, c_mk), lat3(mv, c_mv), "mla_lat")
        y_diff = jnp.concatenate([yd_ctx.reshape(n_ctx, d), yd_lat.reshape(n_lat, d)], axis=0)
        y_mla = jnp.concatenate([ym_ctx.reshape(n_ctx, d), ym_lat.reshape(n_lat, d)], axis=0)

        rw = jnp.pad(router_w[l], ((0, 0), (0, EXPERT_LANES - N_EXPERTS)))
        rb = jnp.pad(router_b[l], (0, EXPERT_LANES - N_EXPERTS), constant_values=NEG_BIG)[None]
        x1, h2, comb = _merge(x, mod, y_pool, y_diff, y_mla, gates, w_branch[l].astype(BF16), w_out[l].astype(BF16),
                              rw, rb, mod_spec)
        x = _moe(h2, comb, x1, mod, moe_w_gu[l].astype(BF16), moe_b_gu[l][:, None, :],
                 moe_w_down[l].astype(BF16), moe_b_down[l][:, None, :], n_ctx, dec_seq)

    return (x[:n_ctx].reshape(batch, seq, d), x[n_ctx:].reshape(dec_batch, dec_seq, d),
            jnp.stack(new_dk, axis=1), jnp.stack(new_dv, axis=1), jnp.stack(new_ckv, axis=1),
            jnp.stack(new_kr, axis=1))
```

```python
import functools
import math

import jax
import jax.numpy as jnp
import numpy as np
from jax import lax
from jax.experimental import pallas as pl
from jax.experimental.pallas import tpu as pltpu

F32 = jnp.float32
BF16 = jnp.bfloat16

D_MODEL = 1024
N_POOL_GROUPS = 4
POOL_GROUP = D_MODEL // N_POOL_GROUPS
DIFF_HD = 64
DIFF_HEADS = 8
HEAD_W = 128
MLA_HEADS = 8
MLA_NOPE = 128
MLA_ROPE = 64
MLA_QK = MLA_NOPE + MLA_ROPE
MLA_W = 256
MLA_Q_RANK = 768
MLA_KV_RANK = 512
N_EXPERTS = 32
TOP_K = 4
D_FF = 1024
EXPERT_LANES = 128
SWIGLU_ALPHA = 1.702
SWIGLU_LIMIT = 7.0
ROPE_THETA = 10000.0
NORM_EPS = 1e-6
GRID_W = 64
LOG2E = math.log2(math.e)
NEG_BIG = -1e30
VMEM_LIMIT = 56 * 1024 * 1024

ROW_BLOCK = 256
MOE_TILE = 256
DISPATCH_ROWS = 256
COMBINE_ROWS = 128
WAIT_UNROLL = 16
ATTN_Q_BLOCK = 512
ATTN_K_CHUNK = 512

_NT = (((1,), (1,)), ((), ()))


def _cparams(*sem):
    return pltpu.CompilerParams(dimension_semantics=sem, vmem_limit_bytes=VMEM_LIMIT)


def _resident(shape):
    nd = len(shape)
    return pl.BlockSpec(shape, lambda *_: (0,) * nd, pipeline_mode=pl.Buffered(1))


def _sigmoid(x):
    return 1.0 / (1.0 + jnp.exp(-x))


def _lane_iota(shape):
    return lax.broadcasted_iota(jnp.int32, shape, len(shape) - 1)


def _rms(x, width):
    ss = jnp.sum(x * x, axis=-1, keepdims=True)
    return x * lax.rsqrt(ss * (1.0 / width) + NORM_EPS)


def _rope(x, cos, sin):
    n = x.shape[-1]
    lane = _lane_iota(x.shape)
    partner = jnp.where((lane & 32) != 0, pltpu.roll(x, 32, 1), pltpu.roll(x, n - 32, 1))
    return x * cos + partner * sin


def _modulated_norm(x, shift, scale):
    return _rms(x, D_MODEL) * (1.0 + scale) + shift


def _ada_kernel(cond_ref, w_ref, b_ref, o_ref):
    cnd = cond_ref[...]
    act = cnd * _sigmoid(cnd)
    o_ref[0] = jnp.dot(act, w_ref[0], preferred_element_type=F32) + b_ref[0]


def _ada(cond8, w_ada, b_ada):
    depth = w_ada.shape[0]
    n_chunk = w_ada.shape[2] // D_MODEL
    return pl.pallas_call(
        _ada_kernel,
        grid=(depth, n_chunk),
        in_specs=[
            pl.BlockSpec((8, D_MODEL), lambda l, j: (0, 0)),
            pl.BlockSpec((1, D_MODEL, D_MODEL), lambda l, j: (l, 0, j)),
            pl.BlockSpec((1, 1, D_MODEL), lambda l, j: (l, 0, j)),
        ],
        out_specs=pl.BlockSpec((1, 8, D_MODEL), lambda l, j: (l, 0, j)),
        out_shape=jax.ShapeDtypeStruct((depth, 8, n_chunk * D_MODEL), F32),
        compiler_params=_cparams("arbitrary", "arbitrary"),
        name="ada",
    )(cond8, w_ada, b_ada.reshape(depth, 1, -1))


def _diff_qk_norm(u, gain, cos, sin):
    outs = []
    for h in range(DIFF_HEADS):
        c = u[:, h * HEAD_W:(h + 1) * HEAD_W]
        lane = _lane_iota(c.shape)
        sq = c * c
        lo = jnp.sum(jnp.where(lane < DIFF_HD, sq, 0.0), axis=-1, keepdims=True)
        hi = jnp.sum(jnp.where(lane >= DIFF_HD, sq, 0.0), axis=-1, keepdims=True)
        ms = jnp.where(lane < DIFF_HD, lo, hi) * (1.0 / DIFF_HD)
        y = c * lax.rsqrt(ms + NORM_EPS) * gain[:, h * HEAD_W:(h + 1) * HEAD_W]
        outs.append((y, _rope(y, cos, sin)))
    return outs


def _proj_a_kernel(x_ref, mod_ref, w_ref, gq_ref, gk_ref, cos_ref, sin_ref,
                   pool_ref, dq_ref, dkf_ref, dkb_ref, dvf_ref, dvb_ref, gate_ref):
    mod = mod_ref[0]
    h = _modulated_norm(x_ref[...], mod[0:1], mod[1:2]).astype(BF16)
    cos, sin = cos_ref[...], sin_ref[...]

    def seg(k):
        return jnp.dot(h, w_ref[:, k * D_MODEL:(k + 1) * D_MODEL], preferred_element_type=F32)

    pool_ref[...] = seg(0).astype(BF16)
    q_scale = DIFF_HD ** -0.5 * LOG2E
    for hd, (_, roped) in enumerate(_diff_qk_norm(seg(1), gq_ref[...], cos, sin)):
        dq_ref[:, hd * HEAD_W:(hd + 1) * HEAD_W] = (roped * q_scale).astype(BF16)
    for hd, (plain, roped) in enumerate(_diff_qk_norm(seg(2), gk_ref[...], cos, sin)):
        dkf_ref[:, hd * HEAD_W:(hd + 1) * HEAD_W] = plain
        dkb_ref[:, hd * HEAD_W:(hd + 1) * HEAD_W] = roped.astype(BF16)
    dv = seg(3)
    dvf_ref[...] = dv
    dvb_ref[...] = dv.astype(BF16)
    for r in range(3):
        gate_ref[:, r * D_MODEL:(r + 1) * D_MODEL] = _sigmoid(seg(4 + r)).astype(BF16)


def _mla_heads(u, gain, cos, sin, rope):
    outs = []
    for h in range(MLA_HEADS):
        c = u[:, h * MLA_W:(h + 1) * MLA_W]
        y = _rms(c, MLA_QK) * gain[:, h * MLA_W:(h + 1) * MLA_W]
        outs.append(_rope(y, cos, sin) if rope else y)
    return outs


def _mla_kv(ckv_n, krw, wkn_ref, wv_ref, gk, cos, sin, rope, k_ref, v_ref):
    kn = jnp.dot(ckv_n, wkn_ref[...], preferred_element_type=F32)
    kfull = jnp.concatenate([kn[:, h * MLA_W:(h + 1) * MLA_W] + krw for h in range(MLA_HEADS)], axis=-1)
    for hd, y in enumerate(_mla_heads(kfull, gk, cos, sin, rope)):
        k_ref[:, hd * MLA_W:(hd + 1) * MLA_W] = y.astype(BF16)
    v_ref[...] = jnp.dot(ckv_n, wv_ref[...], preferred_element_type=F32).astype(BF16)


def _proj_b_kernel(x_ref, mod_ref, w_ref, gqa_ref, gkva_ref, wuq_ref, wkn_ref, wv_ref, gq_ref, gk_ref,
                   cos_ref, sin_ref, mq_ref, ckv_ref, kr_ref, mk_ref, mv_ref):
    mod = mod_ref[0]
    h = _modulated_norm(x_ref[...], mod[0:1], mod[1:2]).astype(BF16)
    cos, sin = cos_ref[...], sin_ref[...]
    cq = jnp.dot(h, w_ref[:, :MLA_Q_RANK], preferred_element_type=F32)
    cq_n = (_rms(cq, MLA_Q_RANK) * gqa_ref[...]).astype(BF16)
    mq = jnp.dot(cq_n, wuq_ref[...], preferred_element_type=F32)
    q_scale = MLA_QK ** -0.5 * LOG2E
    for hd, y in enumerate(_mla_heads(mq, gq_ref[...], cos, sin, True)):
        mq_ref[:, hd * MLA_W:(hd + 1) * MLA_W] = (y * q_scale).astype(BF16)
    ckv = jnp.dot(h, w_ref[:, MLA_Q_RANK:MLA_Q_RANK + MLA_KV_RANK], preferred_element_type=F32)
    ckv_n = _rms(ckv, MLA_KV_RANK) * gkva_ref[...]
    ckv_ref[...] = ckv_n
    krw = jnp.dot(h, w_ref[:, MLA_Q_RANK + MLA_KV_RANK:], preferred_element_type=F32)
    kr_ref[...] = krw[:, MLA_NOPE:MLA_NOPE + MLA_ROPE]
    _mla_kv(ckv_n.astype(BF16), krw, wkn_ref, wv_ref, gk_ref[...], cos, sin, True, mk_ref, mv_ref)


def _cache_kv_kernel(ckv_ref, krw_ref, wkn_ref, wv_ref, gk_ref, mk_ref, mv_ref):
    _mla_kv(ckv_ref[...].astype(BF16), krw_ref[...], wkn_ref, wv_ref, gk_ref[...], None, None, False,
            mk_ref, mv_ref)


def _mod_spec(n_ctx_blocks, blocks_per_lat):
    def row(i):
        return jnp.where(i < n_ctx_blocks, 0, 1 + (i - n_ctx_blocks) // blocks_per_lat)
    return pl.BlockSpec((1, 8, D_MODEL), lambda i: (row(i), 0, 0))


def _rows(width):
    return pl.BlockSpec((ROW_BLOCK, width), lambda i: (i, 0))


def _proj_a(x, mod, w_a, gq, gk, cos, sin, mod_spec):
    t = x.shape[0]
    widths = (D_MODEL,) * 6 + (3 * D_MODEL,)
    dtypes = (BF16, BF16, F32, BF16, F32, BF16, BF16)
    return pl.pallas_call(
        _proj_a_kernel,
        grid=(t // ROW_BLOCK,),
        in_specs=[_rows(D_MODEL), mod_spec, _resident(w_a.shape), _resident(gq.shape), _resident(gk.shape),
                  _rows(HEAD_W), _rows(HEAD_W)],
        out_specs=[_rows(w) for w in widths],
        out_shape=[jax.ShapeDtypeStruct((t, w), dt) for w, dt in zip(widths, dtypes)],
        compiler_params=_cparams("arbitrary"),
        name="proj_a",
    )(x, mod, w_a, gq, gk, cos, sin)


def _proj_b(x, mod, w_b, gqa, gkva, wuq, wkn, wv, gq, gk, cos, sin, mod_spec):
    t = x.shape[0]
    widths = (MLA_HEADS * MLA_W, MLA_KV_RANK, MLA_ROPE, MLA_HEADS * MLA_W, D_MODEL)
    dtypes = (BF16, F32, F32, BF16, BF16)
    consts = (w_b, gqa, gkva, wuq, wkn, wv, gq, gk)
    return pl.pallas_call(
        _proj_b_kernel,
        grid=(t // ROW_BLOCK,),
        in_specs=[_rows(D_MODEL), mod_spec] + [_resident(a.shape) for a in consts] + [_rows(MLA_W), _rows(MLA_W)],
        out_specs=[_rows(w) for w in widths],
        out_shape=[jax.ShapeDtypeStruct((t, w), dt) for w, dt in zip(widths, dtypes)],
        compiler_params=_cparams("arbitrary"),
        name="proj_b",
    )(x, mod, *consts, cos, sin)


def _cache_kv(ckv, krw, wkn, wv, gk):
    n = ckv.shape[0]
    widths = (MLA_HEADS * MLA_W, D_MODEL)
    consts = (wkn, wv, gk)
    return pl.pallas_call(
        _cache_kv_kernel,
        grid=(n // ROW_BLOCK,),
        in_specs=[_rows(MLA_KV_RANK), _rows(MLA_W)] + [_resident(a.shape) for a in consts],
        out_specs=[_rows(w) for w in widths],
        out_shape=[jax.ShapeDtypeStruct((n, w), BF16) for w in widths],
        compiler_params=_cparams("arbitrary"),
        name="cache_kv",
    )(ckv, krw, *consts)


def _pool_kernel(prev_ref, cur_ref, next_ref, w_ref, scale_ref, o_ref, *, n_ctx_blocks, ctx_len, lat_len):
    i = pl.program_id(0)
    g = pl.program_id(1)
    rb = ROW_BLOCK
    is_ctx = i < n_ctx_blocks
    n_ctx_rows = n_ctx_blocks * rb
    seq_len = jnp.where(is_ctx, ctx_len, lat_len)
    seq_start = jnp.where(is_ctx, (i * rb // ctx_len) * ctx_len,
                          n_ctx_rows + ((i * rb - n_ctx_rows) // lat_len) * lat_len)
    window = 2 << g
    row_pos = i * rb - seq_start + lax.broadcasted_iota(jnp.int32, (rb, 3 * rb), 0)
    col_pos = (i - 1) * rb - seq_start + lax.broadcasted_iota(jnp.int32, (rb, 3 * rb), 1)
    lo = jnp.clip(row_pos - window // 2, 0, seq_len)
    hi = jnp.clip(row_pos - window // 2 + window, 0, seq_len)
    band = jnp.where((col_pos >= lo) & (col_pos < hi), 1.0, 0.0).astype(BF16)
    u = jnp.concatenate([prev_ref[...], cur_ref[...], next_ref[...]], axis=0)
    total = jnp.dot(band, u, preferred_element_type=F32)
    cnt = (hi[:, :1] - lo[:, :1]).astype(F32)
    pooled = total / cnt - cur_ref[...].astype(F32)
    mixed = jnp.dot(pooled.astype(BF16), w_ref[0], preferred_element_type=F32)
    o_ref[...] = (mixed * scale_ref[...]).astype(BF16)


def _pool(u_pool, pool_w, pool_scale, n_ctx_blocks, ctx_len, lat_len):
    t = u_pool.shape[0]
    nb = t // ROW_BLOCK
    blk = (ROW_BLOCK, POOL_GROUP)
    kern = functools.partial(_pool_kernel, n_ctx_blocks=n_ctx_blocks, ctx_len=ctx_len, lat_len=lat_len)
    return pl.pallas_call(
        kern,
        grid=(nb, N_POOL_GROUPS),
        in_specs=[
            pl.BlockSpec(blk, lambda i, g: (jnp.maximum(i - 1, 0), g)),
            pl.BlockSpec(blk, lambda i, g: (i, g)),
            pl.BlockSpec(blk, lambda i, g: (jnp.minimum(i + 1, nb - 1), g)),
            pl.BlockSpec((1, POOL_GROUP, POOL_GROUP), lambda i, g: (g, 0, 0)),
            pl.BlockSpec((1, POOL_GROUP), lambda i, g: (0, g)),
        ],
        out_specs=pl.BlockSpec(blk, lambda i, g: (i, g)),
        out_shape=jax.ShapeDtypeStruct((t, D_MODEL), BF16),
        compiler_params=_cparams("arbitrary", "arbitrary"),
        name="pool",
    )(u_pool, u_pool, u_pool, pool_w, pool_scale)


def _online_softmax_step(s, v, m, l, acc):
    m_new = jnp.maximum(m, jnp.max(s, axis=-1, keepdims=True))
    p = jnp.exp2(s - m_new)
    alpha = jnp.exp2(m - m_new)
    l_new = alpha * l + jnp.sum(p, axis=-1, keepdims=True)
    acc_new = alpha * acc + jnp.dot(p.astype(BF16), v, preferred_element_type=F32)
    return m_new, l_new, acc_new


def _softmax_init(tq, width):
    return (jnp.full((tq, 1), -jnp.inf, F32), jnp.zeros((tq, 1), F32), jnp.zeros((tq, width), F32))


def _key_chunks(tk, body, init):
    n_chunks = tk // ATTN_K_CHUNK if tk > ATTN_K_CHUNK else 1
    size = tk // n_chunks
    if n_chunks == 1:
        return body(0, size, init)
    return lax.fori_loop(0, n_chunks, lambda c, carry: body(pl.multiple_of(c * size, size), size, carry), init)


def _diff_attn_kernel(lam_ref, q_ref, k_ref, v_ref, g_ref, o_ref):
    q = q_ref[...]
    tq = q.shape[0]
    lane = _lane_iota(q.shape)
    q1 = jnp.where(lane < DIFF_HD, q, jnp.zeros_like(q))
    q2 = jnp.where(lane >= DIFF_HD, q, jnp.zeros_like(q))

    def body(start, size, carry):
        c1, c2 = carry
        k = k_ref[pl.ds(start, size), :]
        v = v_ref[pl.ds(start, size), :]
        s1 = lax.dot_general(q1, k, _NT, preferred_element_type=F32)
        s2 = lax.dot_general(q2, k, _NT, preferred_element_type=F32)
        return _online_softmax_step(s1, v, *c1), _online_softmax_step(s2, v, *c2)

    (_, l1, a1), (_, l2, a2) = _key_chunks(k_ref.shape[0], body, (_softmax_init(tq, HEAD_W),) * 2)
    o = a1 / l1 - lam_ref[0] * (a2 / l2)
    o_ref[...] = (_rms(o, HEAD_W) * g_ref[...]).astype(o_ref.dtype)


def _mla_attn_kernel(q_ref, k_ref, v_ref, o_ref):
    q = q_ref[...]

    def body(start, size, carry):
        k = k_ref[pl.ds(start, size), :]
        v = v_ref[pl.ds(start, size), :]
        s = lax.dot_general(q, k, _NT, preferred_element_type=F32)
        return _online_softmax_step(s, v, *carry)

    _, l, acc = _key_chunks(k_ref.shape[0], body, _softmax_init(q.shape[0], HEAD_W))
    o_ref[...] = (acc / l).astype(o_ref.dtype)


def _attention(kernel, q, k, v, extra, extra_specs, qk_width, name):
    b, tq_total, _ = q.shape
    tk = k.shape[1]
    n_heads = v.shape[2] // HEAD_W
    tq = min(ATTN_Q_BLOCK, tq_total)
    return pl.pallas_call(
        kernel,
        grid=(b, n_heads, tq_total // tq),
        in_specs=extra_specs[:1] + [
            pl.BlockSpec((None, tq, qk_width), lambda bi, h, qi: (bi, qi, h)),
            pl.BlockSpec((None, tk, qk_width), lambda bi, h, qi: (bi, 0, h)),
            pl.BlockSpec((None, tk, HEAD_W), lambda bi, h, qi: (bi, 0, h)),
        ] + extra_specs[1:],
        out_specs=pl.BlockSpec((None, tq, HEAD_W), lambda bi, h, qi: (bi, qi, h)),
        out_shape=jax.ShapeDtypeStruct((b, tq_total, n_heads * HEAD_W), BF16),
        compiler_params=_cparams("arbitrary", "arbitrary", "arbitrary"),
        name=name,
    )(*extra[:1], q, k, v, *extra[1:])


def _diff_attention(q, k, v, lam, out_gain, name):
    specs = [pl.BlockSpec(memory_space=pltpu.SMEM), pl.BlockSpec((1, HEAD_W), lambda bi, h, qi: (0, 0))]
    return _attention(_diff_attn_kernel, q, k, v, (lam, out_gain), specs, HEAD_W, name)


def _mla_attention(q, k, v, name):
    b, tq_total, _ = q.shape
    tk = k.shape[1]
    tq = min(ATTN_Q_BLOCK, tq_total)
    return pl.pallas_call(
        _mla_attn_kernel,
        grid=(b, MLA_HEADS, tq_total // tq),
        in_specs=[
            pl.BlockSpec((None, tq, MLA_W), lambda bi, h, qi: (bi, qi, h)),
            pl.BlockSpec((None, tk, MLA_W), lambda bi, h, qi: (bi, 0, h)),
            pl.BlockSpec((None, tk, HEAD_W), lambda bi, h, qi: (bi, 0, h)),
        ],
        out_specs=pl.BlockSpec((None, tq, HEAD_W), lambda bi, h, qi: (bi, qi, h)),
        out_shape=jax.ShapeDtypeStruct((b, tq_total, MLA_HEADS * HEAD_W), BF16),
        compiler_params=_cparams("arbitrary", "arbitrary", "arbitrary"),
        name=name,
    )(q, k, v)


def _merge_kernel(x_ref, mod_ref, yp_ref, yd_ref, ym_ref, gate_ref, wb_ref, wo_ref, rw_ref, rb_ref,
                  x1_ref, h2_ref, topw_ref, topi_ref, rank_ref, count_ref, carry_ref):
    @pl.when(pl.program_id(0) == 0)
    def _():
        carry_ref[...] = jnp.zeros_like(carry_ref)

    mod = mod_ref[0]
    merged = jnp.zeros(x_ref.shape, F32)
    for r, y_ref in enumerate((yp_ref, yd_ref, ym_ref)):
        z = jnp.dot(y_ref[...], wb_ref[r], preferred_element_type=F32)
        merged = merged + gate_ref[:, r * D_MODEL:(r + 1) * D_MODEL].astype(F32) * z
    y = jnp.dot(merged.astype(BF16), wo_ref[...], preferred_element_type=F32)
    x1 = x_ref[...] + mod[2:3] * y
    x1_ref[...] = x1
    h2 = _modulated_norm(x1, mod[3:4], mod[4:5])
    h2_ref[...] = h2
    logits = jnp.dot(h2, rw_ref[...], preferred_element_type=F32) + rb_ref[...]
    lane = _lane_iota(logits.shape)
    picks = []
    for _ in range(TOP_K):
        mx = jnp.max(logits, axis=-1, keepdims=True)
        first = jnp.min(jnp.where(logits == mx, lane, EXPERT_LANES), axis=-1, keepdims=True)
        hit = lane == first
        picks.append((mx, first, hit))
        logits = jnp.where(hit, -jnp.inf, logits)
    exps = [jnp.exp(v - picks[0][0]) for v, _, _ in picks]
    denom = exps[0] + exps[1] + exps[2] + exps[3]
    rows = logits.shape[0]
    chosen = jnp.zeros(logits.shape, F32)
    for _, _, hit in picks:
        chosen = chosen + jnp.where(hit, 1.0, 0.0)
    earlier = jnp.where(lax.broadcasted_iota(jnp.int32, (rows, rows), 1)
                        < lax.broadcasted_iota(jnp.int32, (rows, rows), 0), 1.0, 0.0).astype(BF16)
    before = jnp.dot(earlier, chosen.astype(BF16), preferred_element_type=F32) + carry_ref[0:1, :]
    topw = jnp.zeros(logits.shape, F32)
    topi = jnp.zeros(logits.shape, jnp.int32)
    rank = jnp.zeros(logits.shape, jnp.int32)
    for k, (e, (_, first, hit)) in enumerate(zip(exps, picks)):
        slot_rank = jnp.sum(jnp.where(hit, before, 0.0), axis=-1, keepdims=True)
        topw = jnp.where(lane == k, e / denom, topw)
        topi = jnp.where(lane == k, first, topi)
        rank = jnp.where(lane == k, slot_rank.astype(jnp.int32), rank)
    topw_ref[...] = topw
    topi_ref[...] = topi
    rank_ref[...] = rank
    carry_ref[...] = carry_ref[...] + jnp.sum(chosen, axis=0, keepdims=True)
    count_ref[...] = carry_ref[...]


def _merge(x, mod, y_pool, y_diff, y_mla, gates, wb, wo, rw, rb, mod_spec):
    t = x.shape[0]
    consts = (wb, wo, rw, rb)
    lanes = EXPERT_LANES
    return pl.pallas_call(
        _merge_kernel,
        grid=(t // ROW_BLOCK,),
        in_specs=[_rows(D_MODEL), mod_spec, _rows(D_MODEL), _rows(D_MODEL), _rows(D_MODEL), _rows(3 * D_MODEL)]
        + [_resident(a.shape) for a in consts],
        out_specs=[_rows(D_MODEL), _rows(D_MODEL), _rows(lanes), _rows(lanes), _rows(lanes),
                   pl.BlockSpec((8, lanes), lambda i: (0, 0))],
        out_shape=[jax.ShapeDtypeStruct((t, D_MODEL), F32), jax.ShapeDtypeStruct((t, D_MODEL), F32),
                   jax.ShapeDtypeStruct((t, lanes), F32), jax.ShapeDtypeStruct((t, lanes), jnp.int32),
                   jax.ShapeDtypeStruct((t, lanes), jnp.int32), jax.ShapeDtypeStruct((8, lanes), F32)],
        scratch_shapes=[pltpu.VMEM((8, lanes), F32)],
        compiler_params=_cparams("arbitrary"),
        name="merge",
    )(x, mod, y_pool, y_diff, y_mla, gates, *consts)


def _slot_layout(counts, topi, rank):
    tile = MOE_TILE
    cnt = counts[0, :N_EXPERTS].astype(jnp.int32)
    padded = (cnt + tile - 1) // tile * tile
    end = jnp.cumsum(padded)
    start = end - padded
    experts = jnp.arange(N_EXPERTS, dtype=jnp.int32)
    idx4 = topi[:, :TOP_K]
    slot_start = jnp.sum(jnp.where(idx4[:, :, None] == experts, start, 0), axis=-1)
    pos = (slot_start + rank[:, :TOP_K]).reshape(-1)
    n_tiles = (topi.shape[0] * TOP_K + N_EXPERTS * tile) // tile
    tile_row = jnp.arange(n_tiles, dtype=jnp.int32) * tile
    tile_expert = jnp.minimum(jnp.sum(tile_row[:, None] >= end, axis=-1), N_EXPERTS - 1).astype(jnp.int32)
    n_used = (end[-1] // tile).astype(jnp.int32)
    tile_src = jnp.minimum(jnp.arange(n_tiles, dtype=jnp.int32), n_used - 1)
    ragged_last = jnp.any((cnt % tile != 0) & (tile_row[:, None] == end - tile), axis=-1)
    fill_tile = (ragged_last | (tile_row >= end[-1])).astype(jnp.int32)
    return pos, tile_expert, tile_src, n_used.reshape(1), fill_tile


def _row_copy(src, src_row, dst, dst_row, sem):
    return pltpu.make_async_copy(src.at[pl.ds(src_row, 1)], dst.at[pl.ds(dst_row, 1)], sem)


def _dispatch_kernel(fill_tile_ref, pos_ref, h_ref, hs_ref, zero_ref, fill_sem, row_sem):
    i = pl.program_id(0)
    slots = DISPATCH_ROWS * TOP_K

    @pl.when(i == 0)
    def _():
        zero_ref[...] = jnp.zeros_like(zero_ref)

        def fill_copy(tile):
            row = pl.multiple_of(tile * MOE_TILE, MOE_TILE)
            return pltpu.make_async_copy(zero_ref, hs_ref.at[pl.ds(row, MOE_TILE)], fill_sem)

        def start_fill(tile, carry):
            @pl.when(fill_tile_ref[tile] != 0)
            def _():
                fill_copy(tile).start()
            return carry

        def wait_fill(tile, carry):
            @pl.when(fill_tile_ref[tile] != 0)
            def _():
                fill_copy(tile).wait()
            return carry

        lax.fori_loop(0, hs_ref.shape[0] // MOE_TILE, start_fill, 0)
        lax.fori_loop(0, hs_ref.shape[0] // MOE_TILE, wait_fill, 0)

    def issue(r, carry):
        token = i * DISPATCH_ROWS + r
        for k in range(TOP_K):
            _row_copy(h_ref, token, hs_ref, pos_ref[token * TOP_K + k], row_sem).start()
        return carry

    def drain(j, carry):
        for _ in range(WAIT_UNROLL):
            _row_copy(h_ref, 0, hs_ref, 0, row_sem).wait()
        return carry

    lax.fori_loop(0, DISPATCH_ROWS, issue, 0, unroll=2)
    lax.fori_loop(0, slots // WAIT_UNROLL, drain, 0)


def _dispatch(h2, pos, fill_tile):
    t = h2.shape[0]
    n_rows = fill_tile.shape[0] * MOE_TILE
    return pl.pallas_call(
        _dispatch_kernel,
        grid_spec=pltpu.PrefetchScalarGridSpec(
            num_scalar_prefetch=2,
            grid=(t // DISPATCH_ROWS,),
            in_specs=[pl.BlockSpec(memory_space=pl.ANY)],
            out_specs=pl.BlockSpec(memory_space=pl.ANY),
            scratch_shapes=[pltpu.VMEM((MOE_TILE, D_MODEL), F32), pltpu.SemaphoreType.DMA,
                            pltpu.SemaphoreType.DMA],
        ),
        out_shape=jax.ShapeDtypeStruct((n_rows, D_MODEL), F32),
        compiler_params=_cparams("arbitrary"),
        name="dispatch",
    )(fill_tile, pos, h2)


def _expert_kernel(tile_expert_ref, tile_src_ref, n_used_ref, hs_ref, wgu_ref, bgu_ref, wd_ref, bd_ref, ys_ref):
    occupied = pl.program_id(0) < n_used_ref[0]

    @pl.when(occupied)
    def _():
        gu = jnp.dot(hs_ref[...].astype(BF16), wgu_ref[0], preferred_element_type=F32) + bgu_ref[0]
        gate = jnp.minimum(gu[:, :D_FF], SWIGLU_LIMIT)
        up = jnp.clip(gu[:, D_FF:], -SWIGLU_LIMIT, SWIGLU_LIMIT)
        act = (up + 1.0) * gate * _sigmoid(SWIGLU_ALPHA * gate)
        ys_ref[...] = jnp.dot(act.astype(BF16), wd_ref[0], preferred_element_type=F32) + bd_ref[0]

    @pl.when(jnp.logical_not(occupied))
    def _():
        ys_ref[...] = jnp.zeros_like(ys_ref)


def _experts(hs, tile_expert, tile_src, n_used, wgu, bgu, wd, bd):
    n_rows = hs.shape[0]
    rows = lambda i, te, ts, nu: (ts[i], 0)
    weights = lambda i, te, ts, nu: (te[i], 0, 0)
    return pl.pallas_call(
        _expert_kernel,
        grid_spec=pltpu.PrefetchScalarGridSpec(
            num_scalar_prefetch=3,
            grid=(n_rows // MOE_TILE,),
            in_specs=[
                pl.BlockSpec((MOE_TILE, D_MODEL), rows),
                pl.BlockSpec((1, D_MODEL, 2 * D_FF), weights),
                pl.BlockSpec((1, 1, 2 * D_FF), weights),
                pl.BlockSpec((1, D_FF, D_MODEL), weights),
                pl.BlockSpec((1, 1, D_MODEL), weights),
            ],
            out_specs=pl.BlockSpec((MOE_TILE, D_MODEL), lambda i, te, ts, nu: (i, 0)),
        ),
        out_shape=jax.ShapeDtypeStruct((n_rows, D_MODEL), F32),
        compiler_params=_cparams("arbitrary"),
        name="experts",
    )(tile_expert, tile_src, n_used, hs, wgu, bgu, wd, bd)


def _combine_kernel(pos_ref, ys_ref, w_ref, x1_ref, mod_ref, o_ref, buf_ref, sem):
    i = pl.program_id(0)
    slots = COMBINE_ROWS * TOP_K

    def issue(r, carry):
        token = i * COMBINE_ROWS + r
        for k in range(TOP_K):
            _row_copy(ys_ref, pos_ref[token * TOP_K + k], buf_ref.at[k], r, sem).start()
        return carry

    def drain(j, carry):
        for _ in range(WAIT_UNROLL):
            _row_copy(ys_ref, 0, buf_ref.at[0], 0, sem).wait()
        return carry

    lax.fori_loop(0, COMBINE_ROWS, issue, 0, unroll=2)
    lax.fori_loop(0, slots // WAIT_UNROLL, drain, 0)
    w = w_ref[...]
    acc = w[:, 0:1] * buf_ref[0]
    for k in range(1, TOP_K):
        acc = acc + w[:, k:k + 1] * buf_ref[k]
    o_ref[...] = x1_ref[...] + mod_ref[0][5:6] * acc


def _combine(ys, pos, topw, x1, mod, n_ctx_rows, lat_len):
    t = x1.shape[0]
    rb = COMBINE_ROWS
    n_ctx_blocks = n_ctx_rows // rb

    def mod_row(i, pos_ref):
        return (jnp.where(i < n_ctx_blocks, 0, 1 + (i - n_ctx_blocks) // (lat_len // rb)), 0, 0)

    return pl.pallas_call(
        _combine_kernel,
        grid_spec=pltpu.PrefetchScalarGridSpec(
            num_scalar_prefetch=1,
            grid=(t // rb,),
            in_specs=[
                pl.BlockSpec(memory_space=pl.ANY),
                pl.BlockSpec((rb, EXPERT_LANES), lambda i, pos_ref: (i, 0)),
                pl.BlockSpec((rb, D_MODEL), lambda i, pos_ref: (i, 0)),
                pl.BlockSpec((1, 8, D_MODEL), mod_row),
            ],
            out_specs=pl.BlockSpec((rb, D_MODEL), lambda i, pos_ref: (i, 0)),
            scratch_shapes=[pltpu.VMEM((TOP_K, rb, D_MODEL), F32), pltpu.SemaphoreType.DMA],
        ),
        out_shape=jax.ShapeDtypeStruct((t, D_MODEL), F32),
        compiler_params=_cparams("arbitrary"),
        name="combine",
    )(pos, ys, topw, x1, mod)


def _rope_tables(n_lat_tokens, rot_dim):
    n_rows = n_lat_tokens // GRID_W
    row = jnp.repeat(jnp.arange(n_rows, dtype=F32), GRID_W)
    col = jnp.tile(jnp.arange(GRID_W, dtype=F32), n_rows)
    n_freq = rot_dim // 4
    inv_freq = ROPE_THETA ** (-jnp.arange(n_freq, dtype=F32) / n_freq)
    ang = jnp.concatenate([row[:, None] * inv_freq, col[:, None] * inv_freq], axis=-1)
    return jnp.cos(ang), jnp.sin(ang)


def _token_tables(cos_lane, sin_lane, n_ctx_rows, n_lat_seqs):
    width = cos_lane.shape[1]
    cos = jnp.concatenate([jnp.ones((n_ctx_rows, width), F32)] + [cos_lane] * n_lat_seqs, axis=0)
    sin = jnp.concatenate([jnp.zeros((n_ctx_rows, width), F32)] + [sin_lane] * n_lat_seqs, axis=0)
    return cos, sin


def _pack_heads(w, n_heads, lo, hi, width):
    k = w.shape[0]
    per_head = w.shape[1] // n_heads
    part = w.reshape(k, n_heads, per_head)[:, :, lo:hi]
    part = jnp.pad(part, ((0, 0), (0, 0), (0, width - (hi - lo))))
    return part.reshape(k, n_heads * width)


def kernel(x_prompt, x_sample, c, cache_diff_k, cache_diff_v, cache_mla_ckv, cache_mla_krope, c_ctx, w_ada, b_ada,
           w_in, pool_w, pool_scale, diff_q_norm, diff_k_norm, diff_lambda, diff_out_norm, mla_q_a_norm, w_uq,
           mla_kv_a_norm, w_ukv, mla_q_norm, mla_k_norm, w_branch, w_out, router_w, router_b, moe_w_gu, moe_b_gu,
           moe_w_down, moe_b_down):
    batch, seq, d = x_prompt.shape
    dec_batch, dec_seq, _ = x_sample.shape
    depth = w_ada.shape[0]
    past = cache_diff_k.shape[2]
    n_ctx = batch * seq
    n_lat = dec_batch * dec_seq
    n_ctx_blocks = n_ctx // ROW_BLOCK
    mod_spec = _mod_spec(n_ctx_blocks, dec_seq // ROW_BLOCK)

    x = jnp.concatenate([x_prompt.reshape(n_ctx, d), x_sample.reshape(n_lat, d)], axis=0)
    cond8 = jnp.concatenate([c_ctx[None], c, jnp.zeros((8 - 1 - dec_batch, d), F32)], axis=0)
    mod_all = _ada(cond8, w_ada, b_ada).reshape(depth, 8, 6, d)[:, :1 + dec_batch]
    mod_all = jnp.pad(mod_all, ((0, 0), (0, 0), (0, 2), (0, 0)))

    cos_d, sin_d = _rope_tables(dec_seq, DIFF_HD)
    cos_m, sin_m = _rope_tables(dec_seq, MLA_ROPE)
    cos_a, sin_a = _token_tables(jnp.concatenate([cos_d] * 4, axis=1),
                                 jnp.concatenate([-sin_d, sin_d, -sin_d, sin_d], axis=1), n_ctx, dec_batch)
    one, zero = jnp.ones((dec_seq, MLA_NOPE), F32), jnp.zeros((dec_seq, MLA_NOPE), F32)
    cos_b, sin_b = _token_tables(jnp.concatenate([one, cos_m, cos_m, one[:, :64]], axis=1),
                                 jnp.concatenate([zero, -sin_m, sin_m, zero[:, :64]], axis=1), n_ctx, dec_batch)

    offs = np.cumsum((0, 1024, 1024, 1024, 1024, MLA_Q_RANK, MLA_KV_RANK, MLA_ROPE, 3 * 1024))
    new_dk, new_dv, new_ckv, new_kr = [], [], [], []
    for l in range(depth):
        lam_init = 0.8 - 0.6 * math.exp(-0.3 * l)
        wl = w_in[l]
        seg = [wl[:, offs[k]:offs[k + 1]] for k in range(8)]
        w_a = jnp.concatenate(seg[0:4] + [seg[7]], axis=1).astype(BF16)
        kr_wide = jnp.pad(seg[6], ((0, 0), (MLA_NOPE, MLA_W - MLA_NOPE - MLA_ROPE)))
        w_b = jnp.concatenate([seg[4], seg[5], kr_wide], axis=1).astype(BF16)
        wuq = _pack_heads(w_uq[l], MLA_HEADS, 0, MLA_QK, MLA_W).astype(BF16)
        wkn = _pack_heads(w_ukv[l], MLA_HEADS, 0, MLA_NOPE, MLA_W).astype(BF16)
        wv = _pack_heads(w_ukv[l], MLA_HEADS, MLA_NOPE, MLA_NOPE + HEAD_W, HEAD_W).astype(BF16)
        gq_d = jnp.tile(diff_q_norm[l], 2 * DIFF_HEADS)[None]
        gk_d = jnp.tile(diff_k_norm[l], 2 * DIFF_HEADS)[None]
        gq_m = jnp.tile(jnp.pad(mla_q_norm[l], (0, MLA_W - MLA_QK)), MLA_HEADS)[None]
        gk_m = jnp.tile(jnp.pad(mla_k_norm[l], (0, MLA_W - MLA_QK)), MLA_HEADS)[None]
        lp = diff_lambda[l]
        lam = (jnp.exp(jnp.sum(lp[0] * lp[1])) - jnp.exp(jnp.sum(lp[2] * lp[3])) + lam_init).reshape(1)
        out_gain = (diff_out_norm[l] * (1.0 - lam_init))[None]
        mod = mod_all[l]

        u_pool, dq, dk_f, dk_b, dv_f, dv_b, gates = _proj_a(x, mod, w_a, gq_d, gk_d, cos_a, sin_a, mod_spec)
        mq, ckv_n, kr, mk, mv = _proj_b(x, mod, w_b, mla_q_a_norm[l][None], mla_kv_a_norm[l][None], wuq, wkn, wv,
                                        gq_m, gk_m, cos_b, sin_b, mod_spec)
        new_dk.append(dk_f[:n_ctx].reshape(batch, seq, DIFF_HEADS, HEAD_W))
        new_dv.append(dv_f[:n_ctx].reshape(batch, seq, DIFF_HEADS, HEAD_W))
        new_ckv.append(ckv_n[:n_ctx].reshape(batch, seq, MLA_KV_RANK))
        new_kr.append(kr[:n_ctx].reshape(batch, seq, MLA_ROPE))

        y_pool = _pool(u_pool, pool_w[l].astype(BF16), pool_scale[l][None], n_ctx_blocks, seq, dec_seq)

        c_krw = jnp.pad(cache_mla_krope[:, l].reshape(dec_batch * past, MLA_ROPE),
                        ((0, 0), (MLA_NOPE, MLA_W - MLA_NOPE - MLA_ROPE)))
        c_mk, c_mv = _cache_kv(cache_mla_ckv[:, l].reshape(dec_batch * past, MLA_KV_RANK), c_krw, wkn, wv, gk_m)

        def ctx3(a):
            return a[:n_ctx].reshape(batch, seq, a.shape[1])

        def lat3(a, cached):
            new = a[n_ctx:].reshape(dec_batch, dec_seq, a.shape[1])
            return jnp.concatenate([cached.reshape(dec_batch, past, a.shape[1]).astype(a.dtype), new], axis=1)

        lat_q = lambda a: a[n_ctx:].reshape(dec_batch, dec_seq, a.shape[1])
        yd_ctx = _diff_attention(ctx3(dq), ctx3(dk_b), ctx3(dv_b), lam, out_gain, "diff_ctx")
        yd_lat = _diff_attention(lat_q(dq), lat3(dk_b, cache_diff_k[:, l]), lat3(dv_b, cache_diff_v[:, l]),
                                 lam, out_gain, "diff_lat")
        ym_ctx = _mla_attention(ctx3(mq), ctx3(mk), ctx3(mv), "mla_ctx")
        ym_lat = _mla_attention(lat_q(mq), lat3(mk, c_mk), lat3(mv, c_mv), "mla_lat")
        y_diff = jnp.concatenate([yd_ctx.reshape(n_ctx, d), yd_lat.reshape(n_lat, d)], axis=0)
        y_mla = jnp.concatenate([ym_ctx.reshape(n_ctx, d), ym_lat.reshape(n_lat, d)], axis=0)

        rw = jnp.pad(router_w[l], ((0, 0), (0, EXPERT_LANES - N_EXPERTS)))
        rb = jnp.pad(router_b[l], (0, EXPERT_LANES - N_EXPERTS), constant_values=NEG_BIG)[None]
        x1, h2, topw, topi, rank, counts = _merge(x, mod, y_pool, y_diff, y_mla, gates, w_branch[l].astype(BF16),
                                                  w_out[l].astype(BF16), rw, rb, mod_spec)
        pos, tile_expert, tile_src, n_used, fill_tile = _slot_layout(counts, topi, rank)
        hs = _dispatch(h2, pos, fill_tile)
        ys = _experts(hs, tile_expert, tile_src, n_used, moe_w_gu[l].astype(BF16), moe_b_gu[l][:, None, :],
                      moe_w_down[l].astype(BF16), moe_b_down[l][:, None, :])
        x = _combine(ys, pos, topw, x1, mod, n_ctx, dec_seq)

    return (x[:n_ctx].reshape(batch, seq, d), x[n_ctx:].reshape(dec_batch, dec_seq, d),
            jnp.stack(new_dk, axis=1), jnp.stack(new_dv, axis=1), jnp.stack(new_ckv, axis=1),
            jnp.stack(new_kr, axis=1))
```

```python
import functools
import math

import jax
import jax.numpy as jnp
import numpy as np
from jax import lax
from jax.experimental import pallas as pl
from jax.experimental.pallas import tpu as pltpu

F32 = jnp.float32
BF16 = jnp.bfloat16

D_MODEL = 1024
N_POOL_GROUPS = 4
POOL_GROUP = D_MODEL // N_POOL_GROUPS
DIFF_HD = 64
DIFF_HEADS = 8
HEAD_W = 128
MLA_HEADS = 8
MLA_NOPE = 128
MLA_ROPE = 64
MLA_QK = MLA_NOPE + MLA_ROPE
MLA_W = 256
MLA_Q_RANK = 768
MLA_KV_RANK = 512
N_EXPERTS = 32
TOP_K = 4
D_FF = 1024
EXPERT_LANES = 128
SWIGLU_ALPHA = 1.702
SWIGLU_LIMIT = 7.0
ROPE_THETA = 10000.0
NORM_EPS = 1e-6
GRID_W = 64
LOG2E = math.log2(math.e)
NEG_BIG = -1e30
VMEM_LIMIT = 56 * 1024 * 1024

ROW_BLOCK = 256
MOE_TILE = 256
DISPATCH_ROWS = 256
COMBINE_ROWS = 128
WAIT_UNROLL = 16
ATTN_Q_BLOCK = 512
ATTN_K_CHUNK = 512

_NT = (((1,), (1,)), ((), ()))


def _cparams(*sem):
    return pltpu.CompilerParams(dimension_semantics=sem, vmem_limit_bytes=VMEM_LIMIT)


def _resident(shape):
    nd = len(shape)
    return pl.BlockSpec(shape, lambda *_: (0,) * nd, pipeline_mode=pl.Buffered(1))


def _sigmoid(x):
    return 1.0 / (1.0 + jnp.exp(-x))


def _lane_iota(shape):
    return lax.broadcasted_iota(jnp.int32, shape, len(shape) - 1)


def _rms(x, width):
    ss = jnp.sum(x * x, axis=-1, keepdims=True)
    return x * lax.rsqrt(ss * (1.0 / width) + NORM_EPS)


def _rope(x, cos, sin):
    n = x.shape[-1]
    lane = _lane_iota(x.shape)
    partner = jnp.where((lane & 32) != 0, pltpu.roll(x, 32, 1), pltpu.roll(x, n - 32, 1))
    return x * cos + partner * sin


def _modulated_norm(x, shift, scale):
    return _rms(x, D_MODEL) * (1.0 + scale) + shift


def _ada_kernel(cond_ref, w_ref, b_ref, o_ref):
    cnd = cond_ref[...]
    act = cnd * _sigmoid(cnd)
    o_ref[0] = jnp.dot(act, w_ref[0], preferred_element_type=F32) + b_ref[0]


def _ada(cond8, w_ada, b_ada):
    depth = w_ada.shape[0]
    n_chunk = w_ada.shape[2] // D_MODEL
    return pl.pallas_call(
        _ada_kernel,
        grid=(depth, n_chunk),
        in_specs=[
            pl.BlockSpec((8, D_MODEL), lambda l, j: (0, 0)),
            pl.BlockSpec((1, D_MODEL, D_MODEL), lambda l, j: (l, 0, j)),
            pl.BlockSpec((1, 1, D_MODEL), lambda l, j: (l, 0, j)),
        ],
        out_specs=pl.BlockSpec((1, 8, D_MODEL), lambda l, j: (l, 0, j)),
        out_shape=jax.ShapeDtypeStruct((depth, 8, n_chunk * D_MODEL), F32),
        compiler_params=_cparams("arbitrary", "arbitrary"),
        name="ada",
    )(cond8, w_ada, b_ada.reshape(depth, 1, -1))


def _diff_qk_norm(u, gain, cos, sin):
    outs = []
    for h in range(DIFF_HEADS):
        c = u[:, h * HEAD_W:(h + 1) * HEAD_W]
        lane = _lane_iota(c.shape)
        sq = c * c
        lo = jnp.sum(jnp.where(lane < DIFF_HD, sq, 0.0), axis=-1, keepdims=True)
        hi = jnp.sum(jnp.where(lane >= DIFF_HD, sq, 0.0), axis=-1, keepdims=True)
        ms = jnp.where(lane < DIFF_HD, lo, hi) * (1.0 / DIFF_HD)
        y = c * lax.rsqrt(ms + NORM_EPS) * gain[:, h * HEAD_W:(h + 1) * HEAD_W]
        outs.append((y, _rope(y, cos, sin)))
    return outs


def _proj_a_kernel(x_ref, mod_ref, w_ref, gq_ref, gk_ref, cos_ref, sin_ref,
                   pool_ref, dq_ref, dkf_ref, dkb_ref, dvf_ref, dvb_ref, gate_ref):
    mod = mod_ref[0]
    h = _modulated_norm(x_ref[...], mod[0:1], mod[1:2]).astype(BF16)
    cos, sin = cos_ref[...], sin_ref[...]

    def seg(k):
        return jnp.dot(h, w_ref[:, k * D_MODEL:(k + 1) * D_MODEL], preferred_element_type=F32)

    pool_ref[...] = seg(0).astype(BF16)
    q_scale = DIFF_HD ** -0.5 * LOG2E
    for hd, (_, roped) in enumerate(_diff_qk_norm(seg(1), gq_ref[...], cos, sin)):
        dq_ref[:, hd * HEAD_W:(hd + 1) * HEAD_W] = (roped * q_scale).astype(BF16)
    for hd, (plain, roped) in enumerate(_diff_qk_norm(seg(2), gk_ref[...], cos, sin)):
        dkf_ref[:, hd * HEAD_W:(hd + 1) * HEAD_W] = plain
        dkb_ref[:, hd * HEAD_W:(hd + 1) * HEAD_W] = roped.astype(BF16)
    dv = seg(3)
    dvf_ref[...] = dv
    dvb_ref[...] = dv.astype(BF16)
    for r in range(3):
        gate_ref[:, r * D_MODEL:(r + 1) * D_MODEL] = _sigmoid(seg(4 + r)).astype(BF16)


def _mla_heads(u, gain, cos, sin, rope):
    outs = []
    for h in range(MLA_HEADS):
        c = u[:, h * MLA_W:(h + 1) * MLA_W]
        y = _rms(c, MLA_QK) * gain[:, h * MLA_W:(h + 1) * MLA_W]
        outs.append(_rope(y, cos, sin) if rope else y)
    return outs


def _mla_kv(ckv_n, krw, wkn_ref, wv_ref, gk, cos, sin, rope, k_ref, v_ref):
    kn = jnp.dot(ckv_n, wkn_ref[...], preferred_element_type=F32)
    kfull = jnp.concatenate([kn[:, h * MLA_W:(h + 1) * MLA_W] + krw for h in range(MLA_HEADS)], axis=-1)
    for hd, y in enumerate(_mla_heads(kfull, gk, cos, sin, rope)):
        k_ref[:, hd * MLA_W:(hd + 1) * MLA_W] = y.astype(BF16)
    v_ref[...] = jnp.dot(ckv_n, wv_ref[...], preferred_element_type=F32).astype(BF16)


def _proj_b_kernel(x_ref, mod_ref, w_ref, gqa_ref, gkva_ref, wuq_ref, wkn_ref, wv_ref, gq_ref, gk_ref,
                   cos_ref, sin_ref, mq_ref, ckv_ref, kr_ref, mk_ref, mv_ref):
    mod = mod_ref[0]
    h = _modulated_norm(x_ref[...], mod[0:1], mod[1:2]).astype(BF16)
    cos, sin = cos_ref[...], sin_ref[...]
    cq = jnp.dot(h, w_ref[:, :MLA_Q_RANK], preferred_element_type=F32)
    cq_n = (_rms(cq, MLA_Q_RANK) * gqa_ref[...]).astype(BF16)
    mq = jnp.dot(cq_n, wuq_ref[...], preferred_element_type=F32)
    q_scale = MLA_QK ** -0.5 * LOG2E
    for hd, y in enumerate(_mla_heads(mq, gq_ref[...], cos, sin, True)):
        mq_ref[:, hd * MLA_W:(hd + 1) * MLA_W] = (y * q_scale).astype(BF16)
    ckv = jnp.dot(h, w_ref[:, MLA_Q_RANK:MLA_Q_RANK + MLA_KV_RANK], preferred_element_type=F32)
    ckv_n = _rms(ckv, MLA_KV_RANK) * gkva_ref[...]
    ckv_ref[...] = ckv_n
    krw = jnp.dot(h, w_ref[:, MLA_Q_RANK + MLA_KV_RANK:], preferred_element_type=F32)
    kr_ref[...] = krw[:, MLA_NOPE:MLA_NOPE + MLA_ROPE]
    _mla_kv(ckv_n.astype(BF16), krw, wkn_ref, wv_ref, gk_ref[...], cos, sin, True, mk_ref, mv_ref)


def _cache_kv_kernel(ckv_ref, krw_ref, wkn_ref, wv_ref, gk_ref, mk_ref, mv_ref):
    _mla_kv(ckv_ref[...].astype(BF16), krw_ref[...], wkn_ref, wv_ref, gk_ref[...], None, None, False,
            mk_ref, mv_ref)


def _mod_spec(n_ctx_blocks, blocks_per_lat):
    def row(i):
        return jnp.where(i < n_ctx_blocks, 0, 1 + (i - n_ctx_blocks) // blocks_per_lat)
    return pl.BlockSpec((1, 8, D_MODEL), lambda i: (row(i), 0, 0))


def _rows(width):
    return pl.BlockSpec((ROW_BLOCK, width), lambda i: (i, 0))


def _proj_a(x, mod, w_a, gq, gk, cos, sin, mod_spec):
    t = x.shape[0]
    widths = (D_MODEL,) * 6 + (3 * D_MODEL,)
    dtypes = (BF16, BF16, F32, BF16, F32, BF16, BF16)
    return pl.pallas_call(
        _proj_a_kernel,
        grid=(t // ROW_BLOCK,),
        in_specs=[_rows(D_MODEL), mod_spec, _resident(w_a.shape), _resident(gq.shape), _resident(gk.shape),
                  _rows(HEAD_W), _rows(HEAD_W)],
        out_specs=[_rows(w) for w in widths],
        out_shape=[jax.ShapeDtypeStruct((t, w), dt) for w, dt in zip(widths, dtypes)],
        compiler_params=_cparams("arbitrary"),
        name="proj_a",
    )(x, mod, w_a, gq, gk, cos, sin)


def _proj_b(x, mod, w_b, gqa, gkva, wuq, wkn, wv, gq, gk, cos, sin, mod_spec):
    t = x.shape[0]
    widths = (MLA_HEADS * MLA_W, MLA_KV_RANK, MLA_ROPE, MLA_HEADS * MLA_W, D_MODEL)
    dtypes = (BF16, F32, F32, BF16, BF16)
    consts = (w_b, gqa, gkva, wuq, wkn, wv, gq, gk)
    return pl.pallas_call(
        _proj_b_kernel,
        grid=(t // ROW_BLOCK,),
        in_specs=[_rows(D_MODEL), mod_spec] + [_resident(a.shape) for a in consts] + [_rows(MLA_W), _rows(MLA_W)],
        out_specs=[_rows(w) for w in widths],
        out_shape=[jax.ShapeDtypeStruct((t, w), dt) for w, dt in zip(widths, dtypes)],
        compiler_params=_cparams("arbitrary"),
        name="proj_b",
    )(x, mod, *consts, cos, sin)


def _cache_kv(ckv, krw, wkn, wv, gk):
    n = ckv.shape[0]
    widths = (MLA_HEADS * MLA_W, D_MODEL)
    consts = (wkn, wv, gk)
    return pl.pallas_call(
        _cache_kv_kernel,
        grid=(n // ROW_BLOCK,),
        in_specs=[_rows(MLA_KV_RANK), _rows(MLA_W)] + [_resident(a.shape) for a in consts],
        out_specs=[_rows(w) for w in widths],
        out_shape=[jax.ShapeDtypeStruct((n, w), BF16) for w in widths],
        compiler_params=_cparams("arbitrary"),
        name="cache_kv",
    )(ckv, krw, *consts)


def _pool_kernel(prev_ref, cur_ref, next_ref, w_ref, scale_ref, o_ref, *, n_ctx_blocks, ctx_len, lat_len):
    i = pl.program_id(0)
    g = pl.program_id(1)
    rb = ROW_BLOCK
    is_ctx = i < n_ctx_blocks
    n_ctx_rows = n_ctx_blocks * rb
    seq_len = jnp.where(is_ctx, ctx_len, lat_len)
    seq_start = jnp.where(is_ctx, (i * rb // ctx_len) * ctx_len,
                          n_ctx_rows + ((i * rb - n_ctx_rows) // lat_len) * lat_len)
    window = 2 << g
    row_pos = i * rb - seq_start + lax.broadcasted_iota(jnp.int32, (rb, 3 * rb), 0)
    col_pos = (i - 1) * rb - seq_start + lax.broadcasted_iota(jnp.int32, (rb, 3 * rb), 1)
    lo = jnp.clip(row_pos - window // 2, 0, seq_len)
    hi = jnp.clip(row_pos - window // 2 + window, 0, seq_len)
    band = jnp.where((col_pos >= lo) & (col_pos < hi), 1.0, 0.0).astype(BF16)
    u = jnp.concatenate([prev_ref[...], cur_ref[...], next_ref[...]], axis=0)
    total = jnp.dot(band, u, preferred_element_type=F32)
    cnt = (hi[:, :1] - lo[:, :1]).astype(F32)
    pooled = total / cnt - cur_ref[...].astype(F32)
    mixed = jnp.dot(pooled.astype(BF16), w_ref[0], preferred_element_type=F32)
    o_ref[...] = (mixed * scale_ref[...]).astype(BF16)


def _pool(u_pool, pool_w, pool_scale, n_ctx_blocks, ctx_len, lat_len):
    t = u_pool.shape[0]
    nb = t // ROW_BLOCK
    blk = (ROW_BLOCK, POOL_GROUP)
    kern = functools.partial(_pool_kernel, n_ctx_blocks=n_ctx_blocks, ctx_len=ctx_len, lat_len=lat_len)
    return pl.pallas_call(
        kern,
        grid=(nb, N_POOL_GROUPS),
        in_specs=[
            pl.BlockSpec(blk, lambda i, g: (jnp.maximum(i - 1, 0), g)),
            pl.BlockSpec(blk, lambda i, g: (i, g)),
            pl.BlockSpec(blk, lambda i, g: (jnp.minimum(i + 1, nb - 1), g)),
            pl.BlockSpec((1, POOL_GROUP, POOL_GROUP), lambda i, g: (g, 0, 0)),
            pl.BlockSpec((1, POOL_GROUP), lambda i, g: (0, g)),
        ],
        out_specs=pl.BlockSpec(blk, lambda i, g: (i, g)),
        out_shape=jax.ShapeDtypeStruct((t, D_MODEL), BF16),
        compiler_params=_cparams("arbitrary", "arbitrary"),
        name="pool",
    )(u_pool, u_pool, u_pool, pool_w, pool_scale)


def _row_reduce(x, combine, reduce):
    tiles = [x[:, j:j + HEAD_W] for j in range(0, x.shape[1], HEAD_W)]
    acc = tiles[0]
    for t in tiles[1:]:
        acc = combine(acc, t)
    return reduce(acc, axis=-1, keepdims=True)


def _online_softmax_step(s, v, m, l, acc):
    m_new = jnp.maximum(m, _row_reduce(s, jnp.maximum, jnp.max))
    p = jnp.exp2(s - m_new)
    alpha = jnp.exp2(m - m_new)
    l_new = alpha * l + _row_reduce(p, jnp.add, jnp.sum)
    acc_new = alpha * acc + jnp.dot(p.astype(BF16), v, preferred_element_type=F32)
    return m_new, l_new, acc_new


def _softmax_init(tq, width):
    return (jnp.full((tq, 1), -jnp.inf, F32), jnp.zeros((tq, 1), F32), jnp.zeros((tq, width), F32))


def _key_chunks(tk, body, init):
    n_chunks = tk // ATTN_K_CHUNK if tk > ATTN_K_CHUNK else 1
    size = tk // n_chunks
    carry = init
    for c in range(n_chunks):
        carry = body(c * size, size, carry)
    return carry


def _diff_attn_kernel(lam_ref, q_ref, k_ref, v_ref, g_ref, o_ref):
    q = q_ref[...]
    tq = q.shape[0]
    lane = _lane_iota(q.shape)
    q1 = jnp.where(lane < DIFF_HD, q, jnp.zeros_like(q))
    q2 = jnp.where(lane >= DIFF_HD, q, jnp.zeros_like(q))

    def body(start, size, carry):
        c1, c2 = carry
        k = k_ref[pl.ds(start, size), :]
        v = v_ref[pl.ds(start, size), :]
        s1 = lax.dot_general(q1, k, _NT, preferred_element_type=F32)
        s2 = lax.dot_general(q2, k, _NT, preferred_element_type=F32)
        return _online_softmax_step(s1, v, *c1), _online_softmax_step(s2, v, *c2)

    (_, l1, a1), (_, l2, a2) = _key_chunks(k_ref.shape[0], body, (_softmax_init(tq, HEAD_W),) * 2)
    o = a1 / l1 - lam_ref[0] * (a2 / l2)
    o_ref[...] = (_rms(o, HEAD_W) * g_ref[...]).astype(o_ref.dtype)


def _mla_attn_kernel(q_ref, k_ref, v_ref, o_ref):
    q = q_ref[...]

    def body(start, size, carry):
        k = k_ref[pl.ds(start, size), :]
        v = v_ref[pl.ds(start, size), :]
        s = lax.dot_general(q, k, _NT, preferred_element_type=F32)
        return _online_softmax_step(s, v, *carry)

    _, l, acc = _key_chunks(k_ref.shape[0], body, _softmax_init(q.shape[0], HEAD_W))
    o_ref[...] = (acc / l).astype(o_ref.dtype)


def _attention(kernel, q, k, v, extra, extra_specs, qk_width, name):
    b, tq_total, _ = q.shape
    tk = k.shape[1]
    n_heads = v.shape[2] // HEAD_W
    tq = min(ATTN_Q_BLOCK, tq_total)
    return pl.pallas_call(
        kernel,
        grid=(b, n_heads, tq_total // tq),
        in_specs=extra_specs[:1] + [
            pl.BlockSpec((None, tq, qk_width), lambda bi, h, qi: (bi, qi, h)),
            pl.BlockSpec((None, tk, qk_width), lambda bi, h, qi: (bi, 0, h)),
            pl.BlockSpec((None, tk, HEAD_W), lambda bi, h, qi: (bi, 0, h)),
        ] + extra_specs[1:],
        out_specs=pl.BlockSpec((None, tq, HEAD_W), lambda bi, h, qi: (bi, qi, h)),
        out_shape=jax.ShapeDtypeStruct((b, tq_total, n_heads * HEAD_W), BF16),
        compiler_params=_cparams("arbitrary", "arbitrary", "arbitrary"),
        name=name,
    )(*extra[:1], q, k, v, *extra[1:])


def _diff_attention(q, k, v, lam, out_gain, name):
    specs = [pl.BlockSpec(memory_space=pltpu.SMEM), pl.BlockSpec((1, HEAD_W), lambda bi, h, qi: (0, 0))]
    return _attention(_diff_attn_kernel, q, k, v, (lam, out_gain), specs, HEAD_W, name)


def _mla_attention(q, k, v, name):
    b, tq_total, _ = q.shape
    tk = k.shape[1]
    tq = min(ATTN_Q_BLOCK, tq_total)
    return pl.pallas_call(
        _mla_attn_kernel,
        grid=(b, MLA_HEADS, tq_total // tq),
        in_specs=[
            pl.BlockSpec((None, tq, MLA_W), lambda bi, h, qi: (bi, qi, h)),
            pl.BlockSpec((None, tk, MLA_W), lambda bi, h, qi: (bi, 0, h)),
            pl.BlockSpec((None, tk, HEAD_W), lambda bi, h, qi: (bi, 0, h)),
        ],
        out_specs=pl.BlockSpec((None, tq, HEAD_W), lambda bi, h, qi: (bi, qi, h)),
        out_shape=jax.ShapeDtypeStruct((b, tq_total, MLA_HEADS * HEAD_W), BF16),
        compiler_params=_cparams("arbitrary", "arbitrary", "arbitrary"),
        name=name,
    )(q, k, v)


def _merge_kernel(x_ref, mod_ref, yp_ref, yd_ref, ym_ref, gate_ref, wb_ref, wo_ref, rw_ref, rb_ref,
                  x1_ref, h2_ref, topw_ref, topi_ref, rank_ref, count_ref, carry_ref):
    @pl.when(pl.program_id(0) == 0)
    def _():
        carry_ref[...] = jnp.zeros_like(carry_ref)

    mod = mod_ref[0]
    merged = jnp.zeros(x_ref.shape, F32)
    for r, y_ref in enumerate((yp_ref, yd_ref, ym_ref)):
        z = jnp.dot(y_ref[...], wb_ref[r], preferred_element_type=F32)
        merged = merged + gate_ref[:, r * D_MODEL:(r + 1) * D_MODEL].astype(F32) * z
    y = jnp.dot(merged.astype(BF16), wo_ref[...], preferred_element_type=F32)
    x1 = x_ref[...] + mod[2:3] * y
    x1_ref[...] = x1
    h2 = _modulated_norm(x1, mod[3:4], mod[4:5])
    h2_ref[...] = h2
    logits = jnp.dot(h2, rw_ref[...], preferred_element_type=F32) + rb_ref[...]
    lane = _lane_iota(logits.shape)
    picks = []
    for _ in range(TOP_K):
        mx = jnp.max(logits, axis=-1, keepdims=True)
        first = jnp.min(jnp.where(logits == mx, lane, EXPERT_LANES), axis=-1, keepdims=True)
        hit = lane == first
        picks.append((mx, first, hit))
        logits = jnp.where(hit, -jnp.inf, logits)
    exps = [jnp.exp(v - picks[0][0]) for v, _, _ in picks]
    denom = exps[0] + exps[1] + exps[2] + exps[3]
    rows = logits.shape[0]
    chosen = jnp.zeros(logits.shape, F32)
    for _, _, hit in picks:
        chosen = chosen + jnp.where(hit, 1.0, 0.0)
    earlier = jnp.where(lax.broadcasted_iota(jnp.int32, (rows, rows), 1)
                        < lax.broadcasted_iota(jnp.int32, (rows, rows), 0), 1.0, 0.0).astype(BF16)
    before = jnp.dot(earlier, chosen.astype(BF16), preferred_element_type=F32) + carry_ref[0:1, :]
    topw = jnp.zeros(logits.shape, F32)
    topi = jnp.zeros(logits.shape, jnp.int32)
    rank = jnp.zeros(logits.shape, jnp.int32)
    for k, (e, (_, first, hit)) in enumerate(zip(exps, picks)):
        slot_rank = jnp.sum(jnp.where(hit, before, 0.0), axis=-1, keepdims=True)
        topw = jnp.where(lane == k, e / denom, topw)
        topi = jnp.where(lane == k, first, topi)
        rank = jnp.where(lane == k, slot_rank.astype(jnp.int32), rank)
    topw_ref[...] = topw
    topi_ref[...] = topi
    rank_ref[...] = rank
    carry_ref[...] = carry_ref[...] + jnp.sum(chosen, axis=0, keepdims=True)
    count_ref[...] = carry_ref[...]


def _merge(x, mod, y_pool, y_diff, y_mla, gates, wb, wo, rw, rb, mod_spec):
    t = x.shape[0]
    consts = (wb, wo, rw, rb)
    lanes = EXPERT_LANES
    return pl.pallas_call(
        _merge_kernel,
        grid=(t // ROW_BLOCK,),
        in_specs=[_rows(D_MODEL), mod_spec, _rows(D_MODEL), _rows(D_MODEL), _rows(D_MODEL), _rows(3 * D_MODEL)]
        + [_resident(a.shape) for a in consts],
        out_specs=[_rows(D_MODEL), _rows(D_MODEL), _rows(lanes), _rows(lanes), _rows(lanes),
                   pl.BlockSpec((8, lanes), lambda i: (0, 0))],
        out_shape=[jax.ShapeDtypeStruct((t, D_MODEL), F32), jax.ShapeDtypeStruct((t, D_MODEL), F32),
                   jax.ShapeDtypeStruct((t, lanes), F32), jax.ShapeDtypeStruct((t, lanes), jnp.int32),
                   jax.ShapeDtypeStruct((t, lanes), jnp.int32), jax.ShapeDtypeStruct((8, lanes), F32)],
        scratch_shapes=[pltpu.VMEM((8, lanes), F32)],
        compiler_params=_cparams("arbitrary"),
        name="merge",
    )(x, mod, y_pool, y_diff, y_mla, gates, *consts)


def _slot_layout(counts, topi, rank):
    tile = MOE_TILE
    cnt = counts[0, :N_EXPERTS].astype(jnp.int32)
    padded = (cnt + tile - 1) // tile * tile
    end = jnp.cumsum(padded)
    start = end - padded
    experts = jnp.arange(N_EXPERTS, dtype=jnp.int32)
    idx4 = topi[:, :TOP_K]
    slot_start = jnp.sum(jnp.where(idx4[:, :, None] == experts, start, 0), axis=-1)
    pos = (slot_start + rank[:, :TOP_K]).reshape(-1)
    n_tiles = (topi.shape[0] * TOP_K + N_EXPERTS * tile) // tile
    tile_row = jnp.arange(n_tiles, dtype=jnp.int32) * tile
    tile_expert = jnp.minimum(jnp.sum(tile_row[:, None] >= end, axis=-1), N_EXPERTS - 1).astype(jnp.int32)
    n_used = (end[-1] // tile).astype(jnp.int32)
    tile_src = jnp.minimum(jnp.arange(n_tiles, dtype=jnp.int32), n_used - 1)
    ragged_last = jnp.any((cnt % tile != 0) & (tile_row[:, None] == end - tile), axis=-1)
    fill_tile = (ragged_last | (tile_row >= end[-1])).astype(jnp.int32)
    return pos, tile_expert, tile_src, n_used.reshape(1), fill_tile


def _row_copy(src, src_row, dst, dst_row, sem):
    return pltpu.make_async_copy(src.at[pl.ds(src_row, 1)], dst.at[pl.ds(dst_row, 1)], sem)


def _dispatch_kernel(fill_tile_ref, pos_ref, h_ref, hs_ref, zero_ref, fill_sem, row_sem):
    i = pl.program_id(0)
    slots = DISPATCH_ROWS * TOP_K

    @pl.when(i == 0)
    def _():
        zero_ref[...] = jnp.zeros_like(zero_ref)

        def fill_copy(tile):
            row = pl.multiple_of(tile * MOE_TILE, MOE_TILE)
            return pltpu.make_async_copy(zero_ref, hs_ref.at[pl.ds(row, MOE_TILE)], fill_sem)

        def start_fill(tile, carry):
            @pl.when(fill_tile_ref[tile] != 0)
            def _():
                fill_copy(tile).start()
            return carry

        def wait_fill(tile, carry):
            @pl.when(fill_tile_ref[tile] != 0)
            def _():
                fill_copy(tile).wait()
            return carry

        lax.fori_loop(0, hs_ref.shape[0] // MOE_TILE, start_fill, 0)
        lax.fori_loop(0, hs_ref.shape[0] // MOE_TILE, wait_fill, 0)

    def issue(r, carry):
        slot = (i * DISPATCH_ROWS + r) * TOP_K
        for k in range(TOP_K):
            _row_copy(h_ref, r, hs_ref, pos_ref[slot + k], row_sem).start()
        return carry

    def drain(j, carry):
        for _ in range(WAIT_UNROLL):
            _row_copy(h_ref, 0, hs_ref, 0, row_sem).wait()
        return carry

    lax.fori_loop(0, DISPATCH_ROWS, issue, 0, unroll=2)
    lax.fori_loop(0, slots // WAIT_UNROLL, drain, 0)


def _dispatch(h2, pos, fill_tile):
    t = h2.shape[0]
    n_rows = fill_tile.shape[0] * MOE_TILE
    return pl.pallas_call(
        _dispatch_kernel,
        grid_spec=pltpu.PrefetchScalarGridSpec(
            num_scalar_prefetch=2,
            grid=(t // DISPATCH_ROWS,),
            in_specs=[pl.BlockSpec((DISPATCH_ROWS, D_MODEL), lambda i, fill_ref, pos_ref: (i, 0))],
            out_specs=pl.BlockSpec(memory_space=pl.ANY),
            scratch_shapes=[pltpu.VMEM((MOE_TILE, D_MODEL), F32), pltpu.SemaphoreType.DMA,
                            pltpu.SemaphoreType.DMA],
        ),
        out_shape=jax.ShapeDtypeStruct((n_rows, D_MODEL), F32),
        compiler_params=_cparams("arbitrary"),
        name="dispatch",
    )(fill_tile, pos, h2)


def _expert_kernel(tile_expert_ref, tile_src_ref, n_used_ref, hs_ref, wgu_ref, bgu_ref, wd_ref, bd_ref, ys_ref):
    occupied = pl.program_id(0) < n_used_ref[0]

    @pl.when(occupied)
    def _():
        gu = jnp.dot(hs_ref[...].astype(BF16), wgu_ref[0], preferred_element_type=F32) + bgu_ref[0]
        gate = jnp.minimum(gu[:, :D_FF], SWIGLU_LIMIT)
        up = jnp.clip(gu[:, D_FF:], -SWIGLU_LIMIT, SWIGLU_LIMIT)
        act = (up + 1.0) * gate * _sigmoid(SWIGLU_ALPHA * gate)
        ys_ref[...] = jnp.dot(act.astype(BF16), wd_ref[0], preferred_element_type=F32) + bd_ref[0]

    @pl.when(jnp.logical_not(occupied))
    def _():
        ys_ref[...] = jnp.zeros_like(ys_ref)


def _experts(hs, tile_expert, tile_src, n_used, wgu, bgu, wd, bd):
    n_rows = hs.shape[0]
    rows = lambda i, te, ts, nu: (ts[i], 0)
    weights = lambda i, te, ts, nu: (te[i], 0, 0)
    return pl.pallas_call(
        _expert_kernel,
        grid_spec=pltpu.PrefetchScalarGridSpec(
            num_scalar_prefetch=3,
            grid=(n_rows // MOE_TILE,),
            in_specs=[
                pl.BlockSpec((MOE_TILE, D_MODEL), rows),
                pl.BlockSpec((1, D_MODEL, 2 * D_FF), weights),
                pl.BlockSpec((1, 1, 2 * D_FF), weights),
                pl.BlockSpec((1, D_FF, D_MODEL), weights),
                pl.BlockSpec((1, 1, D_MODEL), weights),
            ],
            out_specs=pl.BlockSpec((MOE_TILE, D_MODEL), lambda i, te, ts, nu: (i, 0)),
        ),
        out_shape=jax.ShapeDtypeStruct((n_rows, D_MODEL), F32),
        compiler_params=_cparams("arbitrary"),
        name="experts",
    )(tile_expert, tile_src, n_used, hs, wgu, bgu, wd, bd)


def _combine_kernel(pos_ref, ys_ref, w_ref, x1_ref, mod_ref, o_ref, buf_ref, sem):
    i = pl.program_id(0)
    slots = COMBINE_ROWS * TOP_K

    def issue(r, carry):
        token = i * COMBINE_ROWS + r
        for k in range(TOP_K):
            _row_copy(ys_ref, pos_ref[token * TOP_K + k], buf_ref.at[k], r, sem).start()
        return carry

    def drain(j, carry):
        for _ in range(WAIT_UNROLL):
            _row_copy(ys_ref, 0, buf_ref.at[0], 0, sem).wait()
        return carry

    lax.fori_loop(0, COMBINE_ROWS, issue, 0, unroll=2)
    lax.fori_loop(0, slots // WAIT_UNROLL, drain, 0)
    w = w_ref[...]
    acc = w[:, 0:1] * buf_ref[0]
    for k in range(1, TOP_K):
        acc = acc + w[:, k:k + 1] * buf_ref[k]
    o_ref[...] = x1_ref[...] + mod_ref[0][5:6] * acc


def _combine(ys, pos, topw, x1, mod, n_ctx_rows, lat_len):
    t = x1.shape[0]
    rb = COMBINE_ROWS
    n_ctx_blocks = n_ctx_rows // rb

    def mod_row(i, pos_ref):
        return (jnp.where(i < n_ctx_blocks, 0, 1 + (i - n_ctx_blocks) // (lat_len // rb)), 0, 0)

    return pl.pallas_call(
        _combine_kernel,
        grid_spec=pltpu.PrefetchScalarGridSpec(
            num_scalar_prefetch=1,
            grid=(t // rb,),
            in_specs=[
                pl.BlockSpec(memory_space=pl.ANY),
                pl.BlockSpec((rb, EXPERT_LANES), lambda i, pos_ref: (i, 0)),
                pl.BlockSpec((rb, D_MODEL), lambda i, pos_ref: (i, 0)),
                pl.BlockSpec((1, 8, D_MODEL), mod_row),
            ],
            out_specs=pl.BlockSpec((rb, D_MODEL), lambda i, pos_ref: (i, 0)),
            scratch_shapes=[pltpu.VMEM((TOP_K, rb, D_MODEL), F32), pltpu.SemaphoreType.DMA],
        ),
        out_shape=jax.ShapeDtypeStruct((t, D_MODEL), F32),
        compiler_params=_cparams("arbitrary"),
        name="combine",
    )(pos, ys, topw, x1, mod)


def _rope_tables(n_lat_tokens, rot_dim):
    n_rows = n_lat_tokens // GRID_W
    row = jnp.repeat(jnp.arange(n_rows, dtype=F32), GRID_W)
    col = jnp.tile(jnp.arange(GRID_W, dtype=F32), n_rows)
    n_freq = rot_dim // 4
    inv_freq = ROPE_THETA ** (-jnp.arange(n_freq, dtype=F32) / n_freq)
    ang = jnp.concatenate([row[:, None] * inv_freq, col[:, None] * inv_freq], axis=-1)
    return jnp.cos(ang), jnp.sin(ang)


def _token_tables(cos_lane, sin_lane, n_ctx_rows, n_lat_seqs):
    width = cos_lane.shape[1]
    cos = jnp.concatenate([jnp.ones((n_ctx_rows, width), F32)] + [cos_lane] * n_lat_seqs, axis=0)
    sin = jnp.concatenate([jnp.zeros((n_ctx_rows, width), F32)] + [sin_lane] * n_lat_seqs, axis=0)
    return cos, sin


def _pack_heads(w, n_heads, lo, hi, width):
    k = w.shape[0]
    per_head = w.shape[1] // n_heads
    part = w.reshape(k, n_heads, per_head)[:, :, lo:hi]
    part = jnp.pad(part, ((0, 0), (0, 0), (0, width - (hi - lo))))
    return part.reshape(k, n_heads * width)


def kernel(x_prompt, x_sample, c, cache_diff_k, cache_diff_v, cache_mla_ckv, cache_mla_krope, c_ctx, w_ada, b_ada,
           w_in, pool_w, pool_scale, diff_q_norm, diff_k_norm, diff_lambda, diff_out_norm, mla_q_a_norm, w_uq,
           mla_kv_a_norm, w_ukv, mla_q_norm, mla_k_norm, w_branch, w_out, router_w, router_b, moe_w_gu, moe_b_gu,
           moe_w_down, moe_b_down):
    batch, seq, d = x_prompt.shape
    dec_batch, dec_seq, _ = x_sample.shape
    depth = w_ada.shape[0]
    past = cache_diff_k.shape[2]
    n_ctx = batch * seq
    n_lat = dec_batch * dec_seq
    n_ctx_blocks = n_ctx // ROW_BLOCK
    mod_spec = _mod_spec(n_ctx_blocks, dec_seq // ROW_BLOCK)

    x = jnp.concatenate([x_prompt.reshape(n_ctx, d), x_sample.reshape(n_lat, d)], axis=0)
    cond8 = jnp.concatenate([c_ctx[None], c, jnp.zeros((8 - 1 - dec_batch, d), F32)], axis=0)
    mod_all = _ada(cond8, w_ada, b_ada).reshape(depth, 8, 6, d)[:, :1 + dec_batch]
    mod_all = jnp.pad(mod_all, ((0, 0), (0, 0), (0, 2), (0, 0)))

    cos_d, sin_d = _rope_tables(dec_seq, DIFF_HD)
    cos_m, sin_m = _rope_tables(dec_seq, MLA_ROPE)
    cos_a, sin_a = _token_tables(jnp.concatenate([cos_d] * 4, axis=1),
                                 jnp.concatenate([-sin_d, sin_d, -sin_d, sin_d], axis=1), n_ctx, dec_batch)
    one, zero = jnp.ones((dec_seq, MLA_NOPE), F32), jnp.zeros((dec_seq, MLA_NOPE), F32)
    cos_b, sin_b = _token_tables(jnp.concatenate([one, cos_m, cos_m, one[:, :64]], axis=1),
                                 jnp.concatenate([zero, -sin_m, sin_m, zero[:, :64]], axis=1), n_ctx, dec_batch)

    offs = np.cumsum((0, 1024, 1024, 1024, 1024, MLA_Q_RANK, MLA_KV_RANK, MLA_ROPE, 3 * 1024))
    new_dk, new_dv, new_ckv, new_kr = [], [], [], []
    for l in range(depth):
        lam_init = 0.8 - 0.6 * math.exp(-0.3 * l)
        wl = w_in[l]
        seg = [wl[:, offs[k]:offs[k + 1]] for k in range(8)]
        w_a = jnp.concatenate(seg[0:4] + [seg[7]], axis=1).astype(BF16)
        kr_wide = jnp.pad(seg[6], ((0, 0), (MLA_NOPE, MLA_W - MLA_NOPE - MLA_ROPE)))
        w_b = jnp.concatenate([seg[4], seg[5], kr_wide], axis=1).astype(BF16)
        wuq = _pack_heads(w_uq[l], MLA_HEADS, 0, MLA_QK, MLA_W).astype(BF16)
        wkn = _pack_heads(w_ukv[l], MLA_HEADS, 0, MLA_NOPE, MLA_W).astype(BF16)
        wv = _pack_heads(w_ukv[l], MLA_HEADS, MLA_NOPE, MLA_NOPE + HEAD_W, HEAD_W).astype(BF16)
        gq_d = jnp.tile(diff_q_norm[l], 2 * DIFF_HEADS)[None]
        gk_d = jnp.tile(diff_k_norm[l], 2 * DIFF_HEADS)[None]
        gq_m = jnp.tile(jnp.pad(mla_q_norm[l], (0, MLA_W - MLA_QK)), MLA_HEADS)[None]
        gk_m = jnp.tile(jnp.pad(mla_k_norm[l], (0, MLA_W - MLA_QK)), MLA_HEADS)[None]
        lp = diff_lambda[l]
        lam = (jnp.exp(jnp.sum(lp[0] * lp[1])) - jnp.exp(jnp.sum(lp[2] * lp[3])) + lam_init).reshape(1)
        out_gain = (diff_out_norm[l] * (1.0 - lam_init))[None]
        mod = mod_all[l]

        u_pool, dq, dk_f, dk_b, dv_f, dv_b, gates = _proj_a(x, mod, w_a, gq_d, gk_d, cos_a, sin_a, mod_spec)
        mq, ckv_n, kr, mk, mv = _proj_b(x, mod, w_b, mla_q_a_norm[l][None], mla_kv_a_norm[l][None], wuq, wkn, wv,
                                        gq_m, gk_m, cos_b, sin_b, mod_spec)
        new_dk.append(dk_f[:n_ctx].reshape(batch, seq, DIFF_HEADS, HEAD_W))
        new_dv.append(dv_f[:n_ctx].reshape(batch, seq, DIFF_HEADS, HEAD_W))
        new_ckv.append(ckv_n[:n_ctx].reshape(batch, seq, MLA_KV_RANK))
        new_kr.append(kr[:n_ctx].reshape(batch, seq, MLA_ROPE))

        y_pool = _pool(u_pool, pool_w[l].astype(BF16), pool_scale[l][None], n_ctx_blocks, seq, dec_seq)

        c_krw = jnp.pad(cache_mla_krope[:, l].reshape(dec_batch * past, MLA_ROPE),
                        ((0, 0), (MLA_NOPE, MLA_W - MLA_NOPE - MLA_ROPE)))
        c_mk, c_mv = _cache_kv(cache_mla_ckv[:, l].reshape(dec_batch * past, MLA_KV_RANK), c_krw, wkn, wv, gk_m)

        def ctx3(a):
            return a[:n_ctx].reshape(batch, seq, a.shape[1])

        def lat3(a, cached):
            new = a[n_ctx:].reshape(dec_batch, dec_seq, a.shape[1])
            return jnp.concatenate([cached.reshape(dec_batch, past, a.shape[1]).astype(a.dtype), new], axis=1)

        lat_q = lambda a: a[n_ctx:].reshape(dec_batch, dec_seq, a.shape[1])
        yd_ctx = _diff_attention(ctx3(dq), ctx3(dk_b), ctx3(dv_b), lam, out_gain, "diff_ctx")
        yd_lat = _diff_attention(lat_q(dq), lat3(dk_b, cache_diff_k[:, l]), lat3(dv_b, cache_diff_v[:, l]),
                                 lam, out_gain, "diff_lat")
        ym_ctx = _mla_attention(ctx3(mq), ctx3(mk), ctx3(mv), "mla_ctx")
        ym_lat = _mla_attention(lat_q(mq), lat3(mk, c_mk), lat3(mv, c_mv), "mla_lat")
        y_diff = jnp.concatenate([yd_ctx.reshape(n_ctx, d), yd_lat.reshape(n_lat, d)], axis=0)
        y_mla = jnp.concatenate([ym_ctx.reshape(n_ctx, d), ym_lat.reshape(n_lat, d)], axis=0)

        rw = jnp.pad(router_w[l], ((0, 0), (0, EXPERT_LANES - N_EXPERTS)))
        rb = jnp.pad(router_b[l], (0, EXPERT_LANES - N_EXPERTS), constant_values=NEG_BIG)[None]
        x1, h2, topw, topi, rank, counts = _merge(x, mod, y_pool, y_diff, y_mla, gates, w_branch[l].astype(BF16),
                                                  w_out[l].astype(BF16), rw, rb, mod_spec)
        pos, tile_expert, tile_src, n_used, fill_tile = _slot_layout(counts, topi, rank)
        hs = _dispatch(h2, pos, fill_tile)
        ys = _experts(hs, tile_expert, tile_src, n_used, moe_w_gu[l].astype(BF16), moe_b_gu[l][:, None, :],
                      moe_w_down[l].astype(BF16), moe_b_down[l][:, None, :])
        x = _combine(ys, pos, topw, x1, mod, n_ctx, dec_seq)

    return (x[:n_ctx].reshape(batch, seq, d), x[n_ctx:].reshape(dec_batch, dec_seq, d),
            jnp.stack(new_dk, axis=1), jnp.stack(new_dv, axis=1), jnp.stack(new_ckv, axis=1),
            jnp.stack(new_kr, axis=1))
```

```python
import functools
import math

import jax
import jax.numpy as jnp
import numpy as np
from jax import lax
from jax.experimental import pallas as pl
from jax.experimental.pallas import tpu as pltpu

F32 = jnp.float32
BF16 = jnp.bfloat16

D_MODEL = 1024
N_POOL_GROUPS = 4
POOL_GROUP = D_MODEL // N_POOL_GROUPS
DIFF_HD = 64
DIFF_HEADS = 8
HEAD_W = 128
MLA_HEADS = 8
MLA_NOPE = 128
MLA_ROPE = 64
MLA_QK = MLA_NOPE + MLA_ROPE
MLA_W = 256
MLA_Q_RANK = 768
MLA_KV_RANK = 512
N_EXPERTS = 32
TOP_K = 4
D_FF = 1024
EXPERT_LANES = 128
SWIGLU_ALPHA = 1.702
SWIGLU_LIMIT = 7.0
ROPE_THETA = 10000.0
NORM_EPS = 1e-6
GRID_W = 64
LOG2E = math.log2(math.e)
NEG_BIG = -1e30
VMEM_LIMIT = 56 * 1024 * 1024

ROW_BLOCK = 256
MOE_TILE = 256
DISPATCH_ROWS = 256
COMBINE_ROWS = 128
WAIT_UNROLL = 16
ISSUE_UNROLL = 4
ATTN_Q_BLOCK = 512
ATTN_K_CHUNK = 512

_NT = (((1,), (1,)), ((), ()))


def _cparams(*sem):
    return pltpu.CompilerParams(dimension_semantics=sem, vmem_limit_bytes=VMEM_LIMIT)


def _resident(shape):
    nd = len(shape)
    return pl.BlockSpec(shape, lambda *_: (0,) * nd, pipeline_mode=pl.Buffered(1))


def _sigmoid(x):
    return 1.0 / (1.0 + jnp.exp(-x))


def _lane_iota(shape):
    return lax.broadcasted_iota(jnp.int32, shape, len(shape) - 1)


def _rms(x, width):
    ss = jnp.sum(x * x, axis=-1, keepdims=True)
    return x * lax.rsqrt(ss * (1.0 / width) + NORM_EPS)


def _rope(x, cos, sin):
    n = x.shape[-1]
    lane = _lane_iota(x.shape)
    partner = jnp.where((lane & 32) != 0, pltpu.roll(x, 32, 1), pltpu.roll(x, n - 32, 1))
    return x * cos + partner * sin


def _modulated_norm(x, shift, scale):
    return _rms(x, D_MODEL) * (1.0 + scale) + shift


def _ada_kernel(cond_ref, w_ref, b_ref, o_ref):
    cnd = cond_ref[...]
    act = cnd * _sigmoid(cnd)
    o_ref[0] = jnp.dot(act, w_ref[0], preferred_element_type=F32) + b_ref[0]


def _ada(cond8, w_ada, b_ada):
    depth = w_ada.shape[0]
    n_chunk = w_ada.shape[2] // D_MODEL
    return pl.pallas_call(
        _ada_kernel,
        grid=(depth, n_chunk),
        in_specs=[
            pl.BlockSpec((8, D_MODEL), lambda l, j: (0, 0)),
            pl.BlockSpec((1, D_MODEL, D_MODEL), lambda l, j: (l, 0, j)),
            pl.BlockSpec((1, 1, D_MODEL), lambda l, j: (l, 0, j)),
        ],
        out_specs=pl.BlockSpec((1, 8, D_MODEL), lambda l, j: (l, 0, j)),
        out_shape=jax.ShapeDtypeStruct((depth, 8, n_chunk * D_MODEL), F32),
        compiler_params=_cparams("arbitrary", "arbitrary"),
        name="ada",
    )(cond8, w_ada, b_ada.reshape(depth, 1, -1))


def _diff_qk_norm(u, gain, cos, sin):
    outs = []
    for h in range(DIFF_HEADS):
        c = u[:, h * HEAD_W:(h + 1) * HEAD_W]
        lane = _lane_iota(c.shape)
        sq = c * c
        lo = jnp.sum(jnp.where(lane < DIFF_HD, sq, 0.0), axis=-1, keepdims=True)
        hi = jnp.sum(jnp.where(lane >= DIFF_HD, sq, 0.0), axis=-1, keepdims=True)
        ms = jnp.where(lane < DIFF_HD, lo, hi) * (1.0 / DIFF_HD)
        y = c * lax.rsqrt(ms + NORM_EPS) * gain[:, h * HEAD_W:(h + 1) * HEAD_W]
        outs.append((y, _rope(y, cos, sin)))
    return outs


def _proj_a_kernel(x_ref, mod_ref, w_ref, gq_ref, gk_ref, cos_ref, sin_ref,
                   pool_ref, dq_ref, dkf_ref, dkb_ref, dvf_ref, dvb_ref, gate_ref, *, n_ctx_blocks):
    is_ctx = pl.program_id(0) < n_ctx_blocks
    mod = mod_ref[0]
    h = _modulated_norm(x_ref[...], mod[0:1], mod[1:2]).astype(BF16)
    cos, sin = cos_ref[...], sin_ref[...]

    def seg(k):
        return jnp.dot(h, w_ref[:, k * D_MODEL:(k + 1) * D_MODEL], preferred_element_type=F32)

    pool_ref[...] = seg(0).astype(BF16)
    q_scale = DIFF_HD ** -0.5 * LOG2E
    for hd, (_, roped) in enumerate(_diff_qk_norm(seg(1), gq_ref[...], cos, sin)):
        dq_ref[:, hd * HEAD_W:(hd + 1) * HEAD_W] = (roped * q_scale).astype(BF16)
    dk_heads = _diff_qk_norm(seg(2), gk_ref[...], cos, sin)
    for hd, (_, roped) in enumerate(dk_heads):
        dkb_ref[:, hd * HEAD_W:(hd + 1) * HEAD_W] = roped.astype(BF16)
    dv = seg(3)
    dvb_ref[...] = dv.astype(BF16)

    @pl.when(is_ctx)
    def _():
        for hd, (plain, _) in enumerate(dk_heads):
            dkf_ref[:, hd * HEAD_W:(hd + 1) * HEAD_W] = plain
        dvf_ref[...] = dv

    for r in range(3):
        gate_ref[:, r * D_MODEL:(r + 1) * D_MODEL] = _sigmoid(seg(4 + r)).astype(BF16)


def _mla_heads(u, gain, cos, sin, rope):
    outs = []
    for h in range(MLA_HEADS):
        c = u[:, h * MLA_W:(h + 1) * MLA_W]
        y = _rms(c, MLA_QK) * gain[:, h * MLA_W:(h + 1) * MLA_W]
        outs.append(_rope(y, cos, sin) if rope else y)
    return outs


def _mla_kv(ckv_n, krw, wkn_ref, wv_ref, gk, cos, sin, rope, k_ref, v_ref):
    kn = jnp.dot(ckv_n, wkn_ref[...], preferred_element_type=F32)
    kfull = jnp.concatenate([kn[:, h * MLA_W:(h + 1) * MLA_W] + krw for h in range(MLA_HEADS)], axis=-1)
    for hd, y in enumerate(_mla_heads(kfull, gk, cos, sin, rope)):
        k_ref[:, hd * MLA_W:(hd + 1) * MLA_W] = y.astype(BF16)
    v_ref[...] = jnp.dot(ckv_n, wv_ref[...], preferred_element_type=F32).astype(BF16)


def _proj_b_kernel(x_ref, mod_ref, w_ref, gqa_ref, gkva_ref, wuq_ref, wkn_ref, wv_ref, gq_ref, gk_ref,
                   cos_ref, sin_ref, mq_ref, ckv_ref, kr_ref, mk_ref, mv_ref, *, n_ctx_blocks):
    is_ctx = pl.program_id(0) < n_ctx_blocks
    mod = mod_ref[0]
    h = _modulated_norm(x_ref[...], mod[0:1], mod[1:2]).astype(BF16)
    cos, sin = cos_ref[...], sin_ref[...]
    cq = jnp.dot(h, w_ref[:, :MLA_Q_RANK], preferred_element_type=F32)
    cq_n = (_rms(cq, MLA_Q_RANK) * gqa_ref[...]).astype(BF16)
    mq = jnp.dot(cq_n, wuq_ref[...], preferred_element_type=F32)
    q_scale = MLA_QK ** -0.5 * LOG2E
    for hd, y in enumerate(_mla_heads(mq, gq_ref[...], cos, sin, True)):
        mq_ref[:, hd * MLA_W:(hd + 1) * MLA_W] = (y * q_scale).astype(BF16)
    ckv = jnp.dot(h, w_ref[:, MLA_Q_RANK:MLA_Q_RANK + MLA_KV_RANK], preferred_element_type=F32)
    ckv_n = _rms(ckv, MLA_KV_RANK) * gkva_ref[...]
    krw = jnp.dot(h, w_ref[:, MLA_Q_RANK + MLA_KV_RANK:], preferred_element_type=F32)

    @pl.when(is_ctx)
    def _():
        ckv_ref[...] = ckv_n
        kr_ref[...] = krw[:, MLA_NOPE:MLA_NOPE + MLA_ROPE]

    _mla_kv(ckv_n.astype(BF16), krw, wkn_ref, wv_ref, gk_ref[...], cos, sin, True, mk_ref, mv_ref)


def _cache_kv_kernel(ckv_ref, krw_ref, wkn_ref, wv_ref, gk_ref, mk_ref, mv_ref):
    _mla_kv(ckv_ref[...].astype(BF16), krw_ref[...], wkn_ref, wv_ref, gk_ref[...], None, None, False,
            mk_ref, mv_ref)


def _mod_spec(n_ctx_blocks, blocks_per_lat):
    def row(i):
        return jnp.where(i < n_ctx_blocks, 0, 1 + (i - n_ctx_blocks) // blocks_per_lat)
    return pl.BlockSpec((1, 8, D_MODEL), lambda i: (row(i), 0, 0))


def _rows(width):
    return pl.BlockSpec((ROW_BLOCK, width), lambda i: (i, 0))


def _ctx_rows(width, n_ctx_blocks):
    return pl.BlockSpec((ROW_BLOCK, width), lambda i: (jnp.minimum(i, n_ctx_blocks - 1), 0))


def _proj_outputs(t, n_ctx_blocks, widths, dtypes, ctx_only):
    specs = [_ctx_rows(w, n_ctx_blocks) if c else _rows(w) for w, c in zip(widths, ctx_only)]
    shapes = [jax.ShapeDtypeStruct((n_ctx_blocks * ROW_BLOCK if c else t, w), dt)
              for w, dt, c in zip(widths, dtypes, ctx_only)]
    return specs, shapes


def _proj_a(x, mod, w_a, gq, gk, cos, sin, mod_spec, n_ctx_blocks):
    t = x.shape[0]
    widths = (D_MODEL,) * 6 + (3 * D_MODEL,)
    dtypes = (BF16, BF16, F32, BF16, F32, BF16, BF16)
    out_specs, out_shape = _proj_outputs(t, n_ctx_blocks, widths, dtypes, (0, 0, 1, 0, 1, 0, 0))
    return pl.pallas_call(
        functools.partial(_proj_a_kernel, n_ctx_blocks=n_ctx_blocks),
        grid=(t // ROW_BLOCK,),
        in_specs=[_rows(D_MODEL), mod_spec, _resident(w_a.shape), _resident(gq.shape), _resident(gk.shape),
                  _rows(HEAD_W), _rows(HEAD_W)],
        out_specs=out_specs,
        out_shape=out_shape,
        compiler_params=_cparams("arbitrary"),
        name="proj_a",
    )(x, mod, w_a, gq, gk, cos, sin)


def _proj_b(x, mod, w_b, gqa, gkva, wuq, wkn, wv, gq, gk, cos, sin, mod_spec, n_ctx_blocks):
    t = x.shape[0]
    widths = (MLA_HEADS * MLA_W, MLA_KV_RANK, MLA_ROPE, MLA_HEADS * MLA_W, D_MODEL)
    dtypes = (BF16, F32, F32, BF16, BF16)
    out_specs, out_shape = _proj_outputs(t, n_ctx_blocks, widths, dtypes, (0, 1, 1, 0, 0))
    consts = (w_b, gqa, gkva, wuq, wkn, wv, gq, gk)
    return pl.pallas_call(
        functools.partial(_proj_b_kernel, n_ctx_blocks=n_ctx_blocks),
        grid=(t // ROW_BLOCK,),
        in_specs=[_rows(D_MODEL), mod_spec] + [_resident(a.shape) for a in consts] + [_rows(MLA_W), _rows(MLA_W)],
        out_specs=out_specs,
        out_shape=out_shape,
        compiler_params=_cparams("arbitrary"),
        name="proj_b",
    )(x, mod, *consts, cos, sin)


def _cache_kv(ckv, krw, wkn, wv, gk):
    n = ckv.shape[0]
    widths = (MLA_HEADS * MLA_W, D_MODEL)
    consts = (wkn, wv, gk)
    return pl.pallas_call(
        _cache_kv_kernel,
        grid=(n // ROW_BLOCK,),
        in_specs=[_rows(MLA_KV_RANK), _rows(MLA_W)] + [_resident(a.shape) for a in consts],
        out_specs=[_rows(w) for w in widths],
        out_shape=[jax.ShapeDtypeStruct((n, w), BF16) for w in widths],
        compiler_params=_cparams("arbitrary"),
        name="cache_kv",
    )(ckv, krw, *consts)


def _pool_kernel(prev_ref, cur_ref, next_ref, w_ref, scale_ref, o_ref, *, n_ctx_blocks, ctx_len, lat_len):
    i = pl.program_id(0)
    g = pl.program_id(1)
    rb = ROW_BLOCK
    is_ctx = i < n_ctx_blocks
    n_ctx_rows = n_ctx_blocks * rb
    seq_len = jnp.where(is_ctx, ctx_len, lat_len)
    seq_start = jnp.where(is_ctx, (i * rb // ctx_len) * ctx_len,
                          n_ctx_rows + ((i * rb - n_ctx_rows) // lat_len) * lat_len)
    window = 2 << g
    row_pos = i * rb - seq_start + lax.broadcasted_iota(jnp.int32, (rb, 3 * rb), 0)
    col_pos = (i - 1) * rb - seq_start + lax.broadcasted_iota(jnp.int32, (rb, 3 * rb), 1)
    lo = jnp.clip(row_pos - window // 2, 0, seq_len)
    hi = jnp.clip(row_pos - window // 2 + window, 0, seq_len)
    band = jnp.where((col_pos >= lo) & (col_pos < hi), 1.0, 0.0).astype(BF16)
    u = jnp.concatenate([prev_ref[...], cur_ref[...], next_ref[...]], axis=0)
    total = jnp.dot(band, u, preferred_element_type=F32)
    cnt = (hi[:, :1] - lo[:, :1]).astype(F32)
    pooled = total / cnt - cur_ref[...].astype(F32)
    mixed = jnp.dot(pooled.astype(BF16), w_ref[0], preferred_element_type=F32)
    o_ref[...] = (mixed * scale_ref[...]).astype(BF16)


def _pool(u_pool, pool_w, pool_scale, n_ctx_blocks, ctx_len, lat_len):
    t = u_pool.shape[0]
    nb = t // ROW_BLOCK
    blk = (ROW_BLOCK, POOL_GROUP)
    kern = functools.partial(_pool_kernel, n_ctx_blocks=n_ctx_blocks, ctx_len=ctx_len, lat_len=lat_len)
    return pl.pallas_call(
        kern,
        grid=(nb, N_POOL_GROUPS),
        in_specs=[
            pl.BlockSpec(blk, lambda i, g: (jnp.maximum(i - 1, 0), g)),
            pl.BlockSpec(blk, lambda i, g: (i, g)),
            pl.BlockSpec(blk, lambda i, g: (jnp.minimum(i + 1, nb - 1), g)),
            pl.BlockSpec((1, POOL_GROUP, POOL_GROUP), lambda i, g: (g, 0, 0)),
            pl.BlockSpec((1, POOL_GROUP), lambda i, g: (0, g)),
        ],
        out_specs=pl.BlockSpec(blk, lambda i, g: (i, g)),
        out_shape=jax.ShapeDtypeStruct((t, D_MODEL), BF16),
        compiler_params=_cparams("arbitrary", "arbitrary"),
        name="pool",
    )(u_pool, u_pool, u_pool, pool_w, pool_scale)


def _row_reduce(x, combine, reduce):
    tiles = [x[:, j:j + HEAD_W] for j in range(0, x.shape[1], HEAD_W)]
    acc = tiles[0]
    for t in tiles[1:]:
        acc = combine(acc, t)
    return reduce(acc, axis=-1, keepdims=True)


def _online_softmax_step(s, v, m, l, acc):
    m_new = jnp.maximum(m, _row_reduce(s, jnp.maximum, jnp.max))
    p = jnp.exp2(s - m_new)
    alpha = jnp.exp2(m - m_new)
    l_new = alpha * l + _row_reduce(p, jnp.add, jnp.sum)
    acc_new = alpha * acc + jnp.dot(p.astype(BF16), v, preferred_element_type=F32)
    return m_new, l_new, acc_new


def _softmax_init(tq, width):
    return (jnp.full((tq, 1), -jnp.inf, F32), jnp.zeros((tq, 1), F32), jnp.zeros((tq, width), F32))


def _key_value_chunks(kv_refs):
    for k_ref, v_ref in zip(kv_refs[0::2], kv_refs[1::2]):
        n = k_ref.shape[0]
        size = min(n, ATTN_K_CHUNK)
        for start in range(0, n, size):
            yield k_ref[start:start + size, :].astype(BF16), v_ref[start:start + size, :].astype(BF16)


def _diff_attn_kernel(lam_ref, q_ref, *refs, n_kv):
    g_ref, o_ref = refs[n_kv], refs[-1]
    q = q_ref[...]
    lane = _lane_iota(q.shape)
    q1 = jnp.where(lane < DIFF_HD, q, jnp.zeros_like(q))
    q2 = jnp.where(lane >= DIFF_HD, q, jnp.zeros_like(q))
    c1 = c2 = _softmax_init(q.shape[0], HEAD_W)
    for k, v in _key_value_chunks(refs[:n_kv]):
        s1 = lax.dot_general(q1, k, _NT, preferred_element_type=F32)
        s2 = lax.dot_general(q2, k, _NT, preferred_element_type=F32)
        c1 = _online_softmax_step(s1, v, *c1)
        c2 = _online_softmax_step(s2, v, *c2)
    (_, l1, a1), (_, l2, a2) = c1, c2
    o = a1 / l1 - lam_ref[0] * (a2 / l2)
    o_ref[...] = (_rms(o, HEAD_W) * g_ref[...]).astype(o_ref.dtype)


def _mla_attn_kernel(q_ref, *refs, n_kv):
    o_ref = refs[-1]
    q = q_ref[...]
    carry = _softmax_init(q.shape[0], HEAD_W)
    for k, v in _key_value_chunks(refs[:n_kv]):
        carry = _online_softmax_step(lax.dot_general(q, k, _NT, preferred_element_type=F32), v, *carry)
    _, l, acc = carry
    o_ref[...] = (acc / l).astype(o_ref.dtype)


def _attention(kernel, q, k, v, cache_k, cache_v, lead, lead_specs, tail, tail_specs, qk_width, geom, name):
    n_ctx, seq, dec_seq = geom
    t = q.shape[0]
    n_heads = v.shape[1] // HEAD_W
    dec_batch, past, _ = cache_k.shape
    tq = ATTN_Q_BLOCK
    nq, q0, k0 = dec_seq // tq, n_ctx // tq, n_ctx // dec_seq
    assert n_ctx % tq == 0 and n_ctx % dec_seq == 0 and dec_seq % tq == 0
    params = _cparams("arbitrary", "arbitrary", "arbitrary")
    seq_block = lambda w: pl.BlockSpec((seq, w), lambda b, h, qi: (b, h))
    ctx = pl.pallas_call(
        functools.partial(kernel, n_kv=2),
        grid=(n_ctx // seq, n_heads, 1),
        in_specs=lead_specs + [seq_block(qk_width), seq_block(qk_width), seq_block(HEAD_W)] + tail_specs,
        out_specs=seq_block(HEAD_W),
        out_shape=jax.ShapeDtypeStruct((n_ctx, n_heads * HEAD_W), BF16),
        compiler_params=params,
        name=name + "_ctx",
    )(*lead, q, k, v, *tail)
    lat_rows = lambda w, first: pl.BlockSpec((tq, w), lambda b, h, qi: (first + b * nq + qi, h))
    cached = lambda w: pl.BlockSpec((None, past, w), lambda b, h, qi: (b, 0, h))
    lat_keys = lambda w: pl.BlockSpec((dec_seq, w), lambda b, h, qi: (k0 + b, h))
    lat = pl.pallas_call(
        functools.partial(kernel, n_kv=4),
        grid=(dec_batch, n_heads, nq),
        in_specs=lead_specs + [lat_rows(qk_width, q0), cached(qk_width), cached(HEAD_W), lat_keys(qk_width),
                               lat_keys(HEAD_W)] + tail_specs,
        out_specs=lat_rows(HEAD_W, 0),
        out_shape=jax.ShapeDtypeStruct((t - n_ctx, n_heads * HEAD_W), BF16),
        compiler_params=params,
        name=name + "_lat",
    )(*lead, q, cache_k, cache_v, k, v, *tail)
    return ctx, lat


def _diff_attention(q, k, v, cache_k, cache_v, lam, out_gain, geom):
    lead_specs = [pl.BlockSpec(memory_space=pltpu.SMEM)]
    tail_specs = [pl.BlockSpec((1, HEAD_W), lambda b, h, qi: (0, 0))]
    return _attention(_diff_attn_kernel, q, k, v, cache_k, cache_v, (lam,), lead_specs, (out_gain,), tail_specs,
                      HEAD_W, geom, "diff")


def _mla_attention(q, k, v, cache_k, cache_v, geom):
    return _attention(_mla_attn_kernel, q, k, v, cache_k, cache_v, (), [], (), [], MLA_W, geom, "mla")


def _merge_kernel(x_ref, mod_ref, yp_ref, ydc_ref, ydl_ref, ymc_ref, yml_ref, gate_ref, wb_ref, wo_ref, rw_ref,
                  rb_ref, x1_ref, h2_ref, topw_ref, topi_ref, rank_ref, count_ref, carry_ref, *, n_ctx_blocks):
    @pl.when(pl.program_id(0) == 0)
    def _():
        carry_ref[...] = jnp.zeros_like(carry_ref)

    is_ctx = pl.program_id(0) < n_ctx_blocks
    mod = mod_ref[0]
    merged = jnp.zeros(x_ref.shape, F32)
    branches = (yp_ref[...], jnp.where(is_ctx, ydc_ref[...], ydl_ref[...]),
                jnp.where(is_ctx, ymc_ref[...], yml_ref[...]))
    for r, y in enumerate(branches):
        z = jnp.dot(y, wb_ref[r], preferred_element_type=F32)
        merged = merged + gate_ref[:, r * D_MODEL:(r + 1) * D_MODEL].astype(F32) * z
    y = jnp.dot(merged.astype(BF16), wo_ref[...], preferred_element_type=F32)
    x1 = x_ref[...] + mod[2:3] * y
    x1_ref[...] = x1
    h2 = _modulated_norm(x1, mod[3:4], mod[4:5])
    h2_ref[...] = h2
    logits = jnp.dot(h2, rw_ref[...], preferred_element_type=F32) + rb_ref[...]
    lane = _lane_iota(logits.shape)
    picks = []
    for _ in range(TOP_K):
        mx = jnp.max(logits, axis=-1, keepdims=True)
        first = jnp.min(jnp.where(logits == mx, lane, EXPERT_LANES), axis=-1, keepdims=True)
        hit = lane == first
        picks.append((mx, first, hit))
        logits = jnp.where(hit, -jnp.inf, logits)
    exps = [jnp.exp(v - picks[0][0]) for v, _, _ in picks]
    denom = exps[0] + exps[1] + exps[2] + exps[3]
    rows = logits.shape[0]
    chosen = jnp.zeros(logits.shape, F32)
    for _, _, hit in picks:
        chosen = chosen + jnp.where(hit, 1.0, 0.0)
    earlier = jnp.where(lax.broadcasted_iota(jnp.int32, (rows, rows), 1)
                        < lax.broadcasted_iota(jnp.int32, (rows, rows), 0), 1.0, 0.0).astype(BF16)
    before = jnp.dot(earlier, chosen.astype(BF16), preferred_element_type=F32) + carry_ref[0:1, :]
    topw = jnp.zeros(logits.shape, F32)
    topi = jnp.zeros(logits.shape, jnp.int32)
    rank = jnp.zeros(logits.shape, jnp.int32)
    for k, (e, (_, first, hit)) in enumerate(zip(exps, picks)):
        slot_rank = jnp.sum(jnp.where(hit, before, 0.0), axis=-1, keepdims=True)
        topw = jnp.where(lane == k, e / denom, topw)
        topi = jnp.where(lane == k, first, topi)
        rank = jnp.where(lane == k, slot_rank.astype(jnp.int32), rank)
    topw_ref[...] = topw
    topi_ref[...] = topi
    rank_ref[...] = rank
    carry_ref[...] = carry_ref[...] + jnp.sum(chosen, axis=0, keepdims=True)
    count_ref[...] = carry_ref[...]


def _merge(x, mod, y_pool, y_diff, y_mla, gates, wb, wo, rw, rb, mod_spec, n_ctx_blocks):
    t = x.shape[0]
    consts = (wb, wo, rw, rb)
    lanes = EXPERT_LANES
    ctx_rows = _ctx_rows(D_MODEL, n_ctx_blocks)
    lat_rows = pl.BlockSpec((ROW_BLOCK, D_MODEL), lambda i: (jnp.maximum(i - n_ctx_blocks, 0), 0))
    return pl.pallas_call(
        functools.partial(_merge_kernel, n_ctx_blocks=n_ctx_blocks),
        grid=(t // ROW_BLOCK,),
        in_specs=[_rows(D_MODEL), mod_spec, _rows(D_MODEL), ctx_rows, lat_rows, ctx_rows, lat_rows,
                  _rows(3 * D_MODEL)] + [_resident(a.shape) for a in consts],
        out_specs=[_rows(D_MODEL), _rows(D_MODEL), _rows(lanes), _rows(lanes), _rows(lanes),
                   pl.BlockSpec((8, lanes), lambda i: (0, 0))],
        out_shape=[jax.ShapeDtypeStruct((t, D_MODEL), F32), jax.ShapeDtypeStruct((t, D_MODEL), F32),
                   jax.ShapeDtypeStruct((t, lanes), F32), jax.ShapeDtypeStruct((t, lanes), jnp.int32),
                   jax.ShapeDtypeStruct((t, lanes), jnp.int32), jax.ShapeDtypeStruct((8, lanes), F32)],
        scratch_shapes=[pltpu.VMEM((8, lanes), F32)],
        compiler_params=_cparams("arbitrary"),
        name="merge",
    )(x, mod, y_pool, *y_diff, *y_mla, gates, *consts)


def _slot_layout(counts, topi, rank):
    tile = MOE_TILE
    cnt = counts[0, :N_EXPERTS].astype(jnp.int32)
    padded = (cnt + tile - 1) // tile * tile
    end = jnp.cumsum(padded)
    start = end - padded
    experts = jnp.arange(N_EXPERTS, dtype=jnp.int32)
    idx4 = topi[:, :TOP_K]
    slot_start = jnp.sum(jnp.where(idx4[:, :, None] == experts, start, 0), axis=-1)
    pos = (slot_start + rank[:, :TOP_K]).reshape(-1)
    n_tiles = (topi.shape[0] * TOP_K + N_EXPERTS * tile) // tile
    tile_row = jnp.arange(n_tiles, dtype=jnp.int32) * tile
    tile_expert = jnp.minimum(jnp.sum(tile_row[:, None] >= end, axis=-1), N_EXPERTS - 1).astype(jnp.int32)
    n_used = (end[-1] // tile).astype(jnp.int32)
    tile_src = jnp.minimum(jnp.arange(n_tiles, dtype=jnp.int32), n_used - 1)
    ragged_last = jnp.any((cnt % tile != 0) & (tile_row[:, None] == end - tile), axis=-1)
    fill_tile = (ragged_last | (tile_row >= end[-1])).astype(jnp.int32)
    return pos, tile_expert, tile_src, n_used.reshape(1), fill_tile


def _row_copy(src, src_row, dst, dst_row, sem):
    return pltpu.make_async_copy(src.at[pl.ds(src_row, 1)], dst.at[pl.ds(dst_row, 1)], sem)


def _dispatch_kernel(fill_tile_ref, pos_ref, h_ref, hs_ref, zero_ref, fill_sem, row_sem):
    i = pl.program_id(0)
    slots = DISPATCH_ROWS * TOP_K

    @pl.when(i == 0)
    def _():
        zero_ref[...] = jnp.zeros_like(zero_ref)

        def fill_copy(tile):
            row = pl.multiple_of(tile * MOE_TILE, MOE_TILE)
            return pltpu.make_async_copy(zero_ref, hs_ref.at[pl.ds(row, MOE_TILE)], fill_sem)

        def start_fill(tile, carry):
            @pl.when(fill_tile_ref[tile] != 0)
            def _():
                fill_copy(tile).start()
            return carry

        def wait_fill(tile, carry):
            @pl.when(fill_tile_ref[tile] != 0)
            def _():
                fill_copy(tile).wait()
            return carry

        lax.fori_loop(0, hs_ref.shape[0] // MOE_TILE, start_fill, 0)
        lax.fori_loop(0, hs_ref.shape[0] // MOE_TILE, wait_fill, 0)

    def issue(r, carry):
        slot = (i * DISPATCH_ROWS + r) * TOP_K
        for k in range(TOP_K):
            _row_copy(h_ref, r, hs_ref, pos_ref[slot + k], row_sem).start()
        return carry

    def drain(j, carry):
        for _ in range(WAIT_UNROLL):
            _row_copy(h_ref, 0, hs_ref, 0, row_sem).wait()
        return carry

    lax.fori_loop(0, DISPATCH_ROWS, issue, 0, unroll=ISSUE_UNROLL)
    lax.fori_loop(0, slots // WAIT_UNROLL, drain, 0)


def _dispatch(h2, pos, fill_tile):
    t = h2.shape[0]
    n_rows = fill_tile.shape[0] * MOE_TILE
    return pl.pallas_call(
        _dispatch_kernel,
        grid_spec=pltpu.PrefetchScalarGridSpec(
            num_scalar_prefetch=2,
            grid=(t // DISPATCH_ROWS,),
            in_specs=[pl.BlockSpec((DISPATCH_ROWS, D_MODEL), lambda i, fill_ref, pos_ref: (i, 0))],
            out_specs=pl.BlockSpec(memory_space=pl.ANY),
            scratch_shapes=[pltpu.VMEM((MOE_TILE, D_MODEL), F32), pltpu.SemaphoreType.DMA,
                            pltpu.SemaphoreType.DMA],
        ),
        out_shape=jax.ShapeDtypeStruct((n_rows, D_MODEL), F32),
        compiler_params=_cparams("arbitrary"),
        name="dispatch",
    )(fill_tile, pos, h2)


def _expert_kernel(tile_expert_ref, tile_src_ref, n_used_ref, hs_ref, wgu_ref, bgu_ref, wd_ref, bd_ref, ys_ref,
                   wgu_bf_ref, wd_bf_ref):
    i = pl.program_id(0)
    occupied = i < n_used_ref[0]
    new_expert = (i == 0) | (tile_expert_ref[i] != tile_expert_ref[jnp.maximum(i - 1, 0)])

    @pl.when(occupied & new_expert)
    def _():
        wgu_bf_ref[...] = wgu_ref[...].astype(BF16)
        wd_bf_ref[...] = wd_ref[...].astype(BF16)

    @pl.when(occupied)
    def _():
        gu = jnp.dot(hs_ref[...].astype(BF16), wgu_bf_ref[...], preferred_element_type=F32) + bgu_ref[0]
        gate = jnp.minimum(gu[:, :D_FF], SWIGLU_LIMIT)
        up = jnp.clip(gu[:, D_FF:], -SWIGLU_LIMIT, SWIGLU_LIMIT)
        act = (up + 1.0) * gate * _sigmoid(SWIGLU_ALPHA * gate)
        ys_ref[...] = jnp.dot(act.astype(BF16), wd_bf_ref[...], preferred_element_type=F32) + bd_ref[0]

    @pl.when(jnp.logical_not(occupied))
    def _():
        ys_ref[...] = jnp.zeros_like(ys_ref)


def _experts(hs, tile_expert, tile_src, n_used, layer, wgu, bgu, wd, bd):
    n_rows = hs.shape[0]
    rows = lambda i, te, ts, nu: (ts[i], 0)
    bias = lambda i, te, ts, nu: (te[i], 0, 0)
    weight = lambda i, te, ts, nu: (layer, te[i], 0, 0)
    return pl.pallas_call(
        _expert_kernel,
        grid_spec=pltpu.PrefetchScalarGridSpec(
            num_scalar_prefetch=3,
            grid=(n_rows // MOE_TILE,),
            in_specs=[
                pl.BlockSpec((MOE_TILE, D_MODEL), rows),
                pl.BlockSpec((None, None, D_MODEL, 2 * D_FF), weight),
                pl.BlockSpec((1, 1, 2 * D_FF), bias),
                pl.BlockSpec((None, None, D_FF, D_MODEL), weight),
                pl.BlockSpec((1, 1, D_MODEL), bias),
            ],
            out_specs=pl.BlockSpec((MOE_TILE, D_MODEL), lambda i, te, ts, nu: (i, 0)),
            scratch_shapes=[pltpu.VMEM((D_MODEL, 2 * D_FF), BF16), pltpu.VMEM((D_FF, D_MODEL), BF16)],
        ),
        out_shape=jax.ShapeDtypeStruct((n_rows, D_MODEL), F32),
        compiler_params=_cparams("arbitrary"),
        name="experts",
    )(tile_expert, tile_src, n_used, hs, wgu, bgu, wd, bd)


def _combine_kernel(pos_ref, ys_ref, w_ref, x1_ref, mod_ref, o_ref, buf_ref, sem):
    i = pl.program_id(0)
    slots = COMBINE_ROWS * TOP_K

    def issue(r, carry):
        token = i * COMBINE_ROWS + r
        for k in range(TOP_K):
            _row_copy(ys_ref, pos_ref[token * TOP_K + k], buf_ref.at[k], r, sem).start()
        return carry

    def drain(j, carry):
        for _ in range(WAIT_UNROLL):
            _row_copy(ys_ref, 0, buf_ref.at[0], 0, sem).wait()
        return carry

    lax.fori_loop(0, COMBINE_ROWS, issue, 0, unroll=ISSUE_UNROLL)
    lax.fori_loop(0, slots // WAIT_UNROLL, drain, 0)
    w = w_ref[...]
    acc = w[:, 0:1] * buf_ref[0]
    for k in range(1, TOP_K):
        acc = acc + w[:, k:k + 1] * buf_ref[k]
    o_ref[...] = x1_ref[...] + mod_ref[0][5:6] * acc


def _combine(ys, pos, topw, x1, mod, n_ctx_rows, lat_len):
    t = x1.shape[0]
    rb = COMBINE_ROWS
    n_ctx_blocks = n_ctx_rows // rb

    def mod_row(i, pos_ref):
        return (jnp.where(i < n_ctx_blocks, 0, 1 + (i - n_ctx_blocks) // (lat_len // rb)), 0, 0)

    return pl.pallas_call(
        _combine_kernel,
        grid_spec=pltpu.PrefetchScalarGridSpec(
            num_scalar_prefetch=1,
            grid=(t // rb,),
            in_specs=[
                pl.BlockSpec(memory_space=pl.ANY),
                pl.BlockSpec((rb, EXPERT_LANES), lambda i, pos_ref: (i, 0)),
                pl.BlockSpec((rb, D_MODEL), lambda i, pos_ref: (i, 0)),
                pl.BlockSpec((1, 8, D_MODEL), mod_row),
            ],
            out_specs=pl.BlockSpec((rb, D_MODEL), lambda i, pos_ref: (i, 0)),
            scratch_shapes=[pltpu.VMEM((TOP_K, rb, D_MODEL), F32), pltpu.SemaphoreType.DMA],
        ),
        out_shape=jax.ShapeDtypeStruct((t, D_MODEL), F32),
        compiler_params=_cparams("arbitrary"),
        name="combine",
    )(pos, ys, topw, x1, mod)


def _rope_tables(n_lat_tokens, rot_dim):
    n_rows = n_lat_tokens // GRID_W
    row = jnp.repeat(jnp.arange(n_rows, dtype=F32), GRID_W)
    col = jnp.tile(jnp.arange(GRID_W, dtype=F32), n_rows)
    n_freq = rot_dim // 4
    inv_freq = ROPE_THETA ** (-jnp.arange(n_freq, dtype=F32) / n_freq)
    ang = jnp.concatenate([row[:, None] * inv_freq, col[:, None] * inv_freq], axis=-1)
    return jnp.cos(ang), jnp.sin(ang)


def _token_tables(cos_lane, sin_lane, n_ctx_rows, n_lat_seqs):
    width = cos_lane.shape[1]
    cos = jnp.concatenate([jnp.ones((n_ctx_rows, width), F32)] + [cos_lane] * n_lat_seqs, axis=0)
    sin = jnp.concatenate([jnp.zeros((n_ctx_rows, width), F32)] + [sin_lane] * n_lat_seqs, axis=0)
    return cos, sin


def _pack_heads(w, n_heads, lo, hi, width):
    k = w.shape[0]
    per_head = w.shape[1] // n_heads
    part = w.reshape(k, n_heads, per_head)[:, :, lo:hi]
    part = jnp.pad(part, ((0, 0), (0, 0), (0, width - (hi - lo))))
    return part.reshape(k, n_heads * width)


def kernel(x_prompt, x_sample, c, cache_diff_k, cache_diff_v, cache_mla_ckv, cache_mla_krope, c_ctx, w_ada, b_ada,
           w_in, pool_w, pool_scale, diff_q_norm, diff_k_norm, diff_lambda, diff_out_norm, mla_q_a_norm, w_uq,
           mla_kv_a_norm, w_ukv, mla_q_norm, mla_k_norm, w_branch, w_out, router_w, router_b, moe_w_gu, moe_b_gu,
           moe_w_down, moe_b_down):
    batch, seq, d = x_prompt.shape
    dec_batch, dec_seq, _ = x_sample.shape
    depth = w_ada.shape[0]
    past = cache_diff_k.shape[2]
    n_ctx = batch * seq
    n_lat = dec_batch * dec_seq
    n_ctx_blocks = n_ctx // ROW_BLOCK
    mod_spec = _mod_spec(n_ctx_blocks, dec_seq // ROW_BLOCK)

    x = jnp.concatenate([x_prompt.reshape(n_ctx, d), x_sample.reshape(n_lat, d)], axis=0)
    cond8 = jnp.concatenate([c_ctx[None], c, jnp.zeros((8 - 1 - dec_batch, d), F32)], axis=0)
    mod_all = _ada(cond8, w_ada, b_ada).reshape(depth, 8, 6, d)[:, :1 + dec_batch]
    mod_all = jnp.pad(mod_all, ((0, 0), (0, 0), (0, 2), (0, 0)))

    cos_d, sin_d = _rope_tables(dec_seq, DIFF_HD)
    cos_m, sin_m = _rope_tables(dec_seq, MLA_ROPE)
    cos_a, sin_a = _token_tables(jnp.concatenate([cos_d] * 4, axis=1),
                                 jnp.concatenate([-sin_d, sin_d, -sin_d, sin_d], axis=1), n_ctx, dec_batch)
    one, zero = jnp.ones((dec_seq, MLA_NOPE), F32), jnp.zeros((dec_seq, MLA_NOPE), F32)
    cos_b, sin_b = _token_tables(jnp.concatenate([one, cos_m, cos_m, one[:, :64]], axis=1),
                                 jnp.concatenate([zero, -sin_m, sin_m, zero[:, :64]], axis=1), n_ctx, dec_batch)

    offs = np.cumsum((0, 1024, 1024, 1024, 1024, MLA_Q_RANK, MLA_KV_RANK, MLA_ROPE, 3 * 1024))
    new_dk, new_dv, new_ckv, new_kr = [], [], [], []
    for l in range(depth):
        lam_init = 0.8 - 0.6 * math.exp(-0.3 * l)
        wl = w_in[l]
        seg = [wl[:, offs[k]:offs[k + 1]] for k in range(8)]
        w_a = jnp.concatenate(seg[0:4] + [seg[7]], axis=1).astype(BF16)
        kr_wide = jnp.pad(seg[6], ((0, 0), (MLA_NOPE, MLA_W - MLA_NOPE - MLA_ROPE)))
        w_b = jnp.concatenate([seg[4], seg[5], kr_wide], axis=1).astype(BF16)
        wuq = _pack_heads(w_uq[l], MLA_HEADS, 0, MLA_QK, MLA_W).astype(BF16)
        wkn = _pack_heads(w_ukv[l], MLA_HEADS, 0, MLA_NOPE, MLA_W).astype(BF16)
        wv = _pack_heads(w_ukv[l], MLA_HEADS, MLA_NOPE, MLA_NOPE + HEAD_W, HEAD_W).astype(BF16)
        gq_d = jnp.tile(diff_q_norm[l], 2 * DIFF_HEADS)[None]
        gk_d = jnp.tile(diff_k_norm[l], 2 * DIFF_HEADS)[None]
        gq_m = jnp.tile(jnp.pad(mla_q_norm[l], (0, MLA_W - MLA_QK)), MLA_HEADS)[None]
        gk_m = jnp.tile(jnp.pad(mla_k_norm[l], (0, MLA_W - MLA_QK)), MLA_HEADS)[None]
        lp = diff_lambda[l]
        lam = (jnp.exp(jnp.sum(lp[0] * lp[1])) - jnp.exp(jnp.sum(lp[2] * lp[3])) + lam_init).reshape(1)
        out_gain = (diff_out_norm[l] * (1.0 - lam_init))[None]
        mod = mod_all[l]

        u_pool, dq, dk_f, dk_b, dv_f, dv_b, gates = _proj_a(x, mod, w_a, gq_d, gk_d, cos_a, sin_a, mod_spec,
                                                            n_ctx_blocks)
        mq, ckv_n, kr, mk, mv = _proj_b(x, mod, w_b, mla_q_a_norm[l][None], mla_kv_a_norm[l][None], wuq, wkn, wv,
                                        gq_m, gk_m, cos_b, sin_b, mod_spec, n_ctx_blocks)
        new_dk.append(dk_f.reshape(batch, seq, DIFF_HEADS, HEAD_W))
        new_dv.append(dv_f.reshape(batch, seq, DIFF_HEADS, HEAD_W))
        new_ckv.append(ckv_n.reshape(batch, seq, MLA_KV_RANK))
        new_kr.append(kr.reshape(batch, seq, MLA_ROPE))

        y_pool = _pool(u_pool, pool_w[l].astype(BF16), pool_scale[l][None], n_ctx_blocks, seq, dec_seq)

        c_krw = jnp.pad(cache_mla_krope[:, l].reshape(dec_batch * past, MLA_ROPE),
                        ((0, 0), (MLA_NOPE, MLA_W - MLA_NOPE - MLA_ROPE)))
        c_mk, c_mv = _cache_kv(cache_mla_ckv[:, l].reshape(dec_batch * past, MLA_KV_RANK), c_krw, wkn, wv, gk_m)

        geom = (n_ctx, seq, dec_seq)
        y_diff = _diff_attention(dq, dk_b, dv_b, cache_diff_k[:, l].reshape(dec_batch, past, d),
                                 cache_diff_v[:, l].reshape(dec_batch, past, d), lam, out_gain, geom)
        y_mla = _mla_attention(mq, mk, mv, c_mk.reshape(dec_batch, past, -1), c_mv.reshape(dec_batch, past, -1),
                               geom)

        rw = jnp.pad(router_w[l], ((0, 0), (0, EXPERT_LANES - N_EXPERTS)))
        rb = jnp.pad(router_b[l], (0, EXPERT_LANES - N_EXPERTS), constant_values=NEG_BIG)[None]
        x1, h2, topw, topi, rank, counts = _merge(x, mod, y_pool, y_diff, y_mla, gates, w_branch[l].astype(BF16),
                                                  w_out[l].astype(BF16), rw, rb, mod_spec, n_ctx_blocks)
        pos, tile_expert, tile_src, n_used, fill_tile = _slot_layout(counts, topi, rank)
        hs = _dispatch(h2, pos, fill_tile)
        ys = _experts(hs, tile_expert, tile_src, n_used, l, moe_w_gu, moe_b_gu[l][:, None, :],
                      moe_w_down, moe_b_down[l][:, None, :])
        x = _combine(ys, pos, topw, x1, mod, n_ctx, dec_seq)

    return (x[:n_ctx].reshape(batch, seq, d), x[n_ctx:].reshape(dec_batch, dec_seq, d),
            jnp.stack(new_dk, axis=1), jnp.stack(new_dv, axis=1), jnp.stack(new_ckv, axis=1),
            jnp.stack(new_kr, axis=1))
```

```python
import functools
import math

import jax
import jax.numpy as jnp
import numpy as np
from jax import lax
from jax.experimental import pallas as pl
from jax.experimental.pallas import tpu as pltpu

F32 = jnp.float32
BF16 = jnp.bfloat16

D_MODEL = 1024
N_POOL_GROUPS = 4
POOL_GROUP = D_MODEL // N_POOL_GROUPS
DIFF_HD = 64
DIFF_HEADS = 8
HEAD_W = 128
MLA_HEADS = 8
MLA_NOPE = 128
MLA_ROPE = 64
MLA_QK = MLA_NOPE + MLA_ROPE
MLA_W = 256
MLA_Q_RANK = 768
MLA_KV_RANK = 512
N_EXPERTS = 32
TOP_K = 4
D_FF = 1024
EXPERT_LANES = 128
SWIGLU_ALPHA = 1.702
SWIGLU_LIMIT = 7.0
ROPE_THETA = 10000.0
NORM_EPS = 1e-6
GRID_W = 64
LOG2E = math.log2(math.e)
NEG_BIG = -1e30
VMEM_LIMIT = 56 * 1024 * 1024

ROW_BLOCK = 256
POOL_HALO = 16
MOE_TILE = 256
DISPATCH_ROWS = 256
COMBINE_ROWS = 128
WAIT_UNROLL = 16
ISSUE_UNROLL = 4
ATTN_Q_BLOCK = 1024
ATTN_K_CHUNK = 512

_NT = (((1,), (1,)), ((), ()))
_TN = (((0,), (0,)), ((), ()))


def _cparams(*sem):
    return pltpu.CompilerParams(dimension_semantics=sem, vmem_limit_bytes=VMEM_LIMIT)


def _resident(shape):
    nd = len(shape)
    return pl.BlockSpec(shape, lambda *_: (0,) * nd, pipeline_mode=pl.Buffered(1))


def _sigmoid(x):
    return 1.0 / (1.0 + jnp.exp(-x))


def _lane_iota(shape):
    return lax.broadcasted_iota(jnp.int32, shape, len(shape) - 1)


def _rms(x, width):
    ss = jnp.sum(x * x, axis=-1, keepdims=True)
    return x * lax.rsqrt(ss * (1.0 / width) + NORM_EPS)


def _rope(x, cos, sin):
    n = x.shape[-1]
    lane = _lane_iota(x.shape)
    partner = jnp.where((lane & 32) != 0, pltpu.roll(x, 32, 1), pltpu.roll(x, n - 32, 1))
    return x * cos + partner * sin


def _modulated_norm(x, shift, scale):
    return _rms(x, D_MODEL) * (1.0 + scale) + shift


def _ada_kernel(cond_ref, w_ref, b_ref, o_ref):
    cnd = cond_ref[...]
    act = cnd * _sigmoid(cnd)
    o_ref[0] = jnp.dot(act, w_ref[0], preferred_element_type=F32) + b_ref[0]


def _ada(cond8, w_ada, b_ada):
    depth = w_ada.shape[0]
    n_chunk = w_ada.shape[2] // D_MODEL
    return pl.pallas_call(
        _ada_kernel,
        grid=(depth, n_chunk),
        in_specs=[
            pl.BlockSpec((8, D_MODEL), lambda l, j: (0, 0)),
            pl.BlockSpec((1, D_MODEL, D_MODEL), lambda l, j: (l, 0, j)),
            pl.BlockSpec((1, 1, D_MODEL), lambda l, j: (l, 0, j)),
        ],
        out_specs=pl.BlockSpec((1, 8, D_MODEL), lambda l, j: (l, 0, j)),
        out_shape=jax.ShapeDtypeStruct((depth, 8, n_chunk * D_MODEL), F32),
        compiler_params=_cparams("arbitrary", "arbitrary"),
        name="ada",
    )(cond8, w_ada, b_ada.reshape(depth, 1, -1))


def _diff_qk_norm(u, gain, cos, sin):
    outs = []
    for h in range(DIFF_HEADS):
        c = u[:, h * HEAD_W:(h + 1) * HEAD_W]
        lane = _lane_iota(c.shape)
        sq = c * c
        lo = jnp.sum(jnp.where(lane < DIFF_HD, sq, 0.0), axis=-1, keepdims=True)
        hi = jnp.sum(jnp.where(lane >= DIFF_HD, sq, 0.0), axis=-1, keepdims=True)
        ms = jnp.where(lane < DIFF_HD, lo, hi) * (1.0 / DIFF_HD)
        y = c * lax.rsqrt(ms + NORM_EPS) * gain[:, h * HEAD_W:(h + 1) * HEAD_W]
        outs.append((y, _rope(y, cos, sin)))
    return outs


def _proj_a_kernel(x_ref, mod_ref, w_ref, gq_ref, gk_ref, cos_ref, sin_ref,
                   pool_ref, dq_ref, dkf_ref, dkb_ref, dvf_ref, dvb_ref, gate_ref, *, n_ctx_blocks):
    is_ctx = pl.program_id(0) < n_ctx_blocks
    mod = mod_ref[0]
    h = _modulated_norm(x_ref[...], mod[0:1], mod[1:2]).astype(BF16)
    cos, sin = cos_ref[...], sin_ref[...]

    def seg(k):
        return jnp.dot(h, w_ref[:, k * D_MODEL:(k + 1) * D_MODEL], preferred_element_type=F32)

    pool_ref[...] = seg(0).astype(BF16)
    q_scale = DIFF_HD ** -0.5 * LOG2E
    for hd, (_, roped) in enumerate(_diff_qk_norm(seg(1), gq_ref[...], cos, sin)):
        dq_ref[:, hd * HEAD_W:(hd + 1) * HEAD_W] = (roped * q_scale).astype(BF16)
    dk_heads = _diff_qk_norm(seg(2), gk_ref[...], cos, sin)
    for hd, (_, roped) in enumerate(dk_heads):
        dkb_ref[:, hd * HEAD_W:(hd + 1) * HEAD_W] = roped.astype(BF16)
    dv = seg(3)
    dvb_ref[...] = dv.astype(BF16)

    @pl.when(is_ctx)
    def _():
        for hd, (plain, _) in enumerate(dk_heads):
            dkf_ref[:, hd * HEAD_W:(hd + 1) * HEAD_W] = plain
        dvf_ref[...] = dv

    for r in range(3):
        gate_ref[:, r * D_MODEL:(r + 1) * D_MODEL] = _sigmoid(seg(4 + r)).astype(BF16)


def _mla_heads(u, gain, cos, sin, rope):
    outs = []
    for h in range(MLA_HEADS):
        c = u[:, h * MLA_W:(h + 1) * MLA_W]
        y = _rms(c, MLA_QK) * gain[:, h * MLA_W:(h + 1) * MLA_W]
        outs.append(_rope(y, cos, sin) if rope else y)
    return outs


def _mla_kv(ckv_n, krw, wkn_ref, wv_ref, gk, cos, sin, rope, k_ref, v_ref):
    kn = jnp.dot(ckv_n, wkn_ref[...], preferred_element_type=F32)
    kfull = jnp.concatenate([kn[:, h * MLA_W:(h + 1) * MLA_W] + krw for h in range(MLA_HEADS)], axis=-1)
    for hd, y in enumerate(_mla_heads(kfull, gk, cos, sin, rope)):
        k_ref[:, hd * MLA_W:(hd + 1) * MLA_W] = y.astype(BF16)
    v_ref[...] = jnp.dot(ckv_n, wv_ref[...], preferred_element_type=F32).astype(BF16)


def _proj_b_kernel(x_ref, mod_ref, w_ref, gqa_ref, gkva_ref, wuq_ref, wkn_ref, wv_ref, gq_ref, gk_ref,
                   cos_ref, sin_ref, mq_ref, ckv_ref, kr_ref, mk_ref, mv_ref, *, n_ctx_blocks):
    is_ctx = pl.program_id(0) < n_ctx_blocks
    mod = mod_ref[0]
    h = _modulated_norm(x_ref[...], mod[0:1], mod[1:2]).astype(BF16)
    cos, sin = cos_ref[...], sin_ref[...]
    cq = jnp.dot(h, w_ref[:, :MLA_Q_RANK], preferred_element_type=F32)
    cq_n = (_rms(cq, MLA_Q_RANK) * gqa_ref[...]).astype(BF16)
    mq = jnp.dot(cq_n, wuq_ref[...], preferred_element_type=F32)
    q_scale = MLA_QK ** -0.5 * LOG2E
    for hd, y in enumerate(_mla_heads(mq, gq_ref[...], cos, sin, True)):
        mq_ref[:, hd * MLA_W:(hd + 1) * MLA_W] = (y * q_scale).astype(BF16)
    ckv = jnp.dot(h, w_ref[:, MLA_Q_RANK:MLA_Q_RANK + MLA_KV_RANK], preferred_element_type=F32)
    ckv_n = _rms(ckv, MLA_KV_RANK) * gkva_ref[...]
    krw = jnp.dot(h, w_ref[:, MLA_Q_RANK + MLA_KV_RANK:], preferred_element_type=F32)

    @pl.when(is_ctx)
    def _():
        ckv_ref[...] = ckv_n
        kr_ref[...] = krw[:, MLA_NOPE:MLA_NOPE + MLA_ROPE]

    _mla_kv(ckv_n.astype(BF16), krw, wkn_ref, wv_ref, gk_ref[...], cos, sin, True, mk_ref, mv_ref)


def _cache_kv_kernel(ckv_ref, krw_ref, wkn_ref, wv_ref, gk_ref, mk_ref, mv_ref):
    _mla_kv(ckv_ref[...].astype(BF16), krw_ref[...], wkn_ref, wv_ref, gk_ref[...], None, None, False,
            mk_ref, mv_ref)


def _mod_spec(n_ctx_blocks, blocks_per_lat):
    def row(i):
        return jnp.where(i < n_ctx_blocks, 0, 1 + (i - n_ctx_blocks) // blocks_per_lat)
    return pl.BlockSpec((1, 8, D_MODEL), lambda i: (row(i), 0, 0))


def _rows(width):
    return pl.BlockSpec((ROW_BLOCK, width), lambda i: (i, 0))


def _ctx_rows(width, n_ctx_blocks):
    return pl.BlockSpec((ROW_BLOCK, width), lambda i: (jnp.minimum(i, n_ctx_blocks - 1), 0))


def _proj_outputs(t, n_ctx_blocks, widths, dtypes, ctx_only):
    specs = [_ctx_rows(w, n_ctx_blocks) if c else _rows(w) for w, c in zip(widths, ctx_only)]
    shapes = [jax.ShapeDtypeStruct((n_ctx_blocks * ROW_BLOCK if c else t, w), dt)
              for w, dt, c in zip(widths, dtypes, ctx_only)]
    return specs, shapes


def _proj_a(x, mod, w_a, gq, gk, cos, sin, mod_spec, n_ctx_blocks):
    t = x.shape[0]
    widths = (D_MODEL,) * 6 + (3 * D_MODEL,)
    dtypes = (BF16, BF16, F32, BF16, F32, BF16, BF16)
    out_specs, out_shape = _proj_outputs(t, n_ctx_blocks, widths, dtypes, (0, 0, 1, 0, 1, 0, 0))
    return pl.pallas_call(
        functools.partial(_proj_a_kernel, n_ctx_blocks=n_ctx_blocks),
        grid=(t // ROW_BLOCK,),
        in_specs=[_rows(D_MODEL), mod_spec, _resident(w_a.shape), _resident(gq.shape), _resident(gk.shape),
                  _rows(HEAD_W), _rows(HEAD_W)],
        out_specs=out_specs,
        out_shape=out_shape,
        compiler_params=_cparams("arbitrary"),
        name="proj_a",
    )(x, mod, w_a, gq, gk, cos, sin)


def _proj_b(x, mod, w_b, gqa, gkva, wuq, wkn, wv, gq, gk, cos, sin, mod_spec, n_ctx_blocks):
    t = x.shape[0]
    widths = (MLA_HEADS * MLA_W, MLA_KV_RANK, MLA_ROPE, MLA_HEADS * MLA_W, D_MODEL)
    dtypes = (BF16, F32, F32, BF16, BF16)
    out_specs, out_shape = _proj_outputs(t, n_ctx_blocks, widths, dtypes, (0, 1, 1, 0, 0))
    consts = (w_b, gqa, gkva, wuq, wkn, wv, gq, gk)
    return pl.pallas_call(
        functools.partial(_proj_b_kernel, n_ctx_blocks=n_ctx_blocks),
        grid=(t // ROW_BLOCK,),
        in_specs=[_rows(D_MODEL), mod_spec] + [_resident(a.shape) for a in consts] + [_rows(MLA_W), _rows(MLA_W)],
        out_specs=out_specs,
        out_shape=out_shape,
        compiler_params=_cparams("arbitrary"),
        name="proj_b",
    )(x, mod, *consts, cos, sin)


def _cache_kv(ckv, krw, wkn, wv, gk):
    n = ckv.shape[0]
    widths = (MLA_HEADS * MLA_W, D_MODEL)
    consts = (wkn, wv, gk)
    return pl.pallas_call(
        _cache_kv_kernel,
        grid=(n // ROW_BLOCK,),
        in_specs=[_rows(MLA_KV_RANK), _rows(MLA_W)] + [_resident(a.shape) for a in consts],
        out_specs=[_rows(w) for w in widths],
        out_shape=[jax.ShapeDtypeStruct((n, w), BF16) for w in widths],
        compiler_params=_cparams("arbitrary"),
        name="cache_kv",
    )(ckv, krw, *consts)


def _pool_kernel(prev_ref, cur_ref, next_ref, w_ref, scale_ref, o_ref, *, n_ctx_blocks, ctx_len, lat_len):
    i = pl.program_id(0)
    rb = ROW_BLOCK
    is_ctx = i < n_ctx_blocks
    n_ctx_rows = n_ctx_blocks * rb
    seq_len = jnp.where(is_ctx, ctx_len, lat_len)
    seq_start = jnp.where(is_ctx, (i * rb // ctx_len) * ctx_len,
                          n_ctx_rows + ((i * rb - n_ctx_rows) // lat_len) * lat_len)
    n_cols = rb + 2 * POOL_HALO
    row_pos = i * rb - seq_start + lax.broadcasted_iota(jnp.int32, (rb, n_cols), 0)
    col_pos = i * rb - POOL_HALO - seq_start + lax.broadcasted_iota(jnp.int32, (rb, n_cols), 1)
    cur = cur_ref[...]
    u = jnp.concatenate([prev_ref[...], cur, next_ref[...]], axis=0)
    for g in range(N_POOL_GROUPS):
        cols = slice(g * POOL_GROUP, (g + 1) * POOL_GROUP)
        window = 2 << g
        lo = jnp.clip(row_pos - window // 2, 0, seq_len)
        hi = jnp.clip(row_pos - window // 2 + window, 0, seq_len)
        band = jnp.where((col_pos >= lo) & (col_pos < hi), 1.0, 0.0).astype(BF16)
        total = jnp.dot(band, u[:, cols], preferred_element_type=F32)
        cnt = (hi[:, :1] - lo[:, :1]).astype(F32)
        pooled = total / cnt - cur[:, cols].astype(F32)
        mixed = jnp.dot(pooled.astype(BF16), w_ref[g], preferred_element_type=F32)
        o_ref[:, cols] = (mixed * scale_ref[:, cols]).astype(BF16)


def _pool(u_pool, pool_w, pool_scale, n_ctx_blocks, ctx_len, lat_len):
    t = u_pool.shape[0]
    halo_per_block = ROW_BLOCK // POOL_HALO
    n_halo_blocks = t // POOL_HALO
    assert POOL_HALO >= (2 << (N_POOL_GROUPS - 1)) // 2
    kern = functools.partial(_pool_kernel, n_ctx_blocks=n_ctx_blocks, ctx_len=ctx_len, lat_len=lat_len)
    return pl.pallas_call(
        kern,
        grid=(t // ROW_BLOCK,),
        in_specs=[
            pl.BlockSpec((POOL_HALO, D_MODEL), lambda i: (jnp.maximum(i * halo_per_block - 1, 0), 0)),
            _rows(D_MODEL),
            pl.BlockSpec((POOL_HALO, D_MODEL),
                         lambda i: (jnp.minimum((i + 1) * halo_per_block, n_halo_blocks - 1), 0)),
            _resident(pool_w.shape),
            _resident(pool_scale.shape),
        ],
        out_specs=_rows(D_MODEL),
        out_shape=jax.ShapeDtypeStruct((t, D_MODEL), BF16),
        compiler_params=_cparams("arbitrary"),
        name="pool",
    )(u_pool, u_pool, u_pool, pool_w, pool_scale)


def _online_softmax_step(s, v, m, l, acc):
    m_new = jnp.maximum(m, jnp.max(s, axis=0, keepdims=True))
    p = jnp.exp2(s - m_new)
    alpha = jnp.exp2(m - m_new)
    l_new = alpha * l + jnp.sum(p, axis=0, keepdims=True)
    acc_new = alpha * acc + lax.dot_general(v, p.astype(BF16), _TN, preferred_element_type=F32)
    return m_new, l_new, acc_new


def _softmax_init(tq, width):
    return (jnp.full((1, tq), -jnp.inf, F32), jnp.zeros((1, tq), F32), jnp.zeros((width, tq), F32))


def _key_value_chunks(kv_refs, head, qk_width):
    k_cols = slice(head * qk_width, (head + 1) * qk_width)
    v_cols = slice(head * HEAD_W, (head + 1) * HEAD_W)
    for k_ref, v_ref in zip(kv_refs[0::2], kv_refs[1::2]):
        n = k_ref.shape[0]
        size = min(n, ATTN_K_CHUNK)
        for start in range(0, n, size):
            yield k_ref[start:start + size, k_cols].astype(BF16), v_ref[start:start + size, v_cols].astype(BF16)


def _scores_one_chunk_ahead(chunks, scores, update, carry):
    chunks = list(chunks)
    s_next = scores(chunks[0][0])
    for c, (_, v) in enumerate(chunks):
        s = s_next
        if c + 1 < len(chunks):
            s_next = scores(chunks[c + 1][0])
        carry = update(s, v, carry)
    return carry


def _diff_attn_kernel(lam_ref, q_ref, *refs, n_kv, heads):
    g_ref, o_ref = refs[n_kv], refs[-1]
    for h in range(heads):
        cols = slice(h * HEAD_W, (h + 1) * HEAD_W)
        q = q_ref[:, cols]
        lane = _lane_iota(q.shape)
        q1 = jnp.where(lane < DIFF_HD, q, jnp.zeros_like(q))
        q2 = jnp.where(lane >= DIFF_HD, q, jnp.zeros_like(q))

        def scores(k):
            return (lax.dot_general(k, q1, _NT, preferred_element_type=F32),
                    lax.dot_general(k, q2, _NT, preferred_element_type=F32))

        def update(s, v, carry):
            return _online_softmax_step(s[0], v, *carry[0]), _online_softmax_step(s[1], v, *carry[1])

        init = (_softmax_init(q.shape[0], HEAD_W),) * 2
        chunks = _key_value_chunks(refs[:n_kv], h, HEAD_W)
        (_, l1, a1), (_, l2, a2) = _scores_one_chunk_ahead(chunks, scores, update, init)
        o = (a1 / l1 - lam_ref[0] * (a2 / l2)).T
        o_ref[:, cols] = (_rms(o, HEAD_W) * g_ref[...]).astype(o_ref.dtype)


def _mla_attn_kernel(q_ref, *refs, n_kv, heads):
    o_ref = refs[-1]
    for h in range(heads):
        q = q_ref[:, h * MLA_W:(h + 1) * MLA_W]
        _, l, acc = _scores_one_chunk_ahead(
            _key_value_chunks(refs[:n_kv], h, MLA_W),
            lambda k: lax.dot_general(k, q, _NT, preferred_element_type=F32),
            lambda s, v, carry: _online_softmax_step(s, v, *carry), _softmax_init(q.shape[0], HEAD_W))
        o_ref[:, h * HEAD_W:(h + 1) * HEAD_W] = (acc / l).T.astype(o_ref.dtype)


def _attention(kernel, q, k, v, cache_k, cache_v, lead, lead_specs, tail, tail_specs, qk_width, geom, name):
    n_ctx, seq, dec_seq = geom
    t = q.shape[0]
    n_heads = v.shape[1] // HEAD_W
    dec_batch, past, _ = cache_k.shape
    tq = ATTN_Q_BLOCK
    nq, q0, k0 = dec_seq // tq, n_ctx // tq, n_ctx // dec_seq
    assert n_ctx % tq == 0 and n_ctx % dec_seq == 0 and dec_seq % tq == 0
    params = _cparams("arbitrary", "arbitrary", "arbitrary")
    seq_block = lambda w: pl.BlockSpec((seq, n_heads * w), lambda b, h, qi: (b, 0))
    ctx = pl.pallas_call(
        functools.partial(kernel, n_kv=2, heads=n_heads),
        grid=(n_ctx // seq, 1, 1),
        in_specs=lead_specs + [seq_block(qk_width), seq_block(qk_width), seq_block(HEAD_W)] + tail_specs,
        out_specs=seq_block(HEAD_W),
        out_shape=jax.ShapeDtypeStruct((n_ctx, n_heads * HEAD_W), BF16),
        compiler_params=params,
        name=name + "_ctx",
    )(*lead, q, k, v, *tail)
    lat_rows = lambda w, first: pl.BlockSpec((tq, w), lambda b, h, qi: (first + b * nq + qi, h))
    cached = lambda w: pl.BlockSpec((None, past, w), lambda b, h, qi: (b, 0, h))
    lat_keys = lambda w: pl.BlockSpec((dec_seq, w), lambda b, h, qi: (k0 + b, h))
    lat = pl.pallas_call(
        functools.partial(kernel, n_kv=4, heads=1),
        grid=(dec_batch, n_heads, nq),
        in_specs=lead_specs + [lat_rows(qk_width, q0), cached(qk_width), cached(HEAD_W), lat_keys(qk_width),
                               lat_keys(HEAD_W)] + tail_specs,
        out_specs=lat_rows(HEAD_W, 0),
        out_shape=jax.ShapeDtypeStruct((t - n_ctx, n_heads * HEAD_W), BF16),
        compiler_params=params,
        name=name + "_lat",
    )(*lead, q, cache_k, cache_v, k, v, *tail)
    return ctx, lat


def _diff_attention(q, k, v, cache_k, cache_v, lam, out_gain, geom):
    lead_specs = [pl.BlockSpec(memory_space=pltpu.SMEM)]
    tail_specs = [pl.BlockSpec((1, HEAD_W), lambda b, h, qi: (0, 0))]
    return _attention(_diff_attn_kernel, q, k, v, cache_k, cache_v, (lam,), lead_specs, (out_gain,), tail_specs,
                      HEAD_W, geom, "diff")


def _mla_attention(q, k, v, cache_k, cache_v, geom):
    return _attention(_mla_attn_kernel, q, k, v, cache_k, cache_v, (), [], (), [], MLA_W, geom, "mla")


def _merge_kernel(x_ref, mod_ref, yp_ref, ydc_ref, ydl_ref, ymc_ref, yml_ref, gate_ref, wb_ref, wo_ref, rw_ref,
                  rb_ref, x1_ref, h2_ref, topw_ref, topi_ref, rank_ref, count_ref, carry_ref, *, n_ctx_blocks):
    @pl.when(pl.program_id(0) == 0)
    def _():
        carry_ref[...] = jnp.zeros_like(carry_ref)

    is_ctx = pl.program_id(0) < n_ctx_blocks
    mod = mod_ref[0]
    merged = jnp.zeros(x_ref.shape, F32)
    branches = (yp_ref[...], jnp.where(is_ctx, ydc_ref[...], ydl_ref[...]),
                jnp.where(is_ctx, ymc_ref[...], yml_ref[...]))
    for r, y in enumerate(branches):
        z = jnp.dot(y, wb_ref[r], preferred_element_type=F32)
        merged = merged + gate_ref[:, r * D_MODEL:(r + 1) * D_MODEL].astype(F32) * z
    y = jnp.dot(merged.astype(BF16), wo_ref[...], preferred_element_type=F32)
    x1 = x_ref[...] + mod[2:3] * y
    x1_ref[...] = x1
    h2 = _modulated_norm(x1, mod[3:4], mod[4:5])
    h2_ref[...] = h2
    logits = jnp.dot(h2, rw_ref[...], preferred_element_type=F32) + rb_ref[...]
    lane = _lane_iota(logits.shape)
    picks = []
    for _ in range(TOP_K):
        mx = jnp.max(logits, axis=-1, keepdims=True)
        first = jnp.min(jnp.where(logits == mx, lane, EXPERT_LANES), axis=-1, keepdims=True)
        hit = lane == first
        picks.append((mx, first, hit))
        logits = jnp.where(hit, -jnp.inf, logits)
    exps = [jnp.exp(v - picks[0][0]) for v, _, _ in picks]
    denom = exps[0] + exps[1] + exps[2] + exps[3]
    rows = logits.shape[0]
    chosen = jnp.zeros(logits.shape, F32)
    for _, _, hit in picks:
        chosen = chosen + jnp.where(hit, 1.0, 0.0)
    earlier = jnp.where(lax.broadcasted_iota(jnp.int32, (rows, rows), 1)
                        < lax.broadcasted_iota(jnp.int32, (rows, rows), 0), 1.0, 0.0).astype(BF16)
    before = jnp.dot(earlier, chosen.astype(BF16), preferred_element_type=F32) + carry_ref[0:1, :]
    topw = jnp.zeros(logits.shape, F32)
    topi = jnp.zeros(logits.shape, jnp.int32)
    rank = jnp.zeros(logits.shape, jnp.int32)
    for k, (e, (_, first, hit)) in enumerate(zip(exps, picks)):
        slot_rank = jnp.sum(jnp.where(hit, before, 0.0), axis=-1, keepdims=True)
        topw = jnp.where(lane == k, e / denom, topw)
        topi = jnp.where(lane == k, first, topi)
        rank = jnp.where(lane == k, slot_rank.astype(jnp.int32), rank)
    topw_ref[...] = topw
    topi_ref[...] = topi
    rank_ref[...] = rank
    carry_ref[...] = carry_ref[...] + jnp.sum(chosen, axis=0, keepdims=True)
    count_ref[...] = carry_ref[...]


def _merge(x, mod, y_pool, y_diff, y_mla, gates, wb, wo, rw, rb, mod_spec, n_ctx_blocks):
    t = x.shape[0]
    consts = (wb, wo, rw, rb)
    lanes = EXPERT_LANES
    ctx_rows = _ctx_rows(D_MODEL, n_ctx_blocks)
    lat_rows = pl.BlockSpec((ROW_BLOCK, D_MODEL), lambda i: (jnp.maximum(i - n_ctx_blocks, 0), 0))
    return pl.pallas_call(
        functools.partial(_merge_kernel, n_ctx_blocks=n_ctx_blocks),
        grid=(t // ROW_BLOCK,),
        in_specs=[_rows(D_MODEL), mod_spec, _rows(D_MODEL), ctx_rows, lat_rows, ctx_rows, lat_rows,
                  _rows(3 * D_MODEL)] + [_resident(a.shape) for a in consts],
        out_specs=[_rows(D_MODEL), _rows(D_MODEL), _rows(lanes), _rows(lanes), _rows(lanes),
                   pl.BlockSpec((8, lanes), lambda i: (0, 0))],
        out_shape=[jax.ShapeDtypeStruct((t, D_MODEL), F32), jax.ShapeDtypeStruct((t, D_MODEL), F32),
                   jax.ShapeDtypeStruct((t, lanes), F32), jax.ShapeDtypeStruct((t, lanes), jnp.int32),
                   jax.ShapeDtypeStruct((t, lanes), jnp.int32), jax.ShapeDtypeStruct((8, lanes), F32)],
        scratch_shapes=[pltpu.VMEM((8, lanes), F32)],
        compiler_params=_cparams("arbitrary"),
        name="merge",
    )(x, mod, y_pool, *y_diff, *y_mla, gates, *consts)


def _slot_layout(counts, topi, rank):
    tile = MOE_TILE
    cnt = counts[0, :N_EXPERTS].astype(jnp.int32)
    padded = (cnt + tile - 1) // tile * tile
    end = jnp.cumsum(padded)
    start = end - padded
    experts = jnp.arange(N_EXPERTS, dtype=jnp.int32)
    idx4 = topi[:, :TOP_K]
    slot_start = jnp.sum(jnp.where(idx4[:, :, None] == experts, start, 0), axis=-1)
    pos = (slot_start + rank[:, :TOP_K]).reshape(-1)
    n_tiles = (topi.shape[0] * TOP_K + N_EXPERTS * tile) // tile
    tile_row = jnp.arange(n_tiles, dtype=jnp.int32) * tile
    tile_expert = jnp.minimum(jnp.sum(tile_row[:, None] >= end, axis=-1), N_EXPERTS - 1).astype(jnp.int32)
    n_used = (end[-1] // tile).astype(jnp.int32)
    tile_src = jnp.minimum(jnp.arange(n_tiles, dtype=jnp.int32), n_used - 1)
    ragged_last = jnp.any((cnt % tile != 0) & (tile_row[:, None] == end - tile), axis=-1)
    fill_tile = (ragged_last | (tile_row >= end[-1])).astype(jnp.int32)
    return pos, tile_expert, tile_src, n_used.reshape(1), fill_tile


def _row_copy(src, src_row, dst, dst_row, sem):
    return pltpu.make_async_copy(src.at[pl.ds(src_row, 1)], dst.at[pl.ds(dst_row, 1)], sem)


def _dispatch_kernel(fill_tile_ref, pos_ref, h_ref, hs_ref, zero_ref, fill_sem, row_sem):
    i = pl.program_id(0)
    slots = DISPATCH_ROWS * TOP_K

    @pl.when(i == 0)
    def _():
        zero_ref[...] = jnp.zeros_like(zero_ref)

        def fill_copy(tile):
            row = pl.multiple_of(tile * MOE_TILE, MOE_TILE)
            return pltpu.make_async_copy(zero_ref, hs_ref.at[pl.ds(row, MOE_TILE)], fill_sem)

        def start_fill(tile, carry):
            @pl.when(fill_tile_ref[tile] != 0)
            def _():
                fill_copy(tile).start()
            return carry

        def wait_fill(tile, carry):
            @pl.when(fill_tile_ref[tile] != 0)
            def _():
                fill_copy(tile).wait()
            return carry

        lax.fori_loop(0, hs_ref.shape[0] // MOE_TILE, start_fill, 0)
        lax.fori_loop(0, hs_ref.shape[0] // MOE_TILE, wait_fill, 0)

    def issue(r, carry):
        slot = (i * DISPATCH_ROWS + r) * TOP_K
        for k in range(TOP_K):
            _row_copy(h_ref, r, hs_ref, pos_ref[slot + k], row_sem).start()
        return carry

    def drain(j, carry):
        for _ in range(WAIT_UNROLL):
            _row_copy(h_ref, 0, hs_ref, 0, row_sem).wait()
        return carry

    lax.fori_loop(0, DISPATCH_ROWS, issue, 0, unroll=ISSUE_UNROLL)
    lax.fori_loop(0, slots // WAIT_UNROLL, drain, 0)


def _dispatch(h2, pos, fill_tile):
    t = h2.shape[0]
    n_rows = fill_tile.shape[0] * MOE_TILE
    return pl.pallas_call(
        _dispatch_kernel,
        grid_spec=pltpu.PrefetchScalarGridSpec(
            num_scalar_prefetch=2,
            grid=(t // DISPATCH_ROWS,),
            in_specs=[pl.BlockSpec((DISPATCH_ROWS, D_MODEL), lambda i, fill_ref, pos_ref: (i, 0))],
            out_specs=pl.BlockSpec(memory_space=pl.ANY),
            scratch_shapes=[pltpu.VMEM((MOE_TILE, D_MODEL), F32), pltpu.SemaphoreType.DMA,
                            pltpu.SemaphoreType.DMA],
        ),
        out_shape=jax.ShapeDtypeStruct((n_rows, D_MODEL), F32),
        compiler_params=_cparams("arbitrary"),
        name="dispatch",
    )(fill_tile, pos, h2)


def _expert_kernel(tile_expert_ref, tile_src_ref, n_used_ref, hs_ref, wgu_ref, bgu_ref, wd_ref, bd_ref, ys_ref,
                   wgu_bf_ref, wd_bf_ref):
    i = pl.program_id(0)
    occupied = i < n_used_ref[0]
    new_expert = (i == 0) | (tile_expert_ref[i] != tile_expert_ref[jnp.maximum(i - 1, 0)])

    @pl.when(occupied & new_expert)
    def _():
        wgu_bf_ref[...] = wgu_ref[...].astype(BF16)
        wd_bf_ref[...] = wd_ref[...].astype(BF16)

    @pl.when(occupied)
    def _():
        gu = jnp.dot(hs_ref[...].astype(BF16), wgu_bf_ref[...], preferred_element_type=F32) + bgu_ref[0]
        gate = jnp.minimum(gu[:, :D_FF], SWIGLU_LIMIT)
        up = jnp.clip(gu[:, D_FF:], -SWIGLU_LIMIT, SWIGLU_LIMIT)
        act = (up + 1.0) * gate * _sigmoid(SWIGLU_ALPHA * gate)
        ys_ref[...] = jnp.dot(act.astype(BF16), wd_bf_ref[...], preferred_element_type=F32) + bd_ref[0]

    @pl.when(jnp.logical_not(occupied))
    def _():
        ys_ref[...] = jnp.zeros_like(ys_ref)


def _experts(hs, tile_expert, tile_src, n_used, layer, wgu, bgu, wd, bd):
    n_rows = hs.shape[0]
    rows = lambda i, te, ts, nu: (ts[i], 0)
    bias = lambda i, te, ts, nu: (te[i], 0, 0)
    weight = lambda i, te, ts, nu: (layer, te[i], 0, 0)
    return pl.pallas_call(
        _expert_kernel,
        grid_spec=pltpu.PrefetchScalarGridSpec(
            num_scalar_prefetch=3,
            grid=(n_rows // MOE_TILE,),
            in_specs=[
                pl.BlockSpec((MOE_TILE, D_MODEL), rows),
                pl.BlockSpec((None, None, D_MODEL, 2 * D_FF), weight),
                pl.BlockSpec((1, 1, 2 * D_FF), bias),
                pl.BlockSpec((None, None, D_FF, D_MODEL), weight),
                pl.BlockSpec((1, 1, D_MODEL), bias),
            ],
            out_specs=pl.BlockSpec((MOE_TILE, D_MODEL), lambda i, te, ts, nu: (i, 0)),
            scratch_shapes=[pltpu.VMEM((D_MODEL, 2 * D_FF), BF16), pltpu.VMEM((D_FF, D_MODEL), BF16)],
        ),
        out_shape=jax.ShapeDtypeStruct((n_rows, D_MODEL), F32),
        compiler_params=_cparams("arbitrary"),
        name="experts",
    )(tile_expert, tile_src, n_used, hs, wgu, bgu, wd, bd)


def _combine_kernel(pos_ref, ys_ref, w_ref, x1_ref, mod_ref, o_ref, buf_ref, sem):
    i = pl.program_id(0)
    slots = COMBINE_ROWS * TOP_K

    def issue(r, carry):
        token = i * COMBINE_ROWS + r
        for k in range(TOP_K):
            _row_copy(ys_ref, pos_ref[token * TOP_K + k], buf_ref.at[k], r, sem).start()
        return carry

    def drain(j, carry):
        for _ in range(WAIT_UNROLL):
            _row_copy(ys_ref, 0, buf_ref.at[0], 0, sem).wait()
        return carry

    lax.fori_loop(0, COMBINE_ROWS, issue, 0, unroll=ISSUE_UNROLL)
    lax.fori_loop(0, slots // WAIT_UNROLL, drain, 0)
    w = w_ref[...]
    acc = w[:, 0:1] * buf_ref[0]
    for k in range(1, TOP_K):
        acc = acc + w[:, k:k + 1] * buf_ref[k]
    o_ref[...] = x1_ref[...] + mod_ref[0][5:6] * acc


def _combine(ys, pos, topw, x1, mod, n_ctx_rows, lat_len):
    t = x1.shape[0]
    rb = COMBINE_ROWS
    n_ctx_blocks = n_ctx_rows // rb

    def mod_row(i, pos_ref):
        return (jnp.where(i < n_ctx_blocks, 0, 1 + (i - n_ctx_blocks) // (lat_len // rb)), 0, 0)

    return pl.pallas_call(
        _combine_kernel,
        grid_spec=pltpu.PrefetchScalarGridSpec(
            num_scalar_prefetch=1,
            grid=(t // rb,),
            in_specs=[
                pl.BlockSpec(memory_space=pl.ANY),
                pl.BlockSpec((rb, EXPERT_LANES), lambda i, pos_ref: (i, 0)),
                pl.BlockSpec((rb, D_MODEL), lambda i, pos_ref: (i, 0)),
                pl.BlockSpec((1, 8, D_MODEL), mod_row),
            ],
            out_specs=pl.BlockSpec((rb, D_MODEL), lambda i, pos_ref: (i, 0)),
            scratch_shapes=[pltpu.VMEM((TOP_K, rb, D_MODEL), F32), pltpu.SemaphoreType.DMA],
        ),
        out_shape=jax.ShapeDtypeStruct((t, D_MODEL), F32),
        compiler_params=_cparams("arbitrary"),
        name="combine",
    )(pos, ys, topw, x1, mod)


def _rope_tables(n_lat_tokens, rot_dim):
    n_rows = n_lat_tokens // GRID_W
    row = jnp.repeat(jnp.arange(n_rows, dtype=F32), GRID_W)
    col = jnp.tile(jnp.arange(GRID_W, dtype=F32), n_rows)
    n_freq = rot_dim // 4
    inv_freq = ROPE_THETA ** (-jnp.arange(n_freq, dtype=F32) / n_freq)
    ang = jnp.concatenate([row[:, None] * inv_freq, col[:, None] * inv_freq], axis=-1)
    return jnp.cos(ang), jnp.sin(ang)


def _token_tables(cos_lane, sin_lane, n_ctx_rows, n_lat_seqs):
    width = cos_lane.shape[1]
    cos = jnp.concatenate([jnp.ones((n_ctx_rows, width), F32)] + [cos_lane] * n_lat_seqs, axis=0)
    sin = jnp.concatenate([jnp.zeros((n_ctx_rows, width), F32)] + [sin_lane] * n_lat_seqs, axis=0)
    return cos, sin


def _pack_heads(w, n_heads, lo, hi, width):
    k = w.shape[0]
    per_head = w.shape[1] // n_heads
    part = w.reshape(k, n_heads, per_head)[:, :, lo:hi]
    part = jnp.pad(part, ((0, 0), (0, 0), (0, width - (hi - lo))))
    return part.reshape(k, n_heads * width)


def kernel(x_prompt, x_sample, c, cache_diff_k, cache_diff_v, cache_mla_ckv, cache_mla_krope, c_ctx, w_ada, b_ada,
           w_in, pool_w, pool_scale, diff_q_norm, diff_k_norm, diff_lambda, diff_out_norm, mla_q_a_norm, w_uq,
           mla_kv_a_norm, w_ukv, mla_q_norm, mla_k_norm, w_branch, w_out, router_w, router_b, moe_w_gu, moe_b_gu,
           moe_w_down, moe_b_down):
    batch, seq, d = x_prompt.shape
    dec_batch, dec_seq, _ = x_sample.shape
    depth = w_ada.shape[0]
    past = cache_diff_k.shape[2]
    n_ctx = batch * seq
    n_lat = dec_batch * dec_seq
    n_ctx_blocks = n_ctx // ROW_BLOCK
    mod_spec = _mod_spec(n_ctx_blocks, dec_seq // ROW_BLOCK)

    x = jnp.concatenate([x_prompt.reshape(n_ctx, d), x_sample.reshape(n_lat, d)], axis=0)
    cond8 = jnp.concatenate([c_ctx[None], c, jnp.zeros((8 - 1 - dec_batch, d), F32)], axis=0)
    mod_all = _ada(cond8, w_ada, b_ada).reshape(depth, 8, 6, d)[:, :1 + dec_batch]
    mod_all = jnp.pad(mod_all, ((0, 0), (0, 0), (0, 2), (0, 0)))

    cos_d, sin_d = _rope_tables(dec_seq, DIFF_HD)
    cos_m, sin_m = _rope_tables(dec_seq, MLA_ROPE)
    cos_a, sin_a = _token_tables(jnp.concatenate([cos_d] * 4, axis=1),
                                 jnp.concatenate([-sin_d, sin_d, -sin_d, sin_d], axis=1), n_ctx, dec_batch)
    one, zero = jnp.ones((dec_seq, MLA_NOPE), F32), jnp.zeros((dec_seq, MLA_NOPE), F32)
    cos_b, sin_b = _token_tables(jnp.concatenate([one, cos_m, cos_m, one[:, :64]], axis=1),
                                 jnp.concatenate([zero, -sin_m, sin_m, zero[:, :64]], axis=1), n_ctx, dec_batch)

    offs = np.cumsum((0, 1024, 1024, 1024, 1024, MLA_Q_RANK, MLA_KV_RANK, MLA_ROPE, 3 * 1024))
    new_dk, new_dv, new_ckv, new_kr = [], [], [], []
    for l in range(depth):
        lam_init = 0.8 - 0.6 * math.exp(-0.3 * l)
        wl = w_in[l]
        seg = [wl[:, offs[k]:offs[k + 1]] for k in range(8)]
        w_a = jnp.concatenate(seg[0:4] + [seg[7]], axis=1).astype(BF16)
        kr_wide = jnp.pad(seg[6], ((0, 0), (MLA_NOPE, MLA_W - MLA_NOPE - MLA_ROPE)))
        w_b = jnp.concatenate([seg[4], seg[5], kr_wide], axis=1).astype(BF16)
        wuq = _pack_heads(w_uq[l], MLA_HEADS, 0, MLA_QK, MLA_W).astype(BF16)
        wkn = _pack_heads(w_ukv[l], MLA_HEADS, 0, MLA_NOPE, MLA_W).astype(BF16)
        wv = _pack_heads(w_ukv[l], MLA_HEADS, MLA_NOPE, MLA_NOPE + HEAD_W, HEAD_W).astype(BF16)
        gq_d = jnp.tile(diff_q_norm[l], 2 * DIFF_HEADS)[None]
        gk_d = jnp.tile(diff_k_norm[l], 2 * DIFF_HEADS)[None]
        gq_m = jnp.tile(jnp.pad(mla_q_norm[l], (0, MLA_W - MLA_QK)), MLA_HEADS)[None]
        gk_m = jnp.tile(jnp.pad(mla_k_norm[l], (0, MLA_W - MLA_QK)), MLA_HEADS)[None]
        lp = diff_lambda[l]
        lam = (jnp.exp(jnp.sum(lp[0] * lp[1])) - jnp.exp(jnp.sum(lp[2] * lp[3])) + lam_init).reshape(1)
        out_gain = (diff_out_norm[l] * (1.0 - lam_init))[None]
        mod = mod_all[l]

        u_pool, dq, dk_f, dk_b, dv_f, dv_b, gates = _proj_a(x, mod, w_a, gq_d, gk_d, cos_a, sin_a, mod_spec,
                                                            n_ctx_blocks)
        mq, ckv_n, kr, mk, mv = _proj_b(x, mod, w_b, mla_q_a_norm[l][None], mla_kv_a_norm[l][None], wuq, wkn, wv,
                                        gq_m, gk_m, cos_b, sin_b, mod_spec, n_ctx_blocks)
        new_dk.append(dk_f.reshape(batch, seq, DIFF_HEADS, HEAD_W))
        new_dv.append(dv_f.reshape(batch, seq, DIFF_HEADS, HEAD_W))
        new_ckv.append(ckv_n.reshape(batch, seq, MLA_KV_RANK))
        new_kr.append(kr.reshape(batch, seq, MLA_ROPE))

        y_pool = _pool(u_pool, pool_w[l].astype(BF16), pool_scale[l][None], n_ctx_blocks, seq, dec_seq)

        c_krw = jnp.pad(cache_mla_krope[:, l].reshape(dec_batch * past, MLA_ROPE),
                        ((0, 0), (MLA_NOPE, MLA_W - MLA_NOPE - MLA_ROPE)))
        c_mk, c_mv = _cache_kv(cache_mla_ckv[:, l].reshape(dec_batch * past, MLA_KV_RANK), c_krw, wkn, wv, gk_m)

        geom = (n_ctx, seq, dec_seq)
        y_diff = _diff_attention(dq, dk_b, dv_b, cache_diff_k[:, l].reshape(dec_batch, past, d),
                                 cache_diff_v[:, l].reshape(dec_batch, past, d), lam, out_gain, geom)
        y_mla = _mla_attention(mq, mk, mv, c_mk.reshape(dec_batch, past, -1), c_mv.reshape(dec_batch, past, -1),
                               geom)

        rw = jnp.pad(router_w[l], ((0, 0), (0, EXPERT_LANES - N_EXPERTS)))
        rb = jnp.pad(router_b[l], (0, EXPERT_LANES - N_EXPERTS), constant_values=NEG_BIG)[None]
        x1, h2, topw, topi, rank, counts = _merge(x, mod, y_pool, y_diff, y_mla, gates, w_branch[l].astype(BF16),
                                                  w_out[l].astype(BF16), rw, rb, mod_spec, n_ctx_blocks)
        pos, tile_expert, tile_src, n_used, fill_tile = _slot_layout(counts, topi, rank)
        hs = _dispatch(h2, pos, fill_tile)
        ys = _experts(hs, tile_expert, tile_src, n_used, l, moe_w_gu, moe_b_gu[l][:, None, :],
                      moe_w_down, moe_b_down[l][:, None, :])
        x = _combine(ys, pos, topw, x1, mod, n_ctx, dec_seq)

    return (x[:n_ctx].reshape(batch, seq, d), x[n_ctx:].reshape(dec_batch, dec_seq, d),
            jnp.stack(new_dk, axis=1), jnp.stack(new_dv, axis=1), jnp.stack(new_ckv, axis=1),
            jnp.stack(new_kr, axis=1))
```

```python
import functools
import math

import jax
import jax.numpy as jnp
import numpy as np
from jax import lax
from jax.experimental import pallas as pl
from jax.experimental.pallas import tpu as pltpu

F32 = jnp.float32
BF16 = jnp.bfloat16

D_MODEL = 1024
N_POOL_GROUPS = 4
POOL_GROUP = D_MODEL // N_POOL_GROUPS
DIFF_HD = 64
DIFF_HEADS = 8
HEAD_W = 128
MLA_HEADS = 8
MLA_NOPE = 128
MLA_ROPE = 64
MLA_QK = MLA_NOPE + MLA_ROPE
MLA_W = 256
MLA_Q_RANK = 768
MLA_KV_RANK = 512
N_EXPERTS = 32
TOP_K = 4
D_FF = 1024
EXPERT_LANES = 128
SWIGLU_ALPHA = 1.702
SWIGLU_LIMIT = 7.0
ROPE_THETA = 10000.0
NORM_EPS = 1e-6
GRID_W = 64
LOG2E = math.log2(math.e)
NEG_BIG = -1e30
VMEM_LIMIT = 56 * 1024 * 1024

ROW_BLOCK = 256
POOL_HALO = 16
MOE_TILE = 256
DISPATCH_ROWS = 256
COMBINE_ROWS = 128
WAIT_UNROLL = 16
ISSUE_UNROLL = 4
ATTN_Q_BLOCK = 1024
ATTN_K_CHUNK = 512

_NT = (((1,), (1,)), ((), ()))
_TN = (((0,), (0,)), ((), ()))


def _cparams(*sem):
    return pltpu.CompilerParams(dimension_semantics=sem, vmem_limit_bytes=VMEM_LIMIT)


def _resident(shape):
    nd = len(shape)
    return pl.BlockSpec(shape, lambda *_: (0,) * nd, pipeline_mode=pl.Buffered(1))


def _sigmoid(x):
    return 1.0 / (1.0 + jnp.exp(-x))


def _lane_iota(shape):
    return lax.broadcasted_iota(jnp.int32, shape, len(shape) - 1)


def _rms(x, width):
    ss = jnp.sum(x * x, axis=-1, keepdims=True)
    return x * lax.rsqrt(ss * (1.0 / width) + NORM_EPS)


def _rope(x, cos, sin):
    n = x.shape[-1]
    lane = _lane_iota(x.shape)
    partner = jnp.where((lane & 32) != 0, pltpu.roll(x, 32, 1), pltpu.roll(x, n - 32, 1))
    return x * cos + partner * sin


def _modulated_norm(x, shift, scale):
    return _rms(x, D_MODEL) * (1.0 + scale) + shift


def _ada_kernel(cond_ref, w_ref, b_ref, o_ref):
    cnd = cond_ref[...]
    act = cnd * _sigmoid(cnd)
    o_ref[0] = jnp.dot(act, w_ref[0], preferred_element_type=F32) + b_ref[0]


def _ada(cond8, w_ada, b_ada):
    depth = w_ada.shape[0]
    n_chunk = w_ada.shape[2] // D_MODEL
    return pl.pallas_call(
        _ada_kernel,
        grid=(depth, n_chunk),
        in_specs=[
            pl.BlockSpec((8, D_MODEL), lambda l, j: (0, 0)),
            pl.BlockSpec((1, D_MODEL, D_MODEL), lambda l, j: (l, 0, j)),
            pl.BlockSpec((1, 1, D_MODEL), lambda l, j: (l, 0, j)),
        ],
        out_specs=pl.BlockSpec((1, 8, D_MODEL), lambda l, j: (l, 0, j)),
        out_shape=jax.ShapeDtypeStruct((depth, 8, n_chunk * D_MODEL), F32),
        compiler_params=_cparams("arbitrary", "arbitrary"),
        name="ada",
    )(cond8, w_ada, b_ada.reshape(depth, 1, -1))


def _diff_qk_norm(u, gain, cos, sin):
    outs = []
    for h in range(DIFF_HEADS):
        c = u[:, h * HEAD_W:(h + 1) * HEAD_W]
        lane = _lane_iota(c.shape)
        sq = c * c
        lo = jnp.sum(jnp.where(lane < DIFF_HD, sq, 0.0), axis=-1, keepdims=True)
        hi = jnp.sum(jnp.where(lane >= DIFF_HD, sq, 0.0), axis=-1, keepdims=True)
        ms = jnp.where(lane < DIFF_HD, lo, hi) * (1.0 / DIFF_HD)
        y = c * lax.rsqrt(ms + NORM_EPS) * gain[:, h * HEAD_W:(h + 1) * HEAD_W]
        outs.append((y, _rope(y, cos, sin)))
    return outs


def _proj_a_kernel(x_ref, mod_ref, w_ref, gq_ref, gk_ref, cos_ref, sin_ref,
                   pool_ref, dq_ref, dkf_ref, dkb_ref, dvf_ref, dvb_ref, gate_ref):
    mod = mod_ref[0]
    h = _modulated_norm(x_ref[...], mod[0:1], mod[1:2]).astype(BF16)
    cos, sin = cos_ref[...], sin_ref[...]

    def seg(k):
        return jnp.dot(h, w_ref[:, k * D_MODEL:(k + 1) * D_MODEL], preferred_element_type=F32)

    u_pool, u_dq = seg(0), seg(1)
    pool_ref[...] = u_pool.astype(BF16)
    u_dk = seg(2)
    q_scale = DIFF_HD ** -0.5 * LOG2E
    for hd, (_, roped) in enumerate(_diff_qk_norm(u_dq, gq_ref[...], cos, sin)):
        dq_ref[:, hd * HEAD_W:(hd + 1) * HEAD_W] = (roped * q_scale).astype(BF16)
    dv = seg(3)
    for hd, (plain, roped) in enumerate(_diff_qk_norm(u_dk, gk_ref[...], cos, sin)):
        dkf_ref[:, hd * HEAD_W:(hd + 1) * HEAD_W] = plain
        dkb_ref[:, hd * HEAD_W:(hd + 1) * HEAD_W] = roped.astype(BF16)
    u_gate = seg(4)
    dvf_ref[...] = dv
    dvb_ref[...] = dv.astype(BF16)
    for r in range(3):
        u_next = seg(5 + r) if r < 2 else None
        gate_ref[:, r * D_MODEL:(r + 1) * D_MODEL] = _sigmoid(u_gate).astype(BF16)
        u_gate = u_next


def _mla_heads(u, gain, cos, sin, rope):
    outs = []
    for h in range(MLA_HEADS):
        c = u[:, h * MLA_W:(h + 1) * MLA_W]
        y = _rms(c, MLA_QK) * gain[:, h * MLA_W:(h + 1) * MLA_W]
        if rope:
            y = jnp.concatenate([y[:, :MLA_NOPE], _rope(y[:, MLA_NOPE:], cos, sin)], axis=-1)
        outs.append(y)
    return outs


def _mla_kv(ckv_n, krw, wkn_ref, wv_ref, gk, cos, sin, rope, k_ref, v_ref):
    kn = jnp.dot(ckv_n, wkn_ref[...], preferred_element_type=F32)
    v_ref[...] = jnp.dot(ckv_n, wv_ref[...], preferred_element_type=F32).astype(BF16)
    kfull = jnp.concatenate([kn[:, h * MLA_W:(h + 1) * MLA_W] + krw for h in range(MLA_HEADS)], axis=-1)
    for hd, y in enumerate(_mla_heads(kfull, gk, cos, sin, rope)):
        k_ref[:, hd * MLA_W:(hd + 1) * MLA_W] = y.astype(BF16)


def _proj_b_kernel(x_ref, mod_ref, w_ref, gqa_ref, gkva_ref, wuq_ref, wkn_ref, wv_ref, gq_ref, gk_ref,
                   cos_ref, sin_ref, mq_ref, ckv_ref, kr_ref, mk_ref, mv_ref):
    mod = mod_ref[0]
    h = _modulated_norm(x_ref[...], mod[0:1], mod[1:2]).astype(BF16)
    cos, sin = cos_ref[...], sin_ref[...]
    cq = jnp.dot(h, w_ref[:, :MLA_Q_RANK], preferred_element_type=F32)
    ckv = jnp.dot(h, w_ref[:, MLA_Q_RANK:MLA_Q_RANK + MLA_KV_RANK], preferred_element_type=F32)
    krw = jnp.dot(h, w_ref[:, MLA_Q_RANK + MLA_KV_RANK:], preferred_element_type=F32)
    cq_n = (_rms(cq, MLA_Q_RANK) * gqa_ref[...]).astype(BF16)
    mq = jnp.dot(cq_n, wuq_ref[...], preferred_element_type=F32)
    ckv_n = _rms(ckv, MLA_KV_RANK) * gkva_ref[...]
    ckv_ref[...] = ckv_n
    kr_ref[...] = krw[:, MLA_NOPE:MLA_NOPE + MLA_ROPE]
    _mla_kv(ckv_n.astype(BF16), krw, wkn_ref, wv_ref, gk_ref[...], cos, sin, True, mk_ref, mv_ref)
    q_scale = MLA_QK ** -0.5 * LOG2E
    for hd, y in enumerate(_mla_heads(mq, gq_ref[...], cos, sin, True)):
        mq_ref[:, hd * MLA_W:(hd + 1) * MLA_W] = (y * q_scale).astype(BF16)


def _cache_kv_kernel(ckv_ref, krw_ref, wkn_ref, wv_ref, gk_ref, mk_ref, mv_ref):
    _mla_kv(ckv_ref[...].astype(BF16), krw_ref[...], wkn_ref, wv_ref, gk_ref[...], None, None, False,
            mk_ref, mv_ref)


def _mod_spec(n_ctx_blocks, blocks_per_lat):
    def row(i):
        return jnp.where(i < n_ctx_blocks, 0, 1 + (i - n_ctx_blocks) // blocks_per_lat)
    return pl.BlockSpec((1, 8, D_MODEL), lambda i: (row(i), 0, 0))


def _rows(width):
    return pl.BlockSpec((ROW_BLOCK, width), lambda i: (i, 0))


def _ctx_rows(width, n_ctx_blocks):
    return pl.BlockSpec((ROW_BLOCK, width), lambda i: (jnp.minimum(i, n_ctx_blocks - 1), 0))


def _proj_outputs(t, n_ctx_blocks, widths, dtypes, ctx_only):
    spare = lambda w: pl.BlockSpec((ROW_BLOCK, w), lambda i: (jnp.minimum(i, n_ctx_blocks), 0))
    specs = [spare(w) if c else _rows(w) for w, c in zip(widths, ctx_only)]
    shapes = [jax.ShapeDtypeStruct(((n_ctx_blocks + 1) * ROW_BLOCK if c else t, w), dt)
              for w, dt, c in zip(widths, dtypes, ctx_only)]
    return specs, shapes


def _proj_a(x, mod, w_a, gq, gk, cos, sin, mod_spec, n_ctx_blocks):
    t = x.shape[0]
    widths = (D_MODEL,) * 6 + (3 * D_MODEL,)
    dtypes = (BF16, BF16, F32, BF16, F32, BF16, BF16)
    out_specs, out_shape = _proj_outputs(t, n_ctx_blocks, widths, dtypes, (0, 0, 1, 0, 1, 0, 0))
    return pl.pallas_call(
        _proj_a_kernel,
        grid=(t // ROW_BLOCK,),
        in_specs=[_rows(D_MODEL), mod_spec, _resident(w_a.shape), _resident(gq.shape), _resident(gk.shape),
                  _rows(HEAD_W), _rows(HEAD_W)],
        out_specs=out_specs,
        out_shape=out_shape,
        compiler_params=_cparams("arbitrary"),
        name="proj_a",
    )(x, mod, w_a, gq, gk, cos, sin)


def _proj_b(x, mod, w_b, gqa, gkva, wuq, wkn, wv, gq, gk, cos, sin, mod_spec, n_ctx_blocks):
    t = x.shape[0]
    widths = (MLA_HEADS * MLA_W, MLA_KV_RANK, MLA_ROPE, MLA_HEADS * MLA_W, D_MODEL)
    dtypes = (BF16, F32, F32, BF16, BF16)
    out_specs, out_shape = _proj_outputs(t, n_ctx_blocks, widths, dtypes, (0, 1, 1, 0, 0))
    consts = (w_b, gqa, gkva, wuq, wkn, wv, gq, gk)
    return pl.pallas_call(
        _proj_b_kernel,
        grid=(t // ROW_BLOCK,),
        in_specs=[_rows(D_MODEL), mod_spec] + [_resident(a.shape) for a in consts] + [_rows(HEAD_W), _rows(HEAD_W)],
        out_specs=out_specs,
        out_shape=out_shape,
        compiler_params=_cparams("arbitrary"),
        name="proj_b",
    )(x, mod, *consts, cos, sin)


def _cache_kv(ckv, krw, wkn, wv, gk):
    n = ckv.shape[0]
    widths = (MLA_HEADS * MLA_W, D_MODEL)
    consts = (wkn, wv, gk)
    return pl.pallas_call(
        _cache_kv_kernel,
        grid=(n // ROW_BLOCK,),
        in_specs=[_rows(MLA_KV_RANK), _rows(MLA_W)] + [_resident(a.shape) for a in consts],
        out_specs=[_rows(w) for w in widths],
        out_shape=[jax.ShapeDtypeStruct((n, w), BF16) for w in widths],
        compiler_params=_cparams("arbitrary"),
        name="cache_kv",
    )(ckv, krw, *consts)


def _pool_kernel(prev_ref, cur_ref, next_ref, w_ref, scale_ref, o_ref, *, n_ctx_blocks, ctx_len, lat_len):
    i = pl.program_id(0)
    rb = ROW_BLOCK
    is_ctx = i < n_ctx_blocks
    n_ctx_rows = n_ctx_blocks * rb
    seq_len = jnp.where(is_ctx, ctx_len, lat_len)
    seq_start = jnp.where(is_ctx, (i * rb // ctx_len) * ctx_len,
                          n_ctx_rows + ((i * rb - n_ctx_rows) // lat_len) * lat_len)
    n_cols = rb + 2 * POOL_HALO
    row_pos = i * rb - seq_start + lax.broadcasted_iota(jnp.int32, (rb, n_cols), 0)
    col_pos = i * rb - POOL_HALO - seq_start + lax.broadcasted_iota(jnp.int32, (rb, n_cols), 1)
    cur = cur_ref[...]
    u = jnp.concatenate([prev_ref[...], cur, next_ref[...]], axis=0)
    for g in range(N_POOL_GROUPS):
        cols = slice(g * POOL_GROUP, (g + 1) * POOL_GROUP)
        window = 2 << g
        lo = jnp.clip(row_pos - window // 2, 0, seq_len)
        hi = jnp.clip(row_pos - window // 2 + window, 0, seq_len)
        band = jnp.where((col_pos >= lo) & (col_pos < hi), 1.0, 0.0).astype(BF16)
        total = jnp.dot(band, u[:, cols], preferred_element_type=F32)
        cnt = (hi[:, :1] - lo[:, :1]).astype(F32)
        pooled = total / cnt - cur[:, cols].astype(F32)
        mixed = jnp.dot(pooled.astype(BF16), w_ref[g], preferred_element_type=F32)
        o_ref[:, cols] = (mixed * scale_ref[:, cols]).astype(BF16)


def _pool(u_pool, pool_w, pool_scale, n_ctx_blocks, ctx_len, lat_len):
    t = u_pool.shape[0]
    halo_per_block = ROW_BLOCK // POOL_HALO
    n_halo_blocks = t // POOL_HALO
    assert POOL_HALO >= (2 << (N_POOL_GROUPS - 1)) // 2
    kern = functools.partial(_pool_kernel, n_ctx_blocks=n_ctx_blocks, ctx_len=ctx_len, lat_len=lat_len)
    return pl.pallas_call(
        kern,
        grid=(t // ROW_BLOCK,),
        in_specs=[
            pl.BlockSpec((POOL_HALO, D_MODEL), lambda i: (jnp.maximum(i * halo_per_block - 1, 0), 0)),
            _rows(D_MODEL),
            pl.BlockSpec((POOL_HALO, D_MODEL),
                         lambda i: (jnp.minimum((i + 1) * halo_per_block, n_halo_blocks - 1), 0)),
            _resident(pool_w.shape),
            _resident(pool_scale.shape),
        ],
        out_specs=_rows(D_MODEL),
        out_shape=jax.ShapeDtypeStruct((t, D_MODEL), BF16),
        compiler_params=_cparams("arbitrary"),
        name="pool",
    )(u_pool, u_pool, u_pool, pool_w, pool_scale)


def _softmax_stats(s, m, l):
    m_new = jnp.maximum(m, jnp.max(s, axis=0, keepdims=True))
    p = jnp.exp2(s - m_new)
    alpha = jnp.exp2(m - m_new)
    return m_new, alpha * l + jnp.sum(p, axis=0, keepdims=True), alpha, p.astype(BF16)


def _accumulate(acc, alpha, p, v):
    return alpha * acc + lax.dot_general(v, p, _TN, preferred_element_type=F32)


def _online_softmax(chunks, scores, n_softmax, tq):
    chunks = list(chunks)
    m = [jnp.full((1, tq), -jnp.inf, F32)] * n_softmax
    l = [jnp.zeros((1, tq), F32)] * n_softmax
    acc = [jnp.zeros((HEAD_W, tq), F32)] * n_softmax
    pending = None
    s_next = scores(chunks[0][0])
    for c, (_, v) in enumerate(chunks):
        s = s_next
        if c + 1 < len(chunks):
            s_next = scores(chunks[c + 1][0])
        stats = [_softmax_stats(s[j], m[j], l[j]) for j in range(n_softmax)]
        m = [st[0] for st in stats]
        l = [st[1] for st in stats]
        if pending is not None:
            acc = [_accumulate(acc[j], *pending[j]) for j in range(n_softmax)]
        pending = [(st[2], st[3], v) for st in stats]
    acc = [_accumulate(acc[j], *pending[j]) for j in range(n_softmax)]
    return list(zip(l, acc))


def _key_value_chunks(kv_refs, head, qk_width):
    k_cols = slice(head * qk_width, (head + 1) * qk_width)
    v_cols = slice(head * HEAD_W, (head + 1) * HEAD_W)
    for k_ref, v_ref in zip(kv_refs[0::2], kv_refs[1::2]):
        n = k_ref.shape[0]
        size = min(n, ATTN_K_CHUNK)
        for start in range(0, n, size):
            yield k_ref[start:start + size, k_cols].astype(BF16), v_ref[start:start + size, v_cols].astype(BF16)


def _diff_attn_kernel(lam_ref, q_ref, *refs, n_kv, heads):
    g_ref, o_ref = refs[n_kv], refs[-1]
    for h in range(heads):
        cols = slice(h * HEAD_W, (h + 1) * HEAD_W)
        q = q_ref[:, cols]
        lane = _lane_iota(q.shape)
        q1 = jnp.where(lane < DIFF_HD, q, jnp.zeros_like(q))
        q2 = jnp.where(lane >= DIFF_HD, q, jnp.zeros_like(q))

        def scores(k):
            return (lax.dot_general(k, q1, _NT, preferred_element_type=F32),
                    lax.dot_general(k, q2, _NT, preferred_element_type=F32))

        (l1, a1), (l2, a2) = _online_softmax(_key_value_chunks(refs[:n_kv], h, HEAD_W), scores, 2, q.shape[0])
        o = (a1 / l1 - lam_ref[0] * (a2 / l2)).T
        o_ref[:, cols] = (_rms(o, HEAD_W) * g_ref[...]).astype(o_ref.dtype)


def _mla_attn_kernel(q_ref, *refs, n_kv, heads):
    o_ref = refs[-1]
    for h in range(heads):
        q = q_ref[:, h * MLA_W:(h + 1) * MLA_W]
        ((l, acc),) = _online_softmax(
            _key_value_chunks(refs[:n_kv], h, MLA_W),
            lambda k: (lax.dot_general(k, q, _NT, preferred_element_type=F32),), 1, q.shape[0])
        o_ref[:, h * HEAD_W:(h + 1) * HEAD_W] = (acc / l).T.astype(o_ref.dtype)


def _attention(kernel, q, k, v, cache_k, cache_v, lead, lead_specs, tail, tail_specs, qk_width, geom, name):
    n_ctx, seq, dec_seq = geom
    t = q.shape[0]
    n_heads = v.shape[1] // HEAD_W
    dec_batch, past, _ = cache_k.shape
    tq = ATTN_Q_BLOCK
    nq, q0, k0 = dec_seq // tq, n_ctx // tq, n_ctx // dec_seq
    assert n_ctx % tq == 0 and n_ctx % dec_seq == 0 and dec_seq % tq == 0
    params = _cparams("arbitrary", "arbitrary", "arbitrary")
    seq_block = lambda w: pl.BlockSpec((seq, n_heads * w), lambda b, h, qi: (b, 0))
    ctx = pl.pallas_call(
        functools.partial(kernel, n_kv=2, heads=n_heads),
        grid=(n_ctx // seq, 1, 1),
        in_specs=lead_specs + [seq_block(qk_width), seq_block(qk_width), seq_block(HEAD_W)] + tail_specs,
        out_specs=seq_block(HEAD_W),
        out_shape=jax.ShapeDtypeStruct((n_ctx, n_heads * HEAD_W), BF16),
        compiler_params=params,
        name=name + "_ctx",
    )(*lead, q, k, v, *tail)
    lat_rows = lambda w, first: pl.BlockSpec((tq, w), lambda b, h, qi: (first + b * nq + qi, h))
    cached = lambda w: pl.BlockSpec((None, past, w), lambda b, h, qi: (b, 0, h))
    lat_keys = lambda w: pl.BlockSpec((dec_seq, w), lambda b, h, qi: (k0 + b, h))
    lat = pl.pallas_call(
        functools.partial(kernel, n_kv=4, heads=1),
        grid=(dec_batch, n_heads, nq),
        in_specs=lead_specs + [lat_rows(qk_width, q0), cached(qk_width), cached(HEAD_W), lat_keys(qk_width),
                               lat_keys(HEAD_W)] + tail_specs,
        out_specs=lat_rows(HEAD_W, 0),
        out_shape=jax.ShapeDtypeStruct((t - n_ctx, n_heads * HEAD_W), BF16),
        compiler_params=params,
        name=name + "_lat",
    )(*lead, q, cache_k, cache_v, k, v, *tail)
    return ctx, lat


def _diff_attention(q, k, v, cache_k, cache_v, lam, out_gain, geom):
    lead_specs = [pl.BlockSpec(memory_space=pltpu.SMEM)]
    tail_specs = [pl.BlockSpec((1, HEAD_W), lambda b, h, qi: (0, 0))]
    return _attention(_diff_attn_kernel, q, k, v, cache_k, cache_v, (lam,), lead_specs, (out_gain,), tail_specs,
                      HEAD_W, geom, "diff")


def _mla_attention(q, k, v, cache_k, cache_v, geom):
    return _attention(_mla_attn_kernel, q, k, v, cache_k, cache_v, (), [], (), [], MLA_W, geom, "mla")


def _merge_kernel(x_ref, mod_ref, yp_ref, ydc_ref, ydl_ref, ymc_ref, yml_ref, gate_ref, wb_ref, wo_ref, rw_ref,
                  rb_ref, x1_ref, h2_ref, topw_ref, topi_ref, rank_ref, count_ref, carry_ref, *, n_ctx_blocks):
    @pl.when(pl.program_id(0) == 0)
    def _():
        carry_ref[...] = jnp.zeros_like(carry_ref)

    is_ctx = pl.program_id(0) < n_ctx_blocks
    mod = mod_ref[0]
    merged = jnp.zeros(x_ref.shape, F32)
    branches = (yp_ref[...], jnp.where(is_ctx, ydc_ref[...], ydl_ref[...]),
                jnp.where(is_ctx, ymc_ref[...], yml_ref[...]))
    for r, y in enumerate(branches):
        z = jnp.dot(y, wb_ref[r], preferred_element_type=F32)
        merged = merged + gate_ref[:, r * D_MODEL:(r + 1) * D_MODEL].astype(F32) * z
    y = jnp.dot(merged.astype(BF16), wo_ref[...], preferred_element_type=F32)
    x1 = x_ref[...] + mod[2:3] * y
    x1_ref[...] = x1
    h2 = _modulated_norm(x1, mod[3:4], mod[4:5])
    h2_ref[...] = h2
    logits = jnp.dot(h2, rw_ref[...], preferred_element_type=F32) + rb_ref[...]
    lane = _lane_iota(logits.shape)
    picks = []
    for _ in range(TOP_K):
        mx = jnp.max(logits, axis=-1, keepdims=True)
        first = jnp.min(jnp.where(logits == mx, lane, EXPERT_LANES), axis=-1, keepdims=True)
        hit = lane == first
        picks.append((mx, first, hit))
        logits = jnp.where(hit, -jnp.inf, logits)
    exps = [jnp.exp(v - picks[0][0]) for v, _, _ in picks]
    denom = exps[0] + exps[1] + exps[2] + exps[3]
    rows = logits.shape[0]
    chosen = jnp.zeros(logits.shape, F32)
    for _, _, hit in picks:
        chosen = chosen + jnp.where(hit, 1.0, 0.0)
    earlier = jnp.where(lax.broadcasted_iota(jnp.int32, (rows, rows), 1)
                        < lax.broadcasted_iota(jnp.int32, (rows, rows), 0), 1.0, 0.0).astype(BF16)
    before = jnp.dot(earlier, chosen.astype(BF16), preferred_element_type=F32) + carry_ref[0:1, :]
    topw = jnp.zeros(logits.shape, F32)
    topi = jnp.zeros(logits.shape, jnp.int32)
    rank = jnp.zeros(logits.shape, jnp.int32)
    for k, (e, (_, first, hit)) in enumerate(zip(exps, picks)):
        slot_rank = jnp.sum(jnp.where(hit, before, 0.0), axis=-1, keepdims=True)
        topw = jnp.where(lane == k, e / denom, topw)
        topi = jnp.where(lane == k, first, topi)
        rank = jnp.where(lane == k, slot_rank.astype(jnp.int32), rank)
    topw_ref[...] = topw
    topi_ref[...] = topi
    rank_ref[...] = rank
    carry_ref[...] = carry_ref[...] + jnp.sum(chosen, axis=0, keepdims=True)
    count_ref[...] = carry_ref[...]


def _merge(x, mod, y_pool, y_diff, y_mla, gates, wb, wo, rw, rb, mod_spec, n_ctx_blocks):
    t = x.shape[0]
    consts = (wb, wo, rw, rb)
    lanes = EXPERT_LANES
    ctx_rows = _ctx_rows(D_MODEL, n_ctx_blocks)
    lat_rows = pl.BlockSpec((ROW_BLOCK, D_MODEL), lambda i: (jnp.maximum(i - n_ctx_blocks, 0), 0))
    return pl.pallas_call(
        functools.partial(_merge_kernel, n_ctx_blocks=n_ctx_blocks),
        grid=(t // ROW_BLOCK,),
        in_specs=[_rows(D_MODEL), mod_spec, _rows(D_MODEL), ctx_rows, lat_rows, ctx_rows, lat_rows,
                  _rows(3 * D_MODEL)] + [_resident(a.shape) for a in consts],
        out_specs=[_rows(D_MODEL), _rows(D_MODEL), _rows(lanes), _rows(lanes), _rows(lanes),
                   pl.BlockSpec((8, lanes), lambda i: (0, 0))],
        out_shape=[jax.ShapeDtypeStruct((t, D_MODEL), F32), jax.ShapeDtypeStruct((t, D_MODEL), F32),
                   jax.ShapeDtypeStruct((t, lanes), F32), jax.ShapeDtypeStruct((t, lanes), jnp.int32),
                   jax.ShapeDtypeStruct((t, lanes), jnp.int32), jax.ShapeDtypeStruct((8, lanes), F32)],
        scratch_shapes=[pltpu.VMEM((8, lanes), F32)],
        compiler_params=_cparams("arbitrary"),
        name="merge",
    )(x, mod, y_pool, *y_diff, *y_mla, gates, *consts)


def _slot_layout(counts, topi, rank):
    tile = MOE_TILE
    cnt = counts[0, :N_EXPERTS].astype(jnp.int32)
    padded = (cnt + tile - 1) // tile * tile
    end = jnp.cumsum(padded)
    start = end - padded
    experts = jnp.arange(N_EXPERTS, dtype=jnp.int32)
    idx4 = topi[:, :TOP_K]
    slot_start = jnp.sum(jnp.where(idx4[:, :, None] == experts, start, 0), axis=-1)
    pos = (slot_start + rank[:, :TOP_K]).reshape(-1)
    n_tiles = (topi.shape[0] * TOP_K + N_EXPERTS * tile) // tile
    tile_row = jnp.arange(n_tiles, dtype=jnp.int32) * tile
    tile_expert = jnp.minimum(jnp.sum(tile_row[:, None] >= end, axis=-1), N_EXPERTS - 1).astype(jnp.int32)
    n_used = (end[-1] // tile).astype(jnp.int32)
    tile_src = jnp.minimum(jnp.arange(n_tiles, dtype=jnp.int32), n_used - 1)
    ragged_last = jnp.any((cnt % tile != 0) & (tile_row[:, None] == end - tile), axis=-1)
    fill_tile = (ragged_last | (tile_row >= end[-1])).astype(jnp.int32)
    return pos, tile_expert, tile_src, n_used.reshape(1), fill_tile


def _row_copy(src, src_row, dst, dst_row, sem):
    return pltpu.make_async_copy(src.at[pl.ds(src_row, 1)], dst.at[pl.ds(dst_row, 1)], sem)


def _dispatch_kernel(fill_tile_ref, pos_ref, h_ref, hs_ref, zero_ref, fill_sem, row_sem):
    i = pl.program_id(0)
    slots = DISPATCH_ROWS * TOP_K

    @pl.when(i == 0)
    def _():
        zero_ref[...] = jnp.zeros_like(zero_ref)

        def fill_copy(tile):
            row = pl.multiple_of(tile * MOE_TILE, MOE_TILE)
            return pltpu.make_async_copy(zero_ref, hs_ref.at[pl.ds(row, MOE_TILE)], fill_sem)

        def start_fill(tile, carry):
            @pl.when(fill_tile_ref[tile] != 0)
            def _():
                fill_copy(tile).start()
            return carry

        def wait_fill(tile, carry):
            @pl.when(fill_tile_ref[tile] != 0)
            def _():
                fill_copy(tile).wait()
            return carry

        lax.fori_loop(0, hs_ref.shape[0] // MOE_TILE, start_fill, 0)
        lax.fori_loop(0, hs_ref.shape[0] // MOE_TILE, wait_fill, 0)

    def issue(r, carry):
        slot = (i * DISPATCH_ROWS + r) * TOP_K
        for k in range(TOP_K):
            _row_copy(h_ref, r, hs_ref, pos_ref[slot + k], row_sem).start()
        return carry

    def drain(j, carry):
        for _ in range(WAIT_UNROLL):
            _row_copy(h_ref, 0, hs_ref, 0, row_sem).wait()
        return carry

    lax.fori_loop(0, DISPATCH_ROWS, issue, 0, unroll=ISSUE_UNROLL)
    lax.fori_loop(0, slots // WAIT_UNROLL, drain, 0)


def _dispatch(h2, pos, fill_tile):
    t = h2.shape[0]
    n_rows = fill_tile.shape[0] * MOE_TILE
    return pl.pallas_call(
        _dispatch_kernel,
        grid_spec=pltpu.PrefetchScalarGridSpec(
            num_scalar_prefetch=2,
            grid=(t // DISPATCH_ROWS,),
            in_specs=[pl.BlockSpec((DISPATCH_ROWS, D_MODEL), lambda i, fill_ref, pos_ref: (i, 0))],
            out_specs=pl.BlockSpec(memory_space=pl.ANY),
            scratch_shapes=[pltpu.VMEM((MOE_TILE, D_MODEL), F32), pltpu.SemaphoreType.DMA,
                            pltpu.SemaphoreType.DMA],
        ),
        out_shape=jax.ShapeDtypeStruct((n_rows, D_MODEL), F32),
        compiler_params=_cparams("arbitrary"),
        name="dispatch",
    )(fill_tile, pos, h2)


def _expert_kernel(tile_expert_ref, tile_src_ref, n_used_ref, hs_ref, wgu_ref, bgu_ref, wd_ref, bd_ref, ys_ref,
                   wgu_bf_ref, wd_bf_ref):
    i = pl.program_id(0)
    occupied = i < n_used_ref[0]
    new_expert = (i == 0) | (tile_expert_ref[i] != tile_expert_ref[jnp.maximum(i - 1, 0)])

    @pl.when(occupied & new_expert)
    def _():
        wgu_bf_ref[...] = wgu_ref[...].astype(BF16)
        wd_bf_ref[...] = wd_ref[...].astype(BF16)

    @pl.when(occupied)
    def _():
        gu = jnp.dot(hs_ref[...].astype(BF16), wgu_bf_ref[...], preferred_element_type=F32) + bgu_ref[0]
        gate = jnp.minimum(gu[:, :D_FF], SWIGLU_LIMIT)
        up = jnp.clip(gu[:, D_FF:], -SWIGLU_LIMIT, SWIGLU_LIMIT)
        act = (up + 1.0) * gate * _sigmoid(SWIGLU_ALPHA * gate)
        ys_ref[...] = jnp.dot(act.astype(BF16), wd_bf_ref[...], preferred_element_type=F32) + bd_ref[0]

    @pl.when(jnp.logical_not(occupied))
    def _():
        ys_ref[...] = jnp.zeros_like(ys_ref)


def _experts(hs, tile_expert, tile_src, n_used, layer, wgu, bgu, wd, bd):
    n_rows = hs.shape[0]
    rows = lambda i, te, ts, nu: (ts[i], 0)
    bias = lambda i, te, ts, nu: (te[i], 0, 0)
    weight = lambda i, te, ts, nu: (layer, te[i], 0, 0)
    return pl.pallas_call(
        _expert_kernel,
        grid_spec=pltpu.PrefetchScalarGridSpec(
            num_scalar_prefetch=3,
            grid=(n_rows // MOE_TILE,),
            in_specs=[
                pl.BlockSpec((MOE_TILE, D_MODEL), rows),
                pl.BlockSpec((None, None, D_MODEL, 2 * D_FF), weight),
                pl.BlockSpec((1, 1, 2 * D_FF), bias),
                pl.BlockSpec((None, None, D_FF, D_MODEL), weight),
                pl.BlockSpec((1, 1, D_MODEL), bias),
            ],
            out_specs=pl.BlockSpec((MOE_TILE, D_MODEL), lambda i, te, ts, nu: (i, 0)),
            scratch_shapes=[pltpu.VMEM((D_MODEL, 2 * D_FF), BF16), pltpu.VMEM((D_FF, D_MODEL), BF16)],
        ),
        out_shape=jax.ShapeDtypeStruct((n_rows, D_MODEL), F32),
        compiler_params=_cparams("arbitrary"),
        name="experts",
    )(tile_expert, tile_src, n_used, hs, wgu, bgu, wd, bd)


def _combine_kernel(pos_ref, ys_ref, w_ref, x1_ref, mod_ref, o_ref, buf_ref, sem):
    i = pl.program_id(0)
    slots = COMBINE_ROWS * TOP_K

    def issue(r, carry):
        token = i * COMBINE_ROWS + r
        for k in range(TOP_K):
            _row_copy(ys_ref, pos_ref[token * TOP_K + k], buf_ref.at[k], r, sem).start()
        return carry

    def drain(j, carry):
        for _ in range(WAIT_UNROLL):
            _row_copy(ys_ref, 0, buf_ref.at[0], 0, sem).wait()
        return carry

    lax.fori_loop(0, COMBINE_ROWS, issue, 0, unroll=ISSUE_UNROLL)
    lax.fori_loop(0, slots // WAIT_UNROLL, drain, 0)
    w = w_ref[...]
    acc = w[:, 0:1] * buf_ref[0]
    for k in range(1, TOP_K):
        acc = acc + w[:, k:k + 1] * buf_ref[k]
    o_ref[...] = x1_ref[...] + mod_ref[0][5:6] * acc


def _combine(ys, pos, topw, x1, mod, n_ctx_rows, lat_len):
    t = x1.shape[0]
    rb = COMBINE_ROWS
    n_ctx_blocks = n_ctx_rows // rb

    def mod_row(i, pos_ref):
        return (jnp.where(i < n_ctx_blocks, 0, 1 + (i - n_ctx_blocks) // (lat_len // rb)), 0, 0)

    return pl.pallas_call(
        _combine_kernel,
        grid_spec=pltpu.PrefetchScalarGridSpec(
            num_scalar_prefetch=1,
            grid=(t // rb,),
            in_specs=[
                pl.BlockSpec(memory_space=pl.ANY),
                pl.BlockSpec((rb, EXPERT_LANES), lambda i, pos_ref: (i, 0)),
                pl.BlockSpec((rb, D_MODEL), lambda i, pos_ref: (i, 0)),
                pl.BlockSpec((1, 8, D_MODEL), mod_row),
            ],
            out_specs=pl.BlockSpec((rb, D_MODEL), lambda i, pos_ref: (i, 0)),
            scratch_shapes=[pltpu.VMEM((TOP_K, rb, D_MODEL), F32), pltpu.SemaphoreType.DMA],
        ),
        out_shape=jax.ShapeDtypeStruct((t, D_MODEL), F32),
        compiler_params=_cparams("arbitrary"),
        name="combine",
    )(pos, ys, topw, x1, mod)


def _rope_tables(n_lat_tokens, rot_dim):
    n_rows = n_lat_tokens // GRID_W
    row = jnp.repeat(jnp.arange(n_rows, dtype=F32), GRID_W)
    col = jnp.tile(jnp.arange(GRID_W, dtype=F32), n_rows)
    n_freq = rot_dim // 4
    inv_freq = ROPE_THETA ** (-jnp.arange(n_freq, dtype=F32) / n_freq)
    ang = jnp.concatenate([row[:, None] * inv_freq, col[:, None] * inv_freq], axis=-1)
    return jnp.cos(ang), jnp.sin(ang)


def _token_tables(cos_lane, sin_lane, n_ctx_rows, n_lat_seqs):
    width = cos_lane.shape[1]
    cos = jnp.concatenate([jnp.ones((n_ctx_rows, width), F32)] + [cos_lane] * n_lat_seqs, axis=0)
    sin = jnp.concatenate([jnp.zeros((n_ctx_rows, width), F32)] + [sin_lane] * n_lat_seqs, axis=0)
    return cos, sin


def _pack_heads(w, n_heads, lo, hi, width):
    k = w.shape[0]
    per_head = w.shape[1] // n_heads
    part = w.reshape(k, n_heads, per_head)[:, :, lo:hi]
    part = jnp.pad(part, ((0, 0), (0, 0), (0, width - (hi - lo))))
    return part.reshape(k, n_heads * width)


def kernel(x_prompt, x_sample, c, cache_diff_k, cache_diff_v, cache_mla_ckv, cache_mla_krope, c_ctx, w_ada, b_ada,
           w_in, pool_w, pool_scale, diff_q_norm, diff_k_norm, diff_lambda, diff_out_norm, mla_q_a_norm, w_uq,
           mla_kv_a_norm, w_ukv, mla_q_norm, mla_k_norm, w_branch, w_out, router_w, router_b, moe_w_gu, moe_b_gu,
           moe_w_down, moe_b_down):
    batch, seq, d = x_prompt.shape
    dec_batch, dec_seq, _ = x_sample.shape
    depth = w_ada.shape[0]
    past = cache_diff_k.shape[2]
    n_ctx = batch * seq
    n_lat = dec_batch * dec_seq
    n_ctx_blocks = n_ctx // ROW_BLOCK
    mod_spec = _mod_spec(n_ctx_blocks, dec_seq // ROW_BLOCK)

    x = jnp.concatenate([x_prompt.reshape(n_ctx, d), x_sample.reshape(n_lat, d)], axis=0)
    cond8 = jnp.concatenate([c_ctx[None], c, jnp.zeros((8 - 1 - dec_batch, d), F32)], axis=0)
    mod_all = _ada(cond8, w_ada, b_ada).reshape(depth, 8, 6, d)[:, :1 + dec_batch]
    mod_all = jnp.pad(mod_all, ((0, 0), (0, 0), (0, 2), (0, 0)))

    cos_d, sin_d = _rope_tables(dec_seq, DIFF_HD)
    cos_m, sin_m = _rope_tables(dec_seq, MLA_ROPE)
    cos_a, sin_a = _token_tables(jnp.concatenate([cos_d] * 4, axis=1),
                                 jnp.concatenate([-sin_d, sin_d, -sin_d, sin_d], axis=1), n_ctx, dec_batch)
    pad = MLA_W - MLA_QK
    cos_b, sin_b = _token_tables(jnp.concatenate([cos_m, cos_m, jnp.ones((dec_seq, pad), F32)], axis=1),
                                 jnp.concatenate([-sin_m, sin_m, jnp.zeros((dec_seq, pad), F32)], axis=1),
                                 n_ctx, dec_batch)

    offs = np.cumsum((0, 1024, 1024, 1024, 1024, MLA_Q_RANK, MLA_KV_RANK, MLA_ROPE, 3 * 1024))
    new_dk, new_dv, new_ckv, new_kr = [], [], [], []
    for l in range(depth):
        lam_init = 0.8 - 0.6 * math.exp(-0.3 * l)
        wl = w_in[l]
        seg = [wl[:, offs[k]:offs[k + 1]] for k in range(8)]
        w_a = jnp.concatenate(seg[0:4] + [seg[7]], axis=1).astype(BF16)
        kr_wide = jnp.pad(seg[6], ((0, 0), (MLA_NOPE, MLA_W - MLA_NOPE - MLA_ROPE)))
        w_b = jnp.concatenate([seg[4], seg[5], kr_wide], axis=1).astype(BF16)
        wuq = _pack_heads(w_uq[l], MLA_HEADS, 0, MLA_QK, MLA_W).astype(BF16)
        wkn = _pack_heads(w_ukv[l], MLA_HEADS, 0, MLA_NOPE, MLA_W).astype(BF16)
        wv = _pack_heads(w_ukv[l], MLA_HEADS, MLA_NOPE, MLA_NOPE + HEAD_W, HEAD_W).astype(BF16)
        gq_d = jnp.tile(diff_q_norm[l], 2 * DIFF_HEADS)[None]
        gk_d = jnp.tile(diff_k_norm[l], 2 * DIFF_HEADS)[None]
        gq_m = jnp.tile(jnp.pad(mla_q_norm[l], (0, MLA_W - MLA_QK)), MLA_HEADS)[None]
        gk_m = jnp.tile(jnp.pad(mla_k_norm[l], (0, MLA_W - MLA_QK)), MLA_HEADS)[None]
        lp = diff_lambda[l]
        lam = (jnp.exp(jnp.sum(lp[0] * lp[1])) - jnp.exp(jnp.sum(lp[2] * lp[3])) + lam_init).reshape(1)
        out_gain = (diff_out_norm[l] * (1.0 - lam_init))[None]
        mod = mod_all[l]

        u_pool, dq, dk_f, dk_b, dv_f, dv_b, gates = _proj_a(x, mod, w_a, gq_d, gk_d, cos_a, sin_a, mod_spec,
                                                            n_ctx_blocks)
        mq, ckv_n, kr, mk, mv = _proj_b(x, mod, w_b, mla_q_a_norm[l][None], mla_kv_a_norm[l][None], wuq, wkn, wv,
                                        gq_m, gk_m, cos_b, sin_b, mod_spec, n_ctx_blocks)
        new_dk.append(dk_f[:n_ctx].reshape(batch, seq, DIFF_HEADS, HEAD_W))
        new_dv.append(dv_f[:n_ctx].reshape(batch, seq, DIFF_HEADS, HEAD_W))
        new_ckv.append(ckv_n[:n_ctx].reshape(batch, seq, MLA_KV_RANK))
        new_kr.append(kr[:n_ctx].reshape(batch, seq, MLA_ROPE))

        y_pool = _pool(u_pool, pool_w[l].astype(BF16), pool_scale[l][None], n_ctx_blocks, seq, dec_seq)

        c_krw = jnp.pad(cache_mla_krope[:, l].reshape(dec_batch * past, MLA_ROPE),
                        ((0, 0), (MLA_NOPE, MLA_W - MLA_NOPE - MLA_ROPE)))
        c_mk, c_mv = _cache_kv(cache_mla_ckv[:, l].reshape(dec_batch * past, MLA_KV_RANK), c_krw, wkn, wv, gk_m)

        geom = (n_ctx, seq, dec_seq)
        y_diff = _diff_attention(dq, dk_b, dv_b, cache_diff_k[:, l].reshape(dec_batch, past, d),
                                 cache_diff_v[:, l].reshape(dec_batch, past, d), lam, out_gain, geom)
        y_mla = _mla_attention(mq, mk, mv, c_mk.reshape(dec_batch, past, -1), c_mv.reshape(dec_batch, past, -1),
                               geom)

        rw = jnp.pad(router_w[l], ((0, 0), (0, EXPERT_LANES - N_EXPERTS)))
        rb = jnp.pad(router_b[l], (0, EXPERT_LANES - N_EXPERTS), constant_values=NEG_BIG)[None]
        x1, h2, topw, topi, rank, counts = _merge(x, mod, y_pool, y_diff, y_mla, gates, w_branch[l].astype(BF16),
                                                  w_out[l].astype(BF16), rw, rb, mod_spec, n_ctx_blocks)
        pos, tile_expert, tile_src, n_used, fill_tile = _slot_layout(counts, topi, rank)
        hs = _dispatch(h2, pos, fill_tile)
        ys = _experts(hs, tile_expert, tile_src, n_used, l, moe_w_gu, moe_b_gu[l][:, None, :],
                      moe_w_down, moe_b_down[l][:, None, :])
        x = _combine(ys, pos, topw, x1, mod, n_ctx, dec_seq)

    return (x[:n_ctx].reshape(batch, seq, d), x[n_ctx:].reshape(dec_batch, dec_seq, d),
            jnp.stack(new_dk, axis=1), jnp.stack(new_dv, axis=1), jnp.stack(new_ckv, axis=1),
            jnp.stack(new_kr, axis=1))
```

```python
import functools
import math

import jax
import jax.numpy as jnp
import numpy as np
from jax import lax
from jax.experimental import pallas as pl
from jax.experimental.pallas import tpu as pltpu

F32 = jnp.float32
BF16 = jnp.bfloat16

D_MODEL = 1024
N_POOL_GROUPS = 4
POOL_GROUP = D_MODEL // N_POOL_GROUPS
DIFF_HD = 64
DIFF_HEADS = 8
HEAD_W = 128
MLA_HEADS = 8
MLA_NOPE = 128
MLA_ROPE = 64
MLA_QK = MLA_NOPE + MLA_ROPE
MLA_W = 256
MLA_Q_RANK = 768
MLA_KV_RANK = 512
N_EXPERTS = 32
TOP_K = 4
D_FF = 1024
EXPERT_LANES = 128
SWIGLU_ALPHA = 1.702
SWIGLU_LIMIT = 7.0
ROPE_THETA = 10000.0
NORM_EPS = 1e-6
GRID_W = 64
LOG2E = math.log2(math.e)
NEG_BIG = -1e30
VMEM_LIMIT = 56 * 1024 * 1024

ROW_BLOCK = 256
POOL_HALO = 16
MOE_TILE = 512
DISPATCH_ROWS = 256
COMBINE_ROWS = 128
WAIT_UNROLL = 16
ISSUE_UNROLL = 4
ATTN_Q_BLOCK = 1024
ATTN_K_CHUNK = 512

_NT = (((1,), (1,)), ((), ()))
_TN = (((0,), (0,)), ((), ()))


def _cparams(*sem):
    return pltpu.CompilerParams(dimension_semantics=sem, vmem_limit_bytes=VMEM_LIMIT)


def _resident(shape):
    nd = len(shape)
    return pl.BlockSpec(shape, lambda *_: (0,) * nd, pipeline_mode=pl.Buffered(1))


def _sigmoid(x):
    return 1.0 / (1.0 + jnp.exp(-x))


def _lane_iota(shape):
    return lax.broadcasted_iota(jnp.int32, shape, len(shape) - 1)


def _rms(x, width):
    ss = jnp.sum(x * x, axis=-1, keepdims=True)
    return x * lax.rsqrt(ss * (1.0 / width) + NORM_EPS)


def _rope(x, cos, sin):
    n = x.shape[-1]
    lane = _lane_iota(x.shape)
    partner = jnp.where((lane & 32) != 0, pltpu.roll(x, 32, 1), pltpu.roll(x, n - 32, 1))
    return x * cos + partner * sin


def _modulated_norm(x, shift, scale):
    return _rms(x, D_MODEL) * (1.0 + scale) + shift


def _ada_kernel(cond_ref, w_ref, b_ref, o_ref):
    cnd = cond_ref[...]
    act = cnd * _sigmoid(cnd)
    o_ref[0] = jnp.dot(act, w_ref[0], preferred_element_type=F32) + b_ref[0]


def _ada(cond8, w_ada, b_ada):
    depth = w_ada.shape[0]
    n_chunk = w_ada.shape[2] // D_MODEL
    return pl.pallas_call(
        _ada_kernel,
        grid=(depth, n_chunk),
        in_specs=[
            pl.BlockSpec((8, D_MODEL), lambda l, j: (0, 0)),
            pl.BlockSpec((1, D_MODEL, D_MODEL), lambda l, j: (l, 0, j)),
            pl.BlockSpec((1, 1, D_MODEL), lambda l, j: (l, 0, j)),
        ],
        out_specs=pl.BlockSpec((1, 8, D_MODEL), lambda l, j: (l, 0, j)),
        out_shape=jax.ShapeDtypeStruct((depth, 8, n_chunk * D_MODEL), F32),
        compiler_params=_cparams("arbitrary", "arbitrary"),
        name="ada",
    )(cond8, w_ada, b_ada.reshape(depth, 1, -1))


def _diff_qk_norm(u, gain, cos, sin):
    outs = []
    for h in range(DIFF_HEADS):
        c = u[:, h * HEAD_W:(h + 1) * HEAD_W]
        lane = _lane_iota(c.shape)
        sq = c * c
        lo = jnp.sum(jnp.where(lane < DIFF_HD, sq, 0.0), axis=-1, keepdims=True)
        hi = jnp.sum(jnp.where(lane >= DIFF_HD, sq, 0.0), axis=-1, keepdims=True)
        ms = jnp.where(lane < DIFF_HD, lo, hi) * (1.0 / DIFF_HD)
        y = c * lax.rsqrt(ms + NORM_EPS) * gain[:, h * HEAD_W:(h + 1) * HEAD_W]
        outs.append((y, _rope(y, cos, sin)))
    return outs


def _proj_a_kernel(x_ref, mod_ref, w_ref, gq_ref, gk_ref, cos_ref, sin_ref,
                   pool_ref, dq_ref, dkf_ref, dkb_ref, dvf_ref, dvb_ref, gate_ref):
    mod = mod_ref[0]
    h = _modulated_norm(x_ref[...], mod[0:1], mod[1:2]).astype(BF16)
    cos, sin = cos_ref[...], sin_ref[...]

    def seg(k):
        return jnp.dot(h, w_ref[:, k * D_MODEL:(k + 1) * D_MODEL], preferred_element_type=F32)

    u_pool, u_dq = seg(0), seg(1)
    pool_ref[...] = u_pool.astype(BF16)
    u_dk = seg(2)
    q_scale = DIFF_HD ** -0.5 * LOG2E
    for hd, (_, roped) in enumerate(_diff_qk_norm(u_dq, gq_ref[...], cos, sin)):
        dq_ref[:, hd * HEAD_W:(hd + 1) * HEAD_W] = (roped * q_scale).astype(BF16)
    dv = seg(3)
    for hd, (plain, roped) in enumerate(_diff_qk_norm(u_dk, gk_ref[...], cos, sin)):
        dkf_ref[:, hd * HEAD_W:(hd + 1) * HEAD_W] = plain
        dkb_ref[:, hd * HEAD_W:(hd + 1) * HEAD_W] = roped.astype(BF16)
    u_gate = seg(4)
    dvf_ref[...] = dv
    dvb_ref[...] = dv.astype(BF16)
    for r in range(3):
        u_next = seg(5 + r) if r < 2 else None
        gate_ref[:, r * D_MODEL:(r + 1) * D_MODEL] = _sigmoid(u_gate).astype(BF16)
        u_gate = u_next


def _mla_heads(u, gain, cos, sin, rope):
    outs = []
    for h in range(MLA_HEADS):
        c = u[:, h * MLA_W:(h + 1) * MLA_W]
        y = _rms(c, MLA_QK) * gain[:, h * MLA_W:(h + 1) * MLA_W]
        if rope:
            y = jnp.concatenate([y[:, :MLA_NOPE], _rope(y[:, MLA_NOPE:], cos, sin)], axis=-1)
        outs.append(y)
    return outs


def _mla_kv(ckv_n, krw, wkn_ref, wv_ref, gk, cos, sin, rope, k_ref, v_ref):
    kn = jnp.dot(ckv_n, wkn_ref[...], preferred_element_type=F32)
    v_ref[...] = jnp.dot(ckv_n, wv_ref[...], preferred_element_type=F32).astype(BF16)
    kfull = jnp.concatenate([kn[:, h * MLA_W:(h + 1) * MLA_W] + krw for h in range(MLA_HEADS)], axis=-1)
    for hd, y in enumerate(_mla_heads(kfull, gk, cos, sin, rope)):
        k_ref[:, hd * MLA_W:(hd + 1) * MLA_W] = y.astype(BF16)


def _proj_b_kernel(x_ref, mod_ref, w_ref, gqa_ref, gkva_ref, wuq_ref, wkn_ref, wv_ref, gq_ref, gk_ref,
                   cos_ref, sin_ref, mq_ref, ckv_ref, kr_ref, mk_ref, mv_ref):
    mod = mod_ref[0]
    h = _modulated_norm(x_ref[...], mod[0:1], mod[1:2]).astype(BF16)
    cos, sin = cos_ref[...], sin_ref[...]
    cq = jnp.dot(h, w_ref[:, :MLA_Q_RANK], preferred_element_type=F32)
    ckv = jnp.dot(h, w_ref[:, MLA_Q_RANK:MLA_Q_RANK + MLA_KV_RANK], preferred_element_type=F32)
    krw = jnp.dot(h, w_ref[:, MLA_Q_RANK + MLA_KV_RANK:], preferred_element_type=F32)
    cq_n = (_rms(cq, MLA_Q_RANK) * gqa_ref[...]).astype(BF16)
    mq = jnp.dot(cq_n, wuq_ref[...], preferred_element_type=F32)
    ckv_n = _rms(ckv, MLA_KV_RANK) * gkva_ref[...]
    ckv_ref[...] = ckv_n
    kr_ref[...] = krw[:, MLA_NOPE:MLA_NOPE + MLA_ROPE]
    _mla_kv(ckv_n.astype(BF16), krw, wkn_ref, wv_ref, gk_ref[...], cos, sin, True, mk_ref, mv_ref)
    q_scale = MLA_QK ** -0.5 * LOG2E
    for hd, y in enumerate(_mla_heads(mq, gq_ref[...], cos, sin, True)):
        mq_ref[:, hd * MLA_W:(hd + 1) * MLA_W] = (y * q_scale).astype(BF16)


def _cache_kv_kernel(ckv_ref, krw_ref, wkn_ref, wv_ref, gk_ref, mk_ref, mv_ref):
    _mla_kv(ckv_ref[...].astype(BF16), krw_ref[...], wkn_ref, wv_ref, gk_ref[...], None, None, False,
            mk_ref, mv_ref)


def _mod_spec(n_ctx_blocks, blocks_per_lat):
    def row(i):
        return jnp.where(i < n_ctx_blocks, 0, 1 + (i - n_ctx_blocks) // blocks_per_lat)
    return pl.BlockSpec((1, 8, D_MODEL), lambda i: (row(i), 0, 0))


def _rows(width):
    return pl.BlockSpec((ROW_BLOCK, width), lambda i: (i, 0))


def _ctx_rows(width, n_ctx_blocks):
    return pl.BlockSpec((ROW_BLOCK, width), lambda i: (jnp.minimum(i, n_ctx_blocks - 1), 0))


def _proj_outputs(t, n_ctx_blocks, widths, dtypes, ctx_only):
    spare = lambda w: pl.BlockSpec((ROW_BLOCK, w), lambda i: (jnp.minimum(i, n_ctx_blocks), 0))
    specs = [spare(w) if c else _rows(w) for w, c in zip(widths, ctx_only)]
    shapes = [jax.ShapeDtypeStruct(((n_ctx_blocks + 1) * ROW_BLOCK if c else t, w), dt)
              for w, dt, c in zip(widths, dtypes, ctx_only)]
    return specs, shapes


def _proj_a(x, mod, w_a, gq, gk, cos, sin, mod_spec, n_ctx_blocks):
    t = x.shape[0]
    widths = (D_MODEL,) * 6 + (3 * D_MODEL,)
    dtypes = (BF16, BF16, F32, BF16, F32, BF16, BF16)
    out_specs, out_shape = _proj_outputs(t, n_ctx_blocks, widths, dtypes, (0, 0, 1, 0, 1, 0, 0))
    return pl.pallas_call(
        _proj_a_kernel,
        grid=(t // ROW_BLOCK,),
        in_specs=[_rows(D_MODEL), mod_spec, _resident(w_a.shape), _resident(gq.shape), _resident(gk.shape),
                  _rows(HEAD_W), _rows(HEAD_W)],
        out_specs=out_specs,
        out_shape=out_shape,
        compiler_params=_cparams("arbitrary"),
        name="proj_a",
    )(x, mod, w_a, gq, gk, cos, sin)


def _proj_b(x, mod, w_b, gqa, gkva, wuq, wkn, wv, gq, gk, cos, sin, mod_spec, n_ctx_blocks):
    t = x.shape[0]
    widths = (MLA_HEADS * MLA_W, MLA_KV_RANK, MLA_ROPE, MLA_HEADS * MLA_W, D_MODEL)
    dtypes = (BF16, F32, F32, BF16, BF16)
    out_specs, out_shape = _proj_outputs(t, n_ctx_blocks, widths, dtypes, (0, 1, 1, 0, 0))
    consts = (w_b, gqa, gkva, wuq, wkn, wv, gq, gk)
    return pl.pallas_call(
        _proj_b_kernel,
        grid=(t // ROW_BLOCK,),
        in_specs=[_rows(D_MODEL), mod_spec] + [_resident(a.shape) for a in consts] + [_rows(HEAD_W), _rows(HEAD_W)],
        out_specs=out_specs,
        out_shape=out_shape,
        compiler_params=_cparams("arbitrary"),
        name="proj_b",
    )(x, mod, *consts, cos, sin)


def _cache_kv(ckv, krw, wkn, wv, gk):
    n = ckv.shape[0]
    widths = (MLA_HEADS * MLA_W, D_MODEL)
    consts = (wkn, wv, gk)
    return pl.pallas_call(
        _cache_kv_kernel,
        grid=(n // ROW_BLOCK,),
        in_specs=[_rows(MLA_KV_RANK), _rows(MLA_W)] + [_resident(a.shape) for a in consts],
        out_specs=[_rows(w) for w in widths],
        out_shape=[jax.ShapeDtypeStruct((n, w), BF16) for w in widths],
        compiler_params=_cparams("arbitrary"),
        name="cache_kv",
    )(ckv, krw, *consts)


def _pool_kernel(prev_ref, cur_ref, next_ref, w_ref, scale_ref, o_ref, *, n_ctx_blocks, ctx_len, lat_len):
    i = pl.program_id(0)
    rb = ROW_BLOCK
    is_ctx = i < n_ctx_blocks
    n_ctx_rows = n_ctx_blocks * rb
    seq_len = jnp.where(is_ctx, ctx_len, lat_len)
    seq_start = jnp.where(is_ctx, (i * rb // ctx_len) * ctx_len,
                          n_ctx_rows + ((i * rb - n_ctx_rows) // lat_len) * lat_len)
    n_cols = rb + 2 * POOL_HALO
    row_pos = i * rb - seq_start + lax.broadcasted_iota(jnp.int32, (rb, n_cols), 0)
    col_pos = i * rb - POOL_HALO - seq_start + lax.broadcasted_iota(jnp.int32, (rb, n_cols), 1)
    cur = cur_ref[...]
    u = jnp.concatenate([prev_ref[...], cur, next_ref[...]], axis=0)
    for g in range(N_POOL_GROUPS):
        cols = slice(g * POOL_GROUP, (g + 1) * POOL_GROUP)
        window = 2 << g
        lo = jnp.clip(row_pos - window // 2, 0, seq_len)
        hi = jnp.clip(row_pos - window // 2 + window, 0, seq_len)
        band = jnp.where((col_pos >= lo) & (col_pos < hi), 1.0, 0.0).astype(BF16)
        total = jnp.dot(band, u[:, cols], preferred_element_type=F32)
        cnt = (hi[:, :1] - lo[:, :1]).astype(F32)
        pooled = total / cnt - cur[:, cols].astype(F32)
        mixed = jnp.dot(pooled.astype(BF16), w_ref[g], preferred_element_type=F32)
        o_ref[:, cols] = (mixed * scale_ref[:, cols]).astype(BF16)


def _pool(u_pool, pool_w, pool_scale, n_ctx_blocks, ctx_len, lat_len):
    t = u_pool.shape[0]
    halo_per_block = ROW_BLOCK // POOL_HALO
    n_halo_blocks = t // POOL_HALO
    assert POOL_HALO >= (2 << (N_POOL_GROUPS - 1)) // 2
    kern = functools.partial(_pool_kernel, n_ctx_blocks=n_ctx_blocks, ctx_len=ctx_len, lat_len=lat_len)
    return pl.pallas_call(
        kern,
        grid=(t // ROW_BLOCK,),
        in_specs=[
            pl.BlockSpec((POOL_HALO, D_MODEL), lambda i: (jnp.maximum(i * halo_per_block - 1, 0), 0)),
            _rows(D_MODEL),
            pl.BlockSpec((POOL_HALO, D_MODEL),
                         lambda i: (jnp.minimum((i + 1) * halo_per_block, n_halo_blocks - 1), 0)),
            _resident(pool_w.shape),
            _resident(pool_scale.shape),
        ],
        out_specs=_rows(D_MODEL),
        out_shape=jax.ShapeDtypeStruct((t, D_MODEL), BF16),
        compiler_params=_cparams("arbitrary"),
        name="pool",
    )(u_pool, u_pool, u_pool, pool_w, pool_scale)


def _softmax_stats(s, m, l):
    m_new = jnp.maximum(m, jnp.max(s, axis=0, keepdims=True))
    p = jnp.exp2(s - m_new)
    alpha = jnp.exp2(m - m_new)
    return m_new, alpha * l + jnp.sum(p, axis=0, keepdims=True), alpha, p.astype(BF16)


def _accumulate(acc, alpha, p, v):
    return alpha * acc + lax.dot_general(v, p, _TN, preferred_element_type=F32)


def _online_softmax(chunks, scores, n_softmax, tq):
    chunks = list(chunks)
    m = [jnp.full((1, tq), -jnp.inf, F32)] * n_softmax
    l = [jnp.zeros((1, tq), F32)] * n_softmax
    acc = [jnp.zeros((HEAD_W, tq), F32)] * n_softmax
    pending = None
    s_next = scores(chunks[0][0])
    for c, (_, v) in enumerate(chunks):
        s = s_next
        if c + 1 < len(chunks):
            s_next = scores(chunks[c + 1][0])
        stats = [_softmax_stats(s[j], m[j], l[j]) for j in range(n_softmax)]
        m = [st[0] for st in stats]
        l = [st[1] for st in stats]
        if pending is not None:
            acc = [_accumulate(acc[j], *pending[j]) for j in range(n_softmax)]
        pending = [(st[2], st[3], v) for st in stats]
    acc = [_accumulate(acc[j], *pending[j]) for j in range(n_softmax)]
    return list(zip(l, acc))


def _key_value_chunks(kv_refs, head, qk_width):
    k_cols = slice(head * qk_width, (head + 1) * qk_width)
    v_cols = slice(head * HEAD_W, (head + 1) * HEAD_W)
    for k_ref, v_ref in zip(kv_refs[0::2], kv_refs[1::2]):
        n = k_ref.shape[0]
        size = min(n, ATTN_K_CHUNK)
        for start in range(0, n, size):
            yield k_ref[start:start + size, k_cols].astype(BF16), v_ref[start:start + size, v_cols].astype(BF16)


def _diff_attn_kernel(lam_ref, q_ref, *refs, n_kv, heads):
    g_ref, o_ref = refs[n_kv], refs[-1]
    for h in range(heads):
        cols = slice(h * HEAD_W, (h + 1) * HEAD_W)
        q = q_ref[:, cols]
        lane = _lane_iota(q.shape)
        q1 = jnp.where(lane < DIFF_HD, q, jnp.zeros_like(q))
        q2 = jnp.where(lane >= DIFF_HD, q, jnp.zeros_like(q))

        def scores(k):
            return (lax.dot_general(k, q1, _NT, preferred_element_type=F32),
                    lax.dot_general(k, q2, _NT, preferred_element_type=F32))

        (l1, a1), (l2, a2) = _online_softmax(_key_value_chunks(refs[:n_kv], h, HEAD_W), scores, 2, q.shape[0])
        o = (a1 / l1 - lam_ref[0] * (a2 / l2)).T
        o_ref[:, cols] = (_rms(o, HEAD_W) * g_ref[...]).astype(o_ref.dtype)


def _mla_attn_kernel(q_ref, *refs, n_kv, heads):
    o_ref = refs[-1]
    for h in range(heads):
        q = q_ref[:, h * MLA_W:(h + 1) * MLA_W]
        ((l, acc),) = _online_softmax(
            _key_value_chunks(refs[:n_kv], h, MLA_W),
            lambda k: (lax.dot_general(k, q, _NT, preferred_element_type=F32),), 1, q.shape[0])
        o_ref[:, h * HEAD_W:(h + 1) * HEAD_W] = (acc / l).T.astype(o_ref.dtype)


def _attention(kernel, q, k, v, cache_k, cache_v, lead, lead_specs, tail, tail_specs, qk_width, geom, name):
    n_ctx, seq, dec_seq = geom
    t = q.shape[0]
    n_heads = v.shape[1] // HEAD_W
    dec_batch, past, _ = cache_k.shape
    tq = ATTN_Q_BLOCK
    nq, q0, k0 = dec_seq // tq, n_ctx // tq, n_ctx // dec_seq
    assert n_ctx % tq == 0 and n_ctx % dec_seq == 0 and dec_seq % tq == 0
    params = _cparams("arbitrary", "arbitrary", "arbitrary")
    seq_block = lambda w: pl.BlockSpec((seq, n_heads * w), lambda b, h, qi: (b, 0))
    ctx = pl.pallas_call(
        functools.partial(kernel, n_kv=2, heads=n_heads),
        grid=(n_ctx // seq, 1, 1),
        in_specs=lead_specs + [seq_block(qk_width), seq_block(qk_width), seq_block(HEAD_W)] + tail_specs,
        out_specs=seq_block(HEAD_W),
        out_shape=jax.ShapeDtypeStruct((n_ctx, n_heads * HEAD_W), BF16),
        compiler_params=params,
        name=name + "_ctx",
    )(*lead, q, k, v, *tail)
    lat_rows = lambda w, first: pl.BlockSpec((tq, w), lambda b, h, qi: (first + b * nq + qi, h))
    cached = lambda w: pl.BlockSpec((None, past, w), lambda b, h, qi: (b, 0, h))
    lat_keys = lambda w: pl.BlockSpec((dec_seq, w), lambda b, h, qi: (k0 + b, h))
    lat = pl.pallas_call(
        functools.partial(kernel, n_kv=4, heads=1),
        grid=(dec_batch, n_heads, nq),
        in_specs=lead_specs + [lat_rows(qk_width, q0), cached(qk_width), cached(HEAD_W), lat_keys(qk_width),
                               lat_keys(HEAD_W)] + tail_specs,
        out_specs=lat_rows(HEAD_W, 0),
        out_shape=jax.ShapeDtypeStruct((t - n_ctx, n_heads * HEAD_W), BF16),
        compiler_params=params,
        name=name + "_lat",
    )(*lead, q, cache_k, cache_v, k, v, *tail)
    return ctx, lat


def _diff_attention(q, k, v, cache_k, cache_v, lam, out_gain, geom):
    lead_specs = [pl.BlockSpec(memory_space=pltpu.SMEM)]
    tail_specs = [pl.BlockSpec((1, HEAD_W), lambda b, h, qi: (0, 0))]
    return _attention(_diff_attn_kernel, q, k, v, cache_k, cache_v, (lam,), lead_specs, (out_gain,), tail_specs,
                      HEAD_W, geom, "diff")


def _mla_attention(q, k, v, cache_k, cache_v, geom):
    return _attention(_mla_attn_kernel, q, k, v, cache_k, cache_v, (), [], (), [], MLA_W, geom, "mla")


def _merge_kernel(x_ref, mod_ref, yp_ref, ydc_ref, ydl_ref, ymc_ref, yml_ref, gate_ref, wb_ref, wo_ref, rw_ref,
                  rb_ref, x1_ref, h2_ref, topw_ref, topi_ref, rank_ref, count_ref, carry_ref, *, n_ctx_blocks):
    @pl.when(pl.program_id(0) == 0)
    def _():
        carry_ref[...] = jnp.zeros_like(carry_ref)

    is_ctx = pl.program_id(0) < n_ctx_blocks
    mod = mod_ref[0]
    merged = jnp.zeros(x_ref.shape, F32)
    branches = (yp_ref[...], jnp.where(is_ctx, ydc_ref[...], ydl_ref[...]),
                jnp.where(is_ctx, ymc_ref[...], yml_ref[...]))
    for r, y in enumerate(branches):
        z = jnp.dot(y, wb_ref[r], preferred_element_type=F32)
        merged = merged + gate_ref[:, r * D_MODEL:(r + 1) * D_MODEL].astype(F32) * z
    y = jnp.dot(merged.astype(BF16), wo_ref[...], preferred_element_type=F32)
    x1 = x_ref[...] + mod[2:3] * y
    x1_ref[...] = x1
    h2 = _modulated_norm(x1, mod[3:4], mod[4:5])
    h2_ref[...] = h2
    logits = jnp.dot(h2, rw_ref[...], preferred_element_type=F32) + rb_ref[...]
    lane = _lane_iota(logits.shape)
    picks = []
    for _ in range(TOP_K):
        mx = jnp.max(logits, axis=-1, keepdims=True)
        first = jnp.min(jnp.where(logits == mx, lane, EXPERT_LANES), axis=-1, keepdims=True)
        hit = lane == first
        picks.append((mx, first, hit))
        logits = jnp.where(hit, -jnp.inf, logits)
    exps = [jnp.exp(v - picks[0][0]) for v, _, _ in picks]
    denom = exps[0] + exps[1] + exps[2] + exps[3]
    rows = logits.shape[0]
    chosen = jnp.zeros(logits.shape, F32)
    for _, _, hit in picks:
        chosen = chosen + jnp.where(hit, 1.0, 0.0)
    earlier = jnp.where(lax.broadcasted_iota(jnp.int32, (rows, rows), 1)
                        < lax.broadcasted_iota(jnp.int32, (rows, rows), 0), 1.0, 0.0).astype(BF16)
    before = jnp.dot(earlier, chosen.astype(BF16), preferred_element_type=F32) + carry_ref[0:1, :]
    topw = jnp.zeros(logits.shape, F32)
    topi = jnp.zeros(logits.shape, jnp.int32)
    rank = jnp.zeros(logits.shape, jnp.int32)
    for k, (e, (_, first, hit)) in enumerate(zip(exps, picks)):
        slot_rank = jnp.sum(jnp.where(hit, before, 0.0), axis=-1, keepdims=True)
        topw = jnp.where(lane == k, e / denom, topw)
        topi = jnp.where(lane == k, first, topi)
        rank = jnp.where(lane == k, slot_rank.astype(jnp.int32), rank)
    topw_ref[...] = topw
    topi_ref[...] = topi
    rank_ref[...] = rank
    carry_ref[...] = carry_ref[...] + jnp.sum(chosen, axis=0, keepdims=True)
    count_ref[...] = carry_ref[...]


def _merge(x, mod, y_pool, y_diff, y_mla, gates, wb, wo, rw, rb, mod_spec, n_ctx_blocks):
    t = x.shape[0]
    consts = (wb, wo, rw, rb)
    lanes = EXPERT_LANES
    ctx_rows = _ctx_rows(D_MODEL, n_ctx_blocks)
    lat_rows = pl.BlockSpec((ROW_BLOCK, D_MODEL), lambda i: (jnp.maximum(i - n_ctx_blocks, 0), 0))
    return pl.pallas_call(
        functools.partial(_merge_kernel, n_ctx_blocks=n_ctx_blocks),
        grid=(t // ROW_BLOCK,),
        in_specs=[_rows(D_MODEL), mod_spec, _rows(D_MODEL), ctx_rows, lat_rows, ctx_rows, lat_rows,
                  _rows(3 * D_MODEL)] + [_resident(a.shape) for a in consts],
        out_specs=[_rows(D_MODEL), _rows(D_MODEL), _rows(lanes), _rows(lanes), _rows(lanes),
                   pl.BlockSpec((8, lanes), lambda i: (0, 0))],
        out_shape=[jax.ShapeDtypeStruct((t, D_MODEL), F32), jax.ShapeDtypeStruct((t, D_MODEL), F32),
                   jax.ShapeDtypeStruct((t, lanes), F32), jax.ShapeDtypeStruct((t, lanes), jnp.int32),
                   jax.ShapeDtypeStruct((t, lanes), jnp.int32), jax.ShapeDtypeStruct((8, lanes), F32)],
        scratch_shapes=[pltpu.VMEM((8, lanes), F32)],
        compiler_params=_cparams("arbitrary"),
        name="merge",
    )(x, mod, y_pool, *y_diff, *y_mla, gates, *consts)


def _slot_layout(counts, topi, rank):
    tile = MOE_TILE
    cnt = counts[0, :N_EXPERTS].astype(jnp.int32)
    padded = (cnt + tile - 1) // tile * tile
    end = jnp.cumsum(padded)
    start = end - padded
    experts = jnp.arange(N_EXPERTS, dtype=jnp.int32)
    idx4 = topi[:, :TOP_K]
    slot_start = jnp.sum(jnp.where(idx4[:, :, None] == experts, start, 0), axis=-1)
    pos = (slot_start + rank[:, :TOP_K]).reshape(-1)
    n_tiles = (topi.shape[0] * TOP_K + N_EXPERTS * tile) // tile
    tile_row = jnp.arange(n_tiles, dtype=jnp.int32) * tile
    tile_expert = jnp.minimum(jnp.sum(tile_row[:, None] >= end, axis=-1), N_EXPERTS - 1).astype(jnp.int32)
    n_used = (end[-1] // tile).astype(jnp.int32)
    tile_src = jnp.minimum(jnp.arange(n_tiles, dtype=jnp.int32), n_used - 1)
    ragged_last = jnp.any((cnt % tile != 0) & (tile_row[:, None] == end - tile), axis=-1)
    fill_tile = (ragged_last | (tile_row >= end[-1])).astype(jnp.int32)
    return pos, tile_expert, tile_src, n_used.reshape(1), fill_tile


def _row_copy(src, src_row, dst, dst_row, sem):
    return pltpu.make_async_copy(src.at[pl.ds(src_row, 1)], dst.at[pl.ds(dst_row, 1)], sem)


def _dispatch_kernel(fill_tile_ref, pos_ref, h_ref, hs_ref, zero_ref, fill_sem, row_sem):
    i = pl.program_id(0)
    slots = DISPATCH_ROWS * TOP_K

    @pl.when(i == 0)
    def _():
        zero_ref[...] = jnp.zeros_like(zero_ref)

        def fill_copy(tile):
            row = pl.multiple_of(tile * MOE_TILE, MOE_TILE)
            return pltpu.make_async_copy(zero_ref, hs_ref.at[pl.ds(row, MOE_TILE)], fill_sem)

        def start_fill(tile, carry):
            @pl.when(fill_tile_ref[tile] != 0)
            def _():
                fill_copy(tile).start()
            return carry

        def wait_fill(tile, carry):
            @pl.when(fill_tile_ref[tile] != 0)
            def _():
                fill_copy(tile).wait()
            return carry

        lax.fori_loop(0, hs_ref.shape[0] // MOE_TILE, start_fill, 0)
        lax.fori_loop(0, hs_ref.shape[0] // MOE_TILE, wait_fill, 0)

    def issue(r, carry):
        slot = (i * DISPATCH_ROWS + r) * TOP_K
        for k in range(TOP_K):
            _row_copy(h_ref, r, hs_ref, pos_ref[slot + k], row_sem).start()
        return carry

    def drain(j, carry):
        for _ in range(WAIT_UNROLL):
            _row_copy(h_ref, 0, hs_ref, 0, row_sem).wait()
        return carry

    lax.fori_loop(0, DISPATCH_ROWS, issue, 0, unroll=ISSUE_UNROLL)
    lax.fori_loop(0, slots // WAIT_UNROLL, drain, 0)


def _dispatch(h2, pos, fill_tile):
    t = h2.shape[0]
    n_rows = fill_tile.shape[0] * MOE_TILE
    return pl.pallas_call(
        _dispatch_kernel,
        grid_spec=pltpu.PrefetchScalarGridSpec(
            num_scalar_prefetch=2,
            grid=(t // DISPATCH_ROWS,),
            in_specs=[pl.BlockSpec((DISPATCH_ROWS, D_MODEL), lambda i, fill_ref, pos_ref: (i, 0))],
            out_specs=pl.BlockSpec(memory_space=pl.ANY),
            scratch_shapes=[pltpu.VMEM((MOE_TILE, D_MODEL), F32), pltpu.SemaphoreType.DMA,
                            pltpu.SemaphoreType.DMA],
        ),
        out_shape=jax.ShapeDtypeStruct((n_rows, D_MODEL), F32),
        compiler_params=_cparams("arbitrary"),
        name="dispatch",
    )(fill_tile, pos, h2)


def _expert_kernel(tile_expert_ref, tile_src_ref, n_used_ref, hs_ref, wgu_ref, bgu_ref, wd_ref, bd_ref, ys_ref,
                   wgu_bf_ref, wd_bf_ref):
    i = pl.program_id(0)
    occupied = i < n_used_ref[0]
    new_expert = (i == 0) | (tile_expert_ref[i] != tile_expert_ref[jnp.maximum(i - 1, 0)])

    @pl.when(occupied & new_expert)
    def _():
        wgu_bf_ref[...] = wgu_ref[...].astype(BF16)
        wd_bf_ref[...] = wd_ref[...].astype(BF16)

    @pl.when(occupied)
    def _():
        gu = jnp.dot(hs_ref[...].astype(BF16), wgu_bf_ref[...], preferred_element_type=F32) + bgu_ref[0]
        gate = jnp.minimum(gu[:, :D_FF], SWIGLU_LIMIT)
        up = jnp.clip(gu[:, D_FF:], -SWIGLU_LIMIT, SWIGLU_LIMIT)
        act = (up + 1.0) * gate * _sigmoid(SWIGLU_ALPHA * gate)
        ys_ref[...] = jnp.dot(act.astype(BF16), wd_bf_ref[...], preferred_element_type=F32) + bd_ref[0]

    @pl.when(jnp.logical_not(occupied))
    def _():
        ys_ref[...] = jnp.zeros_like(ys_ref)


def _experts(hs, tile_expert, tile_src, n_used, layer, wgu, bgu, wd, bd):
    n_rows = hs.shape[0]
    rows = lambda i, te, ts, nu: (ts[i], 0)
    bias = lambda i, te, ts, nu: (te[i], 0, 0)
    weight = lambda i, te, ts, nu: (layer, te[i], 0, 0)
    return pl.pallas_call(
        _expert_kernel,
        grid_spec=pltpu.PrefetchScalarGridSpec(
            num_scalar_prefetch=3,
            grid=(n_rows // MOE_TILE,),
            in_specs=[
                pl.BlockSpec((MOE_TILE, D_MODEL), rows),
                pl.BlockSpec((None, None, D_MODEL, 2 * D_FF), weight),
                pl.BlockSpec((1, 1, 2 * D_FF), bias),
                pl.BlockSpec((None, None, D_FF, D_MODEL), weight),
                pl.BlockSpec((1, 1, D_MODEL), bias),
            ],
            out_specs=pl.BlockSpec((MOE_TILE, D_MODEL), lambda i, te, ts, nu: (i, 0)),
            scratch_shapes=[pltpu.VMEM((D_MODEL, 2 * D_FF), BF16), pltpu.VMEM((D_FF, D_MODEL), BF16)],
        ),
        out_shape=jax.ShapeDtypeStruct((n_rows, D_MODEL), F32),
        compiler_params=_cparams("arbitrary"),
        name="experts",
    )(tile_expert, tile_src, n_used, hs, wgu, bgu, wd, bd)


def _combine_kernel(pos_ref, ys_ref, w_ref, x1_ref, mod_ref, o_ref, buf_ref, sems):
    i = pl.program_id(0)
    slots = COMBINE_ROWS * TOP_K

    def gather(step, buf):
        def issue(r, carry):
            token = step * COMBINE_ROWS + r
            for k in range(TOP_K):
                _row_copy(ys_ref, pos_ref[token * TOP_K + k], buf_ref.at[buf, k], r, sems.at[buf]).start()
            return carry

        lax.fori_loop(0, COMBINE_ROWS, issue, 0, unroll=ISSUE_UNROLL)

    @pl.when(i == 0)
    def _():
        gather(0, 0)

    @pl.when(i + 1 < pl.num_programs(0))
    def _():
        gather(i + 1, (i + 1) % 2)

    cur = i % 2

    def drain(j, carry):
        for _ in range(WAIT_UNROLL):
            _row_copy(ys_ref, 0, buf_ref.at[cur, 0], 0, sems.at[cur]).wait()
        return carry

    lax.fori_loop(0, slots // WAIT_UNROLL, drain, 0)
    w = w_ref[...]
    acc = w[:, 0:1] * buf_ref[cur, 0]
    for k in range(1, TOP_K):
        acc = acc + w[:, k:k + 1] * buf_ref[cur, k]
    o_ref[...] = x1_ref[...] + mod_ref[0][5:6] * acc


def _combine(ys, pos, topw, x1, mod, n_ctx_rows, lat_len):
    t = x1.shape[0]
    rb = COMBINE_ROWS
    n_ctx_blocks = n_ctx_rows // rb

    def mod_row(i, pos_ref):
        return (jnp.where(i < n_ctx_blocks, 0, 1 + (i - n_ctx_blocks) // (lat_len // rb)), 0, 0)

    return pl.pallas_call(
        _combine_kernel,
        grid_spec=pltpu.PrefetchScalarGridSpec(
            num_scalar_prefetch=1,
            grid=(t // rb,),
            in_specs=[
                pl.BlockSpec(memory_space=pl.ANY),
                pl.BlockSpec((rb, EXPERT_LANES), lambda i, pos_ref: (i, 0)),
                pl.BlockSpec((rb, D_MODEL), lambda i, pos_ref: (i, 0)),
                pl.BlockSpec((1, 8, D_MODEL), mod_row),
            ],
            out_specs=pl.BlockSpec((rb, D_MODEL), lambda i, pos_ref: (i, 0)),
            scratch_shapes=[pltpu.VMEM((2, TOP_K, rb, D_MODEL), F32), pltpu.SemaphoreType.DMA((2,))],
        ),
        out_shape=jax.ShapeDtypeStruct((t, D_MODEL), F32),
        compiler_params=_cparams("arbitrary"),
        name="combine",
    )(pos, ys, topw, x1, mod)


def _rope_tables(n_lat_tokens, rot_dim):
    n_rows = n_lat_tokens // GRID_W
    row = jnp.repeat(jnp.arange(n_rows, dtype=F32), GRID_W)
    col = jnp.tile(jnp.arange(GRID_W, dtype=F32), n_rows)
    n_freq = rot_dim // 4
    inv_freq = ROPE_THETA ** (-jnp.arange(n_freq, dtype=F32) / n_freq)
    ang = jnp.concatenate([row[:, None] * inv_freq, col[:, None] * inv_freq], axis=-1)
    return jnp.cos(ang), jnp.sin(ang)


def _token_tables(cos_lane, sin_lane, n_ctx_rows, n_lat_seqs):
    width = cos_lane.shape[1]
    cos = jnp.concatenate([jnp.ones((n_ctx_rows, width), F32)] + [cos_lane] * n_lat_seqs, axis=0)
    sin = jnp.concatenate([jnp.zeros((n_ctx_rows, width), F32)] + [sin_lane] * n_lat_seqs, axis=0)
    return cos, sin


def _pack_heads(w, n_heads, lo, hi, width):
    k = w.shape[0]
    per_head = w.shape[1] // n_heads
    part = w.reshape(k, n_heads, per_head)[:, :, lo:hi]
    part = jnp.pad(part, ((0, 0), (0, 0), (0, width - (hi - lo))))
    return part.reshape(k, n_heads * width)


def kernel(x_prompt, x_sample, c, cache_diff_k, cache_diff_v, cache_mla_ckv, cache_mla_krope, c_ctx, w_ada, b_ada,
           w_in, pool_w, pool_scale, diff_q_norm, diff_k_norm, diff_lambda, diff_out_norm, mla_q_a_norm, w_uq,
           mla_kv_a_norm, w_ukv, mla_q_norm, mla_k_norm, w_branch, w_out, router_w, router_b, moe_w_gu, moe_b_gu,
           moe_w_down, moe_b_down):
    batch, seq, d = x_prompt.shape
    dec_batch, dec_seq, _ = x_sample.shape
    depth = w_ada.shape[0]
    past = cache_diff_k.shape[2]
    n_ctx = batch * seq
    n_lat = dec_batch * dec_seq
    n_ctx_blocks = n_ctx // ROW_BLOCK
    mod_spec = _mod_spec(n_ctx_blocks, dec_seq // ROW_BLOCK)

    x = jnp.concatenate([x_prompt.reshape(n_ctx, d), x_sample.reshape(n_lat, d)], axis=0)
    cond8 = jnp.concatenate([c_ctx[None], c, jnp.zeros((8 - 1 - dec_batch, d), F32)], axis=0)
    mod_all = _ada(cond8, w_ada, b_ada).reshape(depth, 8, 6, d)[:, :1 + dec_batch]
    mod_all = jnp.pad(mod_all, ((0, 0), (0, 0), (0, 2), (0, 0)))

    cos_d, sin_d = _rope_tables(dec_seq, DIFF_HD)
    cos_m, sin_m = _rope_tables(dec_seq, MLA_ROPE)
    cos_a, sin_a = _token_tables(jnp.concatenate([cos_d] * 4, axis=1),
                                 jnp.concatenate([-sin_d, sin_d, -sin_d, sin_d], axis=1), n_ctx, dec_batch)
    pad = MLA_W - MLA_QK
    cos_b, sin_b = _token_tables(jnp.concatenate([cos_m, cos_m, jnp.ones((dec_seq, pad), F32)], axis=1),
                                 jnp.concatenate([-sin_m, sin_m, jnp.zeros((dec_seq, pad), F32)], axis=1),
                                 n_ctx, dec_batch)

    offs = np.cumsum((0, 1024, 1024, 1024, 1024, MLA_Q_RANK, MLA_KV_RANK, MLA_ROPE, 3 * 1024))
    new_dk, new_dv, new_ckv, new_kr = [], [], [], []
    for l in range(depth):
        lam_init = 0.8 - 0.6 * math.exp(-0.3 * l)
        wl = w_in[l]
        seg = [wl[:, offs[k]:offs[k + 1]] for k in range(8)]
        w_a = jnp.concatenate(seg[0:4] + [seg[7]], axis=1).astype(BF16)
        kr_wide = jnp.pad(seg[6], ((0, 0), (MLA_NOPE, MLA_W - MLA_NOPE - MLA_ROPE)))
        w_b = jnp.concatenate([seg[4], seg[5], kr_wide], axis=1).astype(BF16)
        wuq = _pack_heads(w_uq[l], MLA_HEADS, 0, MLA_QK, MLA_W).astype(BF16)
        wkn = _pack_heads(w_ukv[l], MLA_HEADS, 0, MLA_NOPE, MLA_W).astype(BF16)
        wv = _pack_heads(w_ukv[l], MLA_HEADS, MLA_NOPE, MLA_NOPE + HEAD_W, HEAD_W).astype(BF16)
        gq_d = jnp.tile(diff_q_norm[l], 2 * DIFF_HEADS)[None]
        gk_d = jnp.tile(diff_k_norm[l], 2 * DIFF_HEADS)[None]
        gq_m = jnp.tile(jnp.pad(mla_q_norm[l], (0, MLA_W - MLA_QK)), MLA_HEADS)[None]
        gk_m = jnp.tile(jnp.pad(mla_k_norm[l], (0, MLA_W - MLA_QK)), MLA_HEADS)[None]
        lp = diff_lambda[l]
        lam = (jnp.exp(jnp.sum(lp[0] * lp[1])) - jnp.exp(jnp.sum(lp[2] * lp[3])) + lam_init).reshape(1)
        out_gain = (diff_out_norm[l] * (1.0 - lam_init))[None]
        mod = mod_all[l]

        u_pool, dq, dk_f, dk_b, dv_f, dv_b, gates = _proj_a(x, mod, w_a, gq_d, gk_d, cos_a, sin_a, mod_spec,
                                                            n_ctx_blocks)
        mq, ckv_n, kr, mk, mv = _proj_b(x, mod, w_b, mla_q_a_norm[l][None], mla_kv_a_norm[l][None], wuq, wkn, wv,
                                        gq_m, gk_m, cos_b, sin_b, mod_spec, n_ctx_blocks)
        new_dk.append(dk_f[:n_ctx].reshape(batch, seq, DIFF_HEADS, HEAD_W))
        new_dv.append(dv_f[:n_ctx].reshape(batch, seq, DIFF_HEADS, HEAD_W))
        new_ckv.append(ckv_n[:n_ctx].reshape(batch, seq, MLA_KV_RANK))
        new_kr.append(kr[:n_ctx].reshape(batch, seq, MLA_ROPE))

        y_pool = _pool(u_pool, pool_w[l].astype(BF16), pool_scale[l][None], n_ctx_blocks, seq, dec_seq)

        c_krw = jnp.pad(cache_mla_krope[:, l].reshape(dec_batch * past, MLA_ROPE),
                        ((0, 0), (MLA_NOPE, MLA_W - MLA_NOPE - MLA_ROPE)))
        c_mk, c_mv = _cache_kv(cache_mla_ckv[:, l].reshape(dec_batch * past, MLA_KV_RANK), c_krw, wkn, wv, gk_m)

        geom = (n_ctx, seq, dec_seq)
        y_diff = _diff_attention(dq, dk_b, dv_b, cache_diff_k[:, l].reshape(dec_batch, past, d),
                                 cache_diff_v[:, l].reshape(dec_batch, past, d), lam, out_gain, geom)
        y_mla = _mla_attention(mq, mk, mv, c_mk.reshape(dec_batch, past, -1), c_mv.reshape(dec_batch, past, -1),
                               geom)

        rw = jnp.pad(router_w[l], ((0, 0), (0, EXPERT_LANES - N_EXPERTS)))
        rb = jnp.pad(router_b[l], (0, EXPERT_LANES - N_EXPERTS), constant_values=NEG_BIG)[None]
        x1, h2, topw, topi, rank, counts = _merge(x, mod, y_pool, y_diff, y_mla, gates, w_branch[l].astype(BF16),
                                                  w_out[l].astype(BF16), rw, rb, mod_spec, n_ctx_blocks)
        pos, tile_expert, tile_src, n_used, fill_tile = _slot_layout(counts, topi, rank)
        hs = _dispatch(h2, pos, fill_tile)
        ys = _experts(hs, tile_expert, tile_src, n_used, l, moe_w_gu, moe_b_gu[l][:, None, :],
                      moe_w_down, moe_b_down[l][:, None, :])
        x = _combine(ys, pos, topw, x1, mod, n_ctx, dec_seq)

    return (x[:n_ctx].reshape(batch, seq, d), x[n_ctx:].reshape(dec_batch, dec_seq, d),
            jnp.stack(new_dk, axis=1), jnp.stack(new_dv, axis=1), jnp.stack(new_ckv, axis=1),
            jnp.stack(new_kr, axis=1))
```

```python
import functools
import math

import jax
import jax.numpy as jnp
import numpy as np
from jax import lax
from jax.experimental import pallas as pl
from jax.experimental.pallas import tpu as pltpu

F32 = jnp.float32
BF16 = jnp.bfloat16

D_MODEL = 1024
N_POOL_GROUPS = 4
POOL_GROUP = D_MODEL // N_POOL_GROUPS
DIFF_HD = 64
DIFF_HEADS = 8
HEAD_W = 128
MLA_HEADS = 8
MLA_NOPE = 128
MLA_ROPE = 64
MLA_QK = MLA_NOPE + MLA_ROPE
MLA_W = 256
MLA_Q_RANK = 768
MLA_KV_RANK = 512
N_EXPERTS = 32
TOP_K = 4
D_FF = 1024
EXPERT_LANES = 128
SWIGLU_ALPHA = 1.702
SWIGLU_LIMIT = 7.0
ROPE_THETA = 10000.0
NORM_EPS = 1e-6
GRID_W = 64
LOG2E = math.log2(math.e)
NEG_BIG = -1e30
VMEM_LIMIT = 56 * 1024 * 1024

ROW_BLOCK = 256
POOL_HALO = 16
MOE_TILE = 512
DISPATCH_ROWS = 256
COMBINE_ROWS = 128
WAIT_UNROLL = 16
ISSUE_UNROLL = 4
ATTN_Q_BLOCK = 1024
ATTN_K_CHUNK = 512

_NT = (((1,), (1,)), ((), ()))
_TN = (((0,), (0,)), ((), ()))


def _cparams(*sem):
    return pltpu.CompilerParams(dimension_semantics=sem, vmem_limit_bytes=VMEM_LIMIT)


def _resident(shape):
    nd = len(shape)
    return pl.BlockSpec(shape, lambda *_: (0,) * nd, pipeline_mode=pl.Buffered(1))


def _sigmoid(x):
    return 1.0 / (1.0 + jnp.exp(-x))


def _lane_iota(shape):
    return lax.broadcasted_iota(jnp.int32, shape, len(shape) - 1)


def _rms(x, width):
    ss = jnp.sum(x * x, axis=-1, keepdims=True)
    return x * lax.rsqrt(ss * (1.0 / width) + NORM_EPS)


def _rope(x, cos, sin):
    n = x.shape[-1]
    lane = _lane_iota(x.shape)
    partner = jnp.where((lane & 32) != 0, pltpu.roll(x, 32, 1), pltpu.roll(x, n - 32, 1))
    return x * cos + partner * sin


def _modulated_norm(x, shift, scale):
    return _rms(x, D_MODEL) * (1.0 + scale) + shift


def _ada_kernel(cond_ref, w_ref, b_ref, o_ref):
    cnd = cond_ref[...]
    act = cnd * _sigmoid(cnd)
    o_ref[0] = jnp.dot(act, w_ref[0], preferred_element_type=F32) + b_ref[0]


def _ada(cond8, w_ada, b_ada):
    depth = w_ada.shape[0]
    n_chunk = w_ada.shape[2] // D_MODEL
    return pl.pallas_call(
        _ada_kernel,
        grid=(depth, n_chunk),
        in_specs=[
            pl.BlockSpec((8, D_MODEL), lambda l, j: (0, 0)),
            pl.BlockSpec((1, D_MODEL, D_MODEL), lambda l, j: (l, 0, j)),
            pl.BlockSpec((1, 1, D_MODEL), lambda l, j: (l, 0, j)),
        ],
        out_specs=pl.BlockSpec((1, 8, D_MODEL), lambda l, j: (l, 0, j)),
        out_shape=jax.ShapeDtypeStruct((depth, 8, n_chunk * D_MODEL), F32),
        compiler_params=_cparams("arbitrary", "arbitrary"),
        name="ada",
    )(cond8, w_ada, b_ada.reshape(depth, 1, -1))


def _diff_qk_norm(u, gain, cos, sin):
    outs = []
    for h in range(DIFF_HEADS):
        c = u[:, h * HEAD_W:(h + 1) * HEAD_W]
        lane = _lane_iota(c.shape)
        sq = c * c
        lo = jnp.sum(jnp.where(lane < DIFF_HD, sq, 0.0), axis=-1, keepdims=True)
        hi = jnp.sum(jnp.where(lane >= DIFF_HD, sq, 0.0), axis=-1, keepdims=True)
        ms = jnp.where(lane < DIFF_HD, lo, hi) * (1.0 / DIFF_HD)
        y = c * lax.rsqrt(ms + NORM_EPS) * gain[:, h * HEAD_W:(h + 1) * HEAD_W]
        outs.append((y, _rope(y, cos, sin)))
    return outs


def _proj_a_kernel(x_ref, mod_ref, w_ref, gq_ref, gk_ref, cos_ref, sin_ref,
                   pool_ref, dq_ref, dkf_ref, dkb_ref, dvf_ref, dvb_ref, gate_ref):
    mod = mod_ref[0]
    h = _modulated_norm(x_ref[...], mod[0:1], mod[1:2]).astype(BF16)
    cos, sin = cos_ref[...], sin_ref[...]

    def seg(k):
        return jnp.dot(h, w_ref[:, k * D_MODEL:(k + 1) * D_MODEL], preferred_element_type=F32)

    u_pool, u_dq = seg(0), seg(1)
    pool_ref[...] = u_pool.astype(BF16)
    u_dk = seg(2)
    q_scale = DIFF_HD ** -0.5 * LOG2E
    for hd, (_, roped) in enumerate(_diff_qk_norm(u_dq, gq_ref[...], cos, sin)):
        dq_ref[:, hd * HEAD_W:(hd + 1) * HEAD_W] = (roped * q_scale).astype(BF16)
    dv = seg(3)
    for hd, (plain, roped) in enumerate(_diff_qk_norm(u_dk, gk_ref[...], cos, sin)):
        dkf_ref[:, hd * HEAD_W:(hd + 1) * HEAD_W] = plain
        dkb_ref[:, hd * HEAD_W:(hd + 1) * HEAD_W] = roped.astype(BF16)
    u_gate = seg(4)
    dvf_ref[...] = dv
    dvb_ref[...] = dv.astype(BF16)
    for r in range(3):
        u_next = seg(5 + r) if r < 2 else None
        gate_ref[:, r * D_MODEL:(r + 1) * D_MODEL] = _sigmoid(u_gate).astype(BF16)
        u_gate = u_next


def _mla_heads(u, gain, cos, sin, rope):
    outs = []
    for h in range(MLA_HEADS):
        c = u[:, h * MLA_W:(h + 1) * MLA_W]
        y = _rms(c, MLA_QK) * gain[:, h * MLA_W:(h + 1) * MLA_W]
        if rope:
            y = jnp.concatenate([y[:, :MLA_NOPE], _rope(y[:, MLA_NOPE:], cos, sin)], axis=-1)
        outs.append(y)
    return outs


def _mla_kv(ckv_n, krw, wkn_ref, wv_ref, gk, cos, sin, rope, k_ref, v_ref):
    kn = jnp.dot(ckv_n, wkn_ref[...], preferred_element_type=F32)
    v_ref[...] = jnp.dot(ckv_n, wv_ref[...], preferred_element_type=F32).astype(BF16)
    kfull = jnp.concatenate([kn[:, h * MLA_W:(h + 1) * MLA_W] + krw for h in range(MLA_HEADS)], axis=-1)
    for hd, y in enumerate(_mla_heads(kfull, gk, cos, sin, rope)):
        k_ref[:, hd * MLA_W:(hd + 1) * MLA_W] = y.astype(BF16)


def _proj_b_kernel(x_ref, mod_ref, w_ref, gqa_ref, gkva_ref, wuq_ref, wkn_ref, wv_ref, gq_ref, gk_ref,
                   cos_ref, sin_ref, mq_ref, ckv_ref, kr_ref, mk_ref, mv_ref):
    mod = mod_ref[0]
    h = _modulated_norm(x_ref[...], mod[0:1], mod[1:2]).astype(BF16)
    cos, sin = cos_ref[...], sin_ref[...]
    cq = jnp.dot(h, w_ref[:, :MLA_Q_RANK], preferred_element_type=F32)
    ckv = jnp.dot(h, w_ref[:, MLA_Q_RANK:MLA_Q_RANK + MLA_KV_RANK], preferred_element_type=F32)
    krw = jnp.dot(h, w_ref[:, MLA_Q_RANK + MLA_KV_RANK:], preferred_element_type=F32)
    cq_n = (_rms(cq, MLA_Q_RANK) * gqa_ref[...]).astype(BF16)
    mq = jnp.dot(cq_n, wuq_ref[...], preferred_element_type=F32)
    ckv_n = _rms(ckv, MLA_KV_RANK) * gkva_ref[...]
    ckv_ref[...] = ckv_n
    kr_ref[...] = krw[:, MLA_NOPE:MLA_NOPE + MLA_ROPE]
    _mla_kv(ckv_n.astype(BF16), krw, wkn_ref, wv_ref, gk_ref[...], cos, sin, True, mk_ref, mv_ref)
    q_scale = MLA_QK ** -0.5 * LOG2E
    for hd, y in enumerate(_mla_heads(mq, gq_ref[...], cos, sin, True)):
        mq_ref[:, hd * MLA_W:(hd + 1) * MLA_W] = (y * q_scale).astype(BF16)


def _cache_kv_kernel(ckv_ref, krw_ref, wkn_ref, wv_ref, gk_ref, mk_ref, mv_ref):
    _mla_kv(ckv_ref[...].astype(BF16), krw_ref[...], wkn_ref, wv_ref, gk_ref[...], None, None, False,
            mk_ref, mv_ref)


def _mod_row(n_ctx_blocks, blocks_per_lat):
    return lambda i: jnp.where(i < n_ctx_blocks, 0, 1 + (i - n_ctx_blocks) // blocks_per_lat)


def _mod_spec(mod_row):
    return pl.BlockSpec((1, 8, D_MODEL), lambda i: (mod_row(i), 0, 0))


def _rows(width):
    return pl.BlockSpec((ROW_BLOCK, width), lambda i: (i, 0))


def _ctx_rows(width, n_ctx_blocks):
    return pl.BlockSpec((ROW_BLOCK, width), lambda i: (jnp.minimum(i, n_ctx_blocks - 1), 0))


def _proj_outputs(t, n_ctx_blocks, widths, dtypes, ctx_only):
    spare = lambda w: pl.BlockSpec((ROW_BLOCK, w), lambda i: (jnp.minimum(i, n_ctx_blocks), 0))
    specs = [spare(w) if c else _rows(w) for w, c in zip(widths, ctx_only)]
    shapes = [jax.ShapeDtypeStruct(((n_ctx_blocks + 1) * ROW_BLOCK if c else t, w), dt)
              for w, dt, c in zip(widths, dtypes, ctx_only)]
    return specs, shapes


def _proj_a(x, mod, w_a, gq, gk, cos, sin, mod_spec, n_ctx_blocks):
    t = x.shape[0]
    widths = (D_MODEL,) * 6 + (3 * D_MODEL,)
    dtypes = (BF16, BF16, F32, BF16, F32, BF16, BF16)
    out_specs, out_shape = _proj_outputs(t, n_ctx_blocks, widths, dtypes, (0, 0, 1, 0, 1, 0, 0))
    return pl.pallas_call(
        _proj_a_kernel,
        grid=(t // ROW_BLOCK,),
        in_specs=[_rows(D_MODEL), mod_spec, _resident(w_a.shape), _resident(gq.shape), _resident(gk.shape),
                  _rows(HEAD_W), _rows(HEAD_W)],
        out_specs=out_specs,
        out_shape=out_shape,
        compiler_params=_cparams("arbitrary"),
        name="proj_a",
    )(x, mod, w_a, gq, gk, cos, sin)


def _proj_b(x, mod, w_b, gqa, gkva, wuq, wkn, wv, gq, gk, cos, sin, mod_spec, n_ctx_blocks):
    t = x.shape[0]
    widths = (MLA_HEADS * MLA_W, MLA_KV_RANK, MLA_ROPE, MLA_HEADS * MLA_W, D_MODEL)
    dtypes = (BF16, F32, F32, BF16, BF16)
    out_specs, out_shape = _proj_outputs(t, n_ctx_blocks, widths, dtypes, (0, 1, 1, 0, 0))
    consts = (w_b, gqa, gkva, wuq, wkn, wv, gq, gk)
    return pl.pallas_call(
        _proj_b_kernel,
        grid=(t // ROW_BLOCK,),
        in_specs=[_rows(D_MODEL), mod_spec] + [_resident(a.shape) for a in consts] + [_rows(HEAD_W), _rows(HEAD_W)],
        out_specs=out_specs,
        out_shape=out_shape,
        compiler_params=_cparams("arbitrary"),
        name="proj_b",
    )(x, mod, *consts, cos, sin)


def _cache_kv(ckv, krw, wkn, wv, gk):
    n = ckv.shape[0]
    widths = (MLA_HEADS * MLA_W, D_MODEL)
    consts = (wkn, wv, gk)
    return pl.pallas_call(
        _cache_kv_kernel,
        grid=(n // ROW_BLOCK,),
        in_specs=[_rows(MLA_KV_RANK), _rows(MLA_W)] + [_resident(a.shape) for a in consts],
        out_specs=[_rows(w) for w in widths],
        out_shape=[jax.ShapeDtypeStruct((n, w), BF16) for w in widths],
        compiler_params=_cparams("arbitrary"),
        name="cache_kv",
    )(ckv, krw, *consts)


def _pool_kernel(prev_ref, cur_ref, next_ref, w_ref, scale_ref, o_ref, *, n_ctx_blocks, ctx_len, lat_len):
    i = pl.program_id(0)
    rb = ROW_BLOCK
    is_ctx = i < n_ctx_blocks
    n_ctx_rows = n_ctx_blocks * rb
    seq_len = jnp.where(is_ctx, ctx_len, lat_len)
    seq_start = jnp.where(is_ctx, (i * rb // ctx_len) * ctx_len,
                          n_ctx_rows + ((i * rb - n_ctx_rows) // lat_len) * lat_len)
    n_cols = rb + 2 * POOL_HALO
    row_pos = i * rb - seq_start + lax.broadcasted_iota(jnp.int32, (rb, n_cols), 0)
    col_pos = i * rb - POOL_HALO - seq_start + lax.broadcasted_iota(jnp.int32, (rb, n_cols), 1)
    cur = cur_ref[...]
    u = jnp.concatenate([prev_ref[...], cur, next_ref[...]], axis=0)
    for g in range(N_POOL_GROUPS):
        cols = slice(g * POOL_GROUP, (g + 1) * POOL_GROUP)
        window = 2 << g
        lo = jnp.clip(row_pos - window // 2, 0, seq_len)
        hi = jnp.clip(row_pos - window // 2 + window, 0, seq_len)
        band = jnp.where((col_pos >= lo) & (col_pos < hi), 1.0, 0.0).astype(BF16)
        total = jnp.dot(band, u[:, cols], preferred_element_type=F32)
        cnt = (hi[:, :1] - lo[:, :1]).astype(F32)
        pooled = total / cnt - cur[:, cols].astype(F32)
        mixed = jnp.dot(pooled.astype(BF16), w_ref[g], preferred_element_type=F32)
        o_ref[:, cols] = (mixed * scale_ref[:, cols]).astype(BF16)


def _pool(u_pool, pool_w, pool_scale, n_ctx_blocks, ctx_len, lat_len):
    t = u_pool.shape[0]
    halo_per_block = ROW_BLOCK // POOL_HALO
    n_halo_blocks = t // POOL_HALO
    assert POOL_HALO >= (2 << (N_POOL_GROUPS - 1)) // 2
    kern = functools.partial(_pool_kernel, n_ctx_blocks=n_ctx_blocks, ctx_len=ctx_len, lat_len=lat_len)
    return pl.pallas_call(
        kern,
        grid=(t // ROW_BLOCK,),
        in_specs=[
            pl.BlockSpec((POOL_HALO, D_MODEL), lambda i: (jnp.maximum(i * halo_per_block - 1, 0), 0)),
            _rows(D_MODEL),
            pl.BlockSpec((POOL_HALO, D_MODEL),
                         lambda i: (jnp.minimum((i + 1) * halo_per_block, n_halo_blocks - 1), 0)),
            _resident(pool_w.shape),
            _resident(pool_scale.shape),
        ],
        out_specs=_rows(D_MODEL),
        out_shape=jax.ShapeDtypeStruct((t, D_MODEL), BF16),
        compiler_params=_cparams("arbitrary"),
        name="pool",
    )(u_pool, u_pool, u_pool, pool_w, pool_scale)


def _softmax_stats(s, m, l):
    m_new = jnp.maximum(m, jnp.max(s, axis=0, keepdims=True))
    p = jnp.exp2(s - m_new)
    alpha = jnp.exp2(m - m_new)
    return m_new, alpha * l + jnp.sum(p, axis=0, keepdims=True), alpha, p.astype(BF16)


def _accumulate(acc, alpha, p, v):
    return alpha * acc + lax.dot_general(v, p, _TN, preferred_element_type=F32)


def _online_softmax(chunks, scores, n_softmax, tq):
    chunks = list(chunks)
    m = [jnp.full((1, tq), -jnp.inf, F32)] * n_softmax
    l = [jnp.zeros((1, tq), F32)] * n_softmax
    acc = [jnp.zeros((HEAD_W, tq), F32)] * n_softmax
    pending = None
    s_next = scores(chunks[0][0])
    for c, (_, v) in enumerate(chunks):
        s = s_next
        if c + 1 < len(chunks):
            s_next = scores(chunks[c + 1][0])
        stats = [_softmax_stats(s[j], m[j], l[j]) for j in range(n_softmax)]
        m = [st[0] for st in stats]
        l = [st[1] for st in stats]
        if pending is not None:
            acc = [_accumulate(acc[j], *pending[j]) for j in range(n_softmax)]
        pending = [(st[2], st[3], v) for st in stats]
    acc = [_accumulate(acc[j], *pending[j]) for j in range(n_softmax)]
    return list(zip(l, acc))


def _key_value_chunks(kv_refs, head, qk_width):
    k_cols = slice(head * qk_width, (head + 1) * qk_width)
    v_cols = slice(head * HEAD_W, (head + 1) * HEAD_W)
    for k_ref, v_ref in zip(kv_refs[0::2], kv_refs[1::2]):
        n = k_ref.shape[0]
        size = min(n, ATTN_K_CHUNK)
        for start in range(0, n, size):
            yield k_ref[start:start + size, k_cols].astype(BF16), v_ref[start:start + size, v_cols].astype(BF16)


def _diff_attn_kernel(lam_ref, q_ref, *refs, n_kv, heads):
    g_ref, o_ref = refs[n_kv], refs[-1]
    for h in range(heads):
        cols = slice(h * HEAD_W, (h + 1) * HEAD_W)
        q = q_ref[:, cols]
        lane = _lane_iota(q.shape)
        q1 = jnp.where(lane < DIFF_HD, q, jnp.zeros_like(q))
        q2 = jnp.where(lane >= DIFF_HD, q, jnp.zeros_like(q))

        def scores(k):
            return (lax.dot_general(k, q1, _NT, preferred_element_type=F32),
                    lax.dot_general(k, q2, _NT, preferred_element_type=F32))

        (l1, a1), (l2, a2) = _online_softmax(_key_value_chunks(refs[:n_kv], h, HEAD_W), scores, 2, q.shape[0])
        o = (a1 / l1 - lam_ref[0] * (a2 / l2)).T
        o_ref[:, cols] = (_rms(o, HEAD_W) * g_ref[...]).astype(o_ref.dtype)


def _mla_attn_kernel(q_ref, *refs, n_kv, heads):
    o_ref = refs[-1]
    for h in range(heads):
        q = q_ref[:, h * MLA_W:(h + 1) * MLA_W]
        ((l, acc),) = _online_softmax(
            _key_value_chunks(refs[:n_kv], h, MLA_W),
            lambda k: (lax.dot_general(k, q, _NT, preferred_element_type=F32),), 1, q.shape[0])
        o_ref[:, h * HEAD_W:(h + 1) * HEAD_W] = (acc / l).T.astype(o_ref.dtype)


def _attention(kernel, q, k, v, cache_k, cache_v, lead, lead_specs, tail, tail_specs, qk_width, geom, name):
    n_ctx, seq, dec_seq = geom
    t = q.shape[0]
    n_heads = v.shape[1] // HEAD_W
    dec_batch, past, _ = cache_k.shape
    tq = ATTN_Q_BLOCK
    nq, q0, k0 = dec_seq // tq, n_ctx // tq, n_ctx // dec_seq
    assert n_ctx % tq == 0 and n_ctx % dec_seq == 0 and dec_seq % tq == 0
    params = _cparams("arbitrary", "arbitrary", "arbitrary")
    seq_block = lambda w: pl.BlockSpec((seq, n_heads * w), lambda b, h, qi: (b, 0))
    ctx = pl.pallas_call(
        functools.partial(kernel, n_kv=2, heads=n_heads),
        grid=(n_ctx // seq, 1, 1),
        in_specs=lead_specs + [seq_block(qk_width), seq_block(qk_width), seq_block(HEAD_W)] + tail_specs,
        out_specs=seq_block(HEAD_W),
        out_shape=jax.ShapeDtypeStruct((n_ctx, n_heads * HEAD_W), BF16),
        compiler_params=params,
        name=name + "_ctx",
    )(*lead, q, k, v, *tail)
    lat_rows = lambda w, first: pl.BlockSpec((tq, w), lambda b, h, qi: (first + b * nq + qi, h))
    cached = lambda w: pl.BlockSpec((None, past, w), lambda b, h, qi: (b, 0, h))
    lat_keys = lambda w: pl.BlockSpec((dec_seq, w), lambda b, h, qi: (k0 + b, h))
    lat = pl.pallas_call(
        functools.partial(kernel, n_kv=4, heads=1),
        grid=(dec_batch, n_heads, nq),
        in_specs=lead_specs + [lat_rows(qk_width, q0), cached(qk_width), cached(HEAD_W), lat_keys(qk_width),
                               lat_keys(HEAD_W)] + tail_specs,
        out_specs=lat_rows(HEAD_W, 0),
        out_shape=jax.ShapeDtypeStruct((t - n_ctx, n_heads * HEAD_W), BF16),
        compiler_params=params,
        name=name + "_lat",
    )(*lead, q, cache_k, cache_v, k, v, *tail)
    return ctx, lat


def _diff_attention(q, k, v, cache_k, cache_v, lam, out_gain, geom):
    lead_specs = [pl.BlockSpec(memory_space=pltpu.SMEM)]
    tail_specs = [pl.BlockSpec((1, HEAD_W), lambda b, h, qi: (0, 0))]
    return _attention(_diff_attn_kernel, q, k, v, cache_k, cache_v, (lam,), lead_specs, (out_gain,), tail_specs,
                      HEAD_W, geom, "diff")


def _mla_attention(q, k, v, cache_k, cache_v, geom):
    return _attention(_mla_attn_kernel, q, k, v, cache_k, cache_v, (), [], (), [], MLA_W, geom, "mla")


def _merge_kernel(x_ref, mod_ref, yp_ref, ydc_ref, ydl_ref, ymc_ref, yml_ref, gate_ref, wb_ref, wo_ref, rw_ref,
                  rb_ref, x1_ref, h2_ref, topw_ref, topi_ref, rank_ref, count_ref, carry_ref, logits_ref, *,
                  n_ctx_blocks):
    step = pl.program_id(0)

    @pl.when(step == 0)
    def _():
        carry_ref[...] = jnp.zeros_like(carry_ref)
        logits_ref[...] = jnp.zeros_like(logits_ref)

    is_ctx = step < n_ctx_blocks
    mod = mod_ref[0]
    merged = jnp.zeros(x_ref.shape, F32)
    branches = (yp_ref[...], jnp.where(is_ctx, ydc_ref[...], ydl_ref[...]),
                jnp.where(is_ctx, ymc_ref[...], yml_ref[...]))
    for r, y in enumerate(branches):
        z = jnp.dot(y, wb_ref[r], preferred_element_type=F32)
        merged = merged + gate_ref[:, r * D_MODEL:(r + 1) * D_MODEL].astype(F32) * z
    y = jnp.dot(merged.astype(BF16), wo_ref[...], preferred_element_type=F32)
    x1 = x_ref[...] + mod[2:3] * y
    x1_ref[...] = x1
    h2 = _modulated_norm(x1, mod[3:4], mod[4:5])
    h2_ref[...] = h2
    new_logits = jnp.dot(h2, rw_ref[...], preferred_element_type=F32) + rb_ref[...]
    logits = logits_ref[...]
    logits_ref[...] = new_logits
    lane = _lane_iota(logits.shape)
    picks = []
    for _ in range(TOP_K):
        mx = jnp.max(logits, axis=-1, keepdims=True)
        first = jnp.min(jnp.where(logits == mx, lane, EXPERT_LANES), axis=-1, keepdims=True)
        hit = lane == first
        picks.append((mx, first, hit))
        logits = jnp.where(hit, -jnp.inf, logits)
    exps = [jnp.exp(v - picks[0][0]) for v, _, _ in picks]
    denom = exps[0] + exps[1] + exps[2] + exps[3]
    rows = logits.shape[0]
    chosen = jnp.zeros(logits.shape, F32)
    for _, _, hit in picks:
        chosen = chosen + jnp.where(hit, 1.0, 0.0)
    earlier = jnp.where(lax.broadcasted_iota(jnp.int32, (rows, rows), 1)
                        < lax.broadcasted_iota(jnp.int32, (rows, rows), 0), 1.0, 0.0).astype(BF16)
    before = jnp.dot(earlier, chosen.astype(BF16), preferred_element_type=F32) + carry_ref[0:1, :]
    topw = jnp.zeros(logits.shape, F32)
    topi = jnp.zeros(logits.shape, jnp.int32)
    rank = jnp.zeros(logits.shape, jnp.int32)
    for k, (e, (_, first, hit)) in enumerate(zip(exps, picks)):
        slot_rank = jnp.sum(jnp.where(hit, before, 0.0), axis=-1, keepdims=True)
        topw = jnp.where(lane == k, e / denom, topw)
        topi = jnp.where(lane == k, first, topi)
        rank = jnp.where(lane == k, slot_rank.astype(jnp.int32), rank)
    topw_ref[...] = topw
    topi_ref[...] = topi
    rank_ref[...] = rank
    routed = jnp.where(step > 0, jnp.sum(chosen, axis=0, keepdims=True), 0.0)
    carry_ref[...] = carry_ref[...] + routed
    count_ref[...] = carry_ref[...]


def _merge(x, mod, y_pool, y_diff, y_mla, gates, wb, wo, rw, rb, mod_row, n_ctx_blocks):
    t = x.shape[0]
    nb = t // ROW_BLOCK
    consts = (wb, wo, rw, rb)
    lanes = EXPERT_LANES
    cur = lambda i: jnp.minimum(i, nb - 1)
    rows = lambda w: pl.BlockSpec((ROW_BLOCK, w), lambda i: (cur(i), 0))
    routed = pl.BlockSpec((ROW_BLOCK, lanes), lambda i: (jnp.maximum(i - 1, 0), 0))
    ctx_rows = _ctx_rows(D_MODEL, n_ctx_blocks)
    lat_rows = pl.BlockSpec((ROW_BLOCK, D_MODEL), lambda i: (jnp.maximum(cur(i) - n_ctx_blocks, 0), 0))
    return pl.pallas_call(
        functools.partial(_merge_kernel, n_ctx_blocks=n_ctx_blocks),
        grid=(nb + 1,),
        in_specs=[rows(D_MODEL), _mod_spec(lambda i: mod_row(cur(i))), rows(D_MODEL), ctx_rows, lat_rows,
                  ctx_rows, lat_rows, rows(3 * D_MODEL)] + [_resident(a.shape) for a in consts],
        out_specs=[rows(D_MODEL), rows(D_MODEL), routed, routed, routed,
                   pl.BlockSpec((8, lanes), lambda i: (0, 0))],
        out_shape=[jax.ShapeDtypeStruct((t, D_MODEL), F32), jax.ShapeDtypeStruct((t, D_MODEL), F32),
                   jax.ShapeDtypeStruct((t, lanes), F32), jax.ShapeDtypeStruct((t, lanes), jnp.int32),
                   jax.ShapeDtypeStruct((t, lanes), jnp.int32), jax.ShapeDtypeStruct((8, lanes), F32)],
        scratch_shapes=[pltpu.VMEM((8, lanes), F32), pltpu.VMEM((ROW_BLOCK, lanes), F32)],
        compiler_params=_cparams("arbitrary"),
        name="merge",
    )(x, mod, y_pool, *y_diff, *y_mla, gates, *consts)


def _slot_layout(counts, topi, rank):
    tile = MOE_TILE
    cnt = counts[0, :N_EXPERTS].astype(jnp.int32)
    padded = (cnt + tile - 1) // tile * tile
    end = jnp.cumsum(padded)
    start = end - padded
    experts = jnp.arange(N_EXPERTS, dtype=jnp.int32)
    idx4 = topi[:, :TOP_K]
    slot_start = jnp.sum(jnp.where(idx4[:, :, None] == experts, start, 0), axis=-1)
    pos = (slot_start + rank[:, :TOP_K]).reshape(-1)
    n_tiles = (topi.shape[0] * TOP_K + N_EXPERTS * tile) // tile
    tile_row = jnp.arange(n_tiles, dtype=jnp.int32) * tile
    tile_expert = jnp.minimum(jnp.sum(tile_row[:, None] >= end, axis=-1), N_EXPERTS - 1).astype(jnp.int32)
    n_used = (end[-1] // tile).astype(jnp.int32)
    tile_src = jnp.minimum(jnp.arange(n_tiles, dtype=jnp.int32), n_used - 1)
    ragged_last = jnp.any((cnt % tile != 0) & (tile_row[:, None] == end - tile), axis=-1)
    fill_tile = (ragged_last | (tile_row >= end[-1])).astype(jnp.int32)
    return pos, tile_expert, tile_src, n_used.reshape(1), fill_tile


def _row_copy(src, src_row, dst, dst_row, sem):
    return pltpu.make_async_copy(src.at[pl.ds(src_row, 1)], dst.at[pl.ds(dst_row, 1)], sem)


def _dispatch_kernel(fill_tile_ref, pos_ref, h_ref, hs_ref, zero_ref, fill_sem, row_sem):
    i = pl.program_id(0)
    slots = DISPATCH_ROWS * TOP_K

    @pl.when(i == 0)
    def _():
        zero_ref[...] = jnp.zeros_like(zero_ref)

        def fill_copy(tile):
            row = pl.multiple_of(tile * MOE_TILE, MOE_TILE)
            return pltpu.make_async_copy(zero_ref, hs_ref.at[pl.ds(row, MOE_TILE)], fill_sem)

        def start_fill(tile, carry):
            @pl.when(fill_tile_ref[tile] != 0)
            def _():
                fill_copy(tile).start()
            return carry

        def wait_fill(tile, carry):
            @pl.when(fill_tile_ref[tile] != 0)
            def _():
                fill_copy(tile).wait()
            return carry

        lax.fori_loop(0, hs_ref.shape[0] // MOE_TILE, start_fill, 0)
        lax.fori_loop(0, hs_ref.shape[0] // MOE_TILE, wait_fill, 0)

    def issue(r, carry):
        slot = (i * DISPATCH_ROWS + r) * TOP_K
        for k in range(TOP_K):
            _row_copy(h_ref, r, hs_ref, pos_ref[slot + k], row_sem).start()
        return carry

    def drain(j, carry):
        for _ in range(WAIT_UNROLL):
            _row_copy(h_ref, 0, hs_ref, 0, row_sem).wait()
        return carry

    lax.fori_loop(0, DISPATCH_ROWS, issue, 0, unroll=ISSUE_UNROLL)
    lax.fori_loop(0, slots // WAIT_UNROLL, drain, 0)


def _dispatch(h2, pos, fill_tile):
    t = h2.shape[0]
    n_rows = fill_tile.shape[0] * MOE_TILE
    return pl.pallas_call(
        _dispatch_kernel,
        grid_spec=pltpu.PrefetchScalarGridSpec(
            num_scalar_prefetch=2,
            grid=(t // DISPATCH_ROWS,),
            in_specs=[pl.BlockSpec((DISPATCH_ROWS, D_MODEL), lambda i, fill_ref, pos_ref: (i, 0))],
            out_specs=pl.BlockSpec(memory_space=pl.ANY),
            scratch_shapes=[pltpu.VMEM((MOE_TILE, D_MODEL), F32), pltpu.SemaphoreType.DMA,
                            pltpu.SemaphoreType.DMA],
        ),
        out_shape=jax.ShapeDtypeStruct((n_rows, D_MODEL), F32),
        compiler_params=_cparams("arbitrary"),
        name="dispatch",
    )(fill_tile, pos, h2)


def _expert_kernel(tile_expert_ref, tile_src_ref, n_used_ref, hs_ref, wgu_ref, bgu_ref, wd_ref, bd_ref, ys_ref,
                   wgu_bf_ref, wd_bf_ref):
    i = pl.program_id(0)
    occupied = i < n_used_ref[0]
    new_expert = (i == 0) | (tile_expert_ref[i] != tile_expert_ref[jnp.maximum(i - 1, 0)])

    @pl.when(occupied & new_expert)
    def _():
        wgu_bf_ref[...] = wgu_ref[...].astype(BF16)
        wd_bf_ref[...] = wd_ref[...].astype(BF16)

    @pl.when(occupied)
    def _():
        gu = jnp.dot(hs_ref[...].astype(BF16), wgu_bf_ref[...], preferred_element_type=F32) + bgu_ref[0]
        gate = jnp.minimum(gu[:, :D_FF], SWIGLU_LIMIT)
        up = jnp.clip(gu[:, D_FF:], -SWIGLU_LIMIT, SWIGLU_LIMIT)
        act = (up + 1.0) * gate * _sigmoid(SWIGLU_ALPHA * gate)
        ys_ref[...] = jnp.dot(act.astype(BF16), wd_bf_ref[...], preferred_element_type=F32) + bd_ref[0]

    @pl.when(jnp.logical_not(occupied))
    def _():
        ys_ref[...] = jnp.zeros_like(ys_ref)


def _experts(hs, tile_expert, tile_src, n_used, layer, wgu, bgu, wd, bd):
    n_rows = hs.shape[0]
    rows = lambda i, te, ts, nu: (ts[i], 0)
    bias = lambda i, te, ts, nu: (te[i], 0, 0)
    weight = lambda i, te, ts, nu: (layer, te[i], 0, 0)
    return pl.pallas_call(
        _expert_kernel,
        grid_spec=pltpu.PrefetchScalarGridSpec(
            num_scalar_prefetch=3,
            grid=(n_rows // MOE_TILE,),
            in_specs=[
                pl.BlockSpec((MOE_TILE, D_MODEL), rows),
                pl.BlockSpec((None, None, D_MODEL, 2 * D_FF), weight),
                pl.BlockSpec((1, 1, 2 * D_FF), bias),
                pl.BlockSpec((None, None, D_FF, D_MODEL), weight),
                pl.BlockSpec((1, 1, D_MODEL), bias),
            ],
            out_specs=pl.BlockSpec((MOE_TILE, D_MODEL), lambda i, te, ts, nu: (i, 0)),
            scratch_shapes=[pltpu.VMEM((D_MODEL, 2 * D_FF), BF16), pltpu.VMEM((D_FF, D_MODEL), BF16)],
        ),
        out_shape=jax.ShapeDtypeStruct((n_rows, D_MODEL), F32),
        compiler_params=_cparams("arbitrary"),
        name="experts",
    )(tile_expert, tile_src, n_used, hs, wgu, bgu, wd, bd)


def _combine_kernel(pos_ref, ys_ref, w_ref, x1_ref, mod_ref, *refs, n_ctx_blocks):
    out_refs, (buf_ref, sems) = refs[:-2], refs[-2:]
    i = pl.program_id(0)
    slots = COMBINE_ROWS * TOP_K

    def gather(step, buf):
        def issue(r, carry):
            token = step * COMBINE_ROWS + r
            for k in range(TOP_K):
                _row_copy(ys_ref, pos_ref[token * TOP_K + k], buf_ref.at[buf, k], r, sems.at[buf]).start()
            return carry

        lax.fori_loop(0, COMBINE_ROWS, issue, 0, unroll=ISSUE_UNROLL)

    @pl.when(i == 0)
    def _():
        gather(0, 0)

    @pl.when(i + 1 < pl.num_programs(0))
    def _():
        gather(i + 1, (i + 1) % 2)

    cur = i % 2

    def drain(j, carry):
        for _ in range(WAIT_UNROLL):
            _row_copy(ys_ref, 0, buf_ref.at[cur, 0], 0, sems.at[cur]).wait()
        return carry

    lax.fori_loop(0, slots // WAIT_UNROLL, drain, 0)
    w = w_ref[...]
    acc = w[:, 0:1] * buf_ref[cur, 0]
    for k in range(1, TOP_K):
        acc = acc + w[:, k:k + 1] * buf_ref[cur, k]
    result = x1_ref[...] + mod_ref[0][5:6] * acc
    if len(out_refs) == 1:
        out_refs[0][...] = result
    else:
        @pl.when(i < n_ctx_blocks)
        def _():
            out_refs[0][...] = result

        @pl.when(i >= n_ctx_blocks)
        def _():
            out_refs[1][...] = result


def _combine(ys, pos, topw, x1, mod, n_ctx_rows, lat_len, split):
    t = x1.shape[0]
    rb = COMBINE_ROWS
    n_ctx_blocks = n_ctx_rows // rb

    def mod_row(i, pos_ref):
        return (jnp.where(i < n_ctx_blocks, 0, 1 + (i - n_ctx_blocks) // (lat_len // rb)), 0, 0)

    if split:
        out_specs = [pl.BlockSpec((rb, D_MODEL), lambda i, pos_ref: (jnp.minimum(i, n_ctx_blocks - 1), 0)),
                     pl.BlockSpec((rb, D_MODEL), lambda i, pos_ref: (jnp.maximum(i - n_ctx_blocks, 0), 0))]
        out_shape = [jax.ShapeDtypeStruct((n_ctx_rows, D_MODEL), F32),
                     jax.ShapeDtypeStruct((t - n_ctx_rows, D_MODEL), F32)]
    else:
        out_specs = pl.BlockSpec((rb, D_MODEL), lambda i, pos_ref: (i, 0))
        out_shape = jax.ShapeDtypeStruct((t, D_MODEL), F32)
    return pl.pallas_call(
        functools.partial(_combine_kernel, n_ctx_blocks=n_ctx_blocks),
        grid_spec=pltpu.PrefetchScalarGridSpec(
            num_scalar_prefetch=1,
            grid=(t // rb,),
            in_specs=[
                pl.BlockSpec(memory_space=pl.ANY),
                pl.BlockSpec((rb, EXPERT_LANES), lambda i, pos_ref: (i, 0)),
                pl.BlockSpec((rb, D_MODEL), lambda i, pos_ref: (i, 0)),
                pl.BlockSpec((1, 8, D_MODEL), mod_row),
            ],
            out_specs=out_specs,
            scratch_shapes=[pltpu.VMEM((2, TOP_K, rb, D_MODEL), F32), pltpu.SemaphoreType.DMA((2,))],
        ),
        out_shape=out_shape,
        compiler_params=_cparams("arbitrary"),
        name="combine",
    )(pos, ys, topw, x1, mod)


def _rope_tables(n_lat_tokens, rot_dim):
    n_rows = n_lat_tokens // GRID_W
    row = jnp.repeat(jnp.arange(n_rows, dtype=F32), GRID_W)
    col = jnp.tile(jnp.arange(GRID_W, dtype=F32), n_rows)
    n_freq = rot_dim // 4
    inv_freq = ROPE_THETA ** (-jnp.arange(n_freq, dtype=F32) / n_freq)
    ang = jnp.concatenate([row[:, None] * inv_freq, col[:, None] * inv_freq], axis=-1)
    return jnp.cos(ang), jnp.sin(ang)


def _token_tables(cos_lane, sin_lane, n_ctx_rows, n_lat_seqs):
    width = cos_lane.shape[1]
    cos = jnp.concatenate([jnp.ones((n_ctx_rows, width), F32)] + [cos_lane] * n_lat_seqs, axis=0)
    sin = jnp.concatenate([jnp.zeros((n_ctx_rows, width), F32)] + [sin_lane] * n_lat_seqs, axis=0)
    return cos, sin


def _pack_heads(w, n_heads, lo, hi, width):
    k = w.shape[0]
    per_head = w.shape[1] // n_heads
    part = w.reshape(k, n_heads, per_head)[:, :, lo:hi]
    part = jnp.pad(part, ((0, 0), (0, 0), (0, width - (hi - lo))))
    return part.reshape(k, n_heads * width)


def kernel(x_prompt, x_sample, c, cache_diff_k, cache_diff_v, cache_mla_ckv, cache_mla_krope, c_ctx, w_ada, b_ada,
           w_in, pool_w, pool_scale, diff_q_norm, diff_k_norm, diff_lambda, diff_out_norm, mla_q_a_norm, w_uq,
           mla_kv_a_norm, w_ukv, mla_q_norm, mla_k_norm, w_branch, w_out, router_w, router_b, moe_w_gu, moe_b_gu,
           moe_w_down, moe_b_down):
    batch, seq, d = x_prompt.shape
    dec_batch, dec_seq, _ = x_sample.shape
    depth = w_ada.shape[0]
    past = cache_diff_k.shape[2]
    n_ctx = batch * seq
    n_lat = dec_batch * dec_seq
    n_ctx_blocks = n_ctx // ROW_BLOCK
    mod_row = _mod_row(n_ctx_blocks, dec_seq // ROW_BLOCK)
    mod_spec = _mod_spec(mod_row)

    x = jnp.concatenate([x_prompt.reshape(n_ctx, d), x_sample.reshape(n_lat, d)], axis=0)
    cond8 = jnp.concatenate([c_ctx[None], c, jnp.zeros((8 - 1 - dec_batch, d), F32)], axis=0)
    mod_all = _ada(cond8, w_ada, b_ada).reshape(depth, 8, 6, d)[:, :1 + dec_batch]
    mod_all = jnp.pad(mod_all, ((0, 0), (0, 0), (0, 2), (0, 0)))

    cos_d, sin_d = _rope_tables(dec_seq, DIFF_HD)
    cos_m, sin_m = _rope_tables(dec_seq, MLA_ROPE)
    cos_a, sin_a = _token_tables(jnp.concatenate([cos_d] * 4, axis=1),
                                 jnp.concatenate([-sin_d, sin_d, -sin_d, sin_d], axis=1), n_ctx, dec_batch)
    pad = MLA_W - MLA_QK
    cos_b, sin_b = _token_tables(jnp.concatenate([cos_m, cos_m, jnp.ones((dec_seq, pad), F32)], axis=1),
                                 jnp.concatenate([-sin_m, sin_m, jnp.zeros((dec_seq, pad), F32)], axis=1),
                                 n_ctx, dec_batch)

    offs = np.cumsum((0, 1024, 1024, 1024, 1024, MLA_Q_RANK, MLA_KV_RANK, MLA_ROPE, 3 * 1024))
    new_dk, new_dv, new_ckv, new_kr = [], [], [], []
    for l in range(depth):
        lam_init = 0.8 - 0.6 * math.exp(-0.3 * l)
        wl = w_in[l]
        seg = [wl[:, offs[k]:offs[k + 1]] for k in range(8)]
        w_a = jnp.concatenate(seg[0:4] + [seg[7]], axis=1).astype(BF16)
        kr_wide = jnp.pad(seg[6], ((0, 0), (MLA_NOPE, MLA_W - MLA_NOPE - MLA_ROPE)))
        w_b = jnp.concatenate([seg[4], seg[5], kr_wide], axis=1).astype(BF16)
        wuq = _pack_heads(w_uq[l], MLA_HEADS, 0, MLA_QK, MLA_W).astype(BF16)
        wkn = _pack_heads(w_ukv[l], MLA_HEADS, 0, MLA_NOPE, MLA_W).astype(BF16)
        wv = _pack_heads(w_ukv[l], MLA_HEADS, MLA_NOPE, MLA_NOPE + HEAD_W, HEAD_W).astype(BF16)
        gq_d = jnp.tile(diff_q_norm[l], 2 * DIFF_HEADS)[None]
        gk_d = jnp.tile(diff_k_norm[l], 2 * DIFF_HEADS)[None]
        gq_m = jnp.tile(jnp.pad(mla_q_norm[l], (0, MLA_W - MLA_QK)), MLA_HEADS)[None]
        gk_m = jnp.tile(jnp.pad(mla_k_norm[l], (0, MLA_W - MLA_QK)), MLA_HEADS)[None]
        lp = diff_lambda[l]
        lam = (jnp.exp(jnp.sum(lp[0] * lp[1])) - jnp.exp(jnp.sum(lp[2] * lp[3])) + lam_init).reshape(1)
        out_gain = (diff_out_norm[l] * (1.0 - lam_init))[None]
        mod = mod_all[l]

        u_pool, dq, dk_f, dk_b, dv_f, dv_b, gates = _proj_a(x, mod, w_a, gq_d, gk_d, cos_a, sin_a, mod_spec,
                                                            n_ctx_blocks)
        mq, ckv_n, kr, mk, mv = _proj_b(x, mod, w_b, mla_q_a_norm[l][None], mla_kv_a_norm[l][None], wuq, wkn, wv,
                                        gq_m, gk_m, cos_b, sin_b, mod_spec, n_ctx_blocks)
        new_dk.append(dk_f[:n_ctx].reshape(batch, seq, DIFF_HEADS, HEAD_W))
        new_dv.append(dv_f[:n_ctx].reshape(batch, seq, DIFF_HEADS, HEAD_W))
        new_ckv.append(ckv_n[:n_ctx].reshape(batch, seq, MLA_KV_RANK))
        new_kr.append(kr[:n_ctx].reshape(batch, seq, MLA_ROPE))

        y_pool = _pool(u_pool, pool_w[l].astype(BF16), pool_scale[l][None], n_ctx_blocks, seq, dec_seq)

        c_krw = jnp.pad(cache_mla_krope[:, l].reshape(dec_batch * past, MLA_ROPE),
                        ((0, 0), (MLA_NOPE, MLA_W - MLA_NOPE - MLA_ROPE)))
        c_mk, c_mv = _cache_kv(cache_mla_ckv[:, l].reshape(dec_batch * past, MLA_KV_RANK), c_krw, wkn, wv, gk_m)

        geom = (n_ctx, seq, dec_seq)
        y_diff = _diff_attention(dq, dk_b, dv_b, cache_diff_k[:, l].reshape(dec_batch, past, d),
                                 cache_diff_v[:, l].reshape(dec_batch, past, d), lam, out_gain, geom)
        y_mla = _mla_attention(mq, mk, mv, c_mk.reshape(dec_batch, past, -1), c_mv.reshape(dec_batch, past, -1),
                               geom)

        rw = jnp.pad(router_w[l], ((0, 0), (0, EXPERT_LANES - N_EXPERTS)))
        rb = jnp.pad(router_b[l], (0, EXPERT_LANES - N_EXPERTS), constant_values=NEG_BIG)[None]
        x1, h2, topw, topi, rank, counts = _merge(x, mod, y_pool, y_diff, y_mla, gates, w_branch[l].astype(BF16),
                                                  w_out[l].astype(BF16), rw, rb, mod_row, n_ctx_blocks)
        pos, tile_expert, tile_src, n_used, fill_tile = _slot_layout(counts, topi, rank)
        hs = _dispatch(h2, pos, fill_tile)
        ys = _experts(hs, tile_expert, tile_src, n_used, l, moe_w_gu, moe_b_gu[l][:, None, :],
                      moe_w_down, moe_b_down[l][:, None, :])
        x = _combine(ys, pos, topw, x1, mod, n_ctx, dec_seq, split=l == depth - 1)

    y_ctx, y_lat = x
    return (y_ctx.reshape(batch, seq, d), y_lat.reshape(dec_batch, dec_seq, d),
            jnp.stack(new_dk, axis=1), jnp.stack(new_dv, axis=1), jnp.stack(new_ckv, axis=1),
            jnp.stack(new_kr, axis=1))
```

```python
import functools
import math

import jax
import jax.numpy as jnp
import numpy as np
from jax import lax
from jax.experimental import pallas as pl
from jax.experimental.pallas import tpu as pltpu

F32 = jnp.float32
BF16 = jnp.bfloat16

D_MODEL = 1024
N_POOL_GROUPS = 4
POOL_GROUP = D_MODEL // N_POOL_GROUPS
DIFF_HD = 64
DIFF_HEADS = 8
HEAD_W = 128
MLA_HEADS = 8
MLA_NOPE = 128
MLA_ROPE = 64
MLA_QK = MLA_NOPE + MLA_ROPE
MLA_W = 256
MLA_Q_RANK = 768
MLA_KV_RANK = 512
N_EXPERTS = 32
TOP_K = 4
D_FF = 1024
EXPERT_LANES = 128
SWIGLU_ALPHA = 1.702
SWIGLU_LIMIT = 7.0
ROPE_THETA = 10000.0
NORM_EPS = 1e-6
GRID_W = 64
LOG2E = math.log2(math.e)
VMEM_LIMIT = 56 * 1024 * 1024

ROW_BLOCK = 256
POOL_HALO = 16
MOE_TILE = 512
DISPATCH_ROWS = 256
COMBINE_ROWS = 128
WAIT_UNROLL = 16
ISSUE_UNROLL = 4
DIFF_Q_BLOCK = 1024
MLA_Q_BLOCK = 2048
ATTN_K_CHUNK = 512

_NT = (((1,), (1,)), ((), ()))
_TN = (((0,), (0,)), ((), ()))


def _cparams(*sem):
    return pltpu.CompilerParams(dimension_semantics=sem, vmem_limit_bytes=VMEM_LIMIT)


def _resident(shape):
    nd = len(shape)
    return pl.BlockSpec(shape, lambda *_: (0,) * nd, pipeline_mode=pl.Buffered(1))


def _sigmoid(x):
    return 1.0 / (1.0 + jnp.exp(-x))


def _lane_iota(shape):
    return lax.broadcasted_iota(jnp.int32, shape, len(shape) - 1)


def _rms(x, width):
    ss = jnp.sum(x * x, axis=-1, keepdims=True)
    return x * lax.rsqrt(ss * (1.0 / width) + NORM_EPS)


def _rope(x, cos, sin):
    n = x.shape[-1]
    lane = _lane_iota(x.shape)
    partner = jnp.where((lane & 32) != 0, pltpu.roll(x, 32, 1), pltpu.roll(x, n - 32, 1))
    return x * cos + partner * sin


def _modulated_norm(x, shift, scale):
    return _rms(x, D_MODEL) * (1.0 + scale) + shift


def _ada_kernel(cond_ref, w_ref, b_ref, o_ref):
    cnd = cond_ref[...]
    act = cnd * _sigmoid(cnd)
    o_ref[0] = jnp.dot(act, w_ref[0], preferred_element_type=F32) + b_ref[0]


def _ada(cond8, w_ada, b_ada):
    depth = w_ada.shape[0]
    n_chunk = w_ada.shape[2] // D_MODEL
    return pl.pallas_call(
        _ada_kernel,
        grid=(depth, n_chunk),
        in_specs=[
            pl.BlockSpec((8, D_MODEL), lambda l, j: (0, 0)),
            pl.BlockSpec((1, D_MODEL, D_MODEL), lambda l, j: (l, 0, j)),
            pl.BlockSpec((1, 1, D_MODEL), lambda l, j: (l, 0, j)),
        ],
        out_specs=pl.BlockSpec((1, 8, D_MODEL), lambda l, j: (l, 0, j)),
        out_shape=jax.ShapeDtypeStruct((depth, 8, n_chunk * D_MODEL), F32),
        compiler_params=_cparams("arbitrary", "arbitrary"),
        name="ada",
    )(cond8, w_ada, b_ada.reshape(depth, 1, -1))


def _diff_qk_norm(u, gain, cos, sin):
    outs = []
    for h in range(DIFF_HEADS):
        c = u[:, h * HEAD_W:(h + 1) * HEAD_W]
        lane = _lane_iota(c.shape)
        sq = c * c
        lo = jnp.sum(jnp.where(lane < DIFF_HD, sq, 0.0), axis=-1, keepdims=True)
        hi = jnp.sum(jnp.where(lane >= DIFF_HD, sq, 0.0), axis=-1, keepdims=True)
        ms = jnp.where(lane < DIFF_HD, lo, hi) * (1.0 / DIFF_HD)
        y = c * lax.rsqrt(ms + NORM_EPS) * gain[:, h * HEAD_W:(h + 1) * HEAD_W]
        outs.append((y, _rope(y, cos, sin)))
    return outs


def _token_block(ctx_ref, lat_ref, n_ctx_blocks):
    return jnp.where(pl.program_id(0) < n_ctx_blocks, ctx_ref[...], lat_ref[...])


def _proj_a_kernel(xc_ref, xl_ref, mod_ref, w_ref, gq_ref, gk_ref, cos_ref, sin_ref,
                   pool_ref, dq_ref, dkf_ref, dkb_ref, dvf_ref, dvb_ref, gate_ref, *, n_ctx_blocks):
    mod = mod_ref[0]
    h = _modulated_norm(_token_block(xc_ref, xl_ref, n_ctx_blocks), mod[0:1], mod[1:2]).astype(BF16)
    cos, sin = cos_ref[...], sin_ref[...]

    def seg(k):
        return jnp.dot(h, w_ref[:, k * D_MODEL:(k + 1) * D_MODEL], preferred_element_type=F32)

    u_pool, u_dq = seg(0), seg(1)
    pool_ref[...] = u_pool.astype(BF16)
    u_dk = seg(2)
    q_scale = DIFF_HD ** -0.5 * LOG2E
    for hd, (_, roped) in enumerate(_diff_qk_norm(u_dq, gq_ref[...], cos, sin)):
        dq_ref[:, hd * HEAD_W:(hd + 1) * HEAD_W] = (roped * q_scale).astype(BF16)
    dv = seg(3)
    for hd, (plain, roped) in enumerate(_diff_qk_norm(u_dk, gk_ref[...], cos, sin)):
        dkf_ref[:, hd * HEAD_W:(hd + 1) * HEAD_W] = plain
        dkb_ref[:, hd * HEAD_W:(hd + 1) * HEAD_W] = roped.astype(BF16)
    u_gate = seg(4)
    dvf_ref[...] = dv
    dvb_ref[...] = dv.astype(BF16)
    for r in range(3):
        u_next = seg(5 + r) if r < 2 else None
        gate_ref[:, r * D_MODEL:(r + 1) * D_MODEL] = _sigmoid(u_gate).astype(BF16)
        u_gate = u_next


def _mla_heads(u, gain, cos, sin, rope):
    outs = []
    for h in range(MLA_HEADS):
        c = u[:, h * MLA_W:(h + 1) * MLA_W]
        y = _rms(c, MLA_QK) * gain[:, h * MLA_W:(h + 1) * MLA_W]
        if rope:
            y = jnp.concatenate([y[:, :MLA_NOPE], _rope(y[:, MLA_NOPE:], cos, sin)], axis=-1)
        outs.append(y)
    return outs


def _mla_kv(ckv_n, krw, wkn_ref, wv_ref, gk, cos, sin, rope, k_ref, v_ref):
    kn = jnp.dot(ckv_n, wkn_ref[...], preferred_element_type=F32)
    v_ref[...] = jnp.dot(ckv_n, wv_ref[...], preferred_element_type=F32).astype(BF16)
    kfull = jnp.concatenate([kn[:, h * MLA_W:(h + 1) * MLA_W] + krw for h in range(MLA_HEADS)], axis=-1)
    for hd, y in enumerate(_mla_heads(kfull, gk, cos, sin, rope)):
        k_ref[:, hd * MLA_W:(hd + 1) * MLA_W] = y.astype(BF16)


def _proj_b_kernel(xc_ref, xl_ref, mod_ref, w_ref, gqa_ref, gkva_ref, wuq_ref, wkn_ref, wv_ref, gq_ref, gk_ref,
                   cos_ref, sin_ref, mq_ref, ckv_ref, kr_ref, mk_ref, mv_ref, *, n_ctx_blocks):
    mod = mod_ref[0]
    h = _modulated_norm(_token_block(xc_ref, xl_ref, n_ctx_blocks), mod[0:1], mod[1:2]).astype(BF16)
    cos, sin = cos_ref[...], sin_ref[...]
    cq = jnp.dot(h, w_ref[:, :MLA_Q_RANK], preferred_element_type=F32)
    ckv = jnp.dot(h, w_ref[:, MLA_Q_RANK:MLA_Q_RANK + MLA_KV_RANK], preferred_element_type=F32)
    krw = jnp.dot(h, w_ref[:, MLA_Q_RANK + MLA_KV_RANK:], preferred_element_type=F32)
    cq_n = (_rms(cq, MLA_Q_RANK) * gqa_ref[...]).astype(BF16)
    mq = jnp.dot(cq_n, wuq_ref[...], preferred_element_type=F32)
    ckv_n = _rms(ckv, MLA_KV_RANK) * gkva_ref[...]
    ckv_ref[...] = ckv_n
    kr_ref[...] = krw[:, MLA_NOPE:MLA_NOPE + MLA_ROPE]
    _mla_kv(ckv_n.astype(BF16), krw, wkn_ref, wv_ref, gk_ref[...], cos, sin, True, mk_ref, mv_ref)
    q_scale = MLA_QK ** -0.5 * LOG2E
    for hd, y in enumerate(_mla_heads(mq, gq_ref[...], cos, sin, True)):
        mq_ref[:, hd * MLA_W:(hd + 1) * MLA_W] = (y * q_scale).astype(BF16)


def _cache_kv_kernel(ckv_ref, krw_ref, wkn_ref, wv_ref, gk_ref, mk_ref, mv_ref):
    _mla_kv(ckv_ref[...].astype(BF16), krw_ref[...], wkn_ref, wv_ref, gk_ref[...], None, None, False,
            mk_ref, mv_ref)


def _mod_row(n_ctx_blocks, blocks_per_lat):
    return lambda i: jnp.where(i < n_ctx_blocks, 0, 1 + (i - n_ctx_blocks) // blocks_per_lat)


def _mod_spec(mod_row):
    return pl.BlockSpec((1, 8, D_MODEL), lambda i: (mod_row(i), 0, 0))


def _rows(width):
    return pl.BlockSpec((ROW_BLOCK, width), lambda i: (i, 0))


def _ctx_rows(width, n_ctx_blocks):
    return pl.BlockSpec((ROW_BLOCK, width), lambda i: (jnp.minimum(i, n_ctx_blocks - 1), 0))


def _lat_rows(width, n_ctx_blocks):
    return pl.BlockSpec((ROW_BLOCK, width), lambda i: (jnp.maximum(i - n_ctx_blocks, 0), 0))


def _proj_outputs(t, n_ctx_blocks, widths, dtypes, ctx_only):
    spare = lambda w: pl.BlockSpec((ROW_BLOCK, w), lambda i: (jnp.minimum(i, n_ctx_blocks), 0))
    specs = [spare(w) if c else _rows(w) for w, c in zip(widths, ctx_only)]
    shapes = [jax.ShapeDtypeStruct(((n_ctx_blocks + 1) * ROW_BLOCK if c else t, w), dt)
              for w, dt, c in zip(widths, dtypes, ctx_only)]
    return specs, shapes


def _proj_a(x, mod, w_a, gq, gk, cos, sin, mod_spec, n_ctx_blocks):
    t = x[0].shape[0] + x[1].shape[0]
    widths = (D_MODEL,) * 6 + (3 * D_MODEL,)
    dtypes = (BF16, BF16, F32, BF16, F32, BF16, BF16)
    out_specs, out_shape = _proj_outputs(t, n_ctx_blocks, widths, dtypes, (0, 0, 1, 0, 1, 0, 0))
    return pl.pallas_call(
        functools.partial(_proj_a_kernel, n_ctx_blocks=n_ctx_blocks),
        grid=(t // ROW_BLOCK,),
        in_specs=[_ctx_rows(D_MODEL, n_ctx_blocks), _lat_rows(D_MODEL, n_ctx_blocks), mod_spec,
                  _resident(w_a.shape), _resident(gq.shape), _resident(gk.shape), _rows(HEAD_W), _rows(HEAD_W)],
        out_specs=out_specs,
        out_shape=out_shape,
        compiler_params=_cparams("arbitrary"),
        name="proj_a",
    )(*x, mod, w_a, gq, gk, cos, sin)


def _proj_b(x, mod, w_b, gqa, gkva, wuq, wkn, wv, gq, gk, cos, sin, mod_spec, n_ctx_blocks):
    t = x[0].shape[0] + x[1].shape[0]
    widths = (MLA_HEADS * MLA_W, MLA_KV_RANK, MLA_ROPE, MLA_HEADS * MLA_W, D_MODEL)
    dtypes = (BF16, F32, F32, BF16, BF16)
    out_specs, out_shape = _proj_outputs(t, n_ctx_blocks, widths, dtypes, (0, 1, 1, 0, 0))
    consts = (w_b, gqa, gkva, wuq, wkn, wv, gq, gk)
    return pl.pallas_call(
        functools.partial(_proj_b_kernel, n_ctx_blocks=n_ctx_blocks),
        grid=(t // ROW_BLOCK,),
        in_specs=[_ctx_rows(D_MODEL, n_ctx_blocks), _lat_rows(D_MODEL, n_ctx_blocks), mod_spec]
        + [_resident(a.shape) for a in consts] + [_rows(HEAD_W), _rows(HEAD_W)],
        out_specs=out_specs,
        out_shape=out_shape,
        compiler_params=_cparams("arbitrary"),
        name="proj_b",
    )(*x, mod, *consts, cos, sin)


def _cache_kv(ckv, krw, wkn, wv, gk):
    n = ckv.shape[0]
    widths = (MLA_HEADS * MLA_W, D_MODEL)
    consts = (wkn, wv, gk)
    return pl.pallas_call(
        _cache_kv_kernel,
        grid=(n // ROW_BLOCK,),
        in_specs=[_rows(MLA_KV_RANK), _rows(MLA_W)] + [_resident(a.shape) for a in consts],
        out_specs=[_rows(w) for w in widths],
        out_shape=[jax.ShapeDtypeStruct((n, w), BF16) for w in widths],
        compiler_params=_cparams("arbitrary"),
        name="cache_kv",
    )(ckv, krw, *consts)


def _pool_kernel(prev_ref, cur_ref, next_ref, w_ref, scale_ref, o_ref, *, n_ctx_blocks, ctx_len, lat_len):
    i = pl.program_id(0)
    rb = ROW_BLOCK
    is_ctx = i < n_ctx_blocks
    n_ctx_rows = n_ctx_blocks * rb
    seq_len = jnp.where(is_ctx, ctx_len, lat_len)
    seq_start = jnp.where(is_ctx, (i * rb // ctx_len) * ctx_len,
                          n_ctx_rows + ((i * rb - n_ctx_rows) // lat_len) * lat_len)
    n_cols = rb + 2 * POOL_HALO
    row_pos = i * rb - seq_start + lax.broadcasted_iota(jnp.int32, (rb, n_cols), 0)
    col_pos = i * rb - POOL_HALO - seq_start + lax.broadcasted_iota(jnp.int32, (rb, n_cols), 1)
    cur = cur_ref[...]
    u = jnp.concatenate([prev_ref[...], cur, next_ref[...]], axis=0)
    for g in range(N_POOL_GROUPS):
        cols = slice(g * POOL_GROUP, (g + 1) * POOL_GROUP)
        window = 2 << g
        lo = jnp.clip(row_pos - window // 2, 0, seq_len)
        hi = jnp.clip(row_pos - window // 2 + window, 0, seq_len)
        band = jnp.where((col_pos >= lo) & (col_pos < hi), 1.0, 0.0).astype(BF16)
        total = jnp.dot(band, u[:, cols], preferred_element_type=F32)
        cnt = (hi[:, :1] - lo[:, :1]).astype(F32)
        pooled = total / cnt - cur[:, cols].astype(F32)
        mixed = jnp.dot(pooled.astype(BF16), w_ref[g], preferred_element_type=F32)
        o_ref[:, cols] = (mixed * scale_ref[:, cols]).astype(BF16)


def _pool(u_pool, pool_w, pool_scale, n_ctx_blocks, ctx_len, lat_len):
    t = u_pool.shape[0]
    halo_per_block = ROW_BLOCK // POOL_HALO
    n_halo_blocks = t // POOL_HALO
    assert POOL_HALO >= (2 << (N_POOL_GROUPS - 1)) // 2
    kern = functools.partial(_pool_kernel, n_ctx_blocks=n_ctx_blocks, ctx_len=ctx_len, lat_len=lat_len)
    return pl.pallas_call(
        kern,
        grid=(t // ROW_BLOCK,),
        in_specs=[
            pl.BlockSpec((POOL_HALO, D_MODEL), lambda i: (jnp.maximum(i * halo_per_block - 1, 0), 0)),
            _rows(D_MODEL),
            pl.BlockSpec((POOL_HALO, D_MODEL),
                         lambda i: (jnp.minimum((i + 1) * halo_per_block, n_halo_blocks - 1), 0)),
            _resident(pool_w.shape),
            _resident(pool_scale.shape),
        ],
        out_specs=_rows(D_MODEL),
        out_shape=jax.ShapeDtypeStruct((t, D_MODEL), BF16),
        compiler_params=_cparams("arbitrary"),
        name="pool",
    )(u_pool, u_pool, u_pool, pool_w, pool_scale)


def _softmax_stats(s, m, l):
    m_new = jnp.maximum(m, jnp.max(s, axis=0, keepdims=True))
    p = jnp.exp2(s - m_new)
    alpha = jnp.exp2(m - m_new)
    return m_new, alpha * l + jnp.sum(p, axis=0, keepdims=True), alpha, p.astype(BF16)


def _accumulate(acc, alpha, p, v):
    return alpha * acc + lax.dot_general(v, p, _TN, preferred_element_type=F32)


def _online_softmax(chunks, scores, n_softmax, tq):
    chunks = list(chunks)
    m = [jnp.full((1, tq), -jnp.inf, F32)] * n_softmax
    l = [jnp.zeros((1, tq), F32)] * n_softmax
    acc = [jnp.zeros((HEAD_W, tq), F32)] * n_softmax
    pending = None
    s_next = scores(chunks[0][0])
    for c, (_, v) in enumerate(chunks):
        s = s_next
        if c + 1 < len(chunks):
            s_next = scores(chunks[c + 1][0])
        stats = [_softmax_stats(s[j], m[j], l[j]) for j in range(n_softmax)]
        m = [st[0] for st in stats]
        l = [st[1] for st in stats]
        if pending is not None:
            acc = [_accumulate(acc[j], *pending[j]) for j in range(n_softmax)]
        pending = [(st[2], st[3], v) for st in stats]
    acc = [_accumulate(acc[j], *pending[j]) for j in range(n_softmax)]
    return list(zip(l, acc))


def _key_value_chunks(kv_refs, head, qk_width):
    k_cols = slice(head * qk_width, (head + 1) * qk_width)
    v_cols = slice(head * HEAD_W, (head + 1) * HEAD_W)
    for k_ref, v_ref in zip(kv_refs[0::2], kv_refs[1::2]):
        n = k_ref.shape[0]
        size = min(n, ATTN_K_CHUNK)
        for start in range(0, n, size):
            yield k_ref[start:start + size, k_cols].astype(BF16), v_ref[start:start + size, v_cols].astype(BF16)


def _diff_attn_kernel(lam_ref, q_ref, *refs, n_kv, heads):
    g_ref, o_ref = refs[n_kv], refs[-1]
    for h in range(heads):
        cols = slice(h * HEAD_W, (h + 1) * HEAD_W)
        q = q_ref[:, cols]
        lane = _lane_iota(q.shape)
        q1 = jnp.where(lane < DIFF_HD, q, jnp.zeros_like(q))
        q2 = jnp.where(lane >= DIFF_HD, q, jnp.zeros_like(q))

        def scores(k):
            return (lax.dot_general(k, q1, _NT, preferred_element_type=F32),
                    lax.dot_general(k, q2, _NT, preferred_element_type=F32))

        (l1, a1), (l2, a2) = _online_softmax(_key_value_chunks(refs[:n_kv], h, HEAD_W), scores, 2, q.shape[0])
        o = (a1 / l1 - lam_ref[0] * (a2 / l2)).T
        o_ref[:, cols] = (_rms(o, HEAD_W) * g_ref[...]).astype(o_ref.dtype)


def _mla_attn_kernel(q_ref, *refs, n_kv, heads):
    o_ref = refs[-1]
    for h in range(heads):
        q = q_ref[:, h * MLA_W:(h + 1) * MLA_W]
        ((l, acc),) = _online_softmax(
            _key_value_chunks(refs[:n_kv], h, MLA_W),
            lambda k: (lax.dot_general(k, q, _NT, preferred_element_type=F32),), 1, q.shape[0])
        o_ref[:, h * HEAD_W:(h + 1) * HEAD_W] = (acc / l).T.astype(o_ref.dtype)


def _attention(kernel, q, k, v, cache_k, cache_v, lead, lead_specs, tail, tail_specs, qk_width, tq, geom, name):
    n_ctx, seq, dec_seq = geom
    t = q.shape[0]
    n_heads = v.shape[1] // HEAD_W
    dec_batch, past, _ = cache_k.shape
    nq, q0, k0 =dec_seq // tq, n_ctx // tq, n_ctx // dec_seq
    assert n_ctx % tq == 0 and n_ctx % dec_seq == 0 and dec_seq % tq == 0
    params = _cparams("arbitrary", "arbitrary", "arbitrary")
    seq_block = lambda w: pl.BlockSpec((seq, n_heads * w), lambda b, h, qi: (b, 0))
    ctx = pl.pallas_call(
        functools.partial(kernel, n_kv=2, heads=n_heads),
        grid=(n_ctx // seq, 1, 1),
        in_specs=lead_specs + [seq_block(qk_width), seq_block(qk_width), seq_block(HEAD_W)] + tail_specs,
        out_specs=seq_block(HEAD_W),
        out_shape=jax.ShapeDtypeStruct((n_ctx, n_heads * HEAD_W), BF16),
        compiler_params=params,
        name=name + "_ctx",
    )(*lead, q, k, v, *tail)
    lat_rows = lambda w, first: pl.BlockSpec((tq, w), lambda b, h, qi: (first + b * nq + qi, h))
    cached = lambda w: pl.BlockSpec((None, past, w), lambda b, h, qi: (b, 0, h))
    lat_keys = lambda w: pl.BlockSpec((dec_seq, w), lambda b, h, qi: (k0 + b, h))
    lat = pl.pallas_call(
        functools.partial(kernel, n_kv=4, heads=1),
        grid=(dec_batch, n_heads, nq),
        in_specs=lead_specs + [lat_rows(qk_width, q0), cached(qk_width), cached(HEAD_W), lat_keys(qk_width),
                               lat_keys(HEAD_W)] + tail_specs,
        out_specs=lat_rows(HEAD_W, 0),
        out_shape=jax.ShapeDtypeStruct((t - n_ctx, n_heads * HEAD_W), BF16),
        compiler_params=params,
        name=name + "_lat",
    )(*lead, q, cache_k, cache_v, k, v, *tail)
    return ctx, lat


def _diff_attention(q, k, v, cache_k, cache_v, lam, out_gain, geom):
    lead_specs = [pl.BlockSpec(memory_space=pltpu.SMEM)]
    tail_specs = [pl.BlockSpec((1, HEAD_W), lambda b, h, qi: (0, 0))]
    return _attention(_diff_attn_kernel, q, k, v, cache_k, cache_v, (lam,), lead_specs, (out_gain,), tail_specs,
                      HEAD_W, DIFF_Q_BLOCK, geom, "diff")


def _mla_attention(q, k, v, cache_k, cache_v, geom):
    return _attention(_mla_attn_kernel, q, k, v, cache_k, cache_v, (), [], (), [], MLA_W, MLA_Q_BLOCK, geom, "mla")


def _merge_kernel(xc_ref, xl_ref, mod_ref, yp_ref, ydc_ref, ydl_ref, ymc_ref, yml_ref, gate_ref, wb_ref, wo_ref,
                  rw_ref, rb_ref, x1_ref, h2_ref, topw_ref, topi_ref, rank_ref, count_ref, carry_ref, logits_ref,
                  *, n_ctx_blocks):
    step = pl.program_id(0)

    @pl.when(step == 0)
    def _():
        carry_ref[...] = jnp.zeros_like(carry_ref)
        logits_ref[...] = jnp.zeros_like(logits_ref)

    is_ctx = step < n_ctx_blocks
    mod = mod_ref[0]
    merged = jnp.zeros(xc_ref.shape, F32)
    branches = (yp_ref[...], jnp.where(is_ctx, ydc_ref[...], ydl_ref[...]),
                jnp.where(is_ctx, ymc_ref[...], yml_ref[...]))
    for r, y in enumerate(branches):
        z = jnp.dot(y, wb_ref[r], preferred_element_type=F32)
        merged = merged + gate_ref[:, r * D_MODEL:(r + 1) * D_MODEL].astype(F32) * z
    y = jnp.dot(merged.astype(BF16), wo_ref[...], preferred_element_type=F32)
    x1 = jnp.where(is_ctx, xc_ref[...], xl_ref[...]) + mod[2:3] * y
    x1_ref[...] = x1
    h2 = _modulated_norm(x1, mod[3:4], mod[4:5])
    h2_ref[...] = h2
    new_logits = jnp.dot(h2, rw_ref[...], preferred_element_type=F32) + rb_ref[...]
    logits = logits_ref[...]
    logits_ref[...] = new_logits
    lane = _lane_iota(logits.shape)
    picks = []
    for _ in range(TOP_K):
        mx = jnp.max(logits, axis=-1, keepdims=True)
        first = jnp.min(jnp.where(logits == mx, lane, EXPERT_LANES), axis=-1, keepdims=True)
        hit = lane == first
        picks.append((mx, first, hit))
        logits = jnp.where(hit, -jnp.inf, logits)
    exps = [jnp.exp(v - picks[0][0]) for v, _, _ in picks]
    denom = exps[0] + exps[1] + exps[2] + exps[3]
    rows = logits.shape[0]
    chosen = jnp.zeros(logits.shape, F32)
    for _, _, hit in picks:
        chosen = chosen + jnp.where(hit, 1.0, 0.0)
    earlier = jnp.where(lax.broadcasted_iota(jnp.int32, (rows, rows), 1)
                        < lax.broadcasted_iota(jnp.int32, (rows, rows), 0), 1.0, 0.0).astype(BF16)
    before = jnp.dot(earlier, chosen.astype(BF16), preferred_element_type=F32) + carry_ref[0:1, :]
    topw = jnp.zeros(logits.shape, F32)
    topi = jnp.zeros(logits.shape, jnp.int32)
    rank = jnp.zeros(logits.shape, jnp.int32)
    for k, (e, (_, first, hit)) in enumerate(zip(exps, picks)):
        slot_rank = jnp.sum(jnp.where(hit, before, 0.0), axis=-1, keepdims=True)
        topw = jnp.where(lane == k, e / denom, topw)
        topi = jnp.where(lane == k, first, topi)
        rank = jnp.where(lane == k, slot_rank.astype(jnp.int32), rank)
    topw_ref[...] = topw
    topi_ref[...] = topi
    rank_ref[...] = rank
    routed = jnp.where(step > 0, jnp.sum(chosen, axis=0, keepdims=True), 0.0)
    carry_ref[...] = carry_ref[...] + routed
    count_ref[...] = carry_ref[...]


def _merge(x, mod, y_pool, y_diff, y_mla, gates, wb, wo, rw, rb, mod_row, n_ctx_blocks):
    t = x[0].shape[0] + x[1].shape[0]
    nb = t // ROW_BLOCK
    consts = (wb, wo, rw, rb)
    lanes = EXPERT_LANES
    cur = lambda i: jnp.minimum(i, nb - 1)
    rows = lambda w: pl.BlockSpec((ROW_BLOCK, w), lambda i: (cur(i), 0))
    routed = pl.BlockSpec((ROW_BLOCK, lanes), lambda i: (jnp.maximum(i - 1, 0), 0))
    ctx_rows = _ctx_rows(D_MODEL, n_ctx_blocks)
    lat_rows = pl.BlockSpec((ROW_BLOCK, D_MODEL), lambda i: (jnp.maximum(cur(i) - n_ctx_blocks, 0), 0))
    return pl.pallas_call(
        functools.partial(_merge_kernel, n_ctx_blocks=n_ctx_blocks),
        grid=(nb + 1,),
        in_specs=[ctx_rows, lat_rows, _mod_spec(lambda i: mod_row(cur(i))), rows(D_MODEL), ctx_rows, lat_rows,
                  ctx_rows, lat_rows, rows(3 * D_MODEL)] + [_resident(a.shape) for a in consts],
        out_specs=[rows(D_MODEL), rows(D_MODEL), routed, routed, routed,
                   pl.BlockSpec((8, lanes), lambda i: (0, 0))],
        out_shape=[jax.ShapeDtypeStruct((t, D_MODEL), F32), jax.ShapeDtypeStruct((t, D_MODEL), F32),
                   jax.ShapeDtypeStruct((t, lanes), F32), jax.ShapeDtypeStruct((t, lanes), jnp.int32),
                   jax.ShapeDtypeStruct((t, lanes), jnp.int32), jax.ShapeDtypeStruct((8, lanes), F32)],
        scratch_shapes=[pltpu.VMEM((8, lanes), F32), pltpu.VMEM((ROW_BLOCK, lanes), F32)],
        compiler_params=_cparams("arbitrary"),
        name="merge",
    )(*x, mod, y_pool, *y_diff, *y_mla, gates, *consts)


def _slot_layout(counts, topi, rank):
    tile = MOE_TILE
    cnt = counts[0, :N_EXPERTS].astype(jnp.int32)
    padded = (cnt + tile - 1) // tile * tile
    end = jnp.cumsum(padded)
    start = end - padded
    experts = jnp.arange(N_EXPERTS, dtype=jnp.int32)
    idx4 = topi[:, :TOP_K]
    slot_start = jnp.sum(jnp.where(idx4[:, :, None] == experts, start, 0), axis=-1)
    pos = (slot_start + rank[:, :TOP_K]).reshape(-1)
    n_tiles = (topi.shape[0] * TOP_K + N_EXPERTS * tile) // tile
    tile_row = jnp.arange(n_tiles, dtype=jnp.int32) * tile
    tile_expert = jnp.minimum(jnp.sum(tile_row[:, None] >= end, axis=-1), N_EXPERTS - 1).astype(jnp.int32)
    n_used = (end[-1] // tile).astype(jnp.int32)
    tile_src = jnp.minimum(jnp.arange(n_tiles, dtype=jnp.int32), n_used - 1)
    ragged_last = jnp.any((cnt % tile != 0) & (tile_row[:, None] == end - tile), axis=-1)
    fill_tile = (ragged_last | (tile_row >= end[-1])).astype(jnp.int32)
    return pos, tile_expert, tile_src, n_used.reshape(1), fill_tile


def _row_copy(src, src_row, dst, dst_row, sem):
    return pltpu.make_async_copy(src.at[pl.ds(src_row, 1)], dst.at[pl.ds(dst_row, 1)], sem)


def _dispatch_kernel(fill_tile_ref, pos_ref, h_ref, hs_ref, zero_ref, fill_sem, row_sem):
    i = pl.program_id(0)
    slots = DISPATCH_ROWS * TOP_K

    @pl.when(i == 0)
    def _():
        zero_ref[...] = jnp.zeros_like(zero_ref)

        def fill_copy(tile):
            row = pl.multiple_of(tile * MOE_TILE, MOE_TILE)
            return pltpu.make_async_copy(zero_ref, hs_ref.at[pl.ds(row, MOE_TILE)], fill_sem)

        def start_fill(tile, carry):
            @pl.when(fill_tile_ref[tile] != 0)
            def _():
                fill_copy(tile).start()
            return carry

        def wait_fill(tile, carry):
            @pl.when(fill_tile_ref[tile] != 0)
            def _():
                fill_copy(tile).wait()
            return carry

        lax.fori_loop(0, hs_ref.shape[0] // MOE_TILE, start_fill, 0)
        lax.fori_loop(0, hs_ref.shape[0] // MOE_TILE, wait_fill, 0)

    def issue(r, carry):
        slot = (i * DISPATCH_ROWS + r) * TOP_K
        for k in range(TOP_K):
            _row_copy(h_ref, r, hs_ref, pos_ref[slot + k], row_sem).start()
        return carry

    def drain(j, carry):
        for _ in range(WAIT_UNROLL):
            _row_copy(h_ref, 0, hs_ref, 0, row_sem).wait()
        return carry

    lax.fori_loop(0, DISPATCH_ROWS, issue, 0, unroll=ISSUE_UNROLL)
    lax.fori_loop(0, slots // WAIT_UNROLL, drain, 0)


def _dispatch(h2, pos, fill_tile):
    t = h2.shape[0]
    n_rows = fill_tile.shape[0] * MOE_TILE
    return pl.pallas_call(
        _dispatch_kernel,
        grid_spec=pltpu.PrefetchScalarGridSpec(
            num_scalar_prefetch=2,
            grid=(t // DISPATCH_ROWS,),
            in_specs=[pl.BlockSpec((DISPATCH_ROWS, D_MODEL), lambda i, fill_ref, pos_ref: (i, 0))],
            out_specs=pl.BlockSpec(memory_space=pl.ANY),
            scratch_shapes=[pltpu.VMEM((MOE_TILE, D_MODEL), F32), pltpu.SemaphoreType.DMA,
                            pltpu.SemaphoreType.DMA],
        ),
        out_shape=jax.ShapeDtypeStruct((n_rows, D_MODEL), F32),
        compiler_params=_cparams("arbitrary"),
        name="dispatch",
    )(fill_tile, pos, h2)


def _expert_kernel(tile_expert_ref, tile_src_ref, n_used_ref, hs_ref, wgu_ref, bgu_ref, wd_ref, bd_ref, ys_ref,
                   wgu_bf_ref, wd_bf_ref):
    i = pl.program_id(0)
    occupied = i < n_used_ref[0]
    new_expert = (i == 0) | (tile_expert_ref[i] != tile_expert_ref[jnp.maximum(i - 1, 0)])

    @pl.when(occupied & new_expert)
    def _():
        wgu_bf_ref[...] = wgu_ref[...].astype(BF16)
        wd_bf_ref[...] = wd_ref[...].astype(BF16)

    @pl.when(occupied)
    def _():
        gu = jnp.dot(hs_ref[...].astype(BF16), wgu_bf_ref[...], preferred_element_type=F32) + bgu_ref[0]
        gate = jnp.minimum(gu[:, :D_FF], SWIGLU_LIMIT)
        up = jnp.clip(gu[:, D_FF:], -SWIGLU_LIMIT, SWIGLU_LIMIT)
        act = (up + 1.0) * gate * _sigmoid(SWIGLU_ALPHA * gate)
        ys_ref[...] = jnp.dot(act.astype(BF16), wd_bf_ref[...], preferred_element_type=F32) + bd_ref[0]

    @pl.when(jnp.logical_not(occupied))
    def _():
        ys_ref[...] = jnp.zeros_like(ys_ref)


def _experts(hs, tile_expert, tile_src, n_used, layer, wgu, bgu, wd, bd):
    n_rows = hs.shape[0]
    rows = lambda i, te, ts, nu: (ts[i], 0)
    bias = lambda i, te, ts, nu: (te[i], 0, 0)
    weight = lambda i, te, ts, nu: (layer, te[i], 0, 0)
    return pl.pallas_call(
        _expert_kernel,
        grid_spec=pltpu.PrefetchScalarGridSpec(
            num_scalar_prefetch=3,
            grid=(n_rows // MOE_TILE,),
            in_specs=[
                pl.BlockSpec((MOE_TILE, D_MODEL), rows),
                pl.BlockSpec((None, None, D_MODEL, 2 * D_FF), weight),
                pl.BlockSpec((1, 1, 2 * D_FF), bias),
                pl.BlockSpec((None, None, D_FF, D_MODEL), weight),
                pl.BlockSpec((1, 1, D_MODEL), bias),
            ],
            out_specs=pl.BlockSpec((MOE_TILE, D_MODEL), lambda i, te, ts, nu: (i, 0)),
            scratch_shapes=[pltpu.VMEM((D_MODEL, 2 * D_FF), BF16), pltpu.VMEM((D_FF, D_MODEL), BF16)],
        ),
        out_shape=jax.ShapeDtypeStruct((n_rows, D_MODEL), F32),
        compiler_params=_cparams("arbitrary"),
        name="experts",
    )(tile_expert, tile_src, n_used, hs, wgu, bgu, wd, bd)


def _combine_kernel(pos_ref, ys_ref, w_ref, x1_ref, mod_ref, *refs, n_ctx_blocks):
    oc_ref, ol_ref, buf_ref, sems = refs
    i = pl.program_id(0)
    slots = COMBINE_ROWS * TOP_K

    def gather(step, buf):
        def issue(r, carry):
            token = step * COMBINE_ROWS + r
            for k in range(TOP_K):
                _row_copy(ys_ref, pos_ref[token * TOP_K + k], buf_ref.at[buf, k], r, sems.at[buf]).start()
            return carry

        lax.fori_loop(0, COMBINE_ROWS, issue, 0, unroll=ISSUE_UNROLL)

    @pl.when(i == 0)
    def _():
        gather(0, 0)

    @pl.when(i + 1 < pl.num_programs(0))
    def _():
        gather(i + 1, (i + 1) % 2)

    cur = i % 2

    def drain(j, carry):
        for _ in range(WAIT_UNROLL):
            _row_copy(ys_ref, 0, buf_ref.at[cur, 0], 0, sems.at[cur]).wait()
        return carry

    lax.fori_loop(0, slots // WAIT_UNROLL, drain, 0)
    w = w_ref[...]
    acc = w[:, 0:1] * buf_ref[cur, 0]
    for k in range(1, TOP_K):
        acc = acc + w[:, k:k + 1] * buf_ref[cur, k]
    result = x1_ref[...] + mod_ref[0][5:6] * acc

    @pl.when(i < n_ctx_blocks)
    def _():
        oc_ref[...] = result

    @pl.when(i >= n_ctx_blocks)
    def _():
        ol_ref[...] = result


def _combine(ys, pos, topw, x1, mod, n_ctx_rows, lat_len):
    t = x1.shape[0]
    rb = COMBINE_ROWS
    n_ctx_blocks = n_ctx_rows // rb

    def mod_row(i, pos_ref):
        return (jnp.where(i < n_ctx_blocks, 0, 1 + (i - n_ctx_blocks) // (lat_len // rb)), 0, 0)

    out_specs = [pl.BlockSpec((rb, D_MODEL), lambda i, pos_ref: (jnp.minimum(i, n_ctx_blocks - 1), 0)),
                 pl.BlockSpec((rb, D_MODEL), lambda i, pos_ref: (jnp.maximum(i - n_ctx_blocks, 0), 0))]
    out_shape = [jax.ShapeDtypeStruct((n_ctx_rows, D_MODEL), F32),
                 jax.ShapeDtypeStruct((t - n_ctx_rows, D_MODEL), F32)]
    return pl.pallas_call(
        functools.partial(_combine_kernel, n_ctx_blocks=n_ctx_blocks),
        grid_spec=pltpu.PrefetchScalarGridSpec(
            num_scalar_prefetch=1,
            grid=(t // rb,),
            in_specs=[
                pl.BlockSpec(memory_space=pl.ANY),
                pl.BlockSpec((rb, EXPERT_LANES), lambda i, pos_ref: (i, 0)),
                pl.BlockSpec((rb, D_MODEL), lambda i, pos_ref: (i, 0)),
                pl.BlockSpec((1, 8, D_MODEL), mod_row),
            ],
            out_specs=out_specs,
            scratch_shapes=[pltpu.VMEM((2, TOP_K, rb, D_MODEL), F32), pltpu.SemaphoreType.DMA((2,))],
        ),
        out_shape=out_shape,
        compiler_params=_cparams("arbitrary"),
        name="combine",
    )(pos, ys, topw, x1, mod)


def _rope_tables(n_lat_tokens, rot_dim):
    n_rows = n_lat_tokens // GRID_W
    row = jnp.repeat(jnp.arange(n_rows, dtype=F32), GRID_W)
    col = jnp.tile(jnp.arange(GRID_W, dtype=F32), n_rows)
    n_freq = rot_dim // 4
    inv_freq = ROPE_THETA ** (-jnp.arange(n_freq, dtype=F32) / n_freq)
    ang = jnp.concatenate([row[:, None] * inv_freq, col[:, None] * inv_freq], axis=-1)
    return jnp.cos(ang), jnp.sin(ang)


def _token_tables(cos_lane, sin_lane, n_ctx_rows, n_lat_seqs):
    width = cos_lane.shape[1]
    cos = jnp.concatenate([jnp.ones((n_ctx_rows, width), F32)] + [cos_lane] * n_lat_seqs, axis=0)
    sin = jnp.concatenate([jnp.zeros((n_ctx_rows, width), F32)] + [sin_lane] * n_lat_seqs, axis=0)
    return cos, sin


def _pack_heads(w, n_heads, lo, hi, width):
    k = w.shape[0]
    per_head = w.shape[1] // n_heads
    part = w.reshape(k, n_heads, per_head)[:, :, lo:hi]
    part = jnp.pad(part, ((0, 0), (0, 0), (0, width - (hi - lo))))
    return part.reshape(k, n_heads * width)


def kernel(x_prompt, x_sample, c, cache_diff_k, cache_diff_v, cache_mla_ckv, cache_mla_krope, c_ctx, w_ada, b_ada,
           w_in, pool_w, pool_scale, diff_q_norm, diff_k_norm, diff_lambda, diff_out_norm, mla_q_a_norm, w_uq,
           mla_kv_a_norm, w_ukv, mla_q_norm, mla_k_norm, w_branch, w_out, router_w, router_b, moe_w_gu, moe_b_gu,
           moe_w_down, moe_b_down):
    batch, seq, d = x_prompt.shape
    dec_batch, dec_seq, _ = x_sample.shape
    depth = w_ada.shape[0]
    past = cache_diff_k.shape[2]
    n_ctx = batch * seq
    n_lat = dec_batch * dec_seq
    n_ctx_blocks = n_ctx // ROW_BLOCK
    mod_row = _mod_row(n_ctx_blocks, dec_seq // ROW_BLOCK)
    mod_spec = _mod_spec(mod_row)

    x = (x_prompt.reshape(n_ctx, d), x_sample.reshape(n_lat, d))
    cond8 = jnp.concatenate([c_ctx[None], c, jnp.zeros((8 - 1 - dec_batch, d), F32)], axis=0)
    mod_all = _ada(cond8, w_ada, b_ada).reshape(depth, 8, 6, d)[:, :1 + dec_batch]
    mod_all = jnp.pad(mod_all, ((0, 0), (0, 0), (0, 2), (0, 0)))

    cos_d, sin_d = _rope_tables(dec_seq, DIFF_HD)
    cos_m, sin_m = _rope_tables(dec_seq, MLA_ROPE)
    cos_a, sin_a = _token_tables(jnp.concatenate([cos_d] * 4, axis=1),
                                 jnp.concatenate([-sin_d, sin_d, -sin_d, sin_d], axis=1), n_ctx, dec_batch)
    pad = MLA_W - MLA_QK
    cos_b, sin_b = _token_tables(jnp.concatenate([cos_m, cos_m, jnp.ones((dec_seq, pad), F32)], axis=1),
                                 jnp.concatenate([-sin_m, sin_m, jnp.zeros((dec_seq, pad), F32)], axis=1),
                                 n_ctx, dec_batch)

    offs = np.cumsum((0, 1024, 1024, 1024, 1024, MLA_Q_RANK, MLA_KV_RANK, MLA_ROPE, 3 * 1024))
    new_dk, new_dv, new_ckv, new_kr = [], [], [], []
    for l in range(depth):
        lam_init = 0.8 - 0.6 * math.exp(-0.3 * l)
        wl = w_in[l]
        seg = [wl[:, offs[k]:offs[k + 1]] for k in range(8)]
        w_a = jnp.concatenate(seg[0:4] + [seg[7]], axis=1).astype(BF16)
        kr_wide = jnp.pad(seg[6], ((0, 0), (MLA_NOPE, MLA_W - MLA_NOPE - MLA_ROPE)))
        w_b = jnp.concatenate([seg[4], seg[5], kr_wide], axis=1).astype(BF16)
        wuq = _pack_heads(w_uq[l], MLA_HEADS, 0, MLA_QK, MLA_W).astype(BF16)
        wkn = _pack_heads(w_ukv[l], MLA_HEADS, 0, MLA_NOPE, MLA_W).astype(BF16)
        wv = _pack_heads(w_ukv[l], MLA_HEADS, MLA_NOPE, MLA_NOPE + HEAD_W, HEAD_W).astype(BF16)
        gq_d = jnp.tile(diff_q_norm[l], 2 * DIFF_HEADS)[None]
        gk_d = jnp.tile(diff_k_norm[l], 2 * DIFF_HEADS)[None]
        gq_m = jnp.tile(jnp.pad(mla_q_norm[l], (0, MLA_W - MLA_QK)), MLA_HEADS)[None]
        gk_m = jnp.tile(jnp.pad(mla_k_norm[l], (0, MLA_W - MLA_QK)), MLA_HEADS)[None]
        lp = diff_lambda[l]
        lam = (jnp.exp(jnp.sum(lp[0] * lp[1])) - jnp.exp(jnp.sum(lp[2] * lp[3])) + lam_init).reshape(1)
        out_gain = (diff_out_norm[l] * (1.0 - lam_init))[None]
        mod = mod_all[l]

        u_pool, dq, dk_f, dk_b, dv_f, dv_b, gates = _proj_a(x, mod, w_a, gq_d, gk_d, cos_a, sin_a, mod_spec,
                                                            n_ctx_blocks)
        mq, ckv_n, kr, mk, mv = _proj_b(x, mod, w_b, mla_q_a_norm[l][None], mla_kv_a_norm[l][None], wuq, wkn, wv,
                                        gq_m, gk_m, cos_b, sin_b, mod_spec, n_ctx_blocks)
        new_dk.append(dk_f[:n_ctx].reshape(batch, seq, DIFF_HEADS, HEAD_W))
        new_dv.append(dv_f[:n_ctx].reshape(batch, seq, DIFF_HEADS, HEAD_W))
        new_ckv.append(ckv_n[:n_ctx].reshape(batch, seq, MLA_KV_RANK))
        new_kr.append(kr[:n_ctx].reshape(batch, seq, MLA_ROPE))

        y_pool = _pool(u_pool, pool_w[l].astype(BF16), pool_scale[l][None], n_ctx_blocks, seq, dec_seq)

        c_krw = jnp.pad(cache_mla_krope[:, l].reshape(dec_batch * past, MLA_ROPE),
                        ((0, 0), (MLA_NOPE, MLA_W - MLA_NOPE - MLA_ROPE)))
        c_mk, c_mv = _cache_kv(cache_mla_ckv[:, l].reshape(dec_batch * past, MLA_KV_RANK), c_krw, wkn, wv, gk_m)

        geom = (n_ctx, seq, dec_seq)
        y_diff = _diff_attention(dq, dk_b, dv_b, cache_diff_k[:, l].reshape(dec_batch, past, d),
                                 cache_diff_v[:, l].reshape(dec_batch, past, d), lam, out_gain, geom)
        y_mla = _mla_attention(mq, mk, mv, c_mk.reshape(dec_batch, past, -1), c_mv.reshape(dec_batch, past, -1),
                               geom)

        rw = jnp.pad(router_w[l], ((0, 0), (0, EXPERT_LANES - N_EXPERTS)))
        rb = jnp.pad(router_b[l], (0, EXPERT_LANES - N_EXPERTS), constant_values=-jnp.inf)[None]
        x1, h2, topw, topi, rank, counts = _merge(x, mod, y_pool, y_diff, y_mla, gates, w_branch[l].astype(BF16),
                                                  w_out[l].astype(BF16), rw, rb, mod_row, n_ctx_blocks)
        pos, tile_expert, tile_src, n_used, fill_tile = _slot_layout(counts, topi, rank)
        hs = _dispatch(h2, pos, fill_tile)
        ys = _experts(hs, tile_expert, tile_src, n_used, l, moe_w_gu, moe_b_gu[l][:, None, :],
                      moe_w_down, moe_b_down[l][:, None, :])
        x = _combine(ys, pos, topw, x1, mod, n_ctx, dec_seq)

    y_ctx, y_lat = x
    return (y_ctx.reshape(batch, seq, d), y_lat.reshape(dec_batch, dec_seq, d),
            jnp.stack(new_dk, axis=1), jnp.stack(new_dv, axis=1), jnp.stack(new_ckv, axis=1),
            jnp.stack(new_kr, axis=1))
```

```python
import functools
import math

import jax
import jax.numpy as jnp
import numpy as np
from jax import lax
from jax.experimental import pallas as pl
from jax.experimental.pallas import tpu as pltpu

F32 = jnp.float32
BF16 = jnp.bfloat16

D_MODEL = 1024
N_POOL_GROUPS = 4
POOL_GROUP = D_MODEL // N_POOL_GROUPS
DIFF_HD = 64
DIFF_HEADS = 8
HEAD_W = 128
MLA_HEADS = 8
MLA_NOPE = 128
MLA_ROPE = 64
MLA_QK = MLA_NOPE + MLA_ROPE
MLA_W = 256
MLA_Q_RANK = 768
MLA_KV_RANK = 512
N_EXPERTS = 32
TOP_K = 4
D_FF = 1024
EXPERT_LANES = 128
SWIGLU_ALPHA = 1.702
SWIGLU_LIMIT = 7.0
ROPE_THETA = 10000.0
NORM_EPS = 1e-6
GRID_W = 64
LOG2E = math.log2(math.e)
VMEM_LIMIT = 56 * 1024 * 1024

ROW_BLOCK = 256
POOL_HALO = 16
MOE_TILE = 512
DISPATCH_ROWS = 256
COMBINE_ROWS = 128
WAIT_UNROLL = 16
ISSUE_UNROLL = 4
DIFF_Q_BLOCK = 1024
MLA_Q_BLOCK = 2048
ATTN_K_CHUNK = 512

_NT = (((1,), (1,)), ((), ()))
_TN = (((0,), (0,)), ((), ()))


def _cparams(*sem):
    return pltpu.CompilerParams(dimension_semantics=sem, vmem_limit_bytes=VMEM_LIMIT)


def _resident(shape):
    nd = len(shape)
    return pl.BlockSpec(shape, lambda *_: (0,) * nd, pipeline_mode=pl.Buffered(1))


def _sigmoid(x):
    return 1.0 / (1.0 + jnp.exp(-x))


def _lane_iota(shape):
    return lax.broadcasted_iota(jnp.int32, shape, len(shape) - 1)


def _rms(x, width):
    ss = jnp.sum(x * x, axis=-1, keepdims=True)
    return x * lax.rsqrt(ss * (1.0 / width) + NORM_EPS)


def _rope(x, cos, sin):
    n = x.shape[-1]
    lane = _lane_iota(x.shape)
    partner = jnp.where((lane & 32) != 0, pltpu.roll(x, 32, 1), pltpu.roll(x, n - 32, 1))
    return x * cos + partner * sin


def _modulated_norm(x, shift, scale):
    return _rms(x, D_MODEL) * (1.0 + scale) + shift


def _ada_kernel(cond_ref, w_ref, b_ref, o_ref):
    cnd = cond_ref[...]
    act = cnd * _sigmoid(cnd)
    o_ref[0] = jnp.dot(act, w_ref[0], preferred_element_type=F32) + b_ref[0]


def _ada(cond8, w_ada, b_ada):
    depth = w_ada.shape[0]
    n_chunk = w_ada.shape[2] // D_MODEL
    return pl.pallas_call(
        _ada_kernel,
        grid=(depth, n_chunk),
        in_specs=[
            pl.BlockSpec((8, D_MODEL), lambda l, j: (0, 0)),
            pl.BlockSpec((1, D_MODEL, D_MODEL), lambda l, j: (l, 0, j)),
            pl.BlockSpec((1, 1, D_MODEL), lambda l, j: (l, 0, j)),
        ],
        out_specs=pl.BlockSpec((1, 8, D_MODEL), lambda l, j: (l, 0, j)),
        out_shape=jax.ShapeDtypeStruct((depth, 8, n_chunk * D_MODEL), F32),
        compiler_params=_cparams("arbitrary", "arbitrary"),
        name="ada",
    )(cond8, w_ada, b_ada.reshape(depth, 1, -1))


def _diff_qk_norm(u, gain, cos, sin):
    outs = []
    for h in range(DIFF_HEADS):
        c = u[:, h * HEAD_W:(h + 1) * HEAD_W]
        lane = _lane_iota(c.shape)
        sq = c * c
        lo = jnp.sum(jnp.where(lane < DIFF_HD, sq, 0.0), axis=-1, keepdims=True)
        hi = jnp.sum(jnp.where(lane >= DIFF_HD, sq, 0.0), axis=-1, keepdims=True)
        ms = jnp.where(lane < DIFF_HD, lo, hi) * (1.0 / DIFF_HD)
        y = c * lax.rsqrt(ms + NORM_EPS) * gain[:, h * HEAD_W:(h + 1) * HEAD_W]
        outs.append((y, _rope(y, cos, sin)))
    return outs


def _token_block(ctx_ref, lat_ref, n_ctx_blocks):
    return jnp.where(pl.program_id(0) < n_ctx_blocks, ctx_ref[...], lat_ref[...])


def _proj_a_kernel(xc_ref, xl_ref, mod_ref, w_ref, gq_ref, gk_ref, cos_ref, sin_ref,
                   pool_ref, dq_ref, dkf_ref, dkb_ref, dvf_ref, dvb_ref, gate_ref, *, n_ctx_blocks):
    mod = mod_ref[0]
    h = _modulated_norm(_token_block(xc_ref, xl_ref, n_ctx_blocks), mod[0:1], mod[1:2]).astype(BF16)
    cos, sin = cos_ref[...], sin_ref[...]

    def seg(k):
        return jnp.dot(h, w_ref[:, k * D_MODEL:(k + 1) * D_MODEL], preferred_element_type=F32)

    u_pool, u_dq = seg(0), seg(1)
    pool_ref[...] = u_pool.astype(BF16)
    u_dk = seg(2)
    q_scale = DIFF_HD ** -0.5 * LOG2E
    for hd, (_, roped) in enumerate(_diff_qk_norm(u_dq, gq_ref[...], cos, sin)):
        dq_ref[:, hd * HEAD_W:(hd + 1) * HEAD_W] = (roped * q_scale).astype(BF16)
    dv = seg(3)
    for hd, (plain, roped) in enumerate(_diff_qk_norm(u_dk, gk_ref[...], cos, sin)):
        dkf_ref[:, hd * HEAD_W:(hd + 1) * HEAD_W] = plain
        dkb_ref[:, hd * HEAD_W:(hd + 1) * HEAD_W] = roped.astype(BF16)
    u_gate = seg(4)
    dvf_ref[...] = dv
    dvb_ref[...] = dv.astype(BF16)
    for r in range(3):
        u_next = seg(5 + r) if r < 2 else None
        gate_ref[:, r * D_MODEL:(r + 1) * D_MODEL] = _sigmoid(u_gate).astype(BF16)
        u_gate = u_next


def _mla_heads(u, gain, cos, sin, rope):
    outs = []
    for h in range(MLA_HEADS):
        c = u[:, h * MLA_W:(h + 1) * MLA_W]
        y = _rms(c, MLA_QK) * gain[:, h * MLA_W:(h + 1) * MLA_W]
        if rope:
            y = jnp.concatenate([y[:, :MLA_NOPE], _rope(y[:, MLA_NOPE:], cos, sin)], axis=-1)
        outs.append(y)
    return outs


def _mla_kv(ckv_n, krw, wkn_ref, wv_ref, gk, cos, sin, rope, k_ref, v_ref):
    kn = jnp.dot(ckv_n, wkn_ref[...], preferred_element_type=F32)
    v_ref[...] = jnp.dot(ckv_n, wv_ref[...], preferred_element_type=F32).astype(BF16)
    kfull = jnp.concatenate([kn[:, h * MLA_W:(h + 1) * MLA_W] + krw for h in range(MLA_HEADS)], axis=-1)
    for hd, y in enumerate(_mla_heads(kfull, gk, cos, sin, rope)):
        k_ref[:, hd * MLA_W:(hd + 1) * MLA_W] = y.astype(BF16)


def _proj_b_kernel(xc_ref, xl_ref, mod_ref, w_ref, gqa_ref, gkva_ref, wuq_ref, wkn_ref, wv_ref, gq_ref, gk_ref,
                   cos_ref, sin_ref, mq_ref, ckv_ref, kr_ref, mk_ref, mv_ref, *, n_ctx_blocks):
    mod = mod_ref[0]
    h = _modulated_norm(_token_block(xc_ref, xl_ref, n_ctx_blocks), mod[0:1], mod[1:2]).astype(BF16)
    cos, sin = cos_ref[...], sin_ref[...]
    cq = jnp.dot(h, w_ref[:, :MLA_Q_RANK], preferred_element_type=F32)
    ckv = jnp.dot(h, w_ref[:, MLA_Q_RANK:MLA_Q_RANK + MLA_KV_RANK], preferred_element_type=F32)
    krw = jnp.dot(h, w_ref[:, MLA_Q_RANK + MLA_KV_RANK:], preferred_element_type=F32)
    cq_n = (_rms(cq, MLA_Q_RANK) * gqa_ref[...]).astype(BF16)
    mq = jnp.dot(cq_n, wuq_ref[...], preferred_element_type=F32)
    ckv_n = _rms(ckv, MLA_KV_RANK) * gkva_ref[...]
    ckv_ref[...] = ckv_n
    kr_ref[...] = krw[:, MLA_NOPE:MLA_NOPE + MLA_ROPE]
    _mla_kv(ckv_n.astype(BF16), krw, wkn_ref, wv_ref, gk_ref[...], cos, sin, True, mk_ref, mv_ref)
    q_scale = MLA_QK ** -0.5 * LOG2E
    for hd, y in enumerate(_mla_heads(mq, gq_ref[...], cos, sin, True)):
        mq_ref[:, hd * MLA_W:(hd + 1) * MLA_W] = (y * q_scale).astype(BF16)


def _cache_kv_kernel(ckv_ref, krw_ref, wkn_ref, wv_ref, gk_ref, mk_ref, mv_ref):
    _mla_kv(ckv_ref[...].astype(BF16), krw_ref[...], wkn_ref, wv_ref, gk_ref[...], None, None, False,
            mk_ref, mv_ref)


def _mod_row(n_ctx_blocks, blocks_per_lat):
    return lambda i: jnp.where(i < n_ctx_blocks, 0, 1 + (i - n_ctx_blocks) // blocks_per_lat)


def _mod_spec(mod_row):
    return pl.BlockSpec((1, 8, D_MODEL), lambda i: (mod_row(i), 0, 0))


def _rows(width):
    return pl.BlockSpec((ROW_BLOCK, width), lambda i: (i, 0))


def _ctx_rows(width, n_ctx_blocks):
    return pl.BlockSpec((ROW_BLOCK, width), lambda i: (jnp.minimum(i, n_ctx_blocks - 1), 0))


def _lat_rows(width, n_ctx_blocks):
    return pl.BlockSpec((ROW_BLOCK, width), lambda i: (jnp.maximum(i - n_ctx_blocks, 0), 0))


def _proj_outputs(t, n_ctx_blocks, widths, dtypes, ctx_only):
    spare = lambda w: pl.BlockSpec((ROW_BLOCK, w), lambda i: (jnp.minimum(i, n_ctx_blocks), 0))
    specs = [spare(w) if c else _rows(w) for w, c in zip(widths, ctx_only)]
    shapes = [jax.ShapeDtypeStruct(((n_ctx_blocks + 1) * ROW_BLOCK if c else t, w), dt)
              for w, dt, c in zip(widths, dtypes, ctx_only)]
    return specs, shapes


def _proj_a(x, mod, w_a, gq, gk, cos, sin, mod_spec, n_ctx_blocks):
    t = x[0].shape[0] + x[1].shape[0]
    widths = (D_MODEL,) * 6 + (3 * D_MODEL,)
    dtypes = (BF16, BF16, F32, BF16, F32, BF16, BF16)
    out_specs, out_shape = _proj_outputs(t, n_ctx_blocks, widths, dtypes, (0, 0, 1, 0, 1, 0, 0))
    return pl.pallas_call(
        functools.partial(_proj_a_kernel, n_ctx_blocks=n_ctx_blocks),
        grid=(t // ROW_BLOCK,),
        in_specs=[_ctx_rows(D_MODEL, n_ctx_blocks), _lat_rows(D_MODEL, n_ctx_blocks), mod_spec,
                  _resident(w_a.shape), _resident(gq.shape), _resident(gk.shape), _rows(HEAD_W), _rows(HEAD_W)],
        out_specs=out_specs,
        out_shape=out_shape,
        compiler_params=_cparams("arbitrary"),
        name="proj_a",
    )(*x, mod, w_a, gq, gk, cos, sin)


def _proj_b(x, mod, w_b, gqa, gkva, wuq, wkn, wv, gq, gk, cos, sin, mod_spec, n_ctx_blocks):
    t = x[0].shape[0] + x[1].shape[0]
    widths = (MLA_HEADS * MLA_W, MLA_KV_RANK, MLA_ROPE, MLA_HEADS * MLA_W, D_MODEL)
    dtypes = (BF16, F32, F32, BF16, BF16)
    out_specs, out_shape = _proj_outputs(t, n_ctx_blocks, widths, dtypes, (0, 1, 1, 0, 0))
    consts = (w_b, gqa, gkva, wuq, wkn, wv, gq, gk)
    return pl.pallas_call(
        functools.partial(_proj_b_kernel, n_ctx_blocks=n_ctx_blocks),
        grid=(t // ROW_BLOCK,),
        in_specs=[_ctx_rows(D_MODEL, n_ctx_blocks), _lat_rows(D_MODEL, n_ctx_blocks), mod_spec]
        + [_resident(a.shape) for a in consts] + [_rows(HEAD_W), _rows(HEAD_W)],
        out_specs=out_specs,
        out_shape=out_shape,
        compiler_params=_cparams("arbitrary"),
        name="proj_b",
    )(*x, mod, *consts, cos, sin)


def _cache_kv(ckv, krw, wkn, wv, gk):
    n = ckv.shape[0]
    widths = (MLA_HEADS * MLA_W, D_MODEL)
    consts = (wkn, wv, gk)
    return pl.pallas_call(
        _cache_kv_kernel,
        grid=(n // ROW_BLOCK,),
        in_specs=[_rows(MLA_KV_RANK), _rows(MLA_W)] + [_resident(a.shape) for a in consts],
        out_specs=[_rows(w) for w in widths],
        out_shape=[jax.ShapeDtypeStruct((n, w), BF16) for w in widths],
        compiler_params=_cparams("arbitrary"),
        name="cache_kv",
    )(ckv, krw, *consts)


def _pool_kernel(prev_ref, cur_ref, next_ref, w_ref, scale_ref, o_ref, *, n_ctx_blocks, ctx_len, lat_len):
    i = pl.program_id(0)
    rb = ROW_BLOCK
    is_ctx = i < n_ctx_blocks
    n_ctx_rows = n_ctx_blocks * rb
    seq_len = jnp.where(is_ctx, ctx_len, lat_len)
    seq_start = jnp.where(is_ctx, (i * rb // ctx_len) * ctx_len,
                          n_ctx_rows + ((i * rb - n_ctx_rows) // lat_len) * lat_len)
    n_cols = rb + 2 * POOL_HALO
    row_pos = i * rb - seq_start + lax.broadcasted_iota(jnp.int32, (rb, n_cols), 0)
    col_pos = i * rb - POOL_HALO - seq_start + lax.broadcasted_iota(jnp.int32, (rb, n_cols), 1)
    cur = cur_ref[...]
    u = jnp.concatenate([prev_ref[...], cur, next_ref[...]], axis=0)
    for g in range(N_POOL_GROUPS):
        cols = slice(g * POOL_GROUP, (g + 1) * POOL_GROUP)
        window = 2 << g
        lo = jnp.clip(row_pos - window // 2, 0, seq_len)
        hi = jnp.clip(row_pos - window // 2 + window, 0, seq_len)
        band = jnp.where((col_pos >= lo) & (col_pos < hi), 1.0, 0.0).astype(BF16)
        total = jnp.dot(band, u[:, cols], preferred_element_type=F32)
        cnt = (hi[:, :1] - lo[:, :1]).astype(F32)
        pooled = total / cnt - cur[:, cols].astype(F32)
        mixed = jnp.dot(pooled.astype(BF16), w_ref[g], preferred_element_type=F32)
        o_ref[:, cols] = (mixed * scale_ref[:, cols]).astype(BF16)


def _pool(u_pool, pool_w, pool_scale, n_ctx_blocks, ctx_len, lat_len):
    t = u_pool.shape[0]
    halo_per_block = ROW_BLOCK // POOL_HALO
    n_halo_blocks = t // POOL_HALO
    assert POOL_HALO >= (2 << (N_POOL_GROUPS - 1)) // 2
    kern = functools.partial(_pool_kernel, n_ctx_blocks=n_ctx_blocks, ctx_len=ctx_len, lat_len=lat_len)
    return pl.pallas_call(
        kern,
        grid=(t // ROW_BLOCK,),
        in_specs=[
            pl.BlockSpec((POOL_HALO, D_MODEL), lambda i: (jnp.maximum(i * halo_per_block - 1, 0), 0)),
            _rows(D_MODEL),
            pl.BlockSpec((POOL_HALO, D_MODEL),
                         lambda i: (jnp.minimum((i + 1) * halo_per_block, n_halo_blocks - 1), 0)),
            _resident(pool_w.shape),
            _resident(pool_scale.shape),
        ],
        out_specs=_rows(D_MODEL),
        out_shape=jax.ShapeDtypeStruct((t, D_MODEL), BF16),
        compiler_params=_cparams("arbitrary"),
        name="pool",
    )(u_pool, u_pool, u_pool, pool_w, pool_scale)


def _softmax_stats(s, m, l):
    m_new = jnp.maximum(m, jnp.max(s, axis=0, keepdims=True))
    p = jnp.exp2(s - m_new)
    alpha = jnp.exp2(m - m_new)
    return m_new, alpha * l + jnp.sum(p, axis=0, keepdims=True), alpha, p.astype(BF16)


def _accumulate(acc, alpha, p, v):
    return alpha * acc + lax.dot_general(v, p, _TN, preferred_element_type=F32)


def _online_softmax(chunks, scores, n_softmax, tq):
    chunks = list(chunks)
    m = [jnp.full((1, tq), -jnp.inf, F32)] * n_softmax
    l = [jnp.zeros((1, tq), F32)] * n_softmax
    acc = [jnp.zeros((HEAD_W, tq), F32)] * n_softmax
    pending = None
    s_next = scores(chunks[0][0])
    for c, (_, v) in enumerate(chunks):
        s = s_next
        if c + 1 < len(chunks):
            s_next = scores(chunks[c + 1][0])
        stats = [_softmax_stats(s[j], m[j], l[j]) for j in range(n_softmax)]
        m = [st[0] for st in stats]
        l = [st[1] for st in stats]
        if pending is not None:
            acc = [_accumulate(acc[j], *pending[j]) for j in range(n_softmax)]
        pending = [(st[2], st[3], v) for st in stats]
    acc = [_accumulate(acc[j], *pending[j]) for j in range(n_softmax)]
    return list(zip(l, acc))


def _key_value_chunks(kv_refs, head, qk_width):
    k_cols = slice(head * qk_width, (head + 1) * qk_width)
    v_cols = slice(head * HEAD_W, (head + 1) * HEAD_W)
    for k_ref, v_ref in zip(kv_refs[0::2], kv_refs[1::2]):
        n = k_ref.shape[0]
        size = min(n, ATTN_K_CHUNK)
        for start in range(0, n, size):
            yield k_ref[start:start + size, k_cols].astype(BF16), v_ref[start:start + size, v_cols].astype(BF16)


def _diff_attn_kernel(lam_ref, q_ref, *refs, n_kv, heads):
    g_ref, o_ref = refs[n_kv], refs[-1]
    for h in range(heads):
        cols = slice(h * HEAD_W, (h + 1) * HEAD_W)
        q = q_ref[:, cols]
        lane = _lane_iota(q.shape)
        q1 = jnp.where(lane < DIFF_HD, q, jnp.zeros_like(q))
        q2 = jnp.where(lane >= DIFF_HD, q, jnp.zeros_like(q))

        def scores(k):
            return (lax.dot_general(k, q1, _NT, preferred_element_type=F32),
                    lax.dot_general(k, q2, _NT, preferred_element_type=F32))

        (l1, a1), (l2, a2) = _online_softmax(_key_value_chunks(refs[:n_kv], h, HEAD_W), scores, 2, q.shape[0])
        o = (a1 / l1 - lam_ref[0] * (a2 / l2)).T
        o_ref[:, cols] = (_rms(o, HEAD_W) * g_ref[...]).astype(o_ref.dtype)


def _mla_attn_kernel(q_ref, *refs, n_kv, heads):
    o_ref = refs[-1]
    for h in range(heads):
        q = q_ref[:, h * MLA_W:(h + 1) * MLA_W]
        ((l, acc),) = _online_softmax(
            _key_value_chunks(refs[:n_kv], h, MLA_W),
            lambda k: (lax.dot_general(k, q, _NT, preferred_element_type=F32),), 1, q.shape[0])
        o_ref[:, h * HEAD_W:(h + 1) * HEAD_W] = (acc / l).T.astype(o_ref.dtype)


def _attention(kernel, q, k, v, cache_k, cache_v, lead, lead_specs, tail, tail_specs, qk_width, tq, geom, name):
    n_ctx, seq, dec_seq = geom
    t = q.shape[0]
    n_heads = v.shape[1] // HEAD_W
    dec_batch, past, _ = cache_k.shape
    nq, q0, k0 =dec_seq // tq, n_ctx // tq, n_ctx // dec_seq
    assert n_ctx % tq == 0 and n_ctx % dec_seq == 0 and dec_seq % tq == 0
    params = _cparams("arbitrary", "arbitrary", "arbitrary")
    seq_block = lambda w: pl.BlockSpec((seq, n_heads * w), lambda b, h, qi: (b, 0))
    ctx = pl.pallas_call(
        functools.partial(kernel, n_kv=2, heads=n_heads),
        grid=(n_ctx // seq, 1, 1),
        in_specs=lead_specs + [seq_block(qk_width), seq_block(qk_width), seq_block(HEAD_W)] + tail_specs,
        out_specs=seq_block(HEAD_W),
        out_shape=jax.ShapeDtypeStruct((n_ctx, n_heads * HEAD_W), BF16),
        compiler_params=params,
        name=name + "_ctx",
    )(*lead, q, k, v, *tail)
    lat_rows = lambda w, first: pl.BlockSpec((tq, w), lambda b, h, qi: (first + b * nq + qi, h))
    cached = lambda w: pl.BlockSpec((None, past, w), lambda b, h, qi: (b, 0, h))
    lat_keys = lambda w: pl.BlockSpec((dec_seq, w), lambda b, h, qi: (k0 + b, h))
    lat = pl.pallas_call(
        functools.partial(kernel, n_kv=4, heads=1),
        grid=(dec_batch, n_heads, nq),
        in_specs=lead_specs + [lat_rows(qk_width, q0), cached(qk_width), cached(HEAD_W), lat_keys(qk_width),
                               lat_keys(HEAD_W)] + tail_specs,
        out_specs=lat_rows(HEAD_W, 0),
        out_shape=jax.ShapeDtypeStruct((t - n_ctx, n_heads * HEAD_W), BF16),
        compiler_params=params,
        name=name + "_lat",
    )(*lead, q, cache_k, cache_v, k, v, *tail)
    return ctx, lat


def _diff_attention(q, k, v, cache_k, cache_v, lam, out_gain, geom):
    lead_specs = [pl.BlockSpec(memory_space=pltpu.SMEM)]
    tail_specs = [pl.BlockSpec((1, HEAD_W), lambda b, h, qi: (0, 0))]
    return _attention(_diff_attn_kernel, q, k, v, cache_k, cache_v, (lam,), lead_specs, (out_gain,), tail_specs,
                      HEAD_W, DIFF_Q_BLOCK, geom, "diff")


def _mla_attention(q, k, v, cache_k, cache_v, geom):
    return _attention(_mla_attn_kernel, q, k, v, cache_k, cache_v, (), [], (), [], MLA_W, MLA_Q_BLOCK, geom, "mla")


def _merge_kernel(xc_ref, xl_ref, mod_ref, yp_ref, ydc_ref, ydl_ref, ymc_ref, yml_ref, gate_ref, wb_ref, wo_ref,
                  rw_ref, rb_ref, x1_ref, h2_ref, topw_ref, topi_ref, rank_ref, count_ref, carry_ref, logits_ref,
                  *, n_ctx_blocks):
    step = pl.program_id(0)

    @pl.when(step == 0)
    def _():
        carry_ref[...] = jnp.zeros_like(carry_ref)
        logits_ref[...] = jnp.zeros_like(logits_ref)

    is_ctx = step < n_ctx_blocks
    mod = mod_ref[0]
    merged = jnp.zeros(xc_ref.shape, F32)
    branches = (yp_ref[...], jnp.where(is_ctx, ydc_ref[...], ydl_ref[...]),
                jnp.where(is_ctx, ymc_ref[...], yml_ref[...]))
    for r, y in enumerate(branches):
        z = jnp.dot(y, wb_ref[r], preferred_element_type=F32)
        merged = merged + gate_ref[:, r * D_MODEL:(r + 1) * D_MODEL].astype(F32) * z
    y = jnp.dot(merged.astype(BF16), wo_ref[...], preferred_element_type=F32)
    x1 = jnp.where(is_ctx, xc_ref[...], xl_ref[...]) + mod[2:3] * y
    x1_ref[...] = x1
    h2 = _modulated_norm(x1, mod[3:4], mod[4:5])
    h2_ref[...] = h2
    new_logits = jnp.dot(h2, rw_ref[...], preferred_element_type=F32) + rb_ref[...]
    logits = logits_ref[...]
    logits_ref[...] = new_logits
    lane = _lane_iota(logits.shape)
    picks = []
    for _ in range(TOP_K):
        mx = jnp.max(logits, axis=-1, keepdims=True)
        first = jnp.min(jnp.where(logits == mx, lane, EXPERT_LANES), axis=-1, keepdims=True)
        hit = lane == first
        picks.append((mx, first, hit))
        logits = jnp.where(hit, -jnp.inf, logits)
    exps = [jnp.exp(v - picks[0][0]) for v, _, _ in picks]
    denom = exps[0] + exps[1] + exps[2] + exps[3]
    rows = logits.shape[0]
    chosen = jnp.zeros(logits.shape, F32)
    for _, _, hit in picks:
        chosen = chosen + jnp.where(hit, 1.0, 0.0)
    earlier = jnp.where(lax.broadcasted_iota(jnp.int32, (rows, rows), 1)
                        < lax.broadcasted_iota(jnp.int32, (rows, rows), 0), 1.0, 0.0).astype(BF16)
    before = jnp.dot(earlier, chosen.astype(BF16), preferred_element_type=F32) + carry_ref[0:1, :]
    topw = jnp.zeros(logits.shape, F32)
    topi = jnp.zeros(logits.shape, jnp.int32)
    rank = jnp.zeros(logits.shape, jnp.int32)
    for k, (e, (_, first, hit)) in enumerate(zip(exps, picks)):
        slot_rank = jnp.sum(jnp.where(hit, before, 0.0), axis=-1, keepdims=True)
        topw = jnp.where(lane == k, e / denom, topw)
        topi = jnp.where(lane == k, first, topi)
        rank = jnp.where(lane == k, slot_rank.astype(jnp.int32), rank)
    topw_ref[...] = topw
    topi_ref[...] = topi
    rank_ref[...] = rank
    routed = jnp.where(step > 0, jnp.sum(chosen, axis=0, keepdims=True), 0.0)
    carry_ref[...] = carry_ref[...] + routed
    count_ref[...] = carry_ref[...]


def _merge(x, mod, y_pool, y_diff, y_mla, gates, wb, wo, rw, rb, mod_row, n_ctx_blocks):
    t = x[0].shape[0] + x[1].shape[0]
    nb = t // ROW_BLOCK
    consts = (wb, wo, rw, rb)
    lanes = EXPERT_LANES
    cur = lambda i: jnp.minimum(i, nb - 1)
    rows = lambda w: pl.BlockSpec((ROW_BLOCK, w), lambda i: (cur(i), 0))
    routed = pl.BlockSpec((ROW_BLOCK, lanes), lambda i: (jnp.maximum(i - 1, 0), 0))
    ctx_rows = _ctx_rows(D_MODEL, n_ctx_blocks)
    lat_rows = pl.BlockSpec((ROW_BLOCK, D_MODEL), lambda i: (jnp.maximum(cur(i) - n_ctx_blocks, 0), 0))
    return pl.pallas_call(
        functools.partial(_merge_kernel, n_ctx_blocks=n_ctx_blocks),
        grid=(nb + 1,),
        in_specs=[ctx_rows, lat_rows, _mod_spec(lambda i: mod_row(cur(i))), rows(D_MODEL), ctx_rows, lat_rows,
                  ctx_rows, lat_rows, rows(3 * D_MODEL)] + [_resident(a.shape) for a in consts],
        out_specs=[rows(D_MODEL), rows(D_MODEL), routed, routed, routed,
                   pl.BlockSpec((8, lanes), lambda i: (0, 0))],
        out_shape=[jax.ShapeDtypeStruct((t, D_MODEL), F32), jax.ShapeDtypeStruct((t, D_MODEL), F32),
                   jax.ShapeDtypeStruct((t, lanes), F32), jax.ShapeDtypeStruct((t, lanes), jnp.int32),
                   jax.ShapeDtypeStruct((t, lanes), jnp.int32), jax.ShapeDtypeStruct((8, lanes), F32)],
        scratch_shapes=[pltpu.VMEM((8, lanes), F32), pltpu.VMEM((ROW_BLOCK, lanes), F32)],
        compiler_params=_cparams("arbitrary"),
        name="merge",
    )(*x, mod, y_pool, *y_diff, *y_mla, gates, *consts)


def _slot_layout(counts, topi, rank):
    tile = MOE_TILE
    cnt = counts[0, :N_EXPERTS].astype(jnp.int32)
    padded = (cnt + tile - 1) // tile * tile
    end = jnp.cumsum(padded)
    start = end - padded
    experts = jnp.arange(N_EXPERTS, dtype=jnp.int32)
    idx4 = topi[:, :TOP_K]
    slot_start = jnp.sum(jnp.where(idx4[:, :, None] == experts, start, 0), axis=-1)
    pos = (slot_start + rank[:, :TOP_K]).reshape(-1)
    n_tiles = (topi.shape[0] * TOP_K + N_EXPERTS * tile) // tile
    tile_row = jnp.arange(n_tiles, dtype=jnp.int32) * tile
    tile_expert = jnp.minimum(jnp.sum(tile_row[:, None] >= end, axis=-1), N_EXPERTS - 1).astype(jnp.int32)
    n_used = (end[-1] // tile).astype(jnp.int32)
    tile_src = jnp.minimum(jnp.arange(n_tiles, dtype=jnp.int32), n_used - 1)
    ragged_last = jnp.any((cnt % tile != 0) & (tile_row[:, None] == end - tile), axis=-1)
    fill_tile = (ragged_last | (tile_row >= end[-1])).astype(jnp.int32)
    rows_end = jnp.sum(jnp.where(tile_expert[:, None] == experts, start + cnt, 0), axis=-1)
    tile_rows = jnp.where(tile_row < end[-1], jnp.clip(rows_end - tile_row, 0, tile), 0).astype(jnp.int32)
    return pos, tile_expert, tile_src, tile_rows, fill_tile


def _row_copy(src, src_row, dst, dst_row, sem):
    return pltpu.make_async_copy(src.at[pl.ds(src_row, 1)], dst.at[pl.ds(dst_row, 1)], sem)


def _dispatch_kernel(fill_tile_ref, pos_ref, h_ref, hs_ref, zero_ref, src_ref, fill_sem, fetch_sems, row_sems):
    i = pl.program_id(0)
    n_steps = pl.num_programs(0)
    slots = DISPATCH_ROWS * TOP_K

    def fetch(step, buf):
        rows = pl.ds(pl.multiple_of(step * DISPATCH_ROWS, DISPATCH_ROWS), DISPATCH_ROWS)
        return pltpu.make_async_copy(h_ref.at[rows], src_ref.at[buf], fetch_sems.at[buf])

    @pl.when(i == 0)
    def _():
        fetch(0, 0).start()
        fetch(1, 1).start()

    @pl.when(i == 0)
    def _():
        zero_ref[...] = jnp.zeros_like(zero_ref)

        def fill_copy(tile):
            row = pl.multiple_of(tile * MOE_TILE, MOE_TILE)
            return pltpu.make_async_copy(zero_ref, hs_ref.at[pl.ds(row, MOE_TILE)], fill_sem)

        def start_fill(tile, carry):
            @pl.when(fill_tile_ref[tile] != 0)
            def _():
                fill_copy(tile).start()
            return carry

        def wait_fill(tile, carry):
            @pl.when(fill_tile_ref[tile] != 0)
            def _():
                fill_copy(tile).wait()
            return carry

        lax.fori_loop(0, hs_ref.shape[0] // MOE_TILE, start_fill, 0)
        lax.fori_loop(0, hs_ref.shape[0] // MOE_TILE, wait_fill, 0)

    buf = i % 3
    src = src_ref.at[buf]
    fetch(i, buf).wait()

    def issue(r, carry):
        slot = (i * DISPATCH_ROWS + r) * TOP_K
        for k in range(TOP_K):
            _row_copy(src, r, hs_ref, pos_ref[slot + k], row_sems.at[i % 2]).start()
        return carry

    def drain(sem):
        def body(j, carry):
            for _ in range(WAIT_UNROLL):
                _row_copy(src, 0, hs_ref, 0, sem).wait()
            return carry

        lax.fori_loop(0, slots // WAIT_UNROLL, body, 0)

    lax.fori_loop(0, DISPATCH_ROWS, issue, 0, unroll=ISSUE_UNROLL)

    @pl.when(i > 0)
    def _():
        drain(row_sems.at[(i - 1) % 2])

    @pl.when(i + 2 < n_steps)
    def _():
        fetch(i + 2, (i + 2) % 3).start()

    @pl.when(i == n_steps - 1)
    def _():
        drain(row_sems.at[i % 2])


def _dispatch(h2, pos, fill_tile):
    t = h2.shape[0]
    n_rows = fill_tile.shape[0] * MOE_TILE
    return pl.pallas_call(
        _dispatch_kernel,
        grid_spec=pltpu.PrefetchScalarGridSpec(
            num_scalar_prefetch=2,
            grid=(t // DISPATCH_ROWS,),
            in_specs=[pl.BlockSpec(memory_space=pl.ANY)],
            out_specs=pl.BlockSpec(memory_space=pl.ANY),
            scratch_shapes=[pltpu.VMEM((MOE_TILE, D_MODEL), F32), pltpu.VMEM((3, DISPATCH_ROWS, D_MODEL), F32),
                            pltpu.SemaphoreType.DMA, pltpu.SemaphoreType.DMA((3,)),
                            pltpu.SemaphoreType.DMA((2,))],
        ),
        out_shape=jax.ShapeDtypeStruct((n_rows, D_MODEL), F32),
        compiler_params=_cparams("arbitrary"),
        name="dispatch",
    )(fill_tile, pos, h2)


def _expert_kernel(tile_expert_ref, tile_src_ref, tile_rows_ref, hs_ref, wgu_ref, bgu_ref, wd_ref, bd_ref, ys_ref,
                   wgu_bf_ref, wd_bf_ref):
    i = pl.program_id(0)
    n_rows = tile_rows_ref[i]
    new_expert = (i == 0) | (tile_expert_ref[i] != tile_expert_ref[jnp.maximum(i - 1, 0)])

    @pl.when((n_rows > 0) & new_expert)
    def _():
        wgu_bf_ref[...] = wgu_ref[...].astype(BF16)
        wd_bf_ref[...] = wd_ref[...].astype(BF16)

    half = MOE_TILE // 2
    for first in (0, half):
        rows = slice(first, first + half)

        @pl.when(n_rows > first)
        def _():
            gu = jnp.dot(hs_ref[rows, :].astype(BF16), wgu_bf_ref[...], preferred_element_type=F32) + bgu_ref[0]
            gate = jnp.minimum(gu[:, :D_FF], SWIGLU_LIMIT)
            up = jnp.clip(gu[:, D_FF:], -SWIGLU_LIMIT, SWIGLU_LIMIT)
            act = (up + 1.0) * gate * _sigmoid(SWIGLU_ALPHA * gate)
            ys_ref[rows, :] = jnp.dot(act.astype(BF16), wd_bf_ref[...], preferred_element_type=F32) + bd_ref[0]

        @pl.when(n_rows <= first)
        def _():
            ys_ref[rows, :] = jnp.zeros((half, D_MODEL), F32)


def _experts(hs, tile_expert, tile_src, tile_rows, layer, wgu, bgu, wd, bd):
    n_rows = hs.shape[0]
    rows = lambda i, te, ts, nu: (ts[i], 0)
    bias = lambda i, te, ts, nu: (te[i], 0, 0)
    weight = lambda i, te, ts, nu: (layer, te[i], 0, 0)
    return pl.pallas_call(
        _expert_kernel,
        grid_spec=pltpu.PrefetchScalarGridSpec(
            num_scalar_prefetch=3,
            grid=(n_rows // MOE_TILE,),
            in_specs=[
                pl.BlockSpec((MOE_TILE, D_MODEL), rows),
                pl.BlockSpec((None, None, D_MODEL, 2 * D_FF), weight),
                pl.BlockSpec((1, 1, 2 * D_FF), bias),
                pl.BlockSpec((None, None, D_FF, D_MODEL), weight),
                pl.BlockSpec((1, 1, D_MODEL), bias),
            ],
            out_specs=pl.BlockSpec((MOE_TILE, D_MODEL), lambda i, te, ts, nu: (i, 0)),
            scratch_shapes=[pltpu.VMEM((D_MODEL, 2 * D_FF), BF16), pltpu.VMEM((D_FF, D_MODEL), BF16)],
        ),
        out_shape=jax.ShapeDtypeStruct((n_rows, D_MODEL), F32),
        compiler_params=_cparams("arbitrary"),
        name="experts",
    )(tile_expert, tile_src, tile_rows, hs, wgu, bgu, wd, bd)


def _combine_kernel(pos_ref, ys_ref, w_ref, x1_ref, mod_ref, *refs, n_ctx_blocks):
    oc_ref, ol_ref, buf_ref, sems = refs
    i = pl.program_id(0)
    slots = COMBINE_ROWS * TOP_K

    def gather(step, buf):
        def issue(r, carry):
            token = step * COMBINE_ROWS + r
            for k in range(TOP_K):
                _row_copy(ys_ref, pos_ref[token * TOP_K + k], buf_ref.at[buf, k], r, sems.at[buf]).start()
            return carry

        lax.fori_loop(0, COMBINE_ROWS, issue, 0, unroll=ISSUE_UNROLL)

    @pl.when(i == 0)
    def _():
        gather(0, 0)

    @pl.when(i + 1 < pl.num_programs(0))
    def _():
        gather(i + 1, (i + 1) % 2)

    cur = i % 2

    def drain(j, carry):
        for _ in range(WAIT_UNROLL):
            _row_copy(ys_ref, 0, buf_ref.at[cur, 0], 0, sems.at[cur]).wait()
        return carry

    lax.fori_loop(0, slots // WAIT_UNROLL, drain, 0)
    w = w_ref[...]
    acc = w[:, 0:1] * buf_ref[cur, 0]
    for k in range(1, TOP_K):
        acc = acc + w[:, k:k + 1] * buf_ref[cur, k]
    result = x1_ref[...] + mod_ref[0][5:6] * acc

    @pl.when(i < n_ctx_blocks)
    def _():
        oc_ref[...] = result

    @pl.when(i >= n_ctx_blocks)
    def _():
        ol_ref[...] = result


def _combine(ys, pos, topw, x1, mod, n_ctx_rows, lat_len):
    t = x1.shape[0]
    rb = COMBINE_ROWS
    n_ctx_blocks = n_ctx_rows // rb

    def mod_row(i, pos_ref):
        return (jnp.where(i < n_ctx_blocks, 0, 1 + (i - n_ctx_blocks) // (lat_len // rb)), 0, 0)

    out_specs = [pl.BlockSpec((rb, D_MODEL), lambda i, pos_ref: (jnp.minimum(i, n_ctx_blocks - 1), 0)),
                 pl.BlockSpec((rb, D_MODEL), lambda i, pos_ref: (jnp.maximum(i - n_ctx_blocks, 0), 0))]
    out_shape = [jax.ShapeDtypeStruct((n_ctx_rows, D_MODEL), F32),
                 jax.ShapeDtypeStruct((t - n_ctx_rows, D_MODEL), F32)]
    return pl.pallas_call(
        functools.partial(_combine_kernel, n_ctx_blocks=n_ctx_blocks),
        grid_spec=pltpu.PrefetchScalarGridSpec(
            num_scalar_prefetch=1,
            grid=(t // rb,),
            in_specs=[
                pl.BlockSpec(memory_space=pl.ANY),
                pl.BlockSpec((rb, EXPERT_LANES), lambda i, pos_ref: (i, 0)),
                pl.BlockSpec((rb, D_MODEL), lambda i, pos_ref: (i, 0)),
                pl.BlockSpec((1, 8, D_MODEL), mod_row),
            ],
            out_specs=out_specs,
            scratch_shapes=[pltpu.VMEM((2, TOP_K, rb, D_MODEL), F32), pltpu.SemaphoreType.DMA((2,))],
        ),
        out_shape=out_shape,
        compiler_params=_cparams("arbitrary"),
        name="combine",
    )(pos, ys, topw, x1, mod)


def _rope_tables(n_lat_tokens, rot_dim):
    n_rows = n_lat_tokens // GRID_W
    row = jnp.repeat(jnp.arange(n_rows, dtype=F32), GRID_W)
    col = jnp.tile(jnp.arange(GRID_W, dtype=F32), n_rows)
    n_freq = rot_dim // 4
    inv_freq = ROPE_THETA ** (-jnp.arange(n_freq, dtype=F32) / n_freq)
    ang = jnp.concatenate([row[:, None] * inv_freq, col[:, None] * inv_freq], axis=-1)
    return jnp.cos(ang), jnp.sin(ang)


def _token_tables(cos_lane, sin_lane, n_ctx_rows, n_lat_seqs):
    width = cos_lane.shape[1]
    cos = jnp.concatenate([jnp.ones((n_ctx_rows, width), F32)] + [cos_lane] * n_lat_seqs, axis=0)
    sin = jnp.concatenate([jnp.zeros((n_ctx_rows, width), F32)] + [sin_lane] * n_lat_seqs, axis=0)
    return cos, sin


def _pack_heads(w, n_heads, lo, hi, width):
    k = w.shape[0]
    per_head = w.shape[1] // n_heads
    part = w.reshape(k, n_heads, per_head)[:, :, lo:hi]
    part = jnp.pad(part, ((0, 0), (0, 0), (0, width - (hi - lo))))
    return part.reshape(k, n_heads * width)


def kernel(x_prompt, x_sample, c, cache_diff_k, cache_diff_v, cache_mla_ckv, cache_mla_krope, c_ctx, w_ada, b_ada,
           w_in, pool_w, pool_scale, diff_q_norm, diff_k_norm, diff_lambda, diff_out_norm, mla_q_a_norm, w_uq,
           mla_kv_a_norm, w_ukv, mla_q_norm, mla_k_norm, w_branch, w_out, router_w, router_b, moe_w_gu, moe_b_gu,
           moe_w_down, moe_b_down):
    batch, seq, d = x_prompt.shape
    dec_batch, dec_seq, _ = x_sample.shape
    depth = w_ada.shape[0]
    past = cache_diff_k.shape[2]
    n_ctx = batch * seq
    n_lat = dec_batch * dec_seq
    n_ctx_blocks = n_ctx // ROW_BLOCK
    mod_row = _mod_row(n_ctx_blocks, dec_seq // ROW_BLOCK)
    mod_spec = _mod_spec(mod_row)

    x = (x_prompt.reshape(n_ctx, d), x_sample.reshape(n_lat, d))
    cond8 = jnp.concatenate([c_ctx[None], c, jnp.zeros((8 - 1 - dec_batch, d), F32)], axis=0)
    mod_all = _ada(cond8, w_ada, b_ada).reshape(depth, 8, 6, d)[:, :1 + dec_batch]
    mod_all = jnp.pad(mod_all, ((0, 0), (0, 0), (0, 2), (0, 0)))

    cos_d, sin_d = _rope_tables(dec_seq, DIFF_HD)
    cos_m, sin_m = _rope_tables(dec_seq, MLA_ROPE)
    cos_a, sin_a = _token_tables(jnp.concatenate([cos_d] * 4, axis=1),
                                 jnp.concatenate([-sin_d, sin_d, -sin_d, sin_d], axis=1), n_ctx, dec_batch)
    pad = MLA_W - MLA_QK
    cos_b, sin_b = _token_tables(jnp.concatenate([cos_m, cos_m, jnp.ones((dec_seq, pad), F32)], axis=1),
                                 jnp.concatenate([-sin_m, sin_m, jnp.zeros((dec_seq, pad), F32)], axis=1),
                                 n_ctx, dec_batch)

    offs = np.cumsum((0, 1024, 1024, 1024, 1024, MLA_Q_RANK, MLA_KV_RANK, MLA_ROPE, 3 * 1024))
    new_dk, new_dv, new_ckv, new_kr = [], [], [], []
    for l in range(depth):
        lam_init = 0.8 - 0.6 * math.exp(-0.3 * l)
        wl = w_in[l]
        seg = [wl[:, offs[k]:offs[k + 1]] for k in range(8)]
        w_a = jnp.concatenate(seg[0:4] + [seg[7]], axis=1).astype(BF16)
        kr_wide = jnp.pad(seg[6], ((0, 0), (MLA_NOPE, MLA_W - MLA_NOPE - MLA_ROPE)))
        w_b = jnp.concatenate([seg[4], seg[5], kr_wide], axis=1).astype(BF16)
        wuq = _pack_heads(w_uq[l], MLA_HEADS, 0, MLA_QK, MLA_W).astype(BF16)
        wkn = _pack_heads(w_ukv[l], MLA_HEADS, 0, MLA_NOPE, MLA_W).astype(BF16)
        wv = _pack_heads(w_ukv[l], MLA_HEADS, MLA_NOPE, MLA_NOPE + HEAD_W, HEAD_W).astype(BF16)
        gq_d = jnp.tile(diff_q_norm[l], 2 * DIFF_HEADS)[None]
        gk_d = jnp.tile(diff_k_norm[l], 2 * DIFF_HEADS)[None]
        gq_m = jnp.tile(jnp.pad(mla_q_norm[l], (0, MLA_W - MLA_QK)), MLA_HEADS)[None]
        gk_m = jnp.tile(jnp.pad(mla_k_norm[l], (0, MLA_W - MLA_QK)), MLA_HEADS)[None]
        lp = diff_lambda[l]
        lam = (jnp.exp(jnp.sum(lp[0] * lp[1])) - jnp.exp(jnp.sum(lp[2] * lp[3])) + lam_init).reshape(1)
        out_gain = (diff_out_norm[l] * (1.0 - lam_init))[None]
        mod = mod_all[l]

        u_pool, dq, dk_f, dk_b, dv_f, dv_b, gates = _proj_a(x, mod, w_a, gq_d, gk_d, cos_a, sin_a, mod_spec,
                                                            n_ctx_blocks)
        mq, ckv_n, kr, mk, mv = _proj_b(x, mod, w_b, mla_q_a_norm[l][None], mla_kv_a_norm[l][None], wuq, wkn, wv,
                                        gq_m, gk_m, cos_b, sin_b, mod_spec, n_ctx_blocks)
        new_dk.append(dk_f[:n_ctx].reshape(batch, seq, DIFF_HEADS, HEAD_W))
        new_dv.append(dv_f[:n_ctx].reshape(batch, seq, DIFF_HEADS, HEAD_W))
        new_ckv.append(ckv_n[:n_ctx].reshape(batch, seq, MLA_KV_RANK))
        new_kr.append(kr[:n_ctx].reshape(batch, seq, MLA_ROPE))

        y_pool = _pool(u_pool, pool_w[l].astype(BF16), pool_scale[l][None], n_ctx_blocks, seq, dec_seq)

        c_krw = jnp.pad(cache_mla_krope[:, l].reshape(dec_batch * past, MLA_ROPE),
                        ((0, 0), (MLA_NOPE, MLA_W - MLA_NOPE - MLA_ROPE)))
        c_mk, c_mv = _cache_kv(cache_mla_ckv[:, l].reshape(dec_batch * past, MLA_KV_RANK), c_krw, wkn, wv, gk_m)

        geom = (n_ctx, seq, dec_seq)
        y_diff = _diff_attention(dq, dk_b, dv_b, cache_diff_k[:, l].reshape(dec_batch, past, d),
                                 cache_diff_v[:, l].reshape(dec_batch, past, d), lam, out_gain, geom)
        y_mla = _mla_attention(mq, mk, mv, c_mk.reshape(dec_batch, past, -1), c_mv.reshape(dec_batch, past, -1),
                               geom)

        rw = jnp.pad(router_w[l], ((0, 0), (0, EXPERT_LANES - N_EXPERTS)))
        rb = jnp.pad(router_b[l], (0, EXPERT_LANES - N_EXPERTS), constant_values=-jnp.inf)[None]
        x1, h2, topw, topi, rank, counts = _merge(x, mod, y_pool, y_diff, y_mla, gates, w_branch[l].astype(BF16),
                                                  w_out[l].astype(BF16), rw, rb, mod_row, n_ctx_blocks)
        pos, tile_expert, tile_src, tile_rows, fill_tile = _slot_layout(counts, topi, rank)
        hs = _dispatch(h2, pos, fill_tile)
        ys = _experts(hs, tile_expert, tile_src, tile_rows, l, moe_w_gu, moe_b_gu[l][:, None, :],
                      moe_w_down, moe_b_down[l][:, None, :])
        x = _combine(ys, pos, topw, x1, mod, n_ctx, dec_seq)

    y_ctx, y_lat = x
    return (y_ctx.reshape(batch, seq, d), y_lat.reshape(dec_batch, dec_seq, d),
            jnp.stack(new_dk, axis=1), jnp.stack(new_dv, axis=1), jnp.stack(new_ckv, axis=1),
            jnp.stack(new_kr, axis=1))
```

```python
import functools
import math

import jax
import jax.numpy as jnp
import numpy as np
from jax import lax
from jax.experimental import pallas as pl
from jax.experimental.pallas import tpu as pltpu

F32 = jnp.float32
BF16 = jnp.bfloat16

D_MODEL = 1024
N_POOL_GROUPS = 4
POOL_GROUP = D_MODEL // N_POOL_GROUPS
DIFF_HD = 64
DIFF_HEADS = 8
HEAD_W = 128
MLA_HEADS = 8
MLA_NOPE = 128
MLA_ROPE = 64
MLA_QK = MLA_NOPE + MLA_ROPE
MLA_W = 256
MLA_Q_RANK = 768
MLA_KV_RANK = 512
N_EXPERTS = 32
TOP_K = 4
D_FF = 1024
EXPERT_LANES = 128
SWIGLU_ALPHA = 1.702
SWIGLU_LIMIT = 7.0
ROPE_THETA = 10000.0
NORM_EPS = 1e-6
GRID_W = 64
LOG2E = math.log2(math.e)
VMEM_LIMIT = 56 * 1024 * 1024

ROW_BLOCK = 256
POOL_HALO = 16
MOE_TILE = 512
DISPATCH_ROWS = 256
COMBINE_ROWS = 128
WAIT_UNROLL = 16
ISSUE_UNROLL = 4
DIFF_Q_BLOCK = 1024
MLA_Q_BLOCK = 2048
ATTN_K_CHUNK = 512

_NT = (((1,), (1,)), ((), ()))
_TN = (((0,), (0,)), ((), ()))


def _cparams(*sem):
    return pltpu.CompilerParams(dimension_semantics=sem, vmem_limit_bytes=VMEM_LIMIT)


def _resident(shape):
    nd = len(shape)
    return pl.BlockSpec(shape, lambda *_: (0,) * nd, pipeline_mode=pl.Buffered(1))


def _sigmoid(x):
    return 1.0 / (1.0 + jnp.exp(-x))


def _lane_iota(shape):
    return lax.broadcasted_iota(jnp.int32, shape, len(shape) - 1)


def _rms(x, width):
    ss = jnp.sum(x * x, axis=-1, keepdims=True)
    return x * lax.rsqrt(ss * (1.0 / width) + NORM_EPS)


def _rope(x, cos, sin):
    n = x.shape[-1]
    lane = _lane_iota(x.shape)
    partner = jnp.where((lane & 32) != 0, pltpu.roll(x, 32, 1), pltpu.roll(x, n - 32, 1))
    return x * cos + partner * sin


def _modulated_norm(x, shift, scale):
    return _rms(x, D_MODEL) * (1.0 + scale) + shift


def _ada_kernel(cond_ref, w_ref, b_ref, o_ref):
    cnd = cond_ref[...]
    act = cnd * _sigmoid(cnd)
    o_ref[0] = jnp.dot(act, w_ref[0], preferred_element_type=F32) + b_ref[0]


def _ada(cond8, w_ada, b_ada):
    depth = w_ada.shape[0]
    n_chunk = w_ada.shape[2] // D_MODEL
    return pl.pallas_call(
        _ada_kernel,
        grid=(depth, n_chunk),
        in_specs=[
            pl.BlockSpec((8, D_MODEL), lambda l, j: (0, 0)),
            pl.BlockSpec((1, D_MODEL, D_MODEL), lambda l, j: (l, 0, j)),
            pl.BlockSpec((1, 1, D_MODEL), lambda l, j: (l, 0, j)),
        ],
        out_specs=pl.BlockSpec((1, 8, D_MODEL), lambda l, j: (l, 0, j)),
        out_shape=jax.ShapeDtypeStruct((depth, 8, n_chunk * D_MODEL), F32),
        compiler_params=_cparams("arbitrary", "arbitrary"),
        name="ada",
    )(cond8, w_ada, b_ada.reshape(depth, 1, -1))


def _diff_qk_norm(u, gain, cos, sin):
    outs = []
    for h in range(DIFF_HEADS):
        c = u[:, h * HEAD_W:(h + 1) * HEAD_W]
        lane = _lane_iota(c.shape)
        sq = c * c
        lo = jnp.sum(jnp.where(lane < DIFF_HD, sq, 0.0), axis=-1, keepdims=True)
        hi = jnp.sum(jnp.where(lane >= DIFF_HD, sq, 0.0), axis=-1, keepdims=True)
        ms = jnp.where(lane < DIFF_HD, lo, hi) * (1.0 / DIFF_HD)
        y = c * lax.rsqrt(ms + NORM_EPS) * gain[:, h * HEAD_W:(h + 1) * HEAD_W]
        outs.append((y, _rope(y, cos, sin)))
    return outs


def _token_block(ctx_ref, lat_ref, n_ctx_blocks):
    return jnp.where(pl.program_id(0) < n_ctx_blocks, ctx_ref[...], lat_ref[...])


def _rope_block(cos_ref, sin_ref, n_ctx_blocks):
    is_ctx = pl.program_id(0) < n_ctx_blocks
    return jnp.where(is_ctx, 1.0, cos_ref[...]), jnp.where(is_ctx, 0.0, sin_ref[...])


def _proj_a_kernel(xc_ref, xl_ref, mod_ref, w_ref, gq_ref, gk_ref, cos_ref, sin_ref,
                   pool_ref, dq_ref, dkf_ref, dkb_ref, dvf_ref, dvb_ref, gate_ref, *, n_ctx_blocks):
    mod = mod_ref[0]
    h = _modulated_norm(_token_block(xc_ref, xl_ref, n_ctx_blocks), mod[0:1], mod[1:2]).astype(BF16)
    cos, sin = _rope_block(cos_ref, sin_ref, n_ctx_blocks)

    def seg(k):
        return jnp.dot(h, w_ref[:, k * D_MODEL:(k + 1) * D_MODEL], preferred_element_type=F32)

    u_pool, u_dq = seg(0), seg(1)
    pool_ref[...] = u_pool.astype(BF16)
    u_dk = seg(2)
    q_scale = DIFF_HD ** -0.5 * LOG2E
    for hd, (_, roped) in enumerate(_diff_qk_norm(u_dq, gq_ref[...], cos, sin)):
        dq_ref[:, hd * HEAD_W:(hd + 1) * HEAD_W] = (roped * q_scale).astype(BF16)
    dv = seg(3)
    for hd, (plain, roped) in enumerate(_diff_qk_norm(u_dk, gk_ref[...], cos, sin)):
        dkf_ref[:, hd * HEAD_W:(hd + 1) * HEAD_W] = plain
        dkb_ref[:, hd * HEAD_W:(hd + 1) * HEAD_W] = roped.astype(BF16)
    u_gate = seg(4)
    dvf_ref[...] = dv
    dvb_ref[...] = dv.astype(BF16)
    for r in range(3):
        u_next = seg(5 + r) if r < 2 else None
        gate_ref[:, r * D_MODEL:(r + 1) * D_MODEL] = _sigmoid(u_gate).astype(BF16)
        u_gate = u_next


def _mla_heads(u, gain, cos, sin, rope):
    outs = []
    for h in range(MLA_HEADS):
        c = u[:, h * MLA_W:(h + 1) * MLA_W]
        y = _rms(c, MLA_QK) * gain[:, h * MLA_W:(h + 1) * MLA_W]
        if rope:
            y = jnp.concatenate([y[:, :MLA_NOPE], _rope(y[:, MLA_NOPE:], cos, sin)], axis=-1)
        outs.append(y)
    return outs


def _mla_kv(ckv_n, krw, wkn_ref, wv_ref, gk, cos, sin, rope, k_ref, v_ref):
    kn = jnp.dot(ckv_n, wkn_ref[...], preferred_element_type=F32)
    v_ref[...] = jnp.dot(ckv_n, wv_ref[...], preferred_element_type=F32).astype(BF16)
    kfull = jnp.concatenate([kn[:, h * MLA_W:(h + 1) * MLA_W] + krw for h in range(MLA_HEADS)], axis=-1)
    for hd, y in enumerate(_mla_heads(kfull, gk, cos, sin, rope)):
        k_ref[:, hd * MLA_W:(hd + 1) * MLA_W] = y.astype(BF16)


def _proj_b_kernel(xc_ref, xl_ref, mod_ref, w_ref, gqa_ref, gkva_ref, wuq_ref, wkn_ref, wv_ref, gq_ref, gk_ref,
                   cos_ref, sin_ref, mq_ref, ckv_ref, kr_ref, mk_ref, mv_ref, *, n_ctx_blocks):
    mod = mod_ref[0]
    h = _modulated_norm(_token_block(xc_ref, xl_ref, n_ctx_blocks), mod[0:1], mod[1:2]).astype(BF16)
    cos, sin = _rope_block(cos_ref, sin_ref, n_ctx_blocks)
    cq = jnp.dot(h, w_ref[:, :MLA_Q_RANK], preferred_element_type=F32)
    ckv = jnp.dot(h, w_ref[:, MLA_Q_RANK:MLA_Q_RANK + MLA_KV_RANK], preferred_element_type=F32)
    krw = jnp.dot(h, w_ref[:, MLA_Q_RANK + MLA_KV_RANK:], preferred_element_type=F32)
    cq_n = (_rms(cq, MLA_Q_RANK) * gqa_ref[...]).astype(BF16)
    mq = jnp.dot(cq_n, wuq_ref[...], preferred_element_type=F32)
    ckv_n = _rms(ckv, MLA_KV_RANK) * gkva_ref[...]
    ckv_ref[...] = ckv_n
    kr_ref[...] = krw[:, MLA_NOPE:MLA_NOPE + MLA_ROPE]
    _mla_kv(ckv_n.astype(BF16), krw, wkn_ref, wv_ref, gk_ref[...], cos, sin, True, mk_ref, mv_ref)
    q_scale = MLA_QK ** -0.5 * LOG2E
    for hd, y in enumerate(_mla_heads(mq, gq_ref[...], cos, sin, True)):
        mq_ref[:, hd * MLA_W:(hd + 1) * MLA_W] = (y * q_scale).astype(BF16)


def _cache_kv_kernel(ckv_ref, krw_ref, wkn_ref, wv_ref, gk_ref, mk_ref, mv_ref):
    _mla_kv(ckv_ref[...].astype(BF16), krw_ref[...], wkn_ref, wv_ref, gk_ref[...], None, None, False,
            mk_ref, mv_ref)


def _mod_row(n_ctx_blocks, blocks_per_lat):
    return lambda i: jnp.where(i < n_ctx_blocks, 0, 1 + (i - n_ctx_blocks) // blocks_per_lat)


def _mod_spec(mod_row):
    return pl.BlockSpec((1, 8, D_MODEL), lambda i: (mod_row(i), 0, 0))


def _rows(width):
    return pl.BlockSpec((ROW_BLOCK, width), lambda i: (i, 0))


def _ctx_rows(width, n_ctx_blocks):
    return pl.BlockSpec((ROW_BLOCK, width), lambda i: (jnp.minimum(i, n_ctx_blocks - 1), 0))


def _lat_rows(width, n_ctx_blocks):
    return pl.BlockSpec((ROW_BLOCK, width), lambda i: (jnp.maximum(i - n_ctx_blocks, 0), 0))


def _proj_outputs(t, n_ctx_blocks, widths, dtypes, ctx_only):
    spare = lambda w: pl.BlockSpec((ROW_BLOCK, w), lambda i: (jnp.minimum(i, n_ctx_blocks), 0))
    specs = [spare(w) if c else _rows(w) for w, c in zip(widths, ctx_only)]
    shapes = [jax.ShapeDtypeStruct(((n_ctx_blocks + 1) * ROW_BLOCK if c else t, w), dt)
              for w, dt, c in zip(widths, dtypes, ctx_only)]
    return specs, shapes


def _proj_a(x, mod, w_a, gq, gk, cos, sin, mod_spec, rope_spec, n_ctx_blocks):
    t = x[0].shape[0] + x[1].shape[0]
    widths = (D_MODEL,) * 6 + (3 * D_MODEL,)
    dtypes = (BF16, BF16, F32, BF16, F32, BF16, BF16)
    out_specs, out_shape = _proj_outputs(t, n_ctx_blocks, widths, dtypes, (0, 0, 1, 0, 1, 0, 0))
    return pl.pallas_call(
        functools.partial(_proj_a_kernel, n_ctx_blocks=n_ctx_blocks),
        grid=(t // ROW_BLOCK,),
        in_specs=[_ctx_rows(D_MODEL, n_ctx_blocks), _lat_rows(D_MODEL, n_ctx_blocks), mod_spec,
                  _resident(w_a.shape), _resident(gq.shape), _resident(gk.shape), rope_spec, rope_spec],
        out_specs=out_specs,
        out_shape=out_shape,
        compiler_params=_cparams("arbitrary"),
        name="proj_a",
    )(*x, mod, w_a, gq, gk, cos, sin)


def _proj_b(x, mod, w_b, gqa, gkva, wuq, wkn, wv, gq, gk, cos, sin, mod_spec, rope_spec, n_ctx_blocks):
    t = x[0].shape[0] + x[1].shape[0]
    widths = (MLA_HEADS * MLA_W, MLA_KV_RANK, MLA_ROPE, MLA_HEADS * MLA_W, D_MODEL)
    dtypes = (BF16, F32, F32, BF16, BF16)
    out_specs, out_shape = _proj_outputs(t, n_ctx_blocks, widths, dtypes, (0, 1, 1, 0, 0))
    consts = (w_b, gqa, gkva, wuq, wkn, wv, gq, gk)
    return pl.pallas_call(
        functools.partial(_proj_b_kernel, n_ctx_blocks=n_ctx_blocks),
        grid=(t // ROW_BLOCK,),
        in_specs=[_ctx_rows(D_MODEL, n_ctx_blocks), _lat_rows(D_MODEL, n_ctx_blocks), mod_spec]
        + [_resident(a.shape) for a in consts] + [rope_spec, rope_spec],
        out_specs=out_specs,
        out_shape=out_shape,
        compiler_params=_cparams("arbitrary"),
        name="proj_b",
    )(*x, mod, *consts, cos, sin)


def _cache_kv(ckv, krw, wkn, wv, gk):
    n = ckv.shape[0]
    widths = (MLA_HEADS * MLA_W, D_MODEL)
    consts = (wkn, wv, gk)
    return pl.pallas_call(
        _cache_kv_kernel,
        grid=(n // ROW_BLOCK,),
        in_specs=[_rows(MLA_KV_RANK), _rows(MLA_W)] + [_resident(a.shape) for a in consts],
        out_specs=[_rows(w) for w in widths],
        out_shape=[jax.ShapeDtypeStruct((n, w), BF16) for w in widths],
        compiler_params=_cparams("arbitrary"),
        name="cache_kv",
    )(ckv, krw, *consts)


def _pool_kernel(prev_ref, cur_ref, next_ref, w_ref, scale_ref, o_ref, *, n_ctx_blocks, ctx_len, lat_len):
    i = pl.program_id(0)
    rb = ROW_BLOCK
    is_ctx = i < n_ctx_blocks
    n_ctx_rows = n_ctx_blocks * rb
    seq_len = jnp.where(is_ctx, ctx_len, lat_len)
    seq_start = jnp.where(is_ctx, (i * rb // ctx_len) * ctx_len,
                          n_ctx_rows + ((i * rb - n_ctx_rows) // lat_len) * lat_len)
    n_cols = rb + 2 * POOL_HALO
    row_pos = i * rb - seq_start + lax.broadcasted_iota(jnp.int32, (rb, n_cols), 0)
    col_pos = i * rb - POOL_HALO - seq_start + lax.broadcasted_iota(jnp.int32, (rb, n_cols), 1)
    cur = cur_ref[...]
    u = jnp.concatenate([prev_ref[...], cur, next_ref[...]], axis=0)
    for g in range(N_POOL_GROUPS):
        cols = slice(g * POOL_GROUP, (g + 1) * POOL_GROUP)
        window = 2 << g
        lo = jnp.clip(row_pos - window // 2, 0, seq_len)
        hi = jnp.clip(row_pos - window // 2 + window, 0, seq_len)
        band = jnp.where((col_pos >= lo) & (col_pos < hi), 1.0, 0.0).astype(BF16)
        total = jnp.dot(band, u[:, cols], preferred_element_type=F32)
        cnt = (hi[:, :1] - lo[:, :1]).astype(F32)
        pooled = total / cnt - cur[:, cols].astype(F32)
        mixed = jnp.dot(pooled.astype(BF16), w_ref[g], preferred_element_type=F32)
        o_ref[:, cols] = (mixed * scale_ref[:, cols]).astype(BF16)


def _pool(u_pool, pool_w, pool_scale, n_ctx_blocks, ctx_len, lat_len):
    t = u_pool.shape[0]
    halo_per_block = ROW_BLOCK // POOL_HALO
    n_halo_blocks = t // POOL_HALO
    assert POOL_HALO >= (2 << (N_POOL_GROUPS - 1)) // 2
    kern = functools.partial(_pool_kernel, n_ctx_blocks=n_ctx_blocks, ctx_len=ctx_len, lat_len=lat_len)
    return pl.pallas_call(
        kern,
        grid=(t // ROW_BLOCK,),
        in_specs=[
            pl.BlockSpec((POOL_HALO, D_MODEL), lambda i: (jnp.maximum(i * halo_per_block - 1, 0), 0)),
            _rows(D_MODEL),
            pl.BlockSpec((POOL_HALO, D_MODEL),
                         lambda i: (jnp.minimum((i + 1) * halo_per_block, n_halo_blocks - 1), 0)),
            _resident(pool_w.shape),
            _resident(pool_scale.shape),
        ],
        out_specs=_rows(D_MODEL),
        out_shape=jax.ShapeDtypeStruct((t, D_MODEL), BF16),
        compiler_params=_cparams("arbitrary"),
        name="pool",
    )(u_pool, u_pool, u_pool, pool_w, pool_scale)


def _softmax_stats(s, m, l):
    m_new = jnp.maximum(m, jnp.max(s, axis=0, keepdims=True))
    p = jnp.exp2(s - m_new)
    alpha = jnp.exp2(m - m_new)
    return m_new, alpha * l + jnp.sum(p, axis=0, keepdims=True), alpha, p.astype(BF16)


def _accumulate(acc, alpha, p, v):
    return alpha * acc + lax.dot_general(v, p, _TN, preferred_element_type=F32)


def _online_softmax(chunks, scores, n_softmax, tq):
    chunks = list(chunks)
    m = [jnp.full((1, tq), -jnp.inf, F32)] * n_softmax
    l = [jnp.zeros((1, tq), F32)] * n_softmax
    acc = [jnp.zeros((HEAD_W, tq), F32)] * n_softmax
    pending = None
    s_next = scores(chunks[0][0])
    for c, (_, v) in enumerate(chunks):
        s = s_next
        if c + 1 < len(chunks):
            s_next = scores(chunks[c + 1][0])
        stats = [_softmax_stats(s[j], m[j], l[j]) for j in range(n_softmax)]
        m = [st[0] for st in stats]
        l = [st[1] for st in stats]
        if pending is not None:
            acc = [_accumulate(acc[j], *pending[j]) for j in range(n_softmax)]
        pending = [(st[2], st[3], v) for st in stats]
    acc = [_accumulate(acc[j], *pending[j]) for j in range(n_softmax)]
    return list(zip(l, acc))


def _key_value_chunks(kv_refs, head, qk_width):
    k_cols = slice(head * qk_width, (head + 1) * qk_width)
    v_cols = slice(head * HEAD_W, (head + 1) * HEAD_W)
    for k_ref, v_ref in zip(kv_refs[0::2], kv_refs[1::2]):
        n = k_ref.shape[0]
        size = min(n, ATTN_K_CHUNK)
        for start in range(0, n, size):
            yield k_ref[start:start + size, k_cols].astype(BF16), v_ref[start:start + size, v_cols].astype(BF16)


def _diff_attn_kernel(lam_ref, q_ref, *refs, n_kv, heads):
    g_ref, o_ref = refs[n_kv], refs[-1]
    for h in range(heads):
        cols = slice(h * HEAD_W, (h + 1) * HEAD_W)
        q = q_ref[:, cols]
        lane = _lane_iota(q.shape)
        q1 = jnp.where(lane < DIFF_HD, q, jnp.zeros_like(q))
        q2 = jnp.where(lane >= DIFF_HD, q, jnp.zeros_like(q))

        def scores(k):
            return (lax.dot_general(k, q1, _NT, preferred_element_type=F32),
                    lax.dot_general(k, q2, _NT, preferred_element_type=F32))

        (l1, a1), (l2, a2) = _online_softmax(_key_value_chunks(refs[:n_kv], h, HEAD_W), scores, 2, q.shape[0])
        o = (a1 / l1 - lam_ref[0] * (a2 / l2)).T
        o_ref[:, cols] = (_rms(o, HEAD_W) * g_ref[...]).astype(o_ref.dtype)


def _mla_attn_kernel(q_ref, *refs, n_kv, heads):
    o_ref = refs[-1]
    for h in range(heads):
        q = q_ref[:, h * MLA_W:(h + 1) * MLA_W]
        ((l, acc),) = _online_softmax(
            _key_value_chunks(refs[:n_kv], h, MLA_W),
            lambda k: (lax.dot_general(k, q, _NT, preferred_element_type=F32),), 1, q.shape[0])
        o_ref[:, h * HEAD_W:(h + 1) * HEAD_W] = (acc / l).T.astype(o_ref.dtype)


def _attention(kernel, q, k, v, cache_k, cache_v, lead, lead_specs, tail, tail_specs, qk_width, tq, geom, name):
    n_ctx, seq, dec_seq = geom
    t = q.shape[0]
    n_heads = v.shape[1] // HEAD_W
    dec_batch, past, _ = cache_k.shape
    nq, q0, k0 =dec_seq // tq, n_ctx // tq, n_ctx // dec_seq
    assert n_ctx % tq == 0 and n_ctx % dec_seq == 0 and dec_seq % tq == 0
    params = _cparams("arbitrary", "arbitrary", "arbitrary")
    seq_block = lambda w: pl.BlockSpec((seq, n_heads * w), lambda b, h, qi: (b, 0))
    ctx = pl.pallas_call(
        functools.partial(kernel, n_kv=2, heads=n_heads),
        grid=(n_ctx // seq, 1, 1),
        in_specs=lead_specs + [seq_block(qk_width), seq_block(qk_width), seq_block(HEAD_W)] + tail_specs,
        out_specs=seq_block(HEAD_W),
        out_shape=jax.ShapeDtypeStruct((n_ctx, n_heads * HEAD_W), BF16),
        compiler_params=params,
        name=name + "_ctx",
    )(*lead, q, k, v, *tail)
    lat_rows = lambda w, first: pl.BlockSpec((tq, w), lambda b, h, qi: (first + b * nq + qi, h))
    cached = lambda w: pl.BlockSpec((None, past, w), lambda b, h, qi: (b, 0, h))
    lat_keys = lambda w: pl.BlockSpec((dec_seq, w), lambda b, h, qi: (k0 + b, h))
    lat = pl.pallas_call(
        functools.partial(kernel, n_kv=4, heads=1),
        grid=(dec_batch, n_heads, nq),
        in_specs=lead_specs + [lat_rows(qk_width, q0), cached(qk_width), cached(HEAD_W), lat_keys(qk_width),
                               lat_keys(HEAD_W)] + tail_specs,
        out_specs=lat_rows(HEAD_W, 0),
        out_shape=jax.ShapeDtypeStruct((t - n_ctx, n_heads * HEAD_W), BF16),
        compiler_params=params,
        name=name + "_lat",
    )(*lead, q, cache_k, cache_v, k, v, *tail)
    return ctx, lat


def _diff_attention(q, k, v, cache_k, cache_v, lam, out_gain, geom):
    lead_specs = [pl.BlockSpec(memory_space=pltpu.SMEM)]
    tail_specs = [pl.BlockSpec((1, HEAD_W), lambda b, h, qi: (0, 0))]
    return _attention(_diff_attn_kernel, q, k, v, cache_k, cache_v, (lam,), lead_specs, (out_gain,), tail_specs,
                      HEAD_W, DIFF_Q_BLOCK, geom, "diff")


def _mla_attention(q, k, v, cache_k, cache_v, geom):
    return _attention(_mla_attn_kernel, q, k, v, cache_k, cache_v, (), [], (), [], MLA_W, MLA_Q_BLOCK, geom, "mla")


def _merge_kernel(xc_ref, xl_ref, mod_ref, yp_ref, ydc_ref, ydl_ref, ymc_ref, yml_ref, gate_ref, wb_ref, wo_ref,
                  rw_ref, rb_ref, x1_ref, h2_ref, topw_ref, topi_ref, rank_ref, count_ref, carry_ref, logits_ref,
                  *, n_ctx_blocks):
    step = pl.program_id(0)

    @pl.when(step == 0)
    def _():
        carry_ref[...] = jnp.zeros_like(carry_ref)
        logits_ref[...] = jnp.zeros_like(logits_ref)

    is_ctx = step < n_ctx_blocks
    mod = mod_ref[0]
    merged = jnp.zeros(xc_ref.shape, F32)
    branches = (yp_ref[...], jnp.where(is_ctx, ydc_ref[...], ydl_ref[...]),
                jnp.where(is_ctx, ymc_ref[...], yml_ref[...]))
    for r, y in enumerate(branches):
        z = jnp.dot(y, wb_ref[r], preferred_element_type=F32)
        merged = merged + gate_ref[:, r * D_MODEL:(r + 1) * D_MODEL].astype(F32) * z
    y = jnp.dot(merged.astype(BF16), wo_ref[...], preferred_element_type=F32)
    x1 = jnp.where(is_ctx, xc_ref[...], xl_ref[...]) + mod[2:3] * y
    x1_ref[...] = x1
    h2 = _modulated_norm(x1, mod[3:4], mod[4:5])
    h2_ref[...] = h2
    new_logits = jnp.dot(h2, rw_ref[...], preferred_element_type=F32) + rb_ref[...]
    logits = logits_ref[...]
    logits_ref[...] = new_logits
    lane = _lane_iota(logits.shape)
    picks = []
    for _ in range(TOP_K):
        mx = jnp.max(logits, axis=-1, keepdims=True)
        first = jnp.min(jnp.where(logits == mx, lane, EXPERT_LANES), axis=-1, keepdims=True)
        hit = lane == first
        picks.append((mx, first, hit))
        logits = jnp.where(hit, -jnp.inf, logits)
    exps = [jnp.exp(v - picks[0][0]) for v, _, _ in picks]
    denom = exps[0] + exps[1] + exps[2] + exps[3]
    rows = logits.shape[0]
    chosen = jnp.zeros(logits.shape, F32)
    for _, _, hit in picks:
        chosen = chosen + jnp.where(hit, 1.0, 0.0)
    earlier = jnp.where(lax.broadcasted_iota(jnp.int32, (rows, rows), 1)
                        < lax.broadcasted_iota(jnp.int32, (rows, rows), 0), 1.0, 0.0).astype(BF16)
    before = jnp.dot(earlier, chosen.astype(BF16), preferred_element_type=F32) + carry_ref[0:1, :]
    topw = jnp.zeros(logits.shape, F32)
    topi = jnp.zeros(logits.shape, jnp.int32)
    rank = jnp.zeros(logits.shape, jnp.int32)
    for k, (e, (_, first, hit)) in enumerate(zip(exps, picks)):
        slot_rank = jnp.sum(jnp.where(hit, before, 0.0), axis=-1, keepdims=True)
        topw = jnp.where(lane == k, e / denom, topw)
        topi = jnp.where(lane == k, first, topi)
        rank = jnp.where(lane == k, slot_rank.astype(jnp.int32), rank)
    topw_ref[...] = topw
    topi_ref[...] = topi
    rank_ref[...] = rank
    routed = jnp.where(step > 0, jnp.sum(chosen, axis=0, keepdims=True), 0.0)
    carry_ref[...] = carry_ref[...] + routed
    count_ref[...] = carry_ref[...]


def _merge(x, mod, y_pool, y_diff, y_mla, gates, wb, wo, rw, rb, mod_row, n_ctx_blocks):
    t = x[0].shape[0] + x[1].shape[0]
    nb = t // ROW_BLOCK
    consts = (wb, wo, rw, rb)
    lanes = EXPERT_LANES
    cur = lambda i: jnp.minimum(i, nb - 1)
    rows = lambda w: pl.BlockSpec((ROW_BLOCK, w), lambda i: (cur(i), 0))
    routed = pl.BlockSpec((ROW_BLOCK, lanes), lambda i: (jnp.maximum(i - 1, 0), 0))
    ctx_rows = _ctx_rows(D_MODEL, n_ctx_blocks)
    lat_rows = pl.BlockSpec((ROW_BLOCK, D_MODEL), lambda i: (jnp.maximum(cur(i) - n_ctx_blocks, 0), 0))
    return pl.pallas_call(
        functools.partial(_merge_kernel, n_ctx_blocks=n_ctx_blocks),
        grid=(nb + 1,),
        in_specs=[ctx_rows, lat_rows, _mod_spec(lambda i: mod_row(cur(i))), rows(D_MODEL), ctx_rows, lat_rows,
                  ctx_rows, lat_rows, rows(3 * D_MODEL)] + [_resident(a.shape) for a in consts],
        out_specs=[rows(D_MODEL), rows(D_MODEL), routed, routed, routed,
                   pl.BlockSpec((8, lanes), lambda i: (0, 0))],
        out_shape=[jax.ShapeDtypeStruct((t, D_MODEL), F32), jax.ShapeDtypeStruct((t, D_MODEL), F32),
                   jax.ShapeDtypeStruct((t, lanes), F32), jax.ShapeDtypeStruct((t, lanes), jnp.int32),
                   jax.ShapeDtypeStruct((t, lanes), jnp.int32), jax.ShapeDtypeStruct((8, lanes), F32)],
        scratch_shapes=[pltpu.VMEM((8, lanes), F32), pltpu.VMEM((ROW_BLOCK, lanes), F32)],
        compiler_params=_cparams("arbitrary"),
        name="merge",
    )(*x, mod, y_pool, *y_diff, *y_mla, gates, *consts)


def _slot_layout(counts, topi, rank):
    tile = MOE_TILE
    cnt = counts[0, :N_EXPERTS].astype(jnp.int32)
    padded = (cnt + tile - 1) // tile * tile
    end = jnp.cumsum(padded)
    start = end - padded
    experts = jnp.arange(N_EXPERTS, dtype=jnp.int32)
    idx4 = topi[:, :TOP_K]
    slot_start = jnp.sum(jnp.where(idx4[:, :, None] == experts, start, 0), axis=-1)
    pos = (slot_start + rank[:, :TOP_K]).reshape(-1)
    n_tiles = (topi.shape[0] * TOP_K + N_EXPERTS * tile) // tile
    tile_row = jnp.arange(n_tiles, dtype=jnp.int32) * tile
    tile_expert = jnp.minimum(jnp.sum(tile_row[:, None] >= end, axis=-1), N_EXPERTS - 1).astype(jnp.int32)
    n_used = (end[-1] // tile).astype(jnp.int32)
    tile_src = jnp.minimum(jnp.arange(n_tiles, dtype=jnp.int32), n_used - 1)
    ragged_last = jnp.any((cnt % tile != 0) & (tile_row[:, None] == end - tile), axis=-1)
    fill_tile = (ragged_last | (tile_row >= end[-1])).astype(jnp.int32)
    return pos, tile_expert, tile_src, n_used.reshape(1), fill_tile


def _row_copy(src, src_row, dst, dst_row, sem):
    return pltpu.make_async_copy(src.at[pl.ds(src_row, 1)], dst.at[pl.ds(dst_row, 1)], sem)


def _dispatch_kernel(fill_tile_ref, pos_ref, h_ref, hs_ref, zero_ref, src_ref, fill_sem, fetch_sems, row_sems):
    i = pl.program_id(0)
    n_steps = pl.num_programs(0)
    slots = DISPATCH_ROWS * TOP_K

    def fetch(step, buf):
        rows = pl.ds(pl.multiple_of(step * DISPATCH_ROWS, DISPATCH_ROWS), DISPATCH_ROWS)
        return pltpu.make_async_copy(h_ref.at[rows], src_ref.at[buf], fetch_sems.at[buf])

    @pl.when(i == 0)
    def _():
        fetch(0, 0).start()
        fetch(1, 1).start()

    @pl.when(i == 0)
    def _():
        zero_ref[...] = jnp.zeros_like(zero_ref)

        def fill_copy(tile):
            row = pl.multiple_of(tile * MOE_TILE, MOE_TILE)
            return pltpu.make_async_copy(zero_ref, hs_ref.at[pl.ds(row, MOE_TILE)], fill_sem)

        def start_fill(tile, carry):
            @pl.when(fill_tile_ref[tile] != 0)
            def _():
                fill_copy(tile).start()
            return carry

        def wait_fill(tile, carry):
            @pl.when(fill_tile_ref[tile] != 0)
            def _():
                fill_copy(tile).wait()
            return carry

        lax.fori_loop(0, hs_ref.shape[0] // MOE_TILE, start_fill, 0)
        lax.fori_loop(0, hs_ref.shape[0] // MOE_TILE, wait_fill, 0)

    buf = i % 3
    src = src_ref.at[buf]
    fetch(i, buf).wait()

    def issue(r, carry):
        slot = (i * DISPATCH_ROWS + r) * TOP_K
        for k in range(TOP_K):
            _row_copy(src, r, hs_ref, pos_ref[slot + k], row_sems.at[i % 2]).start()
        return carry

    def drain(sem):
        def body(j, carry):
            for _ in range(WAIT_UNROLL):
                _row_copy(src, 0, hs_ref, 0, sem).wait()
            return carry

        lax.fori_loop(0, slots // WAIT_UNROLL, body, 0)

    lax.fori_loop(0, DISPATCH_ROWS, issue, 0, unroll=ISSUE_UNROLL)

    @pl.when(i > 0)
    def _():
        drain(row_sems.at[(i - 1) % 2])

    @pl.when(i + 2 < n_steps)
    def _():
        fetch(i + 2, (i + 2) % 3).start()

    @pl.when(i == n_steps - 1)
    def _():
        drain(row_sems.at[i % 2])


def _dispatch(h2, pos, fill_tile):
    t = h2.shape[0]
    n_rows = fill_tile.shape[0] * MOE_TILE
    return pl.pallas_call(
        _dispatch_kernel,
        grid_spec=pltpu.PrefetchScalarGridSpec(
            num_scalar_prefetch=2,
            grid=(t // DISPATCH_ROWS,),
            in_specs=[pl.BlockSpec(memory_space=pl.ANY)],
            out_specs=pl.BlockSpec(memory_space=pl.ANY),
            scratch_shapes=[pltpu.VMEM((MOE_TILE, D_MODEL), F32), pltpu.VMEM((3, DISPATCH_ROWS, D_MODEL), F32),
                            pltpu.SemaphoreType.DMA, pltpu.SemaphoreType.DMA((3,)),
                            pltpu.SemaphoreType.DMA((2,))],
        ),
        out_shape=jax.ShapeDtypeStruct((n_rows, D_MODEL), F32),
        compiler_params=_cparams("arbitrary"),
        name="dispatch",
    )(fill_tile, pos, h2)


def _expert_kernel(tile_expert_ref, tile_src_ref, n_used_ref, hs_ref, wgu_ref, bgu_ref, wd_ref, bd_ref, ys_ref,
                   wgu_bf_ref, wd_bf_ref):
    i = pl.program_id(0)
    occupied = i < n_used_ref[0]
    new_expert = (i == 0) | (tile_expert_ref[i] != tile_expert_ref[jnp.maximum(i - 1, 0)])

    @pl.when(occupied & new_expert)
    def _():
        wgu_bf_ref[...] = wgu_ref[...].astype(BF16)
        wd_bf_ref[...] = wd_ref[...].astype(BF16)

    @pl.when(occupied)
    def _():
        gu = jnp.dot(hs_ref[...].astype(BF16), wgu_bf_ref[...], preferred_element_type=F32) + bgu_ref[0]
        gate = jnp.minimum(gu[:, :D_FF], SWIGLU_LIMIT)
        up = jnp.clip(gu[:, D_FF:], -SWIGLU_LIMIT, SWIGLU_LIMIT)
        act = (up + 1.0) * gate * _sigmoid(SWIGLU_ALPHA * gate)
        ys_ref[...] = jnp.dot(act.astype(BF16), wd_bf_ref[...], preferred_element_type=F32) + bd_ref[0]

    @pl.when(jnp.logical_not(occupied))
    def _():
        ys_ref[...] = jnp.zeros_like(ys_ref)


def _experts(hs, tile_expert, tile_src, n_used, layer, wgu, bgu, wd, bd):
    n_rows = hs.shape[0]
    rows = lambda i, te, ts, nu: (ts[i], 0)
    bias = lambda i, te, ts, nu: (te[i], 0, 0)
    weight = lambda i, te, ts, nu: (layer, te[i], 0, 0)
    return pl.pallas_call(
        _expert_kernel,
        grid_spec=pltpu.PrefetchScalarGridSpec(
            num_scalar_prefetch=3,
            grid=(n_rows // MOE_TILE,),
            in_specs=[
                pl.BlockSpec((MOE_TILE, D_MODEL), rows),
                pl.BlockSpec((None, None, D_MODEL, 2 * D_FF), weight),
                pl.BlockSpec((1, 1, 2 * D_FF), bias),
                pl.BlockSpec((None, None, D_FF, D_MODEL), weight),
                pl.BlockSpec((1, 1, D_MODEL), bias),
            ],
            out_specs=pl.BlockSpec((MOE_TILE, D_MODEL), lambda i, te, ts, nu: (i, 0)),
            scratch_shapes=[pltpu.VMEM((D_MODEL, 2 * D_FF), BF16), pltpu.VMEM((D_FF, D_MODEL), BF16)],
        ),
        out_shape=jax.ShapeDtypeStruct((n_rows, D_MODEL), F32),
        compiler_params=_cparams("arbitrary"),
        name="experts",
    )(tile_expert, tile_src, n_used, hs, wgu, bgu, wd, bd)


def _combine_kernel(pos_ref, ys_ref, w_ref, x1_ref, mod_ref, *refs, n_ctx_blocks):
    oc_ref, ol_ref, buf_ref, sems = refs
    i = pl.program_id(0)
    slots = COMBINE_ROWS * TOP_K

    def gather(step, buf):
        def issue(r, carry):
            token = step * COMBINE_ROWS + r
            for k in range(TOP_K):
                _row_copy(ys_ref, pos_ref[token * TOP_K + k], buf_ref.at[buf, k], r, sems.at[buf]).start()
            return carry

        lax.fori_loop(0, COMBINE_ROWS, issue, 0, unroll=ISSUE_UNROLL)

    @pl.when(i == 0)
    def _():
        gather(0, 0)

    @pl.when(i + 1 < pl.num_programs(0))
    def _():
        gather(i + 1, (i + 1) % 2)

    cur = i % 2

    def drain(j, carry):
        for _ in range(WAIT_UNROLL):
            _row_copy(ys_ref, 0, buf_ref.at[cur, 0], 0, sems.at[cur]).wait()
        return carry

    lax.fori_loop(0, slots // WAIT_UNROLL, drain, 0)
    w = w_ref[...]
    acc = w[:, 0:1] * buf_ref[cur, 0]
    for k in range(1, TOP_K):
        acc = acc + w[:, k:k + 1] * buf_ref[cur, k]
    result = x1_ref[...] + mod_ref[0][5:6] * acc

    @pl.when(i < n_ctx_blocks)
    def _():
        oc_ref[...] = result

    @pl.when(i >= n_ctx_blocks)
    def _():
        ol_ref[...] = result


def _combine(ys, pos, topw, x1, mod, n_ctx_rows, lat_len):
    t = x1.shape[0]
    rb = COMBINE_ROWS
    n_ctx_blocks = n_ctx_rows // rb

    def mod_row(i, pos_ref):
        return (jnp.where(i < n_ctx_blocks, 0, 1 + (i - n_ctx_blocks) // (lat_len // rb)), 0, 0)

    out_specs = [pl.BlockSpec((rb, D_MODEL), lambda i, pos_ref: (jnp.minimum(i, n_ctx_blocks - 1), 0)),
                 pl.BlockSpec((rb, D_MODEL), lambda i, pos_ref: (jnp.maximum(i - n_ctx_blocks, 0), 0))]
    out_shape = [jax.ShapeDtypeStruct((n_ctx_rows, D_MODEL), F32),
                 jax.ShapeDtypeStruct((t - n_ctx_rows, D_MODEL), F32)]
    return pl.pallas_call(
        functools.partial(_combine_kernel, n_ctx_blocks=n_ctx_blocks),
        grid_spec=pltpu.PrefetchScalarGridSpec(
            num_scalar_prefetch=1,
            grid=(t // rb,),
            in_specs=[
                pl.BlockSpec(memory_space=pl.ANY),
                pl.BlockSpec((rb, EXPERT_LANES), lambda i, pos_ref: (i, 0)),
                pl.BlockSpec((rb, D_MODEL), lambda i, pos_ref: (i, 0)),
                pl.BlockSpec((1, 8, D_MODEL), mod_row),
            ],
            out_specs=out_specs,
            scratch_shapes=[pltpu.VMEM((2, TOP_K, rb, D_MODEL), F32), pltpu.SemaphoreType.DMA((2,))],
        ),
        out_shape=out_shape,
        compiler_params=_cparams("arbitrary"),
        name="combine",
    )(pos, ys, topw, x1, mod)


def _rope_tables(n_lat_tokens, rot_dim):
    n_rows = n_lat_tokens // GRID_W
    row = jnp.repeat(jnp.arange(n_rows, dtype=F32), GRID_W)
    col = jnp.tile(jnp.arange(GRID_W, dtype=F32), n_rows)
    n_freq = rot_dim // 4
    inv_freq = ROPE_THETA ** (-jnp.arange(n_freq, dtype=F32) / n_freq)
    ang = jnp.concatenate([row[:, None] * inv_freq, col[:, None] * inv_freq], axis=-1)
    return jnp.cos(ang), jnp.sin(ang)


def _pack_heads(w, n_heads, lo, hi, width):
    k = w.shape[0]
    per_head = w.shape[1] // n_heads
    part = w.reshape(k, n_heads, per_head)[:, :, lo:hi]
    part = jnp.pad(part, ((0, 0), (0, 0), (0, width - (hi - lo))))
    return part.reshape(k, n_heads * width)


def kernel(x_prompt, x_sample, c, cache_diff_k, cache_diff_v, cache_mla_ckv, cache_mla_krope, c_ctx, w_ada, b_ada,
           w_in, pool_w, pool_scale, diff_q_norm, diff_k_norm, diff_lambda, diff_out_norm, mla_q_a_norm, w_uq,
           mla_kv_a_norm, w_ukv, mla_q_norm, mla_k_norm, w_branch, w_out, router_w, router_b, moe_w_gu, moe_b_gu,
           moe_w_down, moe_b_down):
    batch, seq, d = x_prompt.shape
    dec_batch, dec_seq, _ = x_sample.shape
    depth = w_ada.shape[0]
    past = cache_diff_k.shape[2]
    n_ctx = batch * seq
    n_lat = dec_batch * dec_seq
    n_ctx_blocks = n_ctx // ROW_BLOCK
    mod_row = _mod_row(n_ctx_blocks, dec_seq // ROW_BLOCK)
    mod_spec = _mod_spec(mod_row)

    x = (x_prompt.reshape(n_ctx, d), x_sample.reshape(n_lat, d))
    cond8 = jnp.concatenate([c_ctx[None], c, jnp.zeros((8 - 1 - dec_batch, d), F32)], axis=0)
    mod_all = _ada(cond8, w_ada, b_ada).reshape(depth, 8, 6, d)[:, :1 + dec_batch]
    mod_all = jnp.pad(mod_all, ((0, 0), (0, 0), (0, 2), (0, 0)))

    cos_d, sin_d = _rope_tables(dec_seq, DIFF_HD)
    cos_m, sin_m = _rope_tables(dec_seq, MLA_ROPE)
    cos_a = jnp.concatenate([cos_d] * 4, axis=1)
    sin_a = jnp.concatenate([-sin_d, sin_d, -sin_d, sin_d], axis=1)
    pad = MLA_W - MLA_QK
    cos_b = jnp.concatenate([cos_m, cos_m, jnp.ones((dec_seq, pad), F32)], axis=1)
    sin_b = jnp.concatenate([-sin_m, sin_m, jnp.zeros((dec_seq, pad), F32)], axis=1)
    rope_spec = pl.BlockSpec(
        (ROW_BLOCK, HEAD_W), lambda i: (jnp.maximum(i - n_ctx_blocks, 0) % (dec_seq // ROW_BLOCK), 0))

    offs = np.cumsum((0, 1024, 1024, 1024, 1024, MLA_Q_RANK, MLA_KV_RANK, MLA_ROPE, 3 * 1024))
    new_dk, new_dv, new_ckv, new_kr = [], [], [], []
    for l in range(depth):
        lam_init = 0.8 - 0.6 * math.exp(-0.3 * l)
        wl = w_in[l]
        seg = [wl[:, offs[k]:offs[k + 1]] for k in range(8)]
        w_a = jnp.concatenate(seg[0:4] + [seg[7]], axis=1).astype(BF16)
        kr_wide = jnp.pad(seg[6], ((0, 0), (MLA_NOPE, MLA_W - MLA_NOPE - MLA_ROPE)))
        w_b = jnp.concatenate([seg[4], seg[5], kr_wide], axis=1).astype(BF16)
        wuq = _pack_heads(w_uq[l], MLA_HEADS, 0, MLA_QK, MLA_W).astype(BF16)
        wkn = _pack_heads(w_ukv[l], MLA_HEADS, 0, MLA_NOPE, MLA_W).astype(BF16)
        wv = _pack_heads(w_ukv[l], MLA_HEADS, MLA_NOPE, MLA_NOPE + HEAD_W, HEAD_W).astype(BF16)
        gq_d = jnp.tile(diff_q_norm[l], 2 * DIFF_HEADS)[None]
        gk_d = jnp.tile(diff_k_norm[l], 2 * DIFF_HEADS)[None]
        gq_m = jnp.tile(jnp.pad(mla_q_norm[l], (0, MLA_W - MLA_QK)), MLA_HEADS)[None]
        gk_m = jnp.tile(jnp.pad(mla_k_norm[l], (0, MLA_W - MLA_QK)), MLA_HEADS)[None]
        lp = diff_lambda[l]
        lam = (jnp.exp(jnp.sum(lp[0] * lp[1])) - jnp.exp(jnp.sum(lp[2] * lp[3])) + lam_init).reshape(1)
        out_gain = (diff_out_norm[l] * (1.0 - lam_init))[None]
        mod = mod_all[l]

        u_pool, dq, dk_f, dk_b, dv_f, dv_b, gates = _proj_a(x, mod, w_a, gq_d, gk_d, cos_a, sin_a, mod_spec,
                                                            rope_spec, n_ctx_blocks)
        mq, ckv_n, kr, mk, mv = _proj_b(x, mod, w_b, mla_q_a_norm[l][None], mla_kv_a_norm[l][None], wuq, wkn, wv,
                                        gq_m, gk_m, cos_b, sin_b, mod_spec, rope_spec, n_ctx_blocks)
        new_dk.append(dk_f[:n_ctx].reshape(batch, seq, DIFF_HEADS, HEAD_W))
        new_dv.append(dv_f[:n_ctx].reshape(batch, seq, DIFF_HEADS, HEAD_W))
        new_ckv.append(ckv_n[:n_ctx].reshape(batch, seq, MLA_KV_RANK))
        new_kr.append(kr[:n_ctx].reshape(batch, seq, MLA_ROPE))

        y_pool = _pool(u_pool, pool_w[l].astype(BF16), pool_scale[l][None], n_ctx_blocks, seq, dec_seq)

        c_krw = jnp.pad(cache_mla_krope[:, l].reshape(dec_batch * past, MLA_ROPE),
                        ((0, 0), (MLA_NOPE, MLA_W - MLA_NOPE - MLA_ROPE)))
        c_mk, c_mv = _cache_kv(cache_mla_ckv[:, l].reshape(dec_batch * past, MLA_KV_RANK), c_krw, wkn, wv, gk_m)

        geom = (n_ctx, seq, dec_seq)
        y_diff = _diff_attention(dq, dk_b, dv_b, cache_diff_k[:, l].reshape(dec_batch, past, d),
                                 cache_diff_v[:, l].reshape(dec_batch, past, d), lam, out_gain, geom)
        y_mla = _mla_attention(mq, mk, mv, c_mk.reshape(dec_batch, past, -1), c_mv.reshape(dec_batch, past, -1),
                               geom)

        rw = jnp.pad(router_w[l], ((0, 0), (0, EXPERT_LANES - N_EXPERTS)))
        rb = jnp.pad(router_b[l], (0, EXPERT_LANES - N_EXPERTS), constant_values=-jnp.inf)[None]
        x1, h2, topw, topi, rank, counts = _merge(x, mod, y_pool, y_diff, y_mla, gates, w_branch[l].astype(BF16),
                                                  w_out[l].astype(BF16), rw, rb, mod_row, n_ctx_blocks)
        pos, tile_expert, tile_src, n_used, fill_tile = _slot_layout(counts, topi, rank)
        hs = _dispatch(h2, pos, fill_tile)
        ys = _experts(hs, tile_expert, tile_src, n_used, l, moe_w_gu, moe_b_gu[l][:, None, :],
                      moe_w_down, moe_b_down[l][:, None, :])
        x = _combine(ys, pos, topw, x1, mod, n_ctx, dec_seq)

    y_ctx, y_lat = x
    return (y_ctx.reshape(batch, seq, d), y_lat.reshape(dec_batch, dec_seq, d),
            jnp.stack(new_dk, axis=1), jnp.stack(new_dv, axis=1), jnp.stack(new_ckv, axis=1),
            jnp.stack(new_kr, axis=1))
```

```python
import functools
import math

import jax
import jax.numpy as jnp
import numpy as np
from jax import lax
from jax.experimental import pallas as pl
from jax.experimental.pallas import tpu as pltpu

F32 = jnp.float32
BF16 = jnp.bfloat16

D_MODEL = 1024
N_POOL_GROUPS = 4
POOL_GROUP = D_MODEL // N_POOL_GROUPS
DIFF_HD = 64
DIFF_HEADS = 8
HEAD_W = 128
MLA_HEADS = 8
MLA_NOPE = 128
MLA_ROPE = 64
MLA_QK = MLA_NOPE + MLA_ROPE
MLA_W = 256
MLA_Q_RANK = 768
MLA_KV_RANK = 512
N_EXPERTS = 32
TOP_K = 4
D_FF = 1024
EXPERT_LANES = 128
SWIGLU_ALPHA = 1.702
SWIGLU_LIMIT = 7.0
ROPE_THETA = 10000.0
NORM_EPS = 1e-6
GRID_W = 64
LOG2E = math.log2(math.e)
VMEM_LIMIT = 56 * 1024 * 1024

ROW_BLOCK = 256
POOL_HALO = 16
MOE_TILE = 512
DISPATCH_ROWS = 256
COMBINE_ROWS = 128
WAIT_UNROLL = 16
ISSUE_UNROLL = 4
DIFF_Q_BLOCK = 1024
MLA_Q_BLOCK = 2048
ATTN_K_CHUNK = 512

_NT = (((1,), (1,)), ((), ()))
_TN = (((0,), (0,)), ((), ()))


def _cparams(*sem):
    return pltpu.CompilerParams(dimension_semantics=sem, vmem_limit_bytes=VMEM_LIMIT)


def _resident(shape):
    nd = len(shape)
    return pl.BlockSpec(shape, lambda *_: (0,) * nd, pipeline_mode=pl.Buffered(1))


def _sigmoid(x):
    return 1.0 / (1.0 + jnp.exp(-x))


def _lane_iota(shape):
    return lax.broadcasted_iota(jnp.int32, shape, len(shape) - 1)


def _rms(x, width):
    ss = jnp.sum(x * x, axis=-1, keepdims=True)
    return x * lax.rsqrt(ss * (1.0 / width) + NORM_EPS)


def _rope(x, cos, sin):
    n = x.shape[-1]
    lane = _lane_iota(x.shape)
    partner = jnp.where((lane & 32) != 0, pltpu.roll(x, 32, 1), pltpu.roll(x, n - 32, 1))
    return x * cos + partner * sin


def _modulated_norm(x, shift, scale):
    return _rms(x, D_MODEL) * (1.0 + scale) + shift


def _ada_kernel(cond_ref, w_ref, b_ref, o_ref):
    cnd = cond_ref[...]
    act = cnd * _sigmoid(cnd)
    o_ref[0] = jnp.dot(act, w_ref[0], preferred_element_type=F32) + b_ref[0]


def _ada(cond8, w_ada, b_ada):
    depth = w_ada.shape[0]
    n_chunk = w_ada.shape[2] // D_MODEL
    return pl.pallas_call(
        _ada_kernel,
        grid=(depth, n_chunk),
        in_specs=[
            pl.BlockSpec((8, D_MODEL), lambda l, j: (0, 0)),
            pl.BlockSpec((1, D_MODEL, D_MODEL), lambda l, j: (l, 0, j)),
            pl.BlockSpec((1, 1, D_MODEL), lambda l, j: (l, 0, j)),
        ],
        out_specs=pl.BlockSpec((1, 8, D_MODEL), lambda l, j: (l, 0, j)),
        out_shape=jax.ShapeDtypeStruct((depth, 8, n_chunk * D_MODEL), F32),
        compiler_params=_cparams("arbitrary", "arbitrary"),
        name="ada",
    )(cond8, w_ada, b_ada.reshape(depth, 1, -1))


def _diff_qk_norm(u, gain, cos, sin):
    outs = []
    for h in range(DIFF_HEADS):
        c = u[:, h * HEAD_W:(h + 1) * HEAD_W]
        lane = _lane_iota(c.shape)
        sq = c * c
        lo = jnp.sum(jnp.where(lane < DIFF_HD, sq, 0.0), axis=-1, keepdims=True)
        hi = jnp.sum(jnp.where(lane >= DIFF_HD, sq, 0.0), axis=-1, keepdims=True)
        ms = jnp.where(lane < DIFF_HD, lo, hi) * (1.0 / DIFF_HD)
        y = c * lax.rsqrt(ms + NORM_EPS) * gain[:, h * HEAD_W:(h + 1) * HEAD_W]
        outs.append((y, _rope(y, cos, sin)))
    return outs


def _token_block(ctx_ref, lat_ref, n_ctx_blocks):
    return jnp.where(pl.program_id(0) < n_ctx_blocks, ctx_ref[...], lat_ref[...])


def _rope_block(cos_ref, sin_ref, n_ctx_blocks):
    is_ctx = pl.program_id(0) < n_ctx_blocks
    return jnp.where(is_ctx, 1.0, cos_ref[...]), jnp.where(is_ctx, 0.0, sin_ref[...])


def _proj_a_kernel(xc_ref, xl_ref, mod_ref, w_ref, wg_ref, gq_ref, gk_ref, cos_ref, sin_ref,
                   pool_ref, dq_ref, dkf_ref, dkb_ref, dvf_ref, dvb_ref, gate_ref, *, n_ctx_blocks):
    mod = mod_ref[0]
    h = _modulated_norm(_token_block(xc_ref, xl_ref, n_ctx_blocks), mod[0:1], mod[1:2]).astype(BF16)
    cos, sin = _rope_block(cos_ref, sin_ref, n_ctx_blocks)

    def seg(k):
        ref, first = (w_ref, k) if k < 4 else (wg_ref, k - 4)
        return jnp.dot(h, ref[:, first * D_MODEL:(first + 1) * D_MODEL], preferred_element_type=F32)

    u_pool, u_dq = seg(0), seg(1)
    pool_ref[...] = u_pool.astype(BF16)
    u_dk = seg(2)
    q_scale = DIFF_HD ** -0.5 * LOG2E
    for hd, (_, roped) in enumerate(_diff_qk_norm(u_dq, gq_ref[...], cos, sin)):
        dq_ref[:, hd * HEAD_W:(hd + 1) * HEAD_W] = (roped * q_scale).astype(BF16)
    dv = seg(3)
    for hd, (plain, roped) in enumerate(_diff_qk_norm(u_dk, gk_ref[...], cos, sin)):
        dkf_ref[:, hd * HEAD_W:(hd + 1) * HEAD_W] = plain
        dkb_ref[:, hd * HEAD_W:(hd + 1) * HEAD_W] = roped.astype(BF16)
    u_gate = seg(4)
    dvf_ref[...] = dv
    dvb_ref[...] = dv.astype(BF16)
    for r in range(3):
        u_next = seg(5 + r) if r < 2 else None
        gate_ref[:, r * D_MODEL:(r + 1) * D_MODEL] = _sigmoid(u_gate).astype(BF16)
        u_gate = u_next


def _mla_heads(u, gain, cos, sin, rope):
    outs = []
    for h in range(MLA_HEADS):
        c = u[:, h * MLA_W:(h + 1) * MLA_W]
        y = _rms(c, MLA_QK) * gain[:, h * MLA_W:(h + 1) * MLA_W]
        if rope:
            y = jnp.concatenate([y[:, :MLA_NOPE], _rope(y[:, MLA_NOPE:], cos, sin)], axis=-1)
        outs.append(y)
    return outs


def _mla_kv(ckv_n, krw, wkn_ref, wv_ref, gk, cos, sin, rope, k_ref, v_ref):
    kn = jnp.dot(ckv_n, wkn_ref[...], preferred_element_type=F32)
    v_ref[...] = jnp.dot(ckv_n, wv_ref[...], preferred_element_type=F32).astype(BF16)
    kfull = jnp.concatenate([kn[:, h * MLA_W:(h + 1) * MLA_W] + krw for h in range(MLA_HEADS)], axis=-1)
    for hd, y in enumerate(_mla_heads(kfull, gk, cos, sin, rope)):
        k_ref[:, hd * MLA_W:(hd + 1) * MLA_W] = y.astype(BF16)


def _proj_b_kernel(xc_ref, xl_ref, mod_ref, w_ref, gqa_ref, gkva_ref, wuq_ref, wkn_ref, wv_ref, gq_ref, gk_ref,
                   cos_ref, sin_ref, mq_ref, ckv_ref, kr_ref, mk_ref, mv_ref, *, n_ctx_blocks):
    mod = mod_ref[0]
    h = _modulated_norm(_token_block(xc_ref, xl_ref, n_ctx_blocks), mod[0:1], mod[1:2]).astype(BF16)
    cos, sin = _rope_block(cos_ref, sin_ref, n_ctx_blocks)
    cq = jnp.dot(h, w_ref[:, :MLA_Q_RANK], preferred_element_type=F32)
    ckv = jnp.dot(h, w_ref[:, MLA_Q_RANK:MLA_Q_RANK + MLA_KV_RANK], preferred_element_type=F32)
    krw = jnp.dot(h, w_ref[:, MLA_Q_RANK + MLA_KV_RANK:], preferred_element_type=F32)
    cq_n = (_rms(cq, MLA_Q_RANK) * gqa_ref[...]).astype(BF16)
    mq = jnp.dot(cq_n, wuq_ref[...], preferred_element_type=F32)
    ckv_n = _rms(ckv, MLA_KV_RANK) * gkva_ref[...]
    ckv_ref[...] = ckv_n
    kr_ref[...] = krw[:, MLA_NOPE:MLA_NOPE + MLA_ROPE]
    _mla_kv(ckv_n.astype(BF16), krw, wkn_ref, wv_ref, gk_ref[...], cos, sin, True, mk_ref, mv_ref)
    q_scale = MLA_QK ** -0.5 * LOG2E
    for hd, y in enumerate(_mla_heads(mq, gq_ref[...], cos, sin, True)):
        mq_ref[:, hd * MLA_W:(hd + 1) * MLA_W] = (y * q_scale).astype(BF16)


def _cache_kv_kernel(ckv_ref, krw_ref, wkn_ref, wv_ref, gk_ref, mk_ref, mv_ref):
    _mla_kv(ckv_ref[...].astype(BF16), krw_ref[...], wkn_ref, wv_ref, gk_ref[...], None, None, False,
            mk_ref, mv_ref)


def _mod_row(n_ctx_blocks, blocks_per_lat):
    return lambda i: jnp.where(i < n_ctx_blocks, 0, 1 + (i - n_ctx_blocks) // blocks_per_lat)


def _mod_spec(mod_row):
    return pl.BlockSpec((1, 8, D_MODEL), lambda i: (mod_row(i), 0, 0))


def _rows(width):
    return pl.BlockSpec((ROW_BLOCK, width), lambda i: (i, 0))


def _ctx_rows(width, n_ctx_blocks):
    return pl.BlockSpec((ROW_BLOCK, width), lambda i: (jnp.minimum(i, n_ctx_blocks - 1), 0))


def _lat_rows(width, n_ctx_blocks):
    return pl.BlockSpec((ROW_BLOCK, width), lambda i: (jnp.maximum(i - n_ctx_blocks, 0), 0))


def _proj_outputs(t, n_ctx_blocks, widths, dtypes, ctx_only):
    spare = lambda w: pl.BlockSpec((ROW_BLOCK, w), lambda i: (jnp.minimum(i, n_ctx_blocks), 0))
    specs = [spare(w) if c else _rows(w) for w, c in zip(widths, ctx_only)]
    shapes = [jax.ShapeDtypeStruct(((n_ctx_blocks + 1) * ROW_BLOCK if c else t, w), dt)
              for w, dt, c in zip(widths, dtypes, ctx_only)]
    return specs, shapes


def _proj_a(x, mod, w_in, layer, w_gate, gq, gk, cos, sin, mod_spec, rope_spec, n_ctx_blocks):
    t = x[0].shape[0] + x[1].shape[0]
    w_main = pl.BlockSpec((None, D_MODEL, 4 * D_MODEL), lambda i: (layer, 0, 0), pipeline_mode=pl.Buffered(1))
    widths = (D_MODEL,) * 6 + (3 * D_MODEL,)
    dtypes = (BF16, BF16, F32, BF16, F32, BF16, BF16)
    out_specs, out_shape = _proj_outputs(t, n_ctx_blocks, widths, dtypes, (0, 0, 1, 0, 1, 0, 0))
    return pl.pallas_call(
        functools.partial(_proj_a_kernel, n_ctx_blocks=n_ctx_blocks),
        grid=(t // ROW_BLOCK,),
        in_specs=[_ctx_rows(D_MODEL, n_ctx_blocks), _lat_rows(D_MODEL, n_ctx_blocks), mod_spec,
                  w_main, _resident(w_gate.shape), _resident(gq.shape), _resident(gk.shape), rope_spec, rope_spec],
        out_specs=out_specs,
        out_shape=out_shape,
        compiler_params=_cparams("arbitrary"),
        name="proj_a",
    )(*x, mod, w_in, w_gate, gq, gk, cos, sin)


def _proj_b(x, mod, w_b, gqa, gkva, wuq, wkn, wv, gq, gk, cos, sin, mod_spec, rope_spec, n_ctx_blocks):
    t = x[0].shape[0] + x[1].shape[0]
    widths = (MLA_HEADS * MLA_W, MLA_KV_RANK, MLA_ROPE, MLA_HEADS * MLA_W, D_MODEL)
    dtypes = (BF16, F32, F32, BF16, BF16)
    out_specs, out_shape = _proj_outputs(t, n_ctx_blocks, widths, dtypes, (0, 1, 1, 0, 0))
    consts = (w_b, gqa, gkva, wuq, wkn, wv, gq, gk)
    return pl.pallas_call(
        functools.partial(_proj_b_kernel, n_ctx_blocks=n_ctx_blocks),
        grid=(t // ROW_BLOCK,),
        in_specs=[_ctx_rows(D_MODEL, n_ctx_blocks), _lat_rows(D_MODEL, n_ctx_blocks), mod_spec]
        + [_resident(a.shape) for a in consts] + [rope_spec, rope_spec],
        out_specs=out_specs,
        out_shape=out_shape,
        compiler_params=_cparams("arbitrary"),
        name="proj_b",
    )(*x, mod, *consts, cos, sin)


def _cache_kv(ckv, krw, wkn, wv, gk):
    n = ckv.shape[0]
    widths = (MLA_HEADS * MLA_W, D_MODEL)
    consts = (wkn, wv, gk)
    return pl.pallas_call(
        _cache_kv_kernel,
        grid=(n // ROW_BLOCK,),
        in_specs=[_rows(MLA_KV_RANK), _rows(MLA_W)] + [_resident(a.shape) for a in consts],
        out_specs=[_rows(w) for w in widths],
        out_shape=[jax.ShapeDtypeStruct((n, w), BF16) for w in widths],
        compiler_params=_cparams("arbitrary"),
        name="cache_kv",
    )(ckv, krw, *consts)


def _pool_kernel(prev_ref, cur_ref, next_ref, w_ref, scale_ref, o_ref, *, n_ctx_blocks, ctx_len, lat_len):
    i = pl.program_id(0)
    rb = ROW_BLOCK
    is_ctx = i < n_ctx_blocks
    n_ctx_rows = n_ctx_blocks * rb
    seq_len = jnp.where(is_ctx, ctx_len, lat_len)
    seq_start = jnp.where(is_ctx, (i * rb // ctx_len) * ctx_len,
                          n_ctx_rows + ((i * rb - n_ctx_rows) // lat_len) * lat_len)
    n_cols = rb + 2 * POOL_HALO
    row_pos = i * rb - seq_start + lax.broadcasted_iota(jnp.int32, (rb, n_cols), 0)
    col_pos = i * rb - POOL_HALO - seq_start + lax.broadcasted_iota(jnp.int32, (rb, n_cols), 1)
    cur = cur_ref[...]
    u = jnp.concatenate([prev_ref[...], cur, next_ref[...]], axis=0)
    for g in range(N_POOL_GROUPS):
        cols = slice(g * POOL_GROUP, (g + 1) * POOL_GROUP)
        window = 2 << g
        lo = jnp.clip(row_pos - window // 2, 0, seq_len)
        hi = jnp.clip(row_pos - window // 2 + window, 0, seq_len)
        band = jnp.where((col_pos >= lo) & (col_pos < hi), 1.0, 0.0).astype(BF16)
        total = jnp.dot(band, u[:, cols], preferred_element_type=F32)
        cnt = (hi[:, :1] - lo[:, :1]).astype(F32)
        pooled = total / cnt - cur[:, cols].astype(F32)
        mixed = jnp.dot(pooled.astype(BF16), w_ref[g], preferred_element_type=F32)
        o_ref[:, cols] = (mixed * scale_ref[:, cols]).astype(BF16)


def _pool(u_pool, pool_w, pool_scale, n_ctx_blocks, ctx_len, lat_len):
    t = u_pool.shape[0]
    halo_per_block = ROW_BLOCK // POOL_HALO
    n_halo_blocks = t // POOL_HALO
    assert POOL_HALO >= (2 << (N_POOL_GROUPS - 1)) // 2
    kern = functools.partial(_pool_kernel, n_ctx_blocks=n_ctx_blocks, ctx_len=ctx_len, lat_len=lat_len)
    return pl.pallas_call(
        kern,
        grid=(t // ROW_BLOCK,),
        in_specs=[
            pl.BlockSpec((POOL_HALO, D_MODEL), lambda i: (jnp.maximum(i * halo_per_block - 1, 0), 0)),
            _rows(D_MODEL),
            pl.BlockSpec((POOL_HALO, D_MODEL),
                         lambda i: (jnp.minimum((i + 1) * halo_per_block, n_halo_blocks - 1), 0)),
            _resident(pool_w.shape),
            _resident(pool_scale.shape),
        ],
        out_specs=_rows(D_MODEL),
        out_shape=jax.ShapeDtypeStruct((t, D_MODEL), BF16),
        compiler_params=_cparams("arbitrary"),
        name="pool",
    )(u_pool, u_pool, u_pool, pool_w, pool_scale)


def _softmax_stats(s, m, l):
    m_new = jnp.maximum(m, jnp.max(s, axis=0, keepdims=True))
    p = jnp.exp2(s - m_new)
    alpha = jnp.exp2(m - m_new)
    return m_new, alpha * l + jnp.sum(p, axis=0, keepdims=True), alpha, p.astype(BF16)


def _accumulate(acc, alpha, p, v):
    return alpha * acc + lax.dot_general(v, p, _TN, preferred_element_type=F32)


def _online_softmax(chunks, scores, n_softmax, tq, defer_values):
    chunks = list(chunks)
    m = [jnp.full((1, tq), -jnp.inf, F32)] * n_softmax
    l = [jnp.zeros((1, tq), F32)] * n_softmax
    acc = [jnp.zeros((HEAD_W, tq), F32)] * n_softmax
    pending = None
    s_next = scores(chunks[0][0])
    for c, (_, v) in enumerate(chunks):
        s = s_next
        if c + 1 < len(chunks):
            s_next = scores(chunks[c + 1][0])
        stats = [_softmax_stats(s[j], m[j], l[j]) for j in range(n_softmax)]
        m = [st[0] for st in stats]
        l = [st[1] for st in stats]
        if pending is not None:
            acc = [_accumulate(acc[j], *pending[j]) for j in range(n_softmax)]
        pending = [(st[2], st[3], v) for st in stats]
        if not defer_values:
            acc = [_accumulate(acc[j], *pending[j]) for j in range(n_softmax)]
            pending = None
    if pending is not None:
        acc = [_accumulate(acc[j], *pending[j]) for j in range(n_softmax)]
    return list(zip(l, acc))


def _key_value_chunks(kv_refs, head, qk_width):
    k_cols = slice(head * qk_width, (head + 1) * qk_width)
    v_cols = slice(head * HEAD_W, (head + 1) * HEAD_W)
    for k_ref, v_ref in zip(kv_refs[0::2], kv_refs[1::2]):
        n = k_ref.shape[0]
        size = min(n, ATTN_K_CHUNK)
        for start in range(0, n, size):
            yield k_ref[start:start + size, k_cols].astype(BF16), v_ref[start:start + size, v_cols].astype(BF16)


def _diff_attn_kernel(lam_ref, q_ref, *refs, n_kv, heads):
    g_ref, o_ref = refs[n_kv], refs[-1]
    for h in range(heads):
        cols = slice(h * HEAD_W, (h + 1) * HEAD_W)
        q = q_ref[:, cols]
        lane = _lane_iota(q.shape)
        q1 = jnp.where(lane < DIFF_HD, q, jnp.zeros_like(q))
        q2 = jnp.where(lane >= DIFF_HD, q, jnp.zeros_like(q))

        def scores(k):
            return (lax.dot_general(k, q1, _NT, preferred_element_type=F32),
                    lax.dot_general(k, q2, _NT, preferred_element_type=F32))

        (l1, a1), (l2, a2) = _online_softmax(_key_value_chunks(refs[:n_kv], h, HEAD_W), scores, 2, q.shape[0],
                                             defer_values=False)
        o = (a1 / l1 - lam_ref[0] * (a2 / l2)).T
        o_ref[:, cols] = (_rms(o, HEAD_W) * g_ref[...]).astype(o_ref.dtype)


def _mla_attn_kernel(q_ref, *refs, n_kv, heads):
    o_ref = refs[-1]
    for h in range(heads):
        q = q_ref[:, h * MLA_W:(h + 1) * MLA_W]
        ((l, acc),) = _online_softmax(
            _key_value_chunks(refs[:n_kv], h, MLA_W),
            lambda k: (lax.dot_general(k, q, _NT, preferred_element_type=F32),), 1, q.shape[0],
            defer_values=True)
        o_ref[:, h * HEAD_W:(h + 1) * HEAD_W] = (acc / l).T.astype(o_ref.dtype)


def _attention(kernel, q, k, v, cache_k, cache_v, lead, lead_specs, tail, tail_specs, qk_width, tq, geom, name):
    n_ctx, seq, dec_seq = geom
    t = q.shape[0]
    n_heads = v.shape[1] // HEAD_W
    dec_batch, past, _ = cache_k.shape
    nq, q0, k0 =dec_seq // tq, n_ctx // tq, n_ctx // dec_seq
    assert n_ctx % tq == 0 and n_ctx % dec_seq == 0 and dec_seq % tq == 0
    params = _cparams("arbitrary", "arbitrary", "arbitrary")
    seq_block = lambda w: pl.BlockSpec((seq, n_heads * w), lambda b, h, qi: (b, 0))
    ctx = pl.pallas_call(
        functools.partial(kernel, n_kv=2, heads=n_heads),
        grid=(n_ctx // seq, 1, 1),
        in_specs=lead_specs + [seq_block(qk_width), seq_block(qk_width), seq_block(HEAD_W)] + tail_specs,
        out_specs=seq_block(HEAD_W),
        out_shape=jax.ShapeDtypeStruct((n_ctx, n_heads * HEAD_W), BF16),
        compiler_params=params,
        name=name + "_ctx",
    )(*lead, q, k, v, *tail)
    lat_rows = lambda w, first: pl.BlockSpec((tq, w), lambda b, h, qi: (first + b * nq + qi, h))
    cached = lambda w: pl.BlockSpec((None, past, w), lambda b, h, qi: (b, 0, h))
    lat_keys = lambda w: pl.BlockSpec((dec_seq, w), lambda b, h, qi: (k0 + b, h))
    lat = pl.pallas_call(
        functools.partial(kernel, n_kv=4, heads=1),
        grid=(dec_batch, n_heads, nq),
        in_specs=lead_specs + [lat_rows(qk_width, q0), cached(qk_width), cached(HEAD_W), lat_keys(qk_width),
                               lat_keys(HEAD_W)] + tail_specs,
        out_specs=lat_rows(HEAD_W, 0),
        out_shape=jax.ShapeDtypeStruct((t - n_ctx, n_heads * HEAD_W), BF16),
        compiler_params=params,
        name=name + "_lat",
    )(*lead, q, cache_k, cache_v, k, v, *tail)
    return ctx, lat


def _diff_attention(q, k, v, cache_k, cache_v, lam, out_gain, geom):
    lead_specs = [pl.BlockSpec(memory_space=pltpu.SMEM)]
    tail_specs = [pl.BlockSpec((1, HEAD_W), lambda b, h, qi: (0, 0))]
    return _attention(_diff_attn_kernel, q, k, v, cache_k, cache_v, (lam,), lead_specs, (out_gain,), tail_specs,
                      HEAD_W, DIFF_Q_BLOCK, geom, "diff")


def _mla_attention(q, k, v, cache_k, cache_v, geom):
    return _attention(_mla_attn_kernel, q, k, v, cache_k, cache_v, (), [], (), [], MLA_W, MLA_Q_BLOCK, geom, "mla")


def _merge_kernel(xc_ref, xl_ref, mod_ref, yp_ref, ydc_ref, ydl_ref, ymc_ref, yml_ref, gate_ref, wb_ref, wo_ref,
                  rw_ref, rb_ref, x1_ref, h2_ref, topw_ref, topi_ref, rank_ref, count_ref, carry_ref, logits_ref,
                  *, n_ctx_blocks):
    step = pl.program_id(0)

    @pl.when(step == 0)
    def _():
        carry_ref[...] = jnp.zeros_like(carry_ref)
        logits_ref[...] = jnp.zeros_like(logits_ref)

    is_ctx = step < n_ctx_blocks
    mod = mod_ref[0]
    merged = jnp.zeros(xc_ref.shape, F32)
    branches = (yp_ref[...], jnp.where(is_ctx, ydc_ref[...], ydl_ref[...]),
                jnp.where(is_ctx, ymc_ref[...], yml_ref[...]))
    for r, y in enumerate(branches):
        z = jnp.dot(y, wb_ref[r], preferred_element_type=F32)
        merged = merged + gate_ref[:, r * D_MODEL:(r + 1) * D_MODEL].astype(F32) * z
    y = jnp.dot(merged.astype(BF16), wo_ref[...], preferred_element_type=F32)
    x1 = jnp.where(is_ctx, xc_ref[...], xl_ref[...]) + mod[2:3] * y
    x1_ref[...] = x1
    h2 = _modulated_norm(x1, mod[3:4], mod[4:5])
    h2_ref[...] = h2
    new_logits = jnp.dot(h2, rw_ref[...], preferred_element_type=F32) + rb_ref[...]
    logits = logits_ref[...]
    logits_ref[...] = new_logits
    lane = _lane_iota(logits.shape)
    picks = []
    for _ in range(TOP_K):
        mx = jnp.max(logits, axis=-1, keepdims=True)
        first = jnp.min(jnp.where(logits == mx, lane, EXPERT_LANES), axis=-1, keepdims=True)
        hit = lane == first
        picks.append((mx, first, hit))
        logits = jnp.where(hit, -jnp.inf, logits)
    exps = [jnp.exp(v - picks[0][0]) for v, _, _ in picks]
    denom = exps[0] + exps[1] + exps[2] + exps[3]
    rows = logits.shape[0]
    chosen = jnp.zeros(logits.shape, F32)
    for _, _, hit in picks:
        chosen = chosen + jnp.where(hit, 1.0, 0.0)
    earlier = jnp.where(lax.broadcasted_iota(jnp.int32, (rows, rows), 1)
                        < lax.broadcasted_iota(jnp.int32, (rows, rows), 0), 1.0, 0.0).astype(BF16)
    before = jnp.dot(earlier, chosen.astype(BF16), preferred_element_type=F32) + carry_ref[0:1, :]
    topw = jnp.zeros(logits.shape, F32)
    topi = jnp.zeros(logits.shape, jnp.int32)
    rank = jnp.zeros(logits.shape, jnp.int32)
    for k, (e, (_, first, hit)) in enumerate(zip(exps, picks)):
        slot_rank = jnp.sum(jnp.where(hit, before, 0.0), axis=-1, keepdims=True)
        topw = jnp.where(lane == k, e / denom, topw)
        topi = jnp.where(lane == k, first, topi)
        rank = jnp.where(lane == k, slot_rank.astype(jnp.int32), rank)
    topw_ref[...] = topw
    topi_ref[...] = topi
    rank_ref[...] = rank
    routed = jnp.where(step > 0, jnp.sum(chosen, axis=0, keepdims=True), 0.0)
    carry_ref[...] = carry_ref[...] + routed
    count_ref[...] = carry_ref[...]


def _merge(x, mod, y_pool, y_diff, y_mla, gates, wb, wo, rw, rb, mod_row, n_ctx_blocks):
    t = x[0].shape[0] + x[1].shape[0]
    nb = t // ROW_BLOCK
    consts = (wb, wo, rw, rb)
    lanes = EXPERT_LANES
    cur = lambda i: jnp.minimum(i, nb - 1)
    rows = lambda w: pl.BlockSpec((ROW_BLOCK, w), lambda i: (cur(i), 0))
    routed = pl.BlockSpec((ROW_BLOCK, lanes), lambda i: (jnp.maximum(i - 1, 0), 0))
    ctx_rows = _ctx_rows(D_MODEL, n_ctx_blocks)
    lat_rows = pl.BlockSpec((ROW_BLOCK, D_MODEL), lambda i: (jnp.maximum(cur(i) - n_ctx_blocks, 0), 0))
    return pl.pallas_call(
        functools.partial(_merge_kernel, n_ctx_blocks=n_ctx_blocks),
        grid=(nb + 1,),
        in_specs=[ctx_rows, lat_rows, _mod_spec(lambda i: mod_row(cur(i))), rows(D_MODEL), ctx_rows, lat_rows,
                  ctx_rows, lat_rows, rows(3 * D_MODEL)] + [_resident(a.shape) for a in consts],
        out_specs=[rows(D_MODEL), rows(D_MODEL), routed, routed, routed,
                   pl.BlockSpec((8, lanes), lambda i: (0, 0))],
        out_shape=[jax.ShapeDtypeStruct((t, D_MODEL), F32), jax.ShapeDtypeStruct((t, D_MODEL), F32),
                   jax.ShapeDtypeStruct((t, lanes), F32), jax.ShapeDtypeStruct((t, lanes), jnp.int32),
                   jax.ShapeDtypeStruct((t, lanes), jnp.int32), jax.ShapeDtypeStruct((8, lanes), F32)],
        scratch_shapes=[pltpu.VMEM((8, lanes), F32), pltpu.VMEM((ROW_BLOCK, lanes), F32)],
        compiler_params=_cparams("arbitrary"),
        name="merge",
    )(*x, mod, y_pool, *y_diff, *y_mla, gates, *consts)


def _slot_layout(counts, topi, rank):
    tile = MOE_TILE
    cnt = counts[0, :N_EXPERTS].astype(jnp.int32)
    padded = (cnt + tile - 1) // tile * tile
    end = jnp.cumsum(padded)
    start = end - padded
    experts = jnp.arange(N_EXPERTS, dtype=jnp.int32)
    idx4 = topi[:, :TOP_K]
    slot_start = jnp.sum(jnp.where(idx4[:, :, None] == experts, start, 0), axis=-1)
    pos = (slot_start + rank[:, :TOP_K]).reshape(-1)
    n_tiles = (topi.shape[0] * TOP_K + N_EXPERTS * tile) // tile
    tile_row = jnp.arange(n_tiles, dtype=jnp.int32) * tile
    tile_expert = jnp.minimum(jnp.sum(tile_row[:, None] >= end, axis=-1), N_EXPERTS - 1).astype(jnp.int32)
    n_used = (end[-1] // tile).astype(jnp.int32)
    tile_src = jnp.minimum(jnp.arange(n_tiles, dtype=jnp.int32), n_used - 1)
    ragged_last = jnp.any((cnt % tile != 0) & (tile_row[:, None] == end - tile), axis=-1)
    fill_tile = (ragged_last | (tile_row >= end[-1])).astype(jnp.int32)
    return pos, tile_expert, tile_src, n_used.reshape(1), fill_tile


def _row_copy(src, src_row, dst, dst_row, sem):
    return pltpu.make_async_copy(src.at[pl.ds(src_row, 1)], dst.at[pl.ds(dst_row, 1)], sem)


def _dispatch_kernel(fill_tile_ref, pos_ref, h_ref, hs_ref, zero_ref, src_ref, fill_sem, fetch_sems, row_sems):
    i = pl.program_id(0)
    n_steps = pl.num_programs(0)
    slots = DISPATCH_ROWS * TOP_K

    def fetch(step, buf):
        rows = pl.ds(pl.multiple_of(step * DISPATCH_ROWS, DISPATCH_ROWS), DISPATCH_ROWS)
        return pltpu.make_async_copy(h_ref.at[rows], src_ref.at[buf], fetch_sems.at[buf])

    @pl.when(i == 0)
    def _():
        fetch(0, 0).start()
        fetch(1, 1).start()

    @pl.when(i == 0)
    def _():
        zero_ref[...] = jnp.zeros_like(zero_ref)

        def fill_copy(tile):
            row = pl.multiple_of(tile * MOE_TILE, MOE_TILE)
            return pltpu.make_async_copy(zero_ref, hs_ref.at[pl.ds(row, MOE_TILE)], fill_sem)

        def start_fill(tile, carry):
            @pl.when(fill_tile_ref[tile] != 0)
            def _():
                fill_copy(tile).start()
            return carry

        def wait_fill(tile, carry):
            @pl.when(fill_tile_ref[tile] != 0)
            def _():
                fill_copy(tile).wait()
            return carry

        lax.fori_loop(0, hs_ref.shape[0] // MOE_TILE, start_fill, 0)
        lax.fori_loop(0, hs_ref.shape[0] // MOE_TILE, wait_fill, 0)

    buf = i % 3
    src = src_ref.at[buf]
    fetch(i, buf).wait()

    def issue(r, carry):
        slot = (i * DISPATCH_ROWS + r) * TOP_K
        for k in range(TOP_K):
            _row_copy(src, r, hs_ref, pos_ref[slot + k], row_sems.at[i % 2]).start()
        return carry

    def drain(sem):
        def body(j, carry):
            for _ in range(WAIT_UNROLL):
                _row_copy(src, 0, hs_ref, 0, sem).wait()
            return carry

        lax.fori_loop(0, slots // WAIT_UNROLL, body, 0)

    lax.fori_loop(0, DISPATCH_ROWS, issue, 0, unroll=ISSUE_UNROLL)

    @pl.when(i > 0)
    def _():
        drain(row_sems.at[(i - 1) % 2])

    @pl.when(i + 2 < n_steps)
    def _():
        fetch(i + 2, (i + 2) % 3).start()

    @pl.when(i == n_steps - 1)
    def _():
        drain(row_sems.at[i % 2])


def _dispatch(h2, pos, fill_tile):
    t = h2.shape[0]
    n_rows = fill_tile.shape[0] * MOE_TILE
    return pl.pallas_call(
        _dispatch_kernel,
        grid_spec=pltpu.PrefetchScalarGridSpec(
            num_scalar_prefetch=2,
            grid=(t // DISPATCH_ROWS,),
            in_specs=[pl.BlockSpec(memory_space=pl.ANY)],
            out_specs=pl.BlockSpec(memory_space=pl.ANY),
            scratch_shapes=[pltpu.VMEM((MOE_TILE, D_MODEL), F32), pltpu.VMEM((3, DISPATCH_ROWS, D_MODEL), F32),
                            pltpu.SemaphoreType.DMA, pltpu.SemaphoreType.DMA((3,)),
                            pltpu.SemaphoreType.DMA((2,))],
        ),
        out_shape=jax.ShapeDtypeStruct((n_rows, D_MODEL), F32),
        compiler_params=_cparams("arbitrary"),
        name="dispatch",
    )(fill_tile, pos, h2)


def _expert_kernel(tile_expert_ref, tile_src_ref, n_used_ref, hs_ref, wgu_ref, bgu_ref, wd_ref, bd_ref, ys_ref,
                   wgu_bf_ref, wd_bf_ref):
    i = pl.program_id(0)
    occupied = i < n_used_ref[0]
    new_expert = (i == 0) | (tile_expert_ref[i] != tile_expert_ref[jnp.maximum(i - 1, 0)])

    @pl.when(occupied & new_expert)
    def _():
        wgu_bf_ref[...] = wgu_ref[...].astype(BF16)
        wd_bf_ref[...] = wd_ref[...].astype(BF16)

    @pl.when(occupied)
    def _():
        gu = jnp.dot(hs_ref[...].astype(BF16), wgu_bf_ref[...], preferred_element_type=F32) + bgu_ref[0]
        gate = jnp.minimum(gu[:, :D_FF], SWIGLU_LIMIT)
        up = jnp.clip(gu[:, D_FF:], -SWIGLU_LIMIT, SWIGLU_LIMIT)
        act = (up + 1.0) * gate * _sigmoid(SWIGLU_ALPHA * gate)
        ys_ref[...] = jnp.dot(act.astype(BF16), wd_bf_ref[...], preferred_element_type=F32) + bd_ref[0]

    @pl.when(jnp.logical_not(occupied))
    def _():
        ys_ref[...] = jnp.zeros_like(ys_ref)


def _experts(hs, tile_expert, tile_src, n_used, layer, wgu, bgu, wd, bd):
    n_rows = hs.shape[0]
    rows = lambda i, te, ts, nu: (ts[i], 0)
    bias = lambda i, te, ts, nu: (te[i], 0, 0)
    weight = lambda i, te, ts, nu: (layer, te[i], 0, 0)
    return pl.pallas_call(
        _expert_kernel,
        grid_spec=pltpu.PrefetchScalarGridSpec(
            num_scalar_prefetch=3,
            grid=(n_rows // MOE_TILE,),
            in_specs=[
                pl.BlockSpec((MOE_TILE, D_MODEL), rows),
                pl.BlockSpec((None, None, D_MODEL, 2 * D_FF), weight),
                pl.BlockSpec((1, 1, 2 * D_FF), bias),
                pl.BlockSpec((None, None, D_FF, D_MODEL), weight),
                pl.BlockSpec((1, 1, D_MODEL), bias),
            ],
            out_specs=pl.BlockSpec((MOE_TILE, D_MODEL), lambda i, te, ts, nu: (i, 0)),
            scratch_shapes=[pltpu.VMEM((D_MODEL, 2 * D_FF), BF16), pltpu.VMEM((D_FF, D_MODEL), BF16)],
        ),
        out_shape=jax.ShapeDtypeStruct((n_rows, D_MODEL), F32),
        compiler_params=_cparams("arbitrary"),
        name="experts",
    )(tile_expert, tile_src, n_used, hs, wgu, bgu, wd, bd)


def _combine_kernel(pos_ref, ys_ref, w_ref, x1_ref, mod_ref, *refs, n_ctx_blocks):
    oc_ref, ol_ref, buf_ref, sems = refs
    i = pl.program_id(0)
    slots = COMBINE_ROWS * TOP_K

    def gather(step, buf):
        def issue(r, carry):
            token = step * COMBINE_ROWS + r
            for k in range(TOP_K):
                _row_copy(ys_ref, pos_ref[token * TOP_K + k], buf_ref.at[buf, k], r, sems.at[buf]).start()
            return carry

        lax.fori_loop(0, COMBINE_ROWS, issue, 0, unroll=ISSUE_UNROLL)

    @pl.when(i == 0)
    def _():
        gather(0, 0)

    @pl.when(i + 1 < pl.num_programs(0))
    def _():
        gather(i + 1, (i + 1) % 2)

    cur = i % 2

    def drain(j, carry):
        for _ in range(WAIT_UNROLL):
            _row_copy(ys_ref, 0, buf_ref.at[cur, 0], 0, sems.at[cur]).wait()
        return carry

    lax.fori_loop(0, slots // WAIT_UNROLL, drain, 0)
    w = w_ref[...]
    acc = w[:, 0:1] * buf_ref[cur, 0]
    for k in range(1, TOP_K):
        acc = acc + w[:, k:k + 1] * buf_ref[cur, k]
    result = x1_ref[...] + mod_ref[0][5:6] * acc

    @pl.when(i < n_ctx_blocks)
    def _():
        oc_ref[...] = result

    @pl.when(i >= n_ctx_blocks)
    def _():
        ol_ref[...] = result


def _combine(ys, pos, topw, x1, mod, n_ctx_rows, lat_len):
    t = x1.shape[0]
    rb = COMBINE_ROWS
    n_ctx_blocks = n_ctx_rows // rb

    def mod_row(i, pos_ref):
        return (jnp.where(i < n_ctx_blocks, 0, 1 + (i - n_ctx_blocks) // (lat_len // rb)), 0, 0)

    out_specs = [pl.BlockSpec((rb, D_MODEL), lambda i, pos_ref: (jnp.minimum(i, n_ctx_blocks - 1), 0)),
                 pl.BlockSpec((rb, D_MODEL), lambda i, pos_ref: (jnp.maximum(i - n_ctx_blocks, 0), 0))]
    out_shape = [jax.ShapeDtypeStruct((n_ctx_rows, D_MODEL), F32),
                 jax.ShapeDtypeStruct((t - n_ctx_rows, D_MODEL), F32)]
    return pl.pallas_call(
        functools.partial(_combine_kernel, n_ctx_blocks=n_ctx_blocks),
        grid_spec=pltpu.PrefetchScalarGridSpec(
            num_scalar_prefetch=1,
            grid=(t // rb,),
            in_specs=[
                pl.BlockSpec(memory_space=pl.ANY),
                pl.BlockSpec((rb, EXPERT_LANES), lambda i, pos_ref: (i, 0)),
                pl.BlockSpec((rb, D_MODEL), lambda i, pos_ref: (i, 0)),
                pl.BlockSpec((1, 8, D_MODEL), mod_row),
            ],
            out_specs=out_specs,
            scratch_shapes=[pltpu.VMEM((2, TOP_K, rb, D_MODEL), F32), pltpu.SemaphoreType.DMA((2,))],
        ),
        out_shape=out_shape,
        compiler_params=_cparams("arbitrary"),
        name="combine",
    )(pos, ys, topw, x1, mod)


def _rope_tables(n_lat_tokens, rot_dim):
    n_rows = n_lat_tokens // GRID_W
    row = jnp.repeat(jnp.arange(n_rows, dtype=F32), GRID_W)
    col = jnp.tile(jnp.arange(GRID_W, dtype=F32), n_rows)
    n_freq = rot_dim // 4
    inv_freq = ROPE_THETA ** (-jnp.arange(n_freq, dtype=F32) / n_freq)
    ang = jnp.concatenate([row[:, None] * inv_freq, col[:, None] * inv_freq], axis=-1)
    return jnp.cos(ang), jnp.sin(ang)


def _pack_heads(w, n_heads, lo, hi, width):
    k = w.shape[0]
    per_head = w.shape[1] // n_heads
    part = w.reshape(k, n_heads, per_head)[:, :, lo:hi]
    part = jnp.pad(part, ((0, 0), (0, 0), (0, width - (hi - lo))))
    return part.reshape(k, n_heads * width)


def kernel(x_prompt, x_sample, c, cache_diff_k, cache_diff_v, cache_mla_ckv, cache_mla_krope, c_ctx, w_ada, b_ada,
           w_in, pool_w, pool_scale, diff_q_norm, diff_k_norm, diff_lambda, diff_out_norm, mla_q_a_norm, w_uq,
           mla_kv_a_norm, w_ukv, mla_q_norm, mla_k_norm, w_branch, w_out, router_w, router_b, moe_w_gu, moe_b_gu,
           moe_w_down, moe_b_down):
    batch, seq, d = x_prompt.shape
    dec_batch, dec_seq, _ = x_sample.shape
    depth = w_ada.shape[0]
    past = cache_diff_k.shape[2]
    n_ctx = batch * seq
    n_lat = dec_batch * dec_seq
    n_ctx_blocks = n_ctx // ROW_BLOCK
    mod_row = _mod_row(n_ctx_blocks, dec_seq // ROW_BLOCK)
    mod_spec = _mod_spec(mod_row)

    x = (x_prompt.reshape(n_ctx, d), x_sample.reshape(n_lat, d))
    cond8 = jnp.concatenate([c_ctx[None], c, jnp.zeros((8 - 1 - dec_batch, d), F32)], axis=0)
    mod_all = _ada(cond8, w_ada, b_ada).reshape(depth, 8, 6, d)[:, :1 + dec_batch]
    mod_all = jnp.pad(mod_all, ((0, 0), (0, 0), (0, 2), (0, 0)))

    cos_d, sin_d = _rope_tables(dec_seq, DIFF_HD)
    cos_m, sin_m = _rope_tables(dec_seq, MLA_ROPE)
    cos_a = jnp.concatenate([cos_d] * 4, axis=1)
    sin_a = jnp.concatenate([-sin_d, sin_d, -sin_d, sin_d], axis=1)
    pad = MLA_W - MLA_QK
    cos_b = jnp.concatenate([cos_m, cos_m, jnp.ones((dec_seq, pad), F32)], axis=1)
    sin_b = jnp.concatenate([-sin_m, sin_m, jnp.zeros((dec_seq, pad), F32)], axis=1)
    rope_spec = pl.BlockSpec(
        (ROW_BLOCK, HEAD_W), lambda i: (jnp.maximum(i - n_ctx_blocks, 0) % (dec_seq // ROW_BLOCK), 0))

    offs = np.cumsum((0, 1024, 1024, 1024, 1024, MLA_Q_RANK, MLA_KV_RANK, MLA_ROPE, 3 * 1024))
    new_dk, new_dv, new_ckv, new_kr = [], [], [], []
    w_in_bf = w_in.astype(BF16)
    for l in range(depth):
        lam_init = 0.8 - 0.6 * math.exp(-0.3 * l)
        seg = [w_in_bf[l][:, offs[k]:offs[k + 1]] for k in range(8)]
        kr_wide = jnp.pad(seg[6], ((0, 0), (MLA_NOPE, MLA_W - MLA_NOPE - MLA_ROPE)))
        w_b = jnp.concatenate([seg[4], seg[5], kr_wide], axis=1)
        wuq = _pack_heads(w_uq[l], MLA_HEADS, 0, MLA_QK, MLA_W).astype(BF16)
        wkn = _pack_heads(w_ukv[l], MLA_HEADS, 0, MLA_NOPE, MLA_W).astype(BF16)
        wv = _pack_heads(w_ukv[l], MLA_HEADS, MLA_NOPE, MLA_NOPE + HEAD_W, HEAD_W).astype(BF16)
        gq_d = jnp.tile(diff_q_norm[l], 2 * DIFF_HEADS)[None]
        gk_d = jnp.tile(diff_k_norm[l], 2 * DIFF_HEADS)[None]
        gq_m = jnp.tile(jnp.pad(mla_q_norm[l], (0, MLA_W - MLA_QK)), MLA_HEADS)[None]
        gk_m = jnp.tile(jnp.pad(mla_k_norm[l], (0, MLA_W - MLA_QK)), MLA_HEADS)[None]
        lp = diff_lambda[l]
        lam = (jnp.exp(jnp.sum(lp[0] * lp[1])) - jnp.exp(jnp.sum(lp[2] * lp[3])) + lam_init).reshape(1)
        out_gain = (diff_out_norm[l] * (1.0 - lam_init))[None]
        mod = mod_all[l]

        u_pool, dq, dk_f, dk_b, dv_f, dv_b, gates = _proj_a(x, mod, w_in_bf, l, seg[7], gq_d, gk_d, cos_a, sin_a,
                                                            mod_spec, rope_spec, n_ctx_blocks)
        mq, ckv_n, kr, mk, mv = _proj_b(x, mod, w_b, mla_q_a_norm[l][None], mla_kv_a_norm[l][None], wuq, wkn, wv,
                                        gq_m, gk_m, cos_b, sin_b, mod_spec, rope_spec, n_ctx_blocks)
        new_dk.append(dk_f[:n_ctx].reshape(batch, seq, DIFF_HEADS, HEAD_W))
        new_dv.append(dv_f[:n_ctx].reshape(batch, seq, DIFF_HEADS, HEAD_W))
        new_ckv.append(ckv_n[:n_ctx].reshape(batch, seq, MLA_KV_RANK))
        new_kr.append(kr[:n_ctx].reshape(batch, seq, MLA_ROPE))

        y_pool = _pool(u_pool, pool_w[l].astype(BF16), pool_scale[l][None], n_ctx_blocks, seq, dec_seq)

        c_krw = jnp.pad(cache_mla_krope[:, l].reshape(dec_batch * past, MLA_ROPE),
                        ((0, 0), (MLA_NOPE, MLA_W - MLA_NOPE - MLA_ROPE)))
        c_mk, c_mv = _cache_kv(cache_mla_ckv[:, l].reshape(dec_batch * past, MLA_KV_RANK), c_krw, wkn, wv, gk_m)

        geom = (n_ctx, seq, dec_seq)
        y_diff = _diff_attention(dq, dk_b, dv_b, cache_diff_k[:, l].reshape(dec_batch, past, d),
                                 cache_diff_v[:, l].reshape(dec_batch, past, d), lam, out_gain, geom)
        y_mla = _mla_attention(mq, mk, mv, c_mk.reshape(dec_batch, past, -1), c_mv.reshape(dec_batch, past, -1),
                               geom)

        rw = jnp.pad(router_w[l], ((0, 0), (0, EXPERT_LANES - N_EXPERTS)))
        rb = jnp.pad(router_b[l], (0, EXPERT_LANES - N_EXPERTS), constant_values=-jnp.inf)[None]
        x1, h2, topw, topi, rank, counts = _merge(x, mod, y_pool, y_diff, y_mla, gates, w_branch[l].astype(BF16),
                                                  w_out[l].astype(BF16), rw, rb, mod_row, n_ctx_blocks)
        pos, tile_expert, tile_src, n_used, fill_tile = _slot_layout(counts, topi, rank)
        hs = _dispatch(h2, pos, fill_tile)
        ys = _experts(hs, tile_expert, tile_src, n_used, l, moe_w_gu, moe_b_gu[l][:, None, :],
                      moe_w_down, moe_b_down[l][:, None, :])
        x = _combine(ys, pos, topw, x1, mod, n_ctx, dec_seq)

    y_ctx, y_lat = x
    return (y_ctx.reshape(batch, seq, d), y_lat.reshape(dec_batch, dec_seq, d),
            jnp.stack(new_dk, axis=1), jnp.stack(new_dv, axis=1), jnp.stack(new_ckv, axis=1),
            jnp.stack(new_kr, axis=1))
```

```python
import functools
import math

import jax
import jax.numpy as jnp
import numpy as np
from jax import lax
from jax.experimental import pallas as pl
from jax.experimental.pallas import tpu as pltpu

F32 = jnp.float32
BF16 = jnp.bfloat16

D_MODEL = 1024
N_POOL_GROUPS = 4
POOL_GROUP = D_MODEL // N_POOL_GROUPS
DIFF_HD = 64
DIFF_HEADS = 8
HEAD_W = 128
MLA_HEADS = 8
MLA_NOPE = 128
MLA_ROPE = 64
MLA_QK = MLA_NOPE + MLA_ROPE
MLA_W = 256
MLA_Q_RANK = 768
MLA_KV_RANK = 512
N_EXPERTS = 32
TOP_K = 4
D_FF = 1024
EXPERT_LANES = 128
SWIGLU_ALPHA = 1.702
SWIGLU_LIMIT = 7.0
ROPE_THETA = 10000.0
NORM_EPS = 1e-6
GRID_W = 64
LOG2E = math.log2(math.e)
VMEM_LIMIT = 56 * 1024 * 1024

ROW_BLOCK = 256
POOL_HALO = 16
MOE_TILE = 512
DISPATCH_ROWS = 512
COMBINE_ROWS = 256
WAIT_UNROLL = 16
ISSUE_UNROLL = 4
DIFF_Q_BLOCK = 1024
MLA_Q_BLOCK = 2048
ATTN_K_CHUNK = 512

_NT = (((1,), (1,)), ((), ()))
_TN = (((0,), (0,)), ((), ()))


def _cparams(*sem):
    return pltpu.CompilerParams(dimension_semantics=sem, vmem_limit_bytes=VMEM_LIMIT)


def _resident(shape):
    nd = len(shape)
    return pl.BlockSpec(shape, lambda *_: (0,) * nd, pipeline_mode=pl.Buffered(1))


def _sigmoid(x):
    return 1.0 / (1.0 + jnp.exp(-x))


def _lane_iota(shape):
    return lax.broadcasted_iota(jnp.int32, shape, len(shape) - 1)


def _rms(x, width):
    ss = jnp.sum(x * x, axis=-1, keepdims=True)
    return x * lax.rsqrt(ss * (1.0 / width) + NORM_EPS)


def _rope(x, cos, sin):
    n = x.shape[-1]
    lane = _lane_iota(x.shape)
    partner = jnp.where((lane & 32) != 0, pltpu.roll(x, 32, 1), pltpu.roll(x, n - 32, 1))
    return x * cos + partner * sin


def _modulated_norm(x, shift, scale):
    return _rms(x, D_MODEL) * (1.0 + scale) + shift


def _ada_kernel(cond_ref, w_ref, b_ref, o_ref):
    cnd = cond_ref[...]
    act = cnd * _sigmoid(cnd)
    o_ref[0] = jnp.dot(act, w_ref[0], preferred_element_type=F32) + b_ref[0]


def _ada(cond8, w_ada, b_ada):
    depth = w_ada.shape[0]
    n_chunk = w_ada.shape[2] // D_MODEL
    return pl.pallas_call(
        _ada_kernel,
        grid=(depth, n_chunk),
        in_specs=[
            pl.BlockSpec((8, D_MODEL), lambda l, j: (0, 0)),
            pl.BlockSpec((1, D_MODEL, D_MODEL), lambda l, j: (l, 0, j)),
            pl.BlockSpec((1, 1, D_MODEL), lambda l, j: (l, 0, j)),
        ],
        out_specs=pl.BlockSpec((1, 8, D_MODEL), lambda l, j: (l, 0, j)),
        out_shape=jax.ShapeDtypeStruct((depth, 8, n_chunk * D_MODEL), F32),
        compiler_params=_cparams("arbitrary", "arbitrary"),
        name="ada",
    )(cond8, w_ada, b_ada.reshape(depth, 1, -1))


def _diff_qk_norm(u, gain, cos, sin):
    outs = []
    for h in range(DIFF_HEADS):
        c = u[:, h * HEAD_W:(h + 1) * HEAD_W]
        lane = _lane_iota(c.shape)
        sq = c * c
        lo = jnp.sum(jnp.where(lane < DIFF_HD, sq, 0.0), axis=-1, keepdims=True)
        hi = jnp.sum(jnp.where(lane >= DIFF_HD, sq, 0.0), axis=-1, keepdims=True)
        ms = jnp.where(lane < DIFF_HD, lo, hi) * (1.0 / DIFF_HD)
        y = c * lax.rsqrt(ms + NORM_EPS) * gain[:, h * HEAD_W:(h + 1) * HEAD_W]
        outs.append((y, _rope(y, cos, sin)))
    return outs


def _token_block(ctx_ref, lat_ref, n_ctx_blocks):
    return jnp.where(pl.program_id(0) < n_ctx_blocks, ctx_ref[...], lat_ref[...])


def _rope_block(cos_ref, sin_ref, n_ctx_blocks):
    is_ctx = pl.program_id(0) < n_ctx_blocks
    return jnp.where(is_ctx, 1.0, cos_ref[...]), jnp.where(is_ctx, 0.0, sin_ref[...])


def _proj_a_kernel(xc_ref, xl_ref, mod_ref, w_ref, wg_ref, gq_ref, gk_ref, cos_ref, sin_ref,
                   pool_ref, dq_ref, dkf_ref, dkb_ref, dvf_ref, dvb_ref, gate_ref, *, n_ctx_blocks):
    mod = mod_ref[0]
    h = _modulated_norm(_token_block(xc_ref, xl_ref, n_ctx_blocks), mod[0:1], mod[1:2]).astype(BF16)
    cos, sin = _rope_block(cos_ref, sin_ref, n_ctx_blocks)

    def seg(k):
        ref, first = (w_ref, k) if k < 4 else (wg_ref, k - 4)
        return jnp.dot(h, ref[:, first * D_MODEL:(first + 1) * D_MODEL], preferred_element_type=F32)

    u_pool, u_dq = seg(0), seg(1)
    pool_ref[...] = u_pool.astype(BF16)
    u_dk = seg(2)
    q_scale = DIFF_HD ** -0.5 * LOG2E
    for hd, (_, roped) in enumerate(_diff_qk_norm(u_dq, gq_ref[...], cos, sin)):
        dq_ref[:, hd * HEAD_W:(hd + 1) * HEAD_W] = (roped * q_scale).astype(BF16)
    dv = seg(3)
    for hd, (plain, roped) in enumerate(_diff_qk_norm(u_dk, gk_ref[...], cos, sin)):
        dkf_ref[:, hd * HEAD_W:(hd + 1) * HEAD_W] = plain
        dkb_ref[:, hd * HEAD_W:(hd + 1) * HEAD_W] = roped.astype(BF16)
    u_gate = seg(4)
    dvf_ref[...] = dv
    dvb_ref[...] = dv.astype(BF16)
    for r in range(3):
        u_next = seg(5 + r) if r < 2 else None
        gate_ref[:, r * D_MODEL:(r + 1) * D_MODEL] = _sigmoid(u_gate).astype(BF16)
        u_gate = u_next


def _mla_heads(u, gain, cos, sin, rope):
    outs = []
    for h in range(MLA_HEADS):
        c = u[:, h * MLA_W:(h + 1) * MLA_W]
        y = _rms(c, MLA_QK) * gain[:, h * MLA_W:(h + 1) * MLA_W]
        if rope:
            y = jnp.concatenate([y[:, :MLA_NOPE], _rope(y[:, MLA_NOPE:], cos, sin)], axis=-1)
        outs.append(y)
    return outs


def _mla_kv(ckv_n, krw, wkn_ref, wv_ref, gk, cos, sin, rope, k_ref, v_ref):
    kn = jnp.dot(ckv_n, wkn_ref[...], preferred_element_type=F32)
    v_ref[...] = jnp.dot(ckv_n, wv_ref[...], preferred_element_type=F32).astype(BF16)
    kfull = jnp.concatenate([kn[:, h * MLA_W:(h + 1) * MLA_W] + krw for h in range(MLA_HEADS)], axis=-1)
    for hd, y in enumerate(_mla_heads(kfull, gk, cos, sin, rope)):
        k_ref[:, hd * MLA_W:(hd + 1) * MLA_W] = y.astype(BF16)


def _proj_b_kernel(xc_ref, xl_ref, mod_ref, w_ref, gqa_ref, gkva_ref, wuq_ref, wkn_ref, wv_ref, gq_ref, gk_ref,
                   cos_ref, sin_ref, mq_ref, ckv_ref, kr_ref, mk_ref, mv_ref, *, n_ctx_blocks):
    mod = mod_ref[0]
    h = _modulated_norm(_token_block(xc_ref, xl_ref, n_ctx_blocks), mod[0:1], mod[1:2]).astype(BF16)
    cos, sin = _rope_block(cos_ref, sin_ref, n_ctx_blocks)
    cq = jnp.dot(h, w_ref[:, :MLA_Q_RANK], preferred_element_type=F32)
    ckv = jnp.dot(h, w_ref[:, MLA_Q_RANK:MLA_Q_RANK + MLA_KV_RANK], preferred_element_type=F32)
    krw = jnp.dot(h, w_ref[:, MLA_Q_RANK + MLA_KV_RANK:], preferred_element_type=F32)
    cq_n = (_rms(cq, MLA_Q_RANK) * gqa_ref[...]).astype(BF16)
    mq = jnp.dot(cq_n, wuq_ref[...], preferred_element_type=F32)
    ckv_n = _rms(ckv, MLA_KV_RANK) * gkva_ref[...]
    ckv_ref[...] = ckv_n
    kr_ref[...] = krw[:, MLA_NOPE:MLA_NOPE + MLA_ROPE]
    _mla_kv(ckv_n.astype(BF16), krw, wkn_ref, wv_ref, gk_ref[...], cos, sin, True, mk_ref, mv_ref)
    q_scale = MLA_QK ** -0.5 * LOG2E
    for hd, y in enumerate(_mla_heads(mq, gq_ref[...], cos, sin, True)):
        mq_ref[:, hd * MLA_W:(hd + 1) * MLA_W] = (y * q_scale).astype(BF16)


def _cache_kv_kernel(ckv_ref, krw_ref, wkn_ref, wv_ref, gk_ref, mk_ref, mv_ref):
    _mla_kv(ckv_ref[...].astype(BF16), krw_ref[...], wkn_ref, wv_ref, gk_ref[...], None, None, False,
            mk_ref, mv_ref)


def _mod_row(n_ctx_blocks, blocks_per_lat):
    return lambda i: jnp.where(i < n_ctx_blocks, 0, 1 + (i - n_ctx_blocks) // blocks_per_lat)


def _mod_spec(mod_row):
    return pl.BlockSpec((1, 8, D_MODEL), lambda i: (mod_row(i), 0, 0))


def _rows(width):
    return pl.BlockSpec((ROW_BLOCK, width), lambda i: (i, 0))


def _ctx_rows(width, n_ctx_blocks):
    return pl.BlockSpec((ROW_BLOCK, width), lambda i: (jnp.minimum(i, n_ctx_blocks - 1), 0))


def _lat_rows(width, n_ctx_blocks):
    return pl.BlockSpec((ROW_BLOCK, width), lambda i: (jnp.maximum(i - n_ctx_blocks, 0), 0))


def _proj_outputs(t, n_ctx_blocks, widths, dtypes, ctx_only):
    spare = lambda w: pl.BlockSpec((ROW_BLOCK, w), lambda i: (jnp.minimum(i, n_ctx_blocks), 0))
    specs = [spare(w) if c else _rows(w) for w, c in zip(widths, ctx_only)]
    shapes = [jax.ShapeDtypeStruct(((n_ctx_blocks + 1) * ROW_BLOCK if c else t, w), dt)
              for w, dt, c in zip(widths, dtypes, ctx_only)]
    return specs, shapes


def _proj_a(x, mod, w_in, w_gate, gq, gk, cos, sin, mod_spec, rope_spec, n_ctx_blocks):
    t = x[0].shape[0] + x[1].shape[0]
    w_main = pl.BlockSpec((D_MODEL, 4 * D_MODEL), lambda i: (0, 0), pipeline_mode=pl.Buffered(1))
    widths = (D_MODEL,) * 6 + (3 * D_MODEL,)
    dtypes = (BF16, BF16, F32, BF16, F32, BF16, BF16)
    out_specs, out_shape = _proj_outputs(t, n_ctx_blocks, widths, dtypes, (0, 0, 1, 0, 1, 0, 0))
    return pl.pallas_call(
        functools.partial(_proj_a_kernel, n_ctx_blocks=n_ctx_blocks),
        grid=(t // ROW_BLOCK,),
        in_specs=[_ctx_rows(D_MODEL, n_ctx_blocks), _lat_rows(D_MODEL, n_ctx_blocks), mod_spec,
                  w_main, _resident(w_gate.shape), _resident(gq.shape), _resident(gk.shape), rope_spec, rope_spec],
        out_specs=out_specs,
        out_shape=out_shape,
        compiler_params=_cparams("arbitrary"),
        name="proj_a",
    )(*x, mod, w_in, w_gate, gq, gk, cos, sin)


def _proj_b(x, mod, w_b, gqa, gkva, wuq, wkn, wv, gq, gk, cos, sin, mod_spec, rope_spec, n_ctx_blocks):
    t = x[0].shape[0] + x[1].shape[0]
    widths = (MLA_HEADS * MLA_W, MLA_KV_RANK, MLA_ROPE, MLA_HEADS * MLA_W, D_MODEL)
    dtypes = (BF16, F32, F32, BF16, BF16)
    out_specs, out_shape = _proj_outputs(t, n_ctx_blocks, widths, dtypes, (0, 1, 1, 0, 0))
    consts = (w_b, gqa, gkva, wuq, wkn, wv, gq, gk)
    return pl.pallas_call(
        functools.partial(_proj_b_kernel, n_ctx_blocks=n_ctx_blocks),
        grid=(t // ROW_BLOCK,),
        in_specs=[_ctx_rows(D_MODEL, n_ctx_blocks), _lat_rows(D_MODEL, n_ctx_blocks), mod_spec]
        + [_resident(a.shape) for a in consts] + [rope_spec, rope_spec],
        out_specs=out_specs,
        out_shape=out_shape,
        compiler_params=_cparams("arbitrary"),
        name="proj_b",
    )(*x, mod, *consts, cos, sin)


def _cache_kv(ckv, krw, wkn, wv, gk):
    n = ckv.shape[0]
    widths = (MLA_HEADS * MLA_W, D_MODEL)
    consts = (wkn, wv, gk)
    return pl.pallas_call(
        _cache_kv_kernel,
        grid=(n // ROW_BLOCK,),
        in_specs=[_rows(MLA_KV_RANK), _rows(MLA_W)] + [_resident(a.shape) for a in consts],
        out_specs=[_rows(w) for w in widths],
        out_shape=[jax.ShapeDtypeStruct((n, w), BF16) for w in widths],
        compiler_params=_cparams("arbitrary"),
        name="cache_kv",
    )(ckv, krw, *consts)


def _pool_kernel(prev_ref, cur_ref, next_ref, w_ref, scale_ref, o_ref, *, n_ctx_blocks, ctx_len, lat_len):
    i = pl.program_id(0)
    rb = ROW_BLOCK
    is_ctx = i < n_ctx_blocks
    n_ctx_rows = n_ctx_blocks * rb
    seq_len = jnp.where(is_ctx, ctx_len, lat_len)
    seq_start = jnp.where(is_ctx, (i * rb // ctx_len) * ctx_len,
                          n_ctx_rows + ((i * rb - n_ctx_rows) // lat_len) * lat_len)
    n_cols = rb + 2 * POOL_HALO
    row_pos = i * rb - seq_start + lax.broadcasted_iota(jnp.int32, (rb, n_cols), 0)
    col_pos = i * rb - POOL_HALO - seq_start + lax.broadcasted_iota(jnp.int32, (rb, n_cols), 1)
    cur = cur_ref[...]
    u = jnp.concatenate([prev_ref[...], cur, next_ref[...]], axis=0)
    for g in range(N_POOL_GROUPS):
        cols = slice(g * POOL_GROUP, (g + 1) * POOL_GROUP)
        window = 2 << g
        lo = jnp.clip(row_pos - window // 2, 0, seq_len)
        hi = jnp.clip(row_pos - window // 2 + window, 0, seq_len)
        band = jnp.where((col_pos >= lo) & (col_pos < hi), 1.0, 0.0).astype(BF16)
        total = jnp.dot(band, u[:, cols], preferred_element_type=F32)
        cnt = (hi[:, :1] - lo[:, :1]).astype(F32)
        pooled = total / cnt - cur[:, cols].astype(F32)
        mixed = jnp.dot(pooled.astype(BF16), w_ref[g], preferred_element_type=F32)
        o_ref[:, cols] = (mixed * scale_ref[:, cols]).astype(BF16)


def _pool(u_pool, pool_w, pool_scale, n_ctx_blocks, ctx_len, lat_len):
    t = u_pool.shape[0]
    halo_per_block = ROW_BLOCK // POOL_HALO
    n_halo_blocks = t // POOL_HALO
    assert POOL_HALO >= (2 << (N_POOL_GROUPS - 1)) // 2
    kern = functools.partial(_pool_kernel, n_ctx_blocks=n_ctx_blocks, ctx_len=ctx_len, lat_len=lat_len)
    return pl.pallas_call(
        kern,
        grid=(t // ROW_BLOCK,),
        in_specs=[
            pl.BlockSpec((POOL_HALO, D_MODEL), lambda i: (jnp.maximum(i * halo_per_block - 1, 0), 0)),
            _rows(D_MODEL),
            pl.BlockSpec((POOL_HALO, D_MODEL),
                         lambda i: (jnp.minimum((i + 1) * halo_per_block, n_halo_blocks - 1), 0)),
            _resident(pool_w.shape),
            _resident(pool_scale.shape),
        ],
        out_specs=_rows(D_MODEL),
        out_shape=jax.ShapeDtypeStruct((t, D_MODEL), BF16),
        compiler_params=_cparams("arbitrary"),
        name="pool",
    )(u_pool, u_pool, u_pool, pool_w, pool_scale)


def _softmax_stats(s, m, l):
    m_new = jnp.maximum(m, jnp.max(s, axis=0, keepdims=True))
    p = jnp.exp2(s - m_new)
    alpha = jnp.exp2(m - m_new)
    return m_new, alpha * l + jnp.sum(p, axis=0, keepdims=True), alpha, p.astype(BF16)


def _accumulate(acc, alpha, p, v):
    return alpha * acc + lax.dot_general(v, p, _TN, preferred_element_type=F32)


def _online_softmax(chunks, scores, n_softmax, tq, defer_values):
    chunks = list(chunks)
    m = [jnp.full((1, tq), -jnp.inf, F32)] * n_softmax
    l = [jnp.zeros((1, tq), F32)] * n_softmax
    acc = [jnp.zeros((HEAD_W, tq), F32)] * n_softmax
    pending = None
    s_next = scores(chunks[0][0])
    for c, (_, v) in enumerate(chunks):
        s = s_next
        if c + 1 < len(chunks):
            s_next = scores(chunks[c + 1][0])
        stats = [_softmax_stats(s[j], m[j], l[j]) for j in range(n_softmax)]
        m = [st[0] for st in stats]
        l = [st[1] for st in stats]
        if pending is not None:
            acc = [_accumulate(acc[j], *pending[j]) for j in range(n_softmax)]
        pending = [(st[2], st[3], v) for st in stats]
        if not defer_values:
            acc = [_accumulate(acc[j], *pending[j]) for j in range(n_softmax)]
            pending = None
    if pending is not None:
        acc = [_accumulate(acc[j], *pending[j]) for j in range(n_softmax)]
    return list(zip(l, acc))


def _key_value_chunks(kv_refs, head, qk_width):
    k_cols = slice(head * qk_width, (head + 1) * qk_width)
    v_cols = slice(head * HEAD_W, (head + 1) * HEAD_W)
    for k_ref, v_ref in zip(kv_refs[0::2], kv_refs[1::2]):
        n = k_ref.shape[0]
        size = min(n, ATTN_K_CHUNK)
        for start in range(0, n, size):
            yield k_ref[start:start + size, k_cols].astype(BF16), v_ref[start:start + size, v_cols].astype(BF16)


def _diff_attn_kernel(lam_ref, q_ref, *refs, n_kv, heads):
    g_ref, o_ref = refs[n_kv], refs[-1]
    for h in range(heads):
        cols = slice(h * HEAD_W, (h + 1) * HEAD_W)
        q = q_ref[:, cols]
        lane = _lane_iota(q.shape)
        q1 = jnp.where(lane < DIFF_HD, q, jnp.zeros_like(q))
        q2 = jnp.where(lane >= DIFF_HD, q, jnp.zeros_like(q))

        def scores(k):
            return (lax.dot_general(k, q1, _NT, preferred_element_type=F32),
                    lax.dot_general(k, q2, _NT, preferred_element_type=F32))

        (l1, a1), (l2, a2) = _online_softmax(_key_value_chunks(refs[:n_kv], h, HEAD_W), scores, 2, q.shape[0],
                                             defer_values=False)
        o = (a1 / l1 - lam_ref[0] * (a2 / l2)).T
        o_ref[:, cols] = (_rms(o, HEAD_W) * g_ref[...]).astype(o_ref.dtype)


def _mla_attn_kernel(q_ref, *refs, n_kv, heads):
    o_ref = refs[-1]
    for h in range(heads):
        q = q_ref[:, h * MLA_W:(h + 1) * MLA_W]
        ((l, acc),) = _online_softmax(
            _key_value_chunks(refs[:n_kv], h, MLA_W),
            lambda k: (lax.dot_general(k, q, _NT, preferred_element_type=F32),), 1, q.shape[0],
            defer_values=True)
        o_ref[:, h * HEAD_W:(h + 1) * HEAD_W] = (acc / l).T.astype(o_ref.dtype)


def _attention(kernel, q, k, v, cache_k, cache_v, lead, lead_specs, tail, tail_specs, qk_width, tq, geom, name):
    n_ctx, seq, dec_seq = geom
    t = q.shape[0]
    n_heads = v.shape[1] // HEAD_W
    dec_batch, past, _ = cache_k.shape
    nq, q0, k0 =dec_seq // tq, n_ctx // tq, n_ctx // dec_seq
    assert n_ctx % tq == 0 and n_ctx % dec_seq == 0 and dec_seq % tq == 0
    params = _cparams("arbitrary", "arbitrary", "arbitrary")
    seq_block = lambda w: pl.BlockSpec((seq, n_heads * w), lambda b, h, qi: (b, 0))
    ctx = pl.pallas_call(
        functools.partial(kernel, n_kv=2, heads=n_heads),
        grid=(n_ctx // seq, 1, 1),
        in_specs=lead_specs + [seq_block(qk_width), seq_block(qk_width), seq_block(HEAD_W)] + tail_specs,
        out_specs=seq_block(HEAD_W),
        out_shape=jax.ShapeDtypeStruct((n_ctx, n_heads * HEAD_W), BF16),
        compiler_params=params,
        name=name + "_ctx",
    )(*lead, q, k, v, *tail)
    lat_rows = lambda w, first: pl.BlockSpec((tq, w), lambda b, h, qi: (first + b * nq + qi, h))
    cached = lambda w: pl.BlockSpec((None, past, w), lambda b, h, qi: (b, 0, h))
    lat_keys = lambda w: pl.BlockSpec((dec_seq, w), lambda b, h, qi: (k0 + b, h))
    lat = pl.pallas_call(
        functools.partial(kernel, n_kv=4, heads=1),
        grid=(dec_batch, n_heads, nq),
        in_specs=lead_specs + [lat_rows(qk_width, q0), cached(qk_width), cached(HEAD_W), lat_keys(qk_width),
                               lat_keys(HEAD_W)] + tail_specs,
        out_specs=lat_rows(HEAD_W, 0),
        out_shape=jax.ShapeDtypeStruct((t - n_ctx, n_heads * HEAD_W), BF16),
        compiler_params=params,
        name=name + "_lat",
    )(*lead, q, cache_k, cache_v, k, v, *tail)
    return ctx, lat


def _diff_attention(q, k, v, cache_k, cache_v, lam, out_gain, geom):
    lead_specs = [pl.BlockSpec(memory_space=pltpu.SMEM)]
    tail_specs = [pl.BlockSpec((1, HEAD_W), lambda b, h, qi: (0, 0))]
    return _attention(_diff_attn_kernel, q, k, v, cache_k, cache_v, (lam,), lead_specs, (out_gain,), tail_specs,
                      HEAD_W, DIFF_Q_BLOCK, geom, "diff")


def _mla_attention(q, k, v, cache_k, cache_v, geom):
    return _attention(_mla_attn_kernel, q, k, v, cache_k, cache_v, (), [], (), [], MLA_W, MLA_Q_BLOCK, geom, "mla")


def _merge_kernel(xc_ref, xl_ref, mod_ref, yp_ref, ydc_ref, ydl_ref, ymc_ref, yml_ref, gate_ref, wb_ref, wo_ref,
                  rw_ref, rb_ref, x1_ref, h2_ref, topw_ref, topi_ref, rank_ref, count_ref, carry_ref, logits_ref,
                  *, n_ctx_blocks):
    step = pl.program_id(0)

    @pl.when(step == 0)
    def _():
        carry_ref[...] = jnp.zeros_like(carry_ref)
        logits_ref[...] = jnp.zeros_like(logits_ref)

    is_ctx = step < n_ctx_blocks
    mod = mod_ref[0]
    merged = jnp.zeros(xc_ref.shape, F32)
    branches = (yp_ref[...], jnp.where(is_ctx, ydc_ref[...], ydl_ref[...]),
                jnp.where(is_ctx, ymc_ref[...], yml_ref[...]))
    for r, y in enumerate(branches):
        z = jnp.dot(y, wb_ref[r], preferred_element_type=F32)
        merged = merged + gate_ref[:, r * D_MODEL:(r + 1) * D_MODEL].astype(F32) * z
    y = jnp.dot(merged.astype(BF16), wo_ref[...], preferred_element_type=F32)
    x1 = jnp.where(is_ctx, xc_ref[...], xl_ref[...]) + mod[2:3] * y
    x1_ref[...] = x1
    h2 = _modulated_norm(x1, mod[3:4], mod[4:5])
    h2_ref[...] = h2
    new_logits = jnp.dot(h2, rw_ref[...], preferred_element_type=F32) + rb_ref[...]
    logits = logits_ref[...]
    logits_ref[...] = new_logits
    lane = _lane_iota(logits.shape)
    picks = []
    for _ in range(TOP_K):
        mx = jnp.max(logits, axis=-1, keepdims=True)
        first = jnp.min(jnp.where(logits == mx, lane, EXPERT_LANES), axis=-1, keepdims=True)
        hit = lane == first
        picks.append((mx, first, hit))
        logits = jnp.where(hit, -jnp.inf, logits)
    exps = [jnp.exp(v - picks[0][0]) for v, _, _ in picks]
    denom = exps[0] + exps[1] + exps[2] + exps[3]
    rows = logits.shape[0]
    chosen = jnp.zeros(logits.shape, F32)
    for _, _, hit in picks:
        chosen = chosen + jnp.where(hit, 1.0, 0.0)
    earlier = jnp.where(lax.broadcasted_iota(jnp.int32, (rows, rows), 1)
                        < lax.broadcasted_iota(jnp.int32, (rows, rows), 0), 1.0, 0.0).astype(BF16)
    before = jnp.dot(earlier, chosen.astype(BF16), preferred_element_type=F32) + carry_ref[0:1, :]
    topw = jnp.zeros(logits.shape, F32)
    topi = jnp.zeros(logits.shape, jnp.int32)
    rank = jnp.zeros(logits.shape, jnp.int32)
    for k, (e, (_, first, hit)) in enumerate(zip(exps, picks)):
        slot_rank = jnp.sum(jnp.where(hit, before, 0.0), axis=-1, keepdims=True)
        topw = jnp.where(lane == k, e / denom, topw)
        topi = jnp.where(lane == k, first, topi)
        rank = jnp.where(lane == k, slot_rank.astype(jnp.int32), rank)
    topw_ref[...] = topw
    topi_ref[...] = topi
    rank_ref[...] = rank
    routed = jnp.where(step > 0, jnp.sum(chosen, axis=0, keepdims=True), 0.0)
    carry_ref[...] = carry_ref[...] + routed
    count_ref[...] = carry_ref[...]


def _merge(x, mod, y_pool, y_diff, y_mla, gates, wb, wo, rw, rb, mod_row, n_ctx_blocks):
    t = x[0].shape[0] + x[1].shape[0]
    nb = t // ROW_BLOCK
    consts = (wb, wo, rw, rb)
    lanes = EXPERT_LANES
    cur = lambda i: jnp.minimum(i, nb - 1)
    rows = lambda w: pl.BlockSpec((ROW_BLOCK, w), lambda i: (cur(i), 0))
    routed = pl.BlockSpec((ROW_BLOCK, lanes), lambda i: (jnp.maximum(i - 1, 0), 0))
    ctx_rows = _ctx_rows(D_MODEL, n_ctx_blocks)
    lat_rows = pl.BlockSpec((ROW_BLOCK, D_MODEL), lambda i: (jnp.maximum(cur(i) - n_ctx_blocks, 0), 0))
    return pl.pallas_call(
        functools.partial(_merge_kernel, n_ctx_blocks=n_ctx_blocks),
        grid=(nb + 1,),
        in_specs=[ctx_rows, lat_rows, _mod_spec(lambda i: mod_row(cur(i))), rows(D_MODEL), ctx_rows, lat_rows,
                  ctx_rows, lat_rows, rows(3 * D_MODEL)] + [_resident(a.shape) for a in consts],
        out_specs=[rows(D_MODEL), rows(D_MODEL), routed, routed, routed,
                   pl.BlockSpec((8, lanes), lambda i: (0, 0))],
        out_shape=[jax.ShapeDtypeStruct((t, D_MODEL), F32), jax.ShapeDtypeStruct((t, D_MODEL), F32),
                   jax.ShapeDtypeStruct((t, lanes), F32), jax.ShapeDtypeStruct((t, lanes), jnp.int32),
                   jax.ShapeDtypeStruct((t, lanes), jnp.int32), jax.ShapeDtypeStruct((8, lanes), F32)],
        scratch_shapes=[pltpu.VMEM((8, lanes), F32), pltpu.VMEM((ROW_BLOCK, lanes), F32)],
        compiler_params=_cparams("arbitrary"),
        name="merge",
    )(*x, mod, y_pool, *y_diff, *y_mla, gates, *consts)


def _slot_layout(counts, topi, rank):
    tile = MOE_TILE
    cnt = counts[0, :N_EXPERTS].astype(jnp.int32)
    padded = (cnt + tile - 1) // tile * tile
    end = jnp.cumsum(padded)
    start = end - padded
    experts = jnp.arange(N_EXPERTS, dtype=jnp.int32)
    idx4 = topi[:, :TOP_K]
    slot_start = jnp.sum(jnp.where(idx4[:, :, None] == experts, start, 0), axis=-1)
    pos = (slot_start + rank[:, :TOP_K]).reshape(-1)
    n_tiles = (topi.shape[0] * TOP_K + N_EXPERTS * tile) // tile
    tile_row = jnp.arange(n_tiles, dtype=jnp.int32) * tile
    tile_expert = jnp.minimum(jnp.sum(tile_row[:, None] >= end, axis=-1), N_EXPERTS - 1).astype(jnp.int32)
    n_used = (end[-1] // tile).astype(jnp.int32)
    tile_src = jnp.minimum(jnp.arange(n_tiles, dtype=jnp.int32), n_used - 1)
    ragged_last = jnp.any((cnt % tile != 0) & (tile_row[:, None] == end - tile), axis=-1)
    fill_tile = (ragged_last | (tile_row >= end[-1])).astype(jnp.int32)
    return pos, tile_expert, tile_src, n_used.reshape(1), fill_tile


def _row_copy(src, src_row, dst, dst_row, sem):
    return pltpu.make_async_copy(src.at[pl.ds(src_row, 1)], dst.at[pl.ds(dst_row, 1)], sem)


def _dispatch_kernel(fill_tile_ref, pos_ref, h_ref, hs_ref, zero_ref, src_ref, fill_sem, fetch_sems, row_sems):
    i = pl.program_id(0)
    n_steps = pl.num_programs(0)
    slots = DISPATCH_ROWS * TOP_K

    def fetch(step, buf):
        rows = pl.ds(pl.multiple_of(step * DISPATCH_ROWS, DISPATCH_ROWS), DISPATCH_ROWS)
        return pltpu.make_async_copy(h_ref.at[rows], src_ref.at[buf], fetch_sems.at[buf])

    @pl.when(i == 0)
    def _():
        fetch(0, 0).start()
        fetch(1, 1).start()

    @pl.when(i == 0)
    def _():
        zero_ref[...] = jnp.zeros_like(zero_ref)

        def fill_copy(tile):
            row = pl.multiple_of(tile * MOE_TILE, MOE_TILE)
            return pltpu.make_async_copy(zero_ref, hs_ref.at[pl.ds(row, MOE_TILE)], fill_sem)

        def start_fill(tile, carry):
            @pl.when(fill_tile_ref[tile] != 0)
            def _():
                fill_copy(tile).start()
            return carry

        def wait_fill(tile, carry):
            @pl.when(fill_tile_ref[tile] != 0)
            def _():
                fill_copy(tile).wait()
            return carry

        lax.fori_loop(0, hs_ref.shape[0] // MOE_TILE, start_fill, 0)
        lax.fori_loop(0, hs_ref.shape[0] // MOE_TILE, wait_fill, 0)

    buf = i % 3
    src = src_ref.at[buf]
    fetch(i, buf).wait()

    def issue(r, carry):
        slot = (i * DISPATCH_ROWS + r) * TOP_K
        for k in range(TOP_K):
            _row_copy(src, r, hs_ref, pos_ref[slot + k], row_sems.at[i % 2]).start()
        return carry

    def drain(sem):
        def body(j, carry):
            for _ in range(WAIT_UNROLL):
                _row_copy(src, 0, hs_ref, 0, sem).wait()
            return carry

        lax.fori_loop(0, slots // WAIT_UNROLL, body, 0)

    lax.fori_loop(0, DISPATCH_ROWS, issue, 0, unroll=ISSUE_UNROLL)

    @pl.when(i > 0)
    def _():
        drain(row_sems.at[(i - 1) % 2])

    @pl.when(i + 2 < n_steps)
    def _():
        fetch(i + 2, (i + 2) % 3).start()

    @pl.when(i == n_steps - 1)
    def _():
        drain(row_sems.at[i % 2])


def _dispatch(h2, pos, fill_tile):
    t = h2.shape[0]
    n_rows = fill_tile.shape[0] * MOE_TILE
    return pl.pallas_call(
        _dispatch_kernel,
        grid_spec=pltpu.PrefetchScalarGridSpec(
            num_scalar_prefetch=2,
            grid=(t // DISPATCH_ROWS,),
            in_specs=[pl.BlockSpec(memory_space=pl.ANY)],
            out_specs=pl.BlockSpec(memory_space=pl.ANY),
            scratch_shapes=[pltpu.VMEM((MOE_TILE, D_MODEL), F32), pltpu.VMEM((3, DISPATCH_ROWS, D_MODEL), F32),
                            pltpu.SemaphoreType.DMA, pltpu.SemaphoreType.DMA((3,)),
                            pltpu.SemaphoreType.DMA((2,))],
        ),
        out_shape=jax.ShapeDtypeStruct((n_rows, D_MODEL), F32),
        compiler_params=_cparams("arbitrary"),
        name="dispatch",
    )(fill_tile, pos, h2)


def _expert_kernel(tile_expert_ref, tile_src_ref, n_used_ref, hs_ref, wgu_ref, bgu_ref, wd_ref, bd_ref, ys_ref,
                   wgu_bf_ref, wd_bf_ref):
    i = pl.program_id(0)
    occupied = i < n_used_ref[0]
    new_expert = (i == 0) | (tile_expert_ref[i] != tile_expert_ref[jnp.maximum(i - 1, 0)])

    @pl.when(occupied & new_expert)
    def _():
        wgu_bf_ref[...] = wgu_ref[...].astype(BF16)
        wd_bf_ref[...] = wd_ref[...].astype(BF16)

    @pl.when(occupied)
    def _():
        gu = jnp.dot(hs_ref[...].astype(BF16), wgu_bf_ref[...], preferred_element_type=F32) + bgu_ref[0]
        gate = jnp.minimum(gu[:, :D_FF], SWIGLU_LIMIT)
        up = jnp.clip(gu[:, D_FF:], -SWIGLU_LIMIT, SWIGLU_LIMIT)
        act = (up + 1.0) * gate * _sigmoid(SWIGLU_ALPHA * gate)
        ys_ref[...] = jnp.dot(act.astype(BF16), wd_bf_ref[...], preferred_element_type=F32) + bd_ref[0]

    @pl.when(jnp.logical_not(occupied))
    def _():
        ys_ref[...] = jnp.zeros_like(ys_ref)


def _experts(hs, tile_expert, tile_src, n_used, layer, wgu, bgu, wd, bd):
    n_rows = hs.shape[0]
    rows = lambda i, te, ts, nu: (ts[i], 0)
    bias = lambda i, te, ts, nu: (te[i], 0, 0)
    weight = lambda i, te, ts, nu: (layer, te[i], 0, 0)
    return pl.pallas_call(
        _expert_kernel,
        grid_spec=pltpu.PrefetchScalarGridSpec(
            num_scalar_prefetch=3,
            grid=(n_rows // MOE_TILE,),
            in_specs=[
                pl.BlockSpec((MOE_TILE, D_MODEL), rows),
                pl.BlockSpec((None, None, D_MODEL, 2 * D_FF), weight),
                pl.BlockSpec((1, 1, 2 * D_FF), bias),
                pl.BlockSpec((None, None, D_FF, D_MODEL), weight),
                pl.BlockSpec((1, 1, D_MODEL), bias),
            ],
            out_specs=pl.BlockSpec((MOE_TILE, D_MODEL), lambda i, te, ts, nu: (i, 0)),
            scratch_shapes=[pltpu.VMEM((D_MODEL, 2 * D_FF), BF16), pltpu.VMEM((D_FF, D_MODEL), BF16)],
        ),
        out_shape=jax.ShapeDtypeStruct((n_rows, D_MODEL), F32),
        compiler_params=_cparams("arbitrary"),
        name="experts",
    )(tile_expert, tile_src, n_used, hs, wgu, bgu, wd, bd)


def _combine_kernel(pos_ref, ys_ref, w_ref, x1_ref, mod_ref, *refs, n_ctx_blocks):
    oc_ref, ol_ref, buf_ref, sems = refs
    i = pl.program_id(0)
    slots = COMBINE_ROWS * TOP_K

    def gather(step, buf):
        def issue(r, carry):
            token = step * COMBINE_ROWS + r
            for k in range(TOP_K):
                _row_copy(ys_ref, pos_ref[token * TOP_K + k], buf_ref.at[buf, k], r, sems.at[buf]).start()
            return carry

        lax.fori_loop(0, COMBINE_ROWS, issue, 0, unroll=ISSUE_UNROLL)

    @pl.when(i == 0)
    def _():
        gather(0, 0)

    @pl.when(i + 1 < pl.num_programs(0))
    def _():
        gather(i + 1, (i + 1) % 2)

    cur = i % 2

    def drain(j, carry):
        for _ in range(WAIT_UNROLL):
            _row_copy(ys_ref, 0, buf_ref.at[cur, 0], 0, sems.at[cur]).wait()
        return carry

    lax.fori_loop(0, slots // WAIT_UNROLL, drain, 0)
    w = w_ref[...]
    acc = w[:, 0:1] * buf_ref[cur, 0]
    for k in range(1, TOP_K):
        acc = acc + w[:, k:k + 1] * buf_ref[cur, k]
    result = x1_ref[...] + mod_ref[0][5:6] * acc

    @pl.when(i < n_ctx_blocks)
    def _():
        oc_ref[...] = result

    @pl.when(i >= n_ctx_blocks)
    def _():
        ol_ref[...] = result


def _combine(ys, pos, topw, x1, mod, n_ctx_rows, lat_len):
    t = x1.shape[0]
    rb = COMBINE_ROWS
    n_ctx_blocks = n_ctx_rows // rb

    def mod_row(i, pos_ref):
        return (jnp.where(i < n_ctx_blocks, 0, 1 + (i - n_ctx_blocks) // (lat_len // rb)), 0, 0)

    out_specs = [pl.BlockSpec((rb, D_MODEL), lambda i, pos_ref: (jnp.minimum(i, n_ctx_blocks - 1), 0)),
                 pl.BlockSpec((rb, D_MODEL), lambda i, pos_ref: (jnp.maximum(i - n_ctx_blocks, 0), 0))]
    out_shape = [jax.ShapeDtypeStruct((n_ctx_rows, D_MODEL), F32),
                 jax.ShapeDtypeStruct((t - n_ctx_rows, D_MODEL), F32)]
    return pl.pallas_call(
        functools.partial(_combine_kernel, n_ctx_blocks=n_ctx_blocks),
        grid_spec=pltpu.PrefetchScalarGridSpec(
            num_scalar_prefetch=1,
            grid=(t // rb,),
            in_specs=[
                pl.BlockSpec(memory_space=pl.ANY),
                pl.BlockSpec((rb, EXPERT_LANES), lambda i, pos_ref: (i, 0)),
                pl.BlockSpec((rb, D_MODEL), lambda i, pos_ref: (i, 0)),
                pl.BlockSpec((1, 8, D_MODEL), mod_row),
            ],
            out_specs=out_specs,
            scratch_shapes=[pltpu.VMEM((2, TOP_K, rb, D_MODEL), F32), pltpu.SemaphoreType.DMA((2,))],
        ),
        out_shape=out_shape,
        compiler_params=_cparams("arbitrary"),
        name="combine",
    )(pos, ys, topw, x1, mod)


def _rope_tables(n_lat_tokens, rot_dim):
    n_rows = n_lat_tokens // GRID_W
    row = jnp.repeat(jnp.arange(n_rows, dtype=F32), GRID_W)
    col = jnp.tile(jnp.arange(GRID_W, dtype=F32), n_rows)
    n_freq = rot_dim // 4
    inv_freq = ROPE_THETA ** (-jnp.arange(n_freq, dtype=F32) / n_freq)
    ang = jnp.concatenate([row[:, None] * inv_freq, col[:, None] * inv_freq], axis=-1)
    return jnp.cos(ang), jnp.sin(ang)


def _pack_heads(w, n_heads, lo, hi, width):
    k = w.shape[0]
    per_head = w.shape[1] // n_heads
    part = w.reshape(k, n_heads, per_head)[:, :, lo:hi]
    part = jnp.pad(part, ((0, 0), (0, 0), (0, width - (hi - lo))))
    return part.reshape(k, n_heads * width)


def kernel(x_prompt, x_sample, c, cache_diff_k, cache_diff_v, cache_mla_ckv, cache_mla_krope, c_ctx, w_ada, b_ada,
           w_in, pool_w, pool_scale, diff_q_norm, diff_k_norm, diff_lambda, diff_out_norm, mla_q_a_norm, w_uq,
           mla_kv_a_norm, w_ukv, mla_q_norm, mla_k_norm, w_branch, w_out, router_w, router_b, moe_w_gu, moe_b_gu,
           moe_w_down, moe_b_down):
    batch, seq, d = x_prompt.shape
    dec_batch, dec_seq, _ = x_sample.shape
    depth = w_ada.shape[0]
    past = cache_diff_k.shape[2]
    n_ctx = batch * seq
    n_lat = dec_batch * dec_seq
    n_ctx_blocks = n_ctx // ROW_BLOCK
    mod_row = _mod_row(n_ctx_blocks, dec_seq // ROW_BLOCK)
    mod_spec = _mod_spec(mod_row)

    x = (x_prompt.reshape(n_ctx, d), x_sample.reshape(n_lat, d))
    cond8 = jnp.concatenate([c_ctx[None], c, jnp.zeros((8 - 1 - dec_batch, d), F32)], axis=0)
    mod_all = _ada(cond8, w_ada, b_ada).reshape(depth, 8, 6, d)[:, :1 + dec_batch]
    mod_all = jnp.pad(mod_all, ((0, 0), (0, 0), (0, 2), (0, 0)))

    cos_d, sin_d = _rope_tables(dec_seq, DIFF_HD)
    cos_m, sin_m = _rope_tables(dec_seq, MLA_ROPE)
    cos_a = jnp.concatenate([cos_d] * 4, axis=1)
    sin_a = jnp.concatenate([-sin_d, sin_d, -sin_d, sin_d], axis=1)
    pad = MLA_W - MLA_QK
    cos_b = jnp.concatenate([cos_m, cos_m, jnp.ones((dec_seq, pad), F32)], axis=1)
    sin_b = jnp.concatenate([-sin_m, sin_m, jnp.zeros((dec_seq, pad), F32)], axis=1)
    rope_spec = pl.BlockSpec(
        (ROW_BLOCK, HEAD_W), lambda i: (jnp.maximum(i - n_ctx_blocks, 0) % (dec_seq // ROW_BLOCK), 0))

    offs = np.cumsum((0, 1024, 1024, 1024, 1024, MLA_Q_RANK, MLA_KV_RANK, MLA_ROPE, 3 * 1024))
    new_dk, new_dv, new_ckv, new_kr = [], [], [], []
    for l in range(depth):
        lam_init = 0.8 - 0.6 * math.exp(-0.3 * l)
        w_in_bf = w_in[l].astype(BF16)
        seg = [w_in_bf[:, offs[k]:offs[k + 1]] for k in range(8)]
        kr_wide = jnp.pad(seg[6], ((0, 0), (MLA_NOPE, MLA_W - MLA_NOPE - MLA_ROPE)))
        w_b = jnp.concatenate([seg[4], seg[5], kr_wide], axis=1)
        wuq = _pack_heads(w_uq[l], MLA_HEADS, 0, MLA_QK, MLA_W).astype(BF16)
        wkn = _pack_heads(w_ukv[l], MLA_HEADS, 0, MLA_NOPE, MLA_W).astype(BF16)
        wv = _pack_heads(w_ukv[l], MLA_HEADS, MLA_NOPE, MLA_NOPE + HEAD_W, HEAD_W).astype(BF16)
        gq_d = jnp.tile(diff_q_norm[l], 2 * DIFF_HEADS)[None]
        gk_d = jnp.tile(diff_k_norm[l], 2 * DIFF_HEADS)[None]
        gq_m = jnp.tile(jnp.pad(mla_q_norm[l], (0, MLA_W - MLA_QK)), MLA_HEADS)[None]
        gk_m = jnp.tile(jnp.pad(mla_k_norm[l], (0, MLA_W - MLA_QK)), MLA_HEADS)[None]
        lp = diff_lambda[l]
        lam = (jnp.exp(jnp.sum(lp[0] * lp[1])) - jnp.exp(jnp.sum(lp[2] * lp[3])) + lam_init).reshape(1)
        out_gain = (diff_out_norm[l] * (1.0 - lam_init))[None]
        mod = mod_all[l]

        u_pool, dq, dk_f, dk_b, dv_f, dv_b, gates = _proj_a(x, mod, w_in_bf, seg[7], gq_d, gk_d, cos_a, sin_a,
                                                            mod_spec, rope_spec, n_ctx_blocks)
        mq, ckv_n, kr, mk, mv = _proj_b(x, mod, w_b, mla_q_a_norm[l][None], mla_kv_a_norm[l][None], wuq, wkn, wv,
                                        gq_m, gk_m, cos_b, sin_b, mod_spec, rope_spec, n_ctx_blocks)
        new_dk.append(dk_f[:n_ctx].reshape(batch, seq, DIFF_HEADS, HEAD_W))
        new_dv.append(dv_f[:n_ctx].reshape(batch, seq, DIFF_HEADS, HEAD_W))
        new_ckv.append(ckv_n[:n_ctx].reshape(batch, seq, MLA_KV_RANK))
        new_kr.append(kr[:n_ctx].reshape(batch, seq, MLA_ROPE))

        y_pool = _pool(u_pool, pool_w[l].astype(BF16), pool_scale[l][None], n_ctx_blocks, seq, dec_seq)

        c_krw = jnp.pad(cache_mla_krope[:, l].reshape(dec_batch * past, MLA_ROPE),
                        ((0, 0), (MLA_NOPE, MLA_W - MLA_NOPE - MLA_ROPE)))
        c_mk, c_mv = _cache_kv(cache_mla_ckv[:, l].reshape(dec_batch * past, MLA_KV_RANK), c_krw, wkn, wv, gk_m)

        geom = (n_ctx, seq, dec_seq)
        y_diff = _diff_attention(dq, dk_b, dv_b, cache_diff_k[:, l].reshape(dec_batch, past, d),
                                 cache_diff_v[:, l].reshape(dec_batch, past, d), lam, out_gain, geom)
        y_mla = _mla_attention(mq, mk, mv, c_mk.reshape(dec_batch, past, -1), c_mv.reshape(dec_batch, past, -1),
                               geom)

        rw = jnp.pad(router_w[l], ((0, 0), (0, EXPERT_LANES - N_EXPERTS)))
        rb = jnp.pad(router_b[l], (0, EXPERT_LANES - N_EXPERTS), constant_values=-jnp.inf)[None]
        x1, h2, topw, topi, rank, counts = _merge(x, mod, y_pool, y_diff, y_mla, gates, w_branch[l].astype(BF16),
                                                  w_out[l].astype(BF16), rw, rb, mod_row, n_ctx_blocks)
        pos, tile_expert, tile_src, n_used, fill_tile = _slot_layout(counts, topi, rank)
        hs = _dispatch(h2, pos, fill_tile)
        ys = _experts(hs, tile_expert, tile_src, n_used, l, moe_w_gu, moe_b_gu[l][:, None, :],
                      moe_w_down, moe_b_down[l][:, None, :])
        x = _combine(ys, pos, topw, x1, mod, n_ctx, dec_seq)

    y_ctx, y_lat = x
    return (y_ctx.reshape(batch, seq, d), y_lat.reshape(dec_batch, dec_seq, d),
            jnp.stack(new_dk, axis=1), jnp.stack(new_dv, axis=1), jnp.stack(new_ckv, axis=1),
            jnp.stack(new_kr, axis=1))
```

```python
import functools
import math

import jax
import jax.numpy as jnp
import numpy as np
from jax import lax
from jax.experimental import pallas as pl
from jax.experimental.pallas import tpu as pltpu

F32 = jnp.float32
BF16 = jnp.bfloat16

D_MODEL = 1024
N_POOL_GROUPS = 4
POOL_GROUP = D_MODEL // N_POOL_GROUPS
DIFF_HD = 64
DIFF_HEADS = 8
HEAD_W = 128
MLA_HEADS = 8
MLA_NOPE = 128
MLA_ROPE = 64
MLA_QK = MLA_NOPE + MLA_ROPE
MLA_W = 256
MLA_Q_RANK = 768
MLA_KV_RANK = 512
N_EXPERTS = 32
TOP_K = 4
D_FF = 1024
EXPERT_LANES = 128
SWIGLU_ALPHA = 1.702
SWIGLU_LIMIT = 7.0
ROPE_THETA = 10000.0
NORM_EPS = 1e-6
GRID_W = 64
LOG2E = math.log2(math.e)
VMEM_LIMIT = 56 * 1024 * 1024

ROW_BLOCK = 256
POOL_HALO = 16
MOE_TILE = 512
DISPATCH_ROWS = 512
COMBINE_ROWS = 256
WAIT_UNROLL = 16
ISSUE_UNROLL = 8
DIFF_Q_BLOCK = 1024
MLA_Q_BLOCK = 2048
ATTN_K_CHUNK = 512

_NT = (((1,), (1,)), ((), ()))
_TN = (((0,), (0,)), ((), ()))


def _cparams(*sem):
    return pltpu.CompilerParams(dimension_semantics=sem, vmem_limit_bytes=VMEM_LIMIT)


def _resident(shape):
    nd = len(shape)
    return pl.BlockSpec(shape, lambda *_: (0,) * nd, pipeline_mode=pl.Buffered(1))


def _sigmoid(x):
    return 1.0 / (1.0 + jnp.exp(-x))


def _lane_iota(shape):
    return lax.broadcasted_iota(jnp.int32, shape, len(shape) - 1)


def _rms(x, width):
    ss = jnp.sum(x * x, axis=-1, keepdims=True)
    return x * lax.rsqrt(ss * (1.0 / width) + NORM_EPS)


def _rope(x, cos, sin):
    n = x.shape[-1]
    lane = _lane_iota(x.shape)
    partner = jnp.where((lane & 32) != 0, pltpu.roll(x, 32, 1), pltpu.roll(x, n - 32, 1))
    return x * cos + partner * sin


def _modulated_norm(x, shift, scale):
    return _rms(x, D_MODEL) * (1.0 + scale) + shift


def _ada_kernel(cond_ref, w_ref, b_ref, o_ref):
    cnd = cond_ref[...]
    act = cnd * _sigmoid(cnd)
    o_ref[0] = jnp.dot(act, w_ref[0], preferred_element_type=F32) + b_ref[0]


def _ada(cond8, w_ada, b_ada):
    depth = w_ada.shape[0]
    n_chunk = w_ada.shape[2] // D_MODEL
    return pl.pallas_call(
        _ada_kernel,
        grid=(depth, n_chunk),
        in_specs=[
            pl.BlockSpec((8, D_MODEL), lambda l, j: (0, 0)),
            pl.BlockSpec((1, D_MODEL, D_MODEL), lambda l, j: (l, 0, j)),
            pl.BlockSpec((1, 1, D_MODEL), lambda l, j: (l, 0, j)),
        ],
        out_specs=pl.BlockSpec((1, 8, D_MODEL), lambda l, j: (l, 0, j)),
        out_shape=jax.ShapeDtypeStruct((depth, 8, n_chunk * D_MODEL), F32),
        compiler_params=_cparams("arbitrary", "arbitrary"),
        name="ada",
    )(cond8, w_ada, b_ada.reshape(depth, 1, -1))


def _diff_qk_norm(u, gain, cos, sin):
    outs = []
    for h in range(DIFF_HEADS):
        c = u[:, h * HEAD_W:(h + 1) * HEAD_W]
        lane = _lane_iota(c.shape)
        sq = c * c
        lo = jnp.sum(jnp.where(lane < DIFF_HD, sq, 0.0), axis=-1, keepdims=True)
        hi = jnp.sum(jnp.where(lane >= DIFF_HD, sq, 0.0), axis=-1, keepdims=True)
        ms = jnp.where(lane < DIFF_HD, lo, hi) * (1.0 / DIFF_HD)
        y = c * lax.rsqrt(ms + NORM_EPS) * gain[:, h * HEAD_W:(h + 1) * HEAD_W]
        outs.append((y, _rope(y, cos, sin)))
    return outs


def _token_block(ctx_ref, lat_ref, n_ctx_blocks):
    return jnp.where(pl.program_id(0) < n_ctx_blocks, ctx_ref[...], lat_ref[...])


def _rope_block(cos_ref, sin_ref, n_ctx_blocks):
    is_ctx = pl.program_id(0) < n_ctx_blocks
    return jnp.where(is_ctx, 1.0, cos_ref[...]), jnp.where(is_ctx, 0.0, sin_ref[...])


def _proj_a_kernel(xc_ref, xl_ref, mod_ref, w_ref, wg_ref, gq_ref, gk_ref, cos_ref, sin_ref,
                   pool_ref, dq_ref, dkf_ref, dkb_ref, dvf_ref, dvb_ref, gate_ref, *, n_ctx_blocks):
    mod = mod_ref[0]
    h = _modulated_norm(_token_block(xc_ref, xl_ref, n_ctx_blocks), mod[0:1], mod[1:2]).astype(BF16)
    cos, sin = _rope_block(cos_ref, sin_ref, n_ctx_blocks)

    def seg(k):
        ref, first = (w_ref, k) if k < 4 else (wg_ref, k - 4)
        return jnp.dot(h, ref[:, first * D_MODEL:(first + 1) * D_MODEL], preferred_element_type=F32)

    u_pool, u_dq = seg(0), seg(1)
    pool_ref[...] = u_pool.astype(BF16)
    u_dk = seg(2)
    q_scale = DIFF_HD ** -0.5 * LOG2E
    for hd, (_, roped) in enumerate(_diff_qk_norm(u_dq, gq_ref[...], cos, sin)):
        dq_ref[:, hd * HEAD_W:(hd + 1) * HEAD_W] = (roped * q_scale).astype(BF16)
    dv = seg(3)
    for hd, (plain, roped) in enumerate(_diff_qk_norm(u_dk, gk_ref[...], cos, sin)):
        dkf_ref[:, hd * HEAD_W:(hd + 1) * HEAD_W] = plain
        dkb_ref[:, hd * HEAD_W:(hd + 1) * HEAD_W] = roped.astype(BF16)
    u_gate = seg(4)
    dvf_ref[...] = dv
    dvb_ref[...] = dv.astype(BF16)
    for r in range(3):
        u_next = seg(5 + r) if r < 2 else None
        gate_ref[:, r * D_MODEL:(r + 1) * D_MODEL] = _sigmoid(u_gate).astype(BF16)
        u_gate = u_next


def _mla_heads(u, gain, cos, sin, rope):
    outs = []
    for h in range(MLA_HEADS):
        c = u[:, h * MLA_W:(h + 1) * MLA_W]
        y = _rms(c, MLA_QK) * gain[:, h * MLA_W:(h + 1) * MLA_W]
        if rope:
            y = jnp.concatenate([y[:, :MLA_NOPE], _rope(y[:, MLA_NOPE:], cos, sin)], axis=-1)
        outs.append(y)
    return outs


def _mla_kv(ckv_n, krw, wkn_ref, wv_ref, gk, cos, sin, rope, k_ref, v_ref):
    kn = jnp.dot(ckv_n, wkn_ref[...], preferred_element_type=F32)
    v_ref[...] = jnp.dot(ckv_n, wv_ref[...], preferred_element_type=F32).astype(BF16)
    kfull = jnp.concatenate([kn[:, h * MLA_W:(h + 1) * MLA_W] + krw for h in range(MLA_HEADS)], axis=-1)
    for hd, y in enumerate(_mla_heads(kfull, gk, cos, sin, rope)):
        k_ref[:, hd * MLA_W:(hd + 1) * MLA_W] = y.astype(BF16)


def _proj_b_kernel(xc_ref, xl_ref, mod_ref, w_ref, gqa_ref, gkva_ref, wuq_ref, wkn_ref, wv_ref, gq_ref, gk_ref,
                   cos_ref, sin_ref, mq_ref, ckv_ref, kr_ref, mk_ref, mv_ref, *, n_ctx_blocks):
    mod = mod_ref[0]
    h = _modulated_norm(_token_block(xc_ref, xl_ref, n_ctx_blocks), mod[0:1], mod[1:2]).astype(BF16)
    cos, sin = _rope_block(cos_ref, sin_ref, n_ctx_blocks)
    cq = jnp.dot(h, w_ref[:, :MLA_Q_RANK], preferred_element_type=F32)
    ckv = jnp.dot(h, w_ref[:, MLA_Q_RANK:MLA_Q_RANK + MLA_KV_RANK], preferred_element_type=F32)
    krw = jnp.dot(h, w_ref[:, MLA_Q_RANK + MLA_KV_RANK:], preferred_element_type=F32)
    cq_n = (_rms(cq, MLA_Q_RANK) * gqa_ref[...]).astype(BF16)
    mq = jnp.dot(cq_n, wuq_ref[...], preferred_element_type=F32)
    ckv_n = _rms(ckv, MLA_KV_RANK) * gkva_ref[...]
    ckv_ref[...] = ckv_n
    kr_ref[...] = krw[:, MLA_NOPE:MLA_NOPE + MLA_ROPE]
    _mla_kv(ckv_n.astype(BF16), krw, wkn_ref, wv_ref, gk_ref[...], cos, sin, True, mk_ref, mv_ref)
    q_scale = MLA_QK ** -0.5 * LOG2E
    for hd, y in enumerate(_mla_heads(mq, gq_ref[...], cos, sin, True)):
        mq_ref[:, hd * MLA_W:(hd + 1) * MLA_W] = (y * q_scale).astype(BF16)


def _cache_kv_kernel(ckv_ref, krw_ref, wkn_ref, wv_ref, gk_ref, mk_ref, mv_ref):
    _mla_kv(ckv_ref[...].astype(BF16), krw_ref[...], wkn_ref, wv_ref, gk_ref[...], None, None, False,
            mk_ref, mv_ref)


def _mod_row(n_ctx_blocks, blocks_per_lat):
    return lambda i: jnp.where(i < n_ctx_blocks, 0, 1 + (i - n_ctx_blocks) // blocks_per_lat)


def _mod_spec(mod_row):
    return pl.BlockSpec((1, 8, D_MODEL), lambda i: (mod_row(i), 0, 0))


def _rows(width):
    return pl.BlockSpec((ROW_BLOCK, width), lambda i: (i, 0))


def _ctx_rows(width, n_ctx_blocks):
    return pl.BlockSpec((ROW_BLOCK, width), lambda i: (jnp.minimum(i, n_ctx_blocks - 1), 0))


def _lat_rows(width, n_ctx_blocks):
    return pl.BlockSpec((ROW_BLOCK, width), lambda i: (jnp.maximum(i - n_ctx_blocks, 0), 0))


def _proj_outputs(t, n_ctx_blocks, widths, dtypes, ctx_only):
    spare = lambda w: pl.BlockSpec((ROW_BLOCK, w), lambda i: (jnp.minimum(i, n_ctx_blocks), 0))
    specs = [spare(w) if c else _rows(w) for w, c in zip(widths, ctx_only)]
    shapes = [jax.ShapeDtypeStruct(((n_ctx_blocks + 1) * ROW_BLOCK if c else t, w), dt)
              for w, dt, c in zip(widths, dtypes, ctx_only)]
    return specs, shapes


def _proj_a(x, mod, w_in, w_gate, gq, gk, cos, sin, mod_spec, rope_spec, n_ctx_blocks):
    t = x[0].shape[0] + x[1].shape[0]
    w_main = pl.BlockSpec((D_MODEL, 4 * D_MODEL), lambda i: (0, 0), pipeline_mode=pl.Buffered(1))
    widths = (D_MODEL,) * 6 + (3 * D_MODEL,)
    dtypes = (BF16, BF16, F32, BF16, F32, BF16, BF16)
    out_specs, out_shape = _proj_outputs(t, n_ctx_blocks, widths, dtypes, (0, 0, 1, 0, 1, 0, 0))
    return pl.pallas_call(
        functools.partial(_proj_a_kernel, n_ctx_blocks=n_ctx_blocks),
        grid=(t // ROW_BLOCK,),
        in_specs=[_ctx_rows(D_MODEL, n_ctx_blocks), _lat_rows(D_MODEL, n_ctx_blocks), mod_spec,
                  w_main, _resident(w_gate.shape), _resident(gq.shape), _resident(gk.shape), rope_spec, rope_spec],
        out_specs=out_specs,
        out_shape=out_shape,
        compiler_params=_cparams("arbitrary"),
        name="proj_a",
    )(*x, mod, w_in, w_gate, gq, gk, cos, sin)


def _proj_b(x, mod, w_b, gqa, gkva, wuq, wkn, wv, gq, gk, cos, sin, mod_spec, rope_spec, n_ctx_blocks):
    t = x[0].shape[0] + x[1].shape[0]
    widths = (MLA_HEADS * MLA_W, MLA_KV_RANK, MLA_ROPE, MLA_HEADS * MLA_W, D_MODEL)
    dtypes = (BF16, F32, F32, BF16, BF16)
    out_specs, out_shape = _proj_outputs(t, n_ctx_blocks, widths, dtypes, (0, 1, 1, 0, 0))
    consts = (w_b, gqa, gkva, wuq, wkn, wv, gq, gk)
    return pl.pallas_call(
        functools.partial(_proj_b_kernel, n_ctx_blocks=n_ctx_blocks),
        grid=(t // ROW_BLOCK,),
        in_specs=[_ctx_rows(D_MODEL, n_ctx_blocks), _lat_rows(D_MODEL, n_ctx_blocks), mod_spec]
        + [_resident(a.shape) for a in consts] + [rope_spec, rope_spec],
        out_specs=out_specs,
        out_shape=out_shape,
        compiler_params=_cparams("arbitrary"),
        name="proj_b",
    )(*x, mod, *consts, cos, sin)


def _cache_kv(ckv, krw, wkn, wv, gk):
    n = ckv.shape[0]
    widths = (MLA_HEADS * MLA_W, D_MODEL)
    consts = (wkn, wv, gk)
    return pl.pallas_call(
        _cache_kv_kernel,
        grid=(n // ROW_BLOCK,),
        in_specs=[_rows(MLA_KV_RANK), _rows(MLA_W)] + [_resident(a.shape) for a in consts],
        out_specs=[_rows(w) for w in widths],
        out_shape=[jax.ShapeDtypeStruct((n, w), BF16) for w in widths],
        compiler_params=_cparams("arbitrary"),
        name="cache_kv",
    )(ckv, krw, *consts)


def _pool_kernel(prev_ref, cur_ref, next_ref, w_ref, scale_ref, o_ref, *, n_ctx_blocks, ctx_len, lat_len):
    i = pl.program_id(0)
    rb = ROW_BLOCK
    is_ctx = i < n_ctx_blocks
    n_ctx_rows = n_ctx_blocks * rb
    seq_len = jnp.where(is_ctx, ctx_len, lat_len)
    seq_start = jnp.where(is_ctx, (i * rb // ctx_len) * ctx_len,
                          n_ctx_rows + ((i * rb - n_ctx_rows) // lat_len) * lat_len)
    n_cols = rb + 2 * POOL_HALO
    row_pos = i * rb - seq_start + lax.broadcasted_iota(jnp.int32, (rb, n_cols), 0)
    col_pos = i * rb - POOL_HALO - seq_start + lax.broadcasted_iota(jnp.int32, (rb, n_cols), 1)
    cur = cur_ref[...]
    u = jnp.concatenate([prev_ref[...], cur, next_ref[...]], axis=0)
    for g in range(N_POOL_GROUPS):
        cols = slice(g * POOL_GROUP, (g + 1) * POOL_GROUP)
        window = 2 << g
        lo = jnp.clip(row_pos - window // 2, 0, seq_len)
        hi = jnp.clip(row_pos - window // 2 + window, 0, seq_len)
        band = jnp.where((col_pos >= lo) & (col_pos < hi), 1.0, 0.0).astype(BF16)
        total = jnp.dot(band, u[:, cols], preferred_element_type=F32)
        cnt = (hi[:, :1] - lo[:, :1]).astype(F32)
        pooled = total / cnt - cur[:, cols].astype(F32)
        mixed = jnp.dot(pooled.astype(BF16), w_ref[g], preferred_element_type=F32)
        o_ref[:, cols] = (mixed * scale_ref[:, cols]).astype(BF16)


def _pool(u_pool, pool_w, pool_scale, n_ctx_blocks, ctx_len, lat_len):
    t = u_pool.shape[0]
    halo_per_block = ROW_BLOCK // POOL_HALO
    n_halo_blocks = t // POOL_HALO
    assert POOL_HALO >= (2 << (N_POOL_GROUPS - 1)) // 2
    kern = functools.partial(_pool_kernel, n_ctx_blocks=n_ctx_blocks, ctx_len=ctx_len, lat_len=lat_len)
    return pl.pallas_call(
        kern,
        grid=(t // ROW_BLOCK,),
        in_specs=[
            pl.BlockSpec((POOL_HALO, D_MODEL), lambda i: (jnp.maximum(i * halo_per_block - 1, 0), 0)),
            _rows(D_MODEL),
            pl.BlockSpec((POOL_HALO, D_MODEL),
                         lambda i: (jnp.minimum((i + 1) * halo_per_block, n_halo_blocks - 1), 0)),
            _resident(pool_w.shape),
            _resident(pool_scale.shape),
        ],
        out_specs=_rows(D_MODEL),
        out_shape=jax.ShapeDtypeStruct((t, D_MODEL), BF16),
        compiler_params=_cparams("arbitrary"),
        name="pool",
    )(u_pool, u_pool, u_pool, pool_w, pool_scale)


def _softmax_stats(s, m, l):
    m_new = jnp.maximum(m, jnp.max(s, axis=0, keepdims=True))
    p = jnp.exp2(s - m_new)
    alpha = jnp.exp2(m - m_new)
    return m_new, alpha * l + jnp.sum(p, axis=0, keepdims=True), alpha, p.astype(BF16)


def _accumulate(acc, alpha, p, v):
    return alpha * acc + lax.dot_general(v, p, _TN, preferred_element_type=F32)


def _online_softmax(chunks, scores, n_softmax, tq, defer_values):
    chunks = list(chunks)
    m = [jnp.full((1, tq), -jnp.inf, F32)] * n_softmax
    l = [jnp.zeros((1, tq), F32)] * n_softmax
    acc = [jnp.zeros((HEAD_W, tq), F32)] * n_softmax
    pending = None
    s_next = scores(chunks[0][0])
    for c, (_, v) in enumerate(chunks):
        s = s_next
        if c + 1 < len(chunks):
            s_next = scores(chunks[c + 1][0])
        stats = [_softmax_stats(s[j], m[j], l[j]) for j in range(n_softmax)]
        m = [st[0] for st in stats]
        l = [st[1] for st in stats]
        if pending is not None:
            acc = [_accumulate(acc[j], *pending[j]) for j in range(n_softmax)]
        pending = [(st[2], st[3], v) for st in stats]
        if not defer_values:
            acc = [_accumulate(acc[j], *pending[j]) for j in range(n_softmax)]
            pending = None
    if pending is not None:
        acc = [_accumulate(acc[j], *pending[j]) for j in range(n_softmax)]
    return list(zip(l, acc))


def _key_value_chunks(kv_refs, head, qk_width):
    k_cols = slice(head * qk_width, (head + 1) * qk_width)
    v_cols = slice(head * HEAD_W, (head + 1) * HEAD_W)
    for k_ref, v_ref in zip(kv_refs[0::2], kv_refs[1::2]):
        n = k_ref.shape[0]
        size = min(n, ATTN_K_CHUNK)
        for start in range(0, n, size):
            yield k_ref[start:start + size, k_cols].astype(BF16), v_ref[start:start + size, v_cols].astype(BF16)


def _diff_attn_kernel(lam_ref, q_ref, *refs, n_kv, heads):
    g_ref, o_ref = refs[n_kv], refs[-1]
    for h in range(heads):
        cols = slice(h * HEAD_W, (h + 1) * HEAD_W)
        q = q_ref[:, cols]
        lane = _lane_iota(q.shape)
        q1 = jnp.where(lane < DIFF_HD, q, jnp.zeros_like(q))
        q2 = jnp.where(lane >= DIFF_HD, q, jnp.zeros_like(q))

        def scores(k):
            return (lax.dot_general(k, q1, _NT, preferred_element_type=F32),
                    lax.dot_general(k, q2, _NT, preferred_element_type=F32))

        (l1, a1), (l2, a2) = _online_softmax(_key_value_chunks(refs[:n_kv], h, HEAD_W), scores, 2, q.shape[0],
                                             defer_values=False)
        o = (a1 / l1 - lam_ref[0] * (a2 / l2)).T
        o_ref[:, cols] = (_rms(o, HEAD_W) * g_ref[...]).astype(o_ref.dtype)


def _mla_attn_kernel(q_ref, *refs, n_kv, heads):
    o_ref = refs[-1]
    for h in range(heads):
        q = q_ref[:, h * MLA_W:(h + 1) * MLA_W]
        ((l, acc),) = _online_softmax(
            _key_value_chunks(refs[:n_kv], h, MLA_W),
            lambda k: (lax.dot_general(k, q, _NT, preferred_element_type=F32),), 1, q.shape[0],
            defer_values=True)
        o_ref[:, h * HEAD_W:(h + 1) * HEAD_W] = (acc / l).T.astype(o_ref.dtype)


def _attention(kernel, q, k, v, cache_k, cache_v, lead, lead_specs, tail, tail_specs, qk_width, tq, geom, name):
    n_ctx, seq, dec_seq = geom
    t = q.shape[0]
    n_heads = v.shape[1] // HEAD_W
    dec_batch, past, _ = cache_k.shape
    nq, q0, k0 =dec_seq // tq, n_ctx // tq, n_ctx // dec_seq
    assert n_ctx % tq == 0 and n_ctx % dec_seq == 0 and dec_seq % tq == 0
    params = _cparams("arbitrary", "arbitrary", "arbitrary")
    seq_block = lambda w: pl.BlockSpec((seq, n_heads * w), lambda b, h, qi: (b, 0))
    ctx = pl.pallas_call(
        functools.partial(kernel, n_kv=2, heads=n_heads),
        grid=(n_ctx // seq, 1, 1),
        in_specs=lead_specs + [seq_block(qk_width), seq_block(qk_width), seq_block(HEAD_W)] + tail_specs,
        out_specs=seq_block(HEAD_W),
        out_shape=jax.ShapeDtypeStruct((n_ctx, n_heads * HEAD_W), BF16),
        compiler_params=params,
        name=name + "_ctx",
    )(*lead, q, k, v, *tail)
    lat_rows = lambda w, first: pl.BlockSpec((tq, w), lambda b, h, qi: (first + b * nq + qi, h))
    cached = lambda w: pl.BlockSpec((None, past, w), lambda b, h, qi: (b, 0, h))
    lat_keys = lambda w: pl.BlockSpec((dec_seq, w), lambda b, h, qi: (k0 + b, h))
    lat = pl.pallas_call(
        functools.partial(kernel, n_kv=4, heads=1),
        grid=(dec_batch, n_heads, nq),
        in_specs=lead_specs + [lat_rows(qk_width, q0), cached(qk_width), cached(HEAD_W), lat_keys(qk_width),
                               lat_keys(HEAD_W)] + tail_specs,
        out_specs=lat_rows(HEAD_W, 0),
        out_shape=jax.ShapeDtypeStruct((t - n_ctx, n_heads * HEAD_W), BF16),
        compiler_params=params,
        name=name + "_lat",
    )(*lead, q, cache_k, cache_v, k, v, *tail)
    return ctx, lat


def _diff_attention(q, k, v, cache_k, cache_v, lam, out_gain, geom):
    lead_specs = [pl.BlockSpec(memory_space=pltpu.SMEM)]
    tail_specs = [pl.BlockSpec((1, HEAD_W), lambda b, h, qi: (0, 0))]
    return _attention(_diff_attn_kernel, q, k, v, cache_k, cache_v, (lam,), lead_specs, (out_gain,), tail_specs,
                      HEAD_W, DIFF_Q_BLOCK, geom, "diff")


def _mla_attention(q, k, v, cache_k, cache_v, geom):
    return _attention(_mla_attn_kernel, q, k, v, cache_k, cache_v, (), [], (), [], MLA_W, MLA_Q_BLOCK, geom, "mla")


def _merge_kernel(xc_ref, xl_ref, mod_ref, yp_ref, ydc_ref, ydl_ref, ymc_ref, yml_ref, gate_ref, wb_ref, wo_ref,
                  rw_ref, rb_ref, x1_ref, h2_ref, topw_ref, topi_ref, rank_ref, count_ref, carry_ref, logits_ref,
                  *, n_ctx_blocks):
    step = pl.program_id(0)

    @pl.when(step == 0)
    def _():
        carry_ref[...] = jnp.zeros_like(carry_ref)
        logits_ref[...] = jnp.zeros_like(logits_ref)

    is_ctx = step < n_ctx_blocks
    mod = mod_ref[0]
    merged = jnp.zeros(xc_ref.shape, F32)
    branches = (yp_ref[...], jnp.where(is_ctx, ydc_ref[...], ydl_ref[...]),
                jnp.where(is_ctx, ymc_ref[...], yml_ref[...]))
    for r, y in enumerate(branches):
        z = jnp.dot(y, wb_ref[r], preferred_element_type=F32)
        merged = merged + gate_ref[:, r * D_MODEL:(r + 1) * D_MODEL].astype(F32) * z
    y = jnp.dot(merged.astype(BF16), wo_ref[...], preferred_element_type=F32)
    x1 = jnp.where(is_ctx, xc_ref[...], xl_ref[...]) + mod[2:3] * y
    x1_ref[...] = x1
    h2 = _modulated_norm(x1, mod[3:4], mod[4:5])
    h2_ref[...] = h2
    new_logits = jnp.dot(h2, rw_ref[...], preferred_element_type=F32) + rb_ref[...]
    logits = logits_ref[...]
    logits_ref[...] = new_logits
    lane = _lane_iota(logits.shape)
    picks = []
    for _ in range(TOP_K):
        mx = jnp.max(logits, axis=-1, keepdims=True)
        first = jnp.min(jnp.where(logits == mx, lane, EXPERT_LANES), axis=-1, keepdims=True)
        hit = lane == first
        picks.append((mx, first, hit))
        logits = jnp.where(hit, -jnp.inf, logits)
    exps = [jnp.exp(v - picks[0][0]) for v, _, _ in picks]
    denom = exps[0] + exps[1] + exps[2] + exps[3]
    rows = logits.shape[0]
    chosen = jnp.zeros(logits.shape, F32)
    for _, _, hit in picks:
        chosen = chosen + jnp.where(hit, 1.0, 0.0)
    earlier = jnp.where(lax.broadcasted_iota(jnp.int32, (rows, rows), 1)
                        < lax.broadcasted_iota(jnp.int32, (rows, rows), 0), 1.0, 0.0).astype(BF16)
    before = jnp.dot(earlier, chosen.astype(BF16), preferred_element_type=F32) + carry_ref[0:1, :]
    topw = jnp.zeros(logits.shape, F32)
    topi = jnp.zeros(logits.shape, jnp.int32)
    rank = jnp.zeros(logits.shape, jnp.int32)
    for k, (e, (_, first, hit)) in enumerate(zip(exps, picks)):
        slot_rank = jnp.sum(jnp.where(hit, before, 0.0), axis=-1, keepdims=True)
        topw = jnp.where(lane == k, e / denom, topw)
        topi = jnp.where(lane == k, first, topi)
        rank = jnp.where(lane == k, slot_rank.astype(jnp.int32), rank)
    topw_ref[...] = topw
    topi_ref[...] = topi
    rank_ref[...] = rank
    routed = jnp.where(step > 0, jnp.sum(chosen, axis=0, keepdims=True), 0.0)
    carry_ref[...] = carry_ref[...] + routed
    count_ref[...] = carry_ref[...]


def _merge(x, mod, y_pool, y_diff, y_mla, gates, wb, wo, rw, rb, mod_row, n_ctx_blocks):
    t = x[0].shape[0] + x[1].shape[0]
    nb = t // ROW_BLOCK
    consts = (wb, wo, rw, rb)
    lanes = EXPERT_LANES
    cur = lambda i: jnp.minimum(i, nb - 1)
    rows = lambda w: pl.BlockSpec((ROW_BLOCK, w), lambda i: (cur(i), 0))
    routed = pl.BlockSpec((ROW_BLOCK, lanes), lambda i: (jnp.maximum(i - 1, 0), 0))
    ctx_rows = _ctx_rows(D_MODEL, n_ctx_blocks)
    lat_rows = pl.BlockSpec((ROW_BLOCK, D_MODEL), lambda i: (jnp.maximum(cur(i) - n_ctx_blocks, 0), 0))
    return pl.pallas_call(
        functools.partial(_merge_kernel, n_ctx_blocks=n_ctx_blocks),
        grid=(nb + 1,),
        in_specs=[ctx_rows, lat_rows, _mod_spec(lambda i: mod_row(cur(i))), rows(D_MODEL), ctx_rows, lat_rows,
                  ctx_rows, lat_rows, rows(3 * D_MODEL)] + [_resident(a.shape) for a in consts],
        out_specs=[rows(D_MODEL), rows(D_MODEL), routed, routed, routed,
                   pl.BlockSpec((8, lanes), lambda i: (0, 0))],
        out_shape=[jax.ShapeDtypeStruct((t, D_MODEL), F32), jax.ShapeDtypeStruct((t, D_MODEL), F32),
                   jax.ShapeDtypeStruct((t, lanes), F32), jax.ShapeDtypeStruct((t, lanes), jnp.int32),
                   jax.ShapeDtypeStruct((t, lanes), jnp.int32), jax.ShapeDtypeStruct((8, lanes), F32)],
        scratch_shapes=[pltpu.VMEM((8, lanes), F32), pltpu.VMEM((ROW_BLOCK, lanes), F32)],
        compiler_params=_cparams("arbitrary"),
        name="merge",
    )(*x, mod, y_pool, *y_diff, *y_mla, gates, *consts)


def _slot_layout(counts, topi, rank):
    tile = MOE_TILE
    cnt = counts[0, :N_EXPERTS].astype(jnp.int32)
    padded = (cnt + tile - 1) // tile * tile
    end = jnp.cumsum(padded)
    start = end - padded
    experts = jnp.arange(N_EXPERTS, dtype=jnp.int32)
    idx4 = topi[:, :TOP_K]
    slot_start = jnp.sum(jnp.where(idx4[:, :, None] == experts, start, 0), axis=-1)
    pos = (slot_start + rank[:, :TOP_K]).reshape(-1)
    n_tiles = (topi.shape[0] * TOP_K + N_EXPERTS * tile) // tile
    tile_row = jnp.arange(n_tiles, dtype=jnp.int32) * tile
    tile_expert = jnp.minimum(jnp.sum(tile_row[:, None] >= end, axis=-1), N_EXPERTS - 1).astype(jnp.int32)
    n_used = (end[-1] // tile).astype(jnp.int32)
    tile_src = jnp.minimum(jnp.arange(n_tiles, dtype=jnp.int32), n_used - 1)
    ragged_last = jnp.any((cnt % tile != 0) & (tile_row[:, None] == end - tile), axis=-1)
    fill_tile = (ragged_last | (tile_row >= end[-1])).astype(jnp.int32)
    return pos, tile_expert, tile_src, n_used.reshape(1), fill_tile


def _row_copy(src, src_row, dst, dst_row, sem):
    return pltpu.make_async_copy(src.at[pl.ds(src_row, 1)], dst.at[pl.ds(dst_row, 1)], sem)


def _dispatch_kernel(fill_tile_ref, pos_ref, h_ref, hs_ref, zero_ref, src_ref, fill_sem, fetch_sems, row_sems):
    i = pl.program_id(0)
    n_steps = pl.num_programs(0)
    slots = DISPATCH_ROWS * TOP_K

    def fetch(step, buf):
        rows = pl.ds(pl.multiple_of(step * DISPATCH_ROWS, DISPATCH_ROWS), DISPATCH_ROWS)
        return pltpu.make_async_copy(h_ref.at[rows], src_ref.at[buf], fetch_sems.at[buf])

    @pl.when(i == 0)
    def _():
        fetch(0, 0).start()
        fetch(1, 1).start()

    @pl.when(i == 0)
    def _():
        zero_ref[...] = jnp.zeros_like(zero_ref)

        def fill_copy(tile):
            row = pl.multiple_of(tile * MOE_TILE, MOE_TILE)
            return pltpu.make_async_copy(zero_ref, hs_ref.at[pl.ds(row, MOE_TILE)], fill_sem)

        def start_fill(tile, carry):
            @pl.when(fill_tile_ref[tile] != 0)
            def _():
                fill_copy(tile).start()
            return carry

        def wait_fill(tile, carry):
            @pl.when(fill_tile_ref[tile] != 0)
            def _():
                fill_copy(tile).wait()
            return carry

        lax.fori_loop(0, hs_ref.shape[0] // MOE_TILE, start_fill, 0)
        lax.fori_loop(0, hs_ref.shape[0] // MOE_TILE, wait_fill, 0)

    buf = i % 3
    src = src_ref.at[buf]
    fetch(i, buf).wait()

    def issue(r, carry):
        slot = (i * DISPATCH_ROWS + r) * TOP_K
        for k in range(TOP_K):
            _row_copy(src, r, hs_ref, pos_ref[slot + k], row_sems.at[i % 2]).start()
        return carry

    def drain(sem):
        def body(j, carry):
            for _ in range(WAIT_UNROLL):
                _row_copy(src, 0, hs_ref, 0, sem).wait()
            return carry

        lax.fori_loop(0, slots // WAIT_UNROLL, body, 0)

    lax.fori_loop(0, DISPATCH_ROWS, issue, 0, unroll=ISSUE_UNROLL)

    @pl.when(i > 0)
    def _():
        drain(row_sems.at[(i - 1) % 2])

    @pl.when(i + 2 < n_steps)
    def _():
        fetch(i + 2, (i + 2) % 3).start()

    @pl.when(i == n_steps - 1)
    def _():
        drain(row_sems.at[i % 2])


def _dispatch(h2, pos, fill_tile):
    t = h2.shape[0]
    n_rows = fill_tile.shape[0] * MOE_TILE
    return pl.pallas_call(
        _dispatch_kernel,
        grid_spec=pltpu.PrefetchScalarGridSpec(
            num_scalar_prefetch=2,
            grid=(t // DISPATCH_ROWS,),
            in_specs=[pl.BlockSpec(memory_space=pl.ANY)],
            out_specs=pl.BlockSpec(memory_space=pl.ANY),
            scratch_shapes=[pltpu.VMEM((MOE_TILE, D_MODEL), F32), pltpu.VMEM((3, DISPATCH_ROWS, D_MODEL), F32),
                            pltpu.SemaphoreType.DMA, pltpu.SemaphoreType.DMA((3,)),
                            pltpu.SemaphoreType.DMA((2,))],
        ),
        out_shape=jax.ShapeDtypeStruct((n_rows, D_MODEL), F32),
        compiler_params=_cparams("arbitrary"),
        name="dispatch",
    )(fill_tile, pos, h2)


def _expert_kernel(tile_expert_ref, tile_src_ref, n_used_ref, hs_ref, wgu_ref, bgu_ref, wd_ref, bd_ref, ys_ref,
                   wgu_bf_ref, wd_bf_ref):
    i = pl.program_id(0)
    occupied = i < n_used_ref[0]
    new_expert = (i == 0) | (tile_expert_ref[i] != tile_expert_ref[jnp.maximum(i - 1, 0)])

    @pl.when(occupied & new_expert)
    def _():
        wgu_bf_ref[...] = wgu_ref[...].astype(BF16)
        wd_bf_ref[...] = wd_ref[...].astype(BF16)

    @pl.when(occupied)
    def _():
        gu = jnp.dot(hs_ref[...].astype(BF16), wgu_bf_ref[...], preferred_element_type=F32) + bgu_ref[0]
        gate = jnp.minimum(gu[:, :D_FF], SWIGLU_LIMIT)
        up = jnp.clip(gu[:, D_FF:], -SWIGLU_LIMIT, SWIGLU_LIMIT)
        act = (up + 1.0) * gate * _sigmoid(SWIGLU_ALPHA * gate)
        ys_ref[...] = jnp.dot(act.astype(BF16), wd_bf_ref[...], preferred_element_type=F32) + bd_ref[0]

    @pl.when(jnp.logical_not(occupied))
    def _():
        ys_ref[...] = jnp.zeros_like(ys_ref)


def _experts(hs, tile_expert, tile_src, n_used, layer, wgu, bgu, wd, bd):
    n_rows = hs.shape[0]
    rows = lambda i, te, ts, nu: (ts[i], 0)
    bias = lambda i, te, ts, nu: (te[i], 0, 0)
    weight = lambda i, te, ts, nu: (layer, te[i], 0, 0)
    return pl.pallas_call(
        _expert_kernel,
        grid_spec=pltpu.PrefetchScalarGridSpec(
            num_scalar_prefetch=3,
            grid=(n_rows // MOE_TILE,),
            in_specs=[
                pl.BlockSpec((MOE_TILE, D_MODEL), rows),
                pl.BlockSpec((None, None, D_MODEL, 2 * D_FF), weight),
                pl.BlockSpec((1, 1, 2 * D_FF), bias),
                pl.BlockSpec((None, None, D_FF, D_MODEL), weight),
                pl.BlockSpec((1, 1, D_MODEL), bias),
            ],
            out_specs=pl.BlockSpec((MOE_TILE, D_MODEL), lambda i, te, ts, nu: (i, 0)),
            scratch_shapes=[pltpu.VMEM((D_MODEL, 2 * D_FF), BF16), pltpu.VMEM((D_FF, D_MODEL), BF16)],
        ),
        out_shape=jax.ShapeDtypeStruct((n_rows, D_MODEL), F32),
        compiler_params=_cparams("arbitrary"),
        name="experts",
    )(tile_expert, tile_src, n_used, hs, wgu, bgu, wd, bd)


def _combine_kernel(pos_ref, ys_ref, w_ref, x1_ref, mod_ref, *refs, n_ctx_blocks):
    oc_ref, ol_ref, buf_ref, sems = refs
    i = pl.program_id(0)
    slots = COMBINE_ROWS * TOP_K

    def gather(step, buf):
        def issue(r, carry):
            token = step * COMBINE_ROWS + r
            for k in range(TOP_K):
                _row_copy(ys_ref, pos_ref[token * TOP_K + k], buf_ref.at[buf, k], r, sems.at[buf]).start()
            return carry

        lax.fori_loop(0, COMBINE_ROWS, issue, 0, unroll=ISSUE_UNROLL)

    @pl.when(i == 0)
    def _():
        gather(0, 0)

    @pl.when(i + 1 < pl.num_programs(0))
    def _():
        gather(i + 1, (i + 1) % 2)

    cur = i % 2

    def drain(j, carry):
        for _ in range(WAIT_UNROLL):
            _row_copy(ys_ref, 0, buf_ref.at[cur, 0], 0, sems.at[cur]).wait()
        return carry

    lax.fori_loop(0, slots // WAIT_UNROLL, drain, 0)
    w = w_ref[...]
    acc = w[:, 0:1] * buf_ref[cur, 0]
    for k in range(1, TOP_K):
        acc = acc + w[:, k:k + 1] * buf_ref[cur, k]
    result = x1_ref[...] + mod_ref[0][5:6] * acc

    @pl.when(i < n_ctx_blocks)
    def _():
        oc_ref[...] = result

    @pl.when(i >= n_ctx_blocks)
    def _():
        ol_ref[...] = result


def _combine(ys, pos, topw, x1, mod, n_ctx_rows, lat_len):
    t = x1.shape[0]
    rb = COMBINE_ROWS
    n_ctx_blocks = n_ctx_rows // rb

    def mod_row(i, pos_ref):
        return (jnp.where(i < n_ctx_blocks, 0, 1 + (i - n_ctx_blocks) // (lat_len // rb)), 0, 0)

    out_specs = [pl.BlockSpec((rb, D_MODEL), lambda i, pos_ref: (jnp.minimum(i, n_ctx_blocks - 1), 0)),
                 pl.BlockSpec((rb, D_MODEL), lambda i, pos_ref: (jnp.maximum(i - n_ctx_blocks, 0), 0))]
    out_shape = [jax.ShapeDtypeStruct((n_ctx_rows, D_MODEL), F32),
                 jax.ShapeDtypeStruct((t - n_ctx_rows, D_MODEL), F32)]
    return pl.pallas_call(
        functools.partial(_combine_kernel, n_ctx_blocks=n_ctx_blocks),
        grid_spec=pltpu.PrefetchScalarGridSpec(
            num_scalar_prefetch=1,
            grid=(t // rb,),
            in_specs=[
                pl.BlockSpec(memory_space=pl.ANY),
                pl.BlockSpec((rb, EXPERT_LANES), lambda i, pos_ref: (i, 0)),
                pl.BlockSpec((rb, D_MODEL), lambda i, pos_ref: (i, 0)),
                pl.BlockSpec((1, 8, D_MODEL), mod_row),
            ],
            out_specs=out_specs,
            scratch_shapes=[pltpu.VMEM((2, TOP_K, rb, D_MODEL), F32), pltpu.SemaphoreType.DMA((2,))],
        ),
        out_shape=out_shape,
        compiler_params=_cparams("arbitrary"),
        name="combine",
    )(pos, ys, topw, x1, mod)


def _rope_tables(n_lat_tokens, rot_dim):
    n_rows = n_lat_tokens // GRID_W
    row = jnp.repeat(jnp.arange(n_rows, dtype=F32), GRID_W)
    col = jnp.tile(jnp.arange(GRID_W, dtype=F32), n_rows)
    n_freq = rot_dim // 4
    inv_freq = ROPE_THETA ** (-jnp.arange(n_freq, dtype=F32) / n_freq)
    ang = jnp.concatenate([row[:, None] * inv_freq, col[:, None] * inv_freq], axis=-1)
    return jnp.cos(ang), jnp.sin(ang)


def _pack_heads(w, n_heads, lo, hi, width):
    k = w.shape[0]
    per_head = w.shape[1] // n_heads
    part = w.reshape(k, n_heads, per_head)[:, :, lo:hi]
    part = jnp.pad(part, ((0, 0), (0, 0), (0, width - (hi - lo))))
    return part.reshape(k, n_heads * width)


def kernel(x_prompt, x_sample, c, cache_diff_k, cache_diff_v, cache_mla_ckv, cache_mla_krope, c_ctx, w_ada, b_ada,
           w_in, pool_w, pool_scale, diff_q_norm, diff_k_norm, diff_lambda, diff_out_norm, mla_q_a_norm, w_uq,
           mla_kv_a_norm, w_ukv, mla_q_norm, mla_k_norm, w_branch, w_out, router_w, router_b, moe_w_gu, moe_b_gu,
           moe_w_down, moe_b_down):
    batch, seq, d = x_prompt.shape
    dec_batch, dec_seq, _ = x_sample.shape
    depth = w_ada.shape[0]
    past = cache_diff_k.shape[2]
    n_ctx = batch * seq
    n_lat = dec_batch * dec_seq
    n_ctx_blocks = n_ctx // ROW_BLOCK
    mod_row = _mod_row(n_ctx_blocks, dec_seq // ROW_BLOCK)
    mod_spec = _mod_spec(mod_row)

    x = (x_prompt.reshape(n_ctx, d), x_sample.reshape(n_lat, d))
    cond8 = jnp.concatenate([c_ctx[None], c, jnp.zeros((8 - 1 - dec_batch, d), F32)], axis=0)
    mod_all = _ada(cond8, w_ada, b_ada).reshape(depth, 8, 6, d)[:, :1 + dec_batch]
    mod_all = jnp.pad(mod_all, ((0, 0), (0, 0), (0, 2), (0, 0)))

    cos_d, sin_d = _rope_tables(dec_seq, DIFF_HD)
    cos_m, sin_m = _rope_tables(dec_seq, MLA_ROPE)
    cos_a = jnp.concatenate([cos_d] * 4, axis=1)
    sin_a = jnp.concatenate([-sin_d, sin_d, -sin_d, sin_d], axis=1)
    pad = MLA_W - MLA_QK
    cos_b = jnp.concatenate([cos_m, cos_m, jnp.ones((dec_seq, pad), F32)], axis=1)
    sin_b = jnp.concatenate([-sin_m, sin_m, jnp.zeros((dec_seq, pad), F32)], axis=1)
    rope_spec = pl.BlockSpec(
        (ROW_BLOCK, HEAD_W), lambda i: (jnp.maximum(i - n_ctx_blocks, 0) % (dec_seq // ROW_BLOCK), 0))

    offs = np.cumsum((0, 1024, 1024, 1024, 1024, MLA_Q_RANK, MLA_KV_RANK, MLA_ROPE, 3 * 1024))
    new_dk, new_dv, new_ckv, new_kr = [], [], [], []
    for l in range(depth):
        lam_init = 0.8 - 0.6 * math.exp(-0.3 * l)
        w_in_bf = w_in[l].astype(BF16)
        seg = [w_in_bf[:, offs[k]:offs[k + 1]] for k in range(8)]
        kr_wide = jnp.pad(seg[6], ((0, 0), (MLA_NOPE, MLA_W - MLA_NOPE - MLA_ROPE)))
        w_b = jnp.concatenate([seg[4], seg[5], kr_wide], axis=1)
        wuq = _pack_heads(w_uq[l], MLA_HEADS, 0, MLA_QK, MLA_W).astype(BF16)
        wkn = _pack_heads(w_ukv[l], MLA_HEADS, 0, MLA_NOPE, MLA_W).astype(BF16)
        wv = _pack_heads(w_ukv[l], MLA_HEADS, MLA_NOPE, MLA_NOPE + HEAD_W, HEAD_W).astype(BF16)
        gq_d = jnp.tile(diff_q_norm[l], 2 * DIFF_HEADS)[None]
        gk_d = jnp.tile(diff_k_norm[l], 2 * DIFF_HEADS)[None]
        gq_m = jnp.tile(jnp.pad(mla_q_norm[l], (0, MLA_W - MLA_QK)), MLA_HEADS)[None]
        gk_m = jnp.tile(jnp.pad(mla_k_norm[l], (0, MLA_W - MLA_QK)), MLA_HEADS)[None]
        lp = diff_lambda[l]
        lam = (jnp.exp(jnp.sum(lp[0] * lp[1])) - jnp.exp(jnp.sum(lp[2] * lp[3])) + lam_init).reshape(1)
        out_gain = (diff_out_norm[l] * (1.0 - lam_init))[None]
        mod = mod_all[l]

        u_pool, dq, dk_f, dk_b, dv_f, dv_b, gates = _proj_a(x, mod, w_in_bf, seg[7], gq_d, gk_d, cos_a, sin_a,
                                                            mod_spec, rope_spec, n_ctx_blocks)
        mq, ckv_n, kr, mk, mv = _proj_b(x, mod, w_b, mla_q_a_norm[l][None], mla_kv_a_norm[l][None], wuq, wkn, wv,
                                        gq_m, gk_m, cos_b, sin_b, mod_spec, rope_spec, n_ctx_blocks)
        new_dk.append(dk_f[:n_ctx].reshape(batch, seq, DIFF_HEADS, HEAD_W))
        new_dv.append(dv_f[:n_ctx].reshape(batch, seq, DIFF_HEADS, HEAD_W))
        new_ckv.append(ckv_n[:n_ctx].reshape(batch, seq, MLA_KV_RANK))
        new_kr.append(kr[:n_ctx].reshape(batch, seq, MLA_ROPE))

        y_pool = _pool(u_pool, pool_w[l].astype(BF16), pool_scale[l][None], n_ctx_blocks, seq, dec_seq)

        c_krw = jnp.pad(cache_mla_krope[:, l].reshape(dec_batch * past, MLA_ROPE),
                        ((0, 0), (MLA_NOPE, MLA_W - MLA_NOPE - MLA_ROPE)))
        c_mk, c_mv = _cache_kv(cache_mla_ckv[:, l].reshape(dec_batch * past, MLA_KV_RANK), c_krw, wkn, wv, gk_m)

        geom = (n_ctx, seq, dec_seq)
        y_diff = _diff_attention(dq, dk_b, dv_b, cache_diff_k[:, l].reshape(dec_batch, past, d),
                                 cache_diff_v[:, l].reshape(dec_batch, past, d), lam, out_gain, geom)
        y_mla = _mla_attention(mq, mk, mv, c_mk.reshape(dec_batch, past, -1), c_mv.reshape(dec_batch, past, -1),
                               geom)

        rw = jnp.pad(router_w[l], ((0, 0), (0, EXPERT_LANES - N_EXPERTS)))
        rb = jnp.pad(router_b[l], (0, EXPERT_LANES - N_EXPERTS), constant_values=-jnp.inf)[None]
        x1, h2, topw, topi, rank, counts = _merge(x, mod, y_pool, y_diff, y_mla, gates, w_branch[l].astype(BF16),
                                                  w_out[l].astype(BF16), rw, rb, mod_row, n_ctx_blocks)
        pos, tile_expert, tile_src, n_used, fill_tile = _slot_layout(counts, topi, rank)
        hs = _dispatch(h2, pos, fill_tile)
        ys = _experts(hs, tile_expert, tile_src, n_used, l, moe_w_gu, moe_b_gu[l][:, None, :],
                      moe_w_down, moe_b_down[l][:, None, :])
        x = _combine(ys, pos, topw, x1, mod, n_ctx, dec_seq)

    y_ctx, y_lat = x
    return (y_ctx.reshape(batch, seq, d), y_lat.reshape(dec_batch, dec_seq, d),
            jnp.stack(new_dk, axis=1), jnp.stack(new_dv, axis=1), jnp.stack(new_ckv, axis=1),
            jnp.stack(new_kr, axis=1))
```
